```python
import math
import jax
import jax.numpy as jnp
from jax import lax
import numpy as np

D_MODEL = 2048
BATCH = 32
SEQ = 256
DEPTH = 2
DEC_BATCH = 8
DEC_SEQ = 2048
PAST_LEN = 512

GRID_W = 64
QBLOCK = 128
ROPE_BASE = 10000.0
EPS = 1e-6
NEG_INF = -1e30
N_BRANCH = 4
BRANCH_W = 512
MLA_V = 64
MLA_HEADS = BRANCH_W // MLA_V
MLA_NOPE = 64
MLA_ROPE = 32
MLA_KV_RANK = 256
DIFF_QK = 64
DIFF_V = 2 * DIFF_QK
DIFF_HEADS = BRANCH_W // DIFF_V
NA_DIM = 64
NA_HEADS = BRANCH_W // NA_DIM
NA_ROWS = 8
NA_COLS = 16
RET_V = 128
RET_HEADS = BRANCH_W // RET_V
RET_QK = 64
RET_CHUNK = 128
D_FF = 5632
CONV_W = 3
IN_SIZES = (MLA_HEADS * (MLA_NOPE + MLA_ROPE), MLA_KV_RANK, MLA_ROPE,
            DIFF_HEADS * 2 * DIFF_QK, DIFF_HEADS * 2 * DIFF_QK, DIFF_HEADS * DIFF_V,
            NA_HEADS * NA_DIM, NA_HEADS * NA_DIM, NA_HEADS * NA_DIM,
            RET_HEADS * RET_QK, RET_HEADS * RET_QK, RET_HEADS * RET_V, BRANCH_W,
            N_BRANCH * D_MODEL)
IN_WIDTH = sum(IN_SIZES)

kernel_name = 'hybrid_diffusion_prefix_trunk_step'


def rmsnorm(x, g):
    xf = x.astype(jnp.float32)
    y = xf * lax.rsqrt(jnp.mean(xf * xf, axis=-1, keepdims=True) + EPS)
    return (y * g.astype(jnp.float32)).astype(x.dtype)


def head_groupnorm(y, g):
    H, d = y.shape[-2:]
    mu = jnp.mean(y, axis=-1, keepdims=True)
    yc = y - mu
    var = jnp.mean(yc * yc, axis=-1, keepdims=True)
    return yc * lax.rsqrt(var + EPS) * g.astype(jnp.float32).reshape(H, d)


def split_cols(p):
    parts, start = [], 0
    for n in IN_SIZES:
        parts.append(p[..., start:start + n])
        start += n
    return parts


def ada_mod(cond, w, b):
    m = jax.nn.silu(cond) @ w + b
    m = m.reshape((-1, 1, 6, D_MODEL))
    return [m[:, :, i] for i in range(6)]


def modulate(x, g, shift, scale):
    return rmsnorm(x, g) * (1 + scale) + shift


def axial_rope(x):
    T, dr = x.shape[1], x.shape[-1]
    nf = dr // 4
    t = jnp.arange(T)
    pos = jnp.stack([t // GRID_W, t % GRID_W], axis=-1).astype(jnp.float32)
    inv = ROPE_BASE ** (-jnp.arange(nf, dtype=jnp.float32) / nf)
    ang = pos[:, None, :, None] * inv
    cos = jnp.cos(ang).astype(x.dtype)
    sin = jnp.sin(ang).astype(x.dtype)
    xr = x.reshape(x.shape[:-1] + (2, 2, nf))
    x1, x2 = xr[..., 0, :], xr[..., 1, :]
    out = jnp.stack([x1 * cos - x2 * sin, x2 * cos + x1 * sin], axis=-2)
    return out.reshape(x.shape)


def map_query_blocks(fn, q):
    B, T = q.shape[:2]
    nb = T // QBLOCK
    qb = jnp.moveaxis(q.reshape((B, nb, QBLOCK) + q.shape[2:]), 1, 0)
    out = jnp.moveaxis(lax.map(fn, qb), 0, 1)
    return out.reshape((B, T) + out.shape[3:])


def attend(q, k, v):
    scale = q.shape[-1] ** -0.5

    def blk(qb):
        s = jnp.einsum('bqhd,bkhd->bhqk', qb, k).astype(jnp.float32) * scale
        p = jax.nn.softmax(s, axis=-1).astype(v.dtype)
        return jnp.einsum('bhqk,bkhe->bqhe', p, v)

    return map_query_blocks(blk, q)


def diff_attend(q, k, v, lam):
    scale = q.shape[-1] ** -0.5

    def blk(qb):
        s = jnp.einsum('bqhcd,bkhcd->bhcqk', qb, k).astype(jnp.float32) * scale
        p = jax.nn.softmax(s, axis=-1)
        a = (p[:, :, 0] - lam * p[:, :, 1]).astype(v.dtype)
        return jnp.einsum('bhqk,bkhe->bqhe', a, v)

    return map_query_blocks(blk, q)


def na_latent(q, k, v, k_ctx, v_ctx, rel_bias):
    B, T, H, d = q.shape
    rows = T // GRID_W
    kh = min(NA_ROWS, rows)
    n_loc = kh * GRID_W
    scale = d ** -0.5
    qg = q.reshape(B, rows, GRID_W, H, d)
    kg = k.reshape(B, rows, GRID_W, H, d)
    vg = v.reshape(B, rows, GRID_W, H, d)
    r_idx = jnp.arange(rows)
    r_start = jnp.clip(r_idx - kh // 2, 0, rows - kh)
    cols = jnp.arange(GRID_W)
    c_start = jnp.clip(cols - NA_COLS // 2, 0, GRID_W - NA_COLS)
    col_ok = (cols[None, :] >= c_start[:, None]) & (cols[None, :] < c_start[:, None] + NA_COLS)
    col_idx = jnp.clip(cols[None, :] - cols[:, None], 1 - NA_COLS, NA_COLS - 1) + NA_COLS - 1

    def row_block(args):
        q_r, r, s = args
        k_loc = lax.dynamic_slice_in_dim(kg, s, kh, axis=1).reshape(B, n_loc, H, d)
        v_loc = lax.dynamic_slice_in_dim(vg, s, kh, axis=1).reshape(B, n_loc, H, d)
        row_idx = s + jnp.arange(kh) - r + NA_ROWS - 1
        bias = rel_bias[:, row_idx[None, :, None], col_idx[:, None, :]].astype(jnp.float32)
        s_loc = jnp.einsum('bqhd,bkhd->bhqk', q_r, k_loc).astype(jnp.float32)
        s_loc = s_loc.reshape(B, H, GRID_W, kh, GRID_W) * scale + bias
        s_loc = jnp.where(col_ok[:, None, :], s_loc, NEG_INF).reshape(B, H, GRID_W, n_loc)
        s_ctx = jnp.einsum('bqhd,bkhd->bhqk', q_r, k_ctx).astype(jnp.float32) * scale
        p = jax.nn.softmax(jnp.concatenate([s_loc, s_ctx], axis=-1), axis=-1).astype(v.dtype)
        return (jnp.einsum('bhqk,bkhe->bqhe', p[..., :n_loc], v_loc)
                + jnp.einsum('bhqk,bkhe->bqhe', p[..., n_loc:], v_ctx))

    out = lax.map(row_block, (jnp.moveaxis(qg, 1, 0), r_idx, r_start))
    return jnp.moveaxis(out, 0, 1).reshape(B, T, H * d)


def retention_scan(q, k, v, log_g, s0, strict):
    B, T, H, dk = q.shape
    C = RET_CHUNK
    n = T // C
    qc = q.reshape(B, n, C, H, dk)
    kc = k.reshape(B, n, C, H, dk)
    vc = v.reshape(B, n, C, H, -1)
    i = jnp.arange(C, dtype=jnp.float32)
    diff = i[:, None] - i[None, :]
    valid = (diff > 0) if strict else (diff >= 0)
    dmat = jnp.where(valid, jnp.exp(jnp.maximum(diff, 0.0)[None] * log_g[:, None, None]), 0.0)
    s = jnp.einsum('bnihd,bnjhd->bnhij', qc, kc) * dmat
    intra = jnp.einsum('bnhij,bnjhe->bnihe', s, vc)
    kdec = jnp.exp((C - 1 - i)[None, :] * log_g[:, None])
    kv = jnp.einsum('bnjhd,bnjhe,hj->nbhde', kc, vc, kdec)
    g_chunk = jnp.exp(C * log_g)[:, None, None]

    def step(S, kv_c):
        return g_chunk * S + kv_c, S

    s_fin, s_prev = lax.scan(step, s0, kv)
    qdec = jnp.exp((i + 1)[:, None] * log_g[None, :])
    cross = jnp.einsum('bnihd,nbhde->bnihe', qc, s_prev) * qdec[:, :, None]
    return (intra + cross).reshape(B, T, H, -1), s_fin


def retention(rq, rk, rv, rg, prm, s0_f, s0_b):
    B, T, _ = rq.shape
    q = rq.reshape(B, T, RET_HEADS, RET_QK).astype(jnp.float32)
    k = rk.reshape(B, T, RET_HEADS, RET_QK).astype(jnp.float32) * RET_QK ** -0.5
    v = rv.reshape(B, T, RET_HEADS, RET_V).astype(jnp.float32)
    log_g = jax.nn.log_sigmoid(prm['ret_decay'].astype(jnp.float32))
    y_f, s_f = retention_scan(q, k, v, log_g[0], s0_f, False)
    y_b, s_b = retention_scan(jnp.flip(q, 1), jnp.flip(k, 1), jnp.flip(v, 1), log_g[1], s0_b, True)
    y = head_groupnorm(y_f + jnp.flip(y_b, 1), prm['ret_norm']).astype(rv.dtype)
    return jax.nn.silu(rg) * y.reshape(B, T, -1), s_f, s_b


def mla_keys(ckv, krope, w_ukv):
    kv = jnp.einsum('btr,rhe->bthe', ckv, w_ukv.reshape(MLA_KV_RANK, MLA_HEADS, MLA_NOPE + MLA_V))
    k_nope, v = kv[..., :MLA_NOPE], kv[..., MLA_NOPE:]
    k_rope = jnp.broadcast_to(krope[:, :, None, :], k_nope.shape[:-1] + (MLA_ROPE,))
    return jnp.concatenate([k_nope, k_rope], axis=-1), v


def diff_branch(dq, dk, dv, prm, lam_init):
    lp = prm['diff_lambda'].astype(jnp.float32)
    lam = jnp.exp(jnp.sum(lp[0] * lp[1])) - jnp.exp(jnp.sum(lp[2] * lp[3])) + lam_init
    o = diff_attend(dq, dk, dv, lam)
    o = rmsnorm(o, prm['diff_norm'].reshape(DIFF_HEADS, DIFF_V)) * (1.0 - lam_init)
    return o.reshape(o.shape[:2] + (-1,))


def merge_branches(ys, gates, prm):
    B, T, _ = gates.shape
    gs = jax.nn.sigmoid(gates.astype(jnp.float32)).astype(ys[0].dtype).reshape(B, T, N_BRANCH, D_MODEL)
    m = gs[:, :, 0] * (ys[0] @ prm['w_branch'][0])
    for i in range(1, N_BRANCH):
        m = m + gs[:, :, i] * (ys[i] @ prm['w_branch'][i])
    return m @ prm['w_o']


def conv_ffn(h, prm):
    T = h.shape[1]
    u = h @ prm['w_up']
    up = jnp.pad(u, ((0, 0), (CONV_W // 2, CONV_W // 2), (0, 0)))
    cw = prm['conv_w']
    u = up[:, 0:T] * cw[0]
    for j in range(1, CONV_W):
        u = u + up[:, j:j + T] * cw[j]
    u = u + prm['conv_b']
    a, b = jnp.split(u, 2, axis=-1)
    return (jax.nn.silu(a) * b) @ prm['w_down']


def context_mixers(h, prm, lam_init):
    B, T, _ = h.shape
    (mq, mckv, mkr, dq, dk, dv, nq, nk, nv, rq, rk, rv, rg, gates) = split_cols(h @ prm['w_in'])
    ckv = rmsnorm(mckv, prm['mla_kv_norm'])
    k_a, v_a = mla_keys(ckv, mkr, prm['w_mla_ukv'])
    y_a = attend(mq.reshape(B, T, MLA_HEADS, MLA_NOPE + MLA_ROPE), k_a, v_a).reshape(B, T, -1)
    dq = dq.reshape(B, T, DIFF_HEADS, 2, DIFF_QK)
    dk = dk.reshape(B, T, DIFF_HEADS, 2, DIFF_QK)
    dv = dv.reshape(B, T, DIFF_HEADS, DIFF_V)
    y_b = diff_branch(dq, dk, dv, prm, lam_init)
    nq, nk, nv = [t.reshape(B, T, NA_HEADS, NA_DIM) for t in (nq, nk, nv)]
    y_c = attend(nq, nk, nv).reshape(B, T, -1)
    s0 = jnp.zeros((B, RET_HEADS, RET_QK, RET_V), jnp.float32)
    y_d, s_f, s_b = retention(rq, rk, rv, rg, prm, s0, s0)
    y = merge_branches((y_a, y_b, y_c, y_d), gates, prm)
    return y, (ckv, mkr, dk, dv, nk, nv, jnp.stack([s_f, s_b], axis=1))


def latent_mixers(h, prm, lam_init, cache):
    B, T, _ = h.shape
    ckv_c, kr_c, dk_c, dv_c, nk_c, nv_c, st = cache
    (mq, mckv, mkr, dq, dk, dv, nq, nk, nv, rq, rk, rv, rg, gates) = split_cols(h @ prm['w_in'])
    ckv = rmsnorm(mckv, prm['mla_kv_norm'])
    kr = axial_rope(mkr[:, :, None, :])[:, :, 0]
    k_lat, v_lat = mla_keys(ckv, kr, prm['w_mla_ukv'])
    k_ctx, v_ctx = mla_keys(ckv_c, kr_c, prm['w_mla_ukv'])
    q = mq.reshape(B, T, MLA_HEADS, MLA_NOPE + MLA_ROPE)
    q = jnp.concatenate([q[..., :MLA_NOPE], axial_rope(q[..., MLA_NOPE:])], axis=-1)
    y_a = attend(q, jnp.concatenate([k_lat, k_ctx], axis=1),
                 jnp.concatenate([v_lat, v_ctx], axis=1)).reshape(B, T, -1)
    dq = axial_rope(dq.reshape(B, T, 2 * DIFF_HEADS, DIFF_QK)).reshape(B, T, DIFF_HEADS, 2, DIFF_QK)
    dk = axial_rope(dk.reshape(B, T, 2 * DIFF_HEADS, DIFF_QK)).reshape(B, T, DIFF_HEADS, 2, DIFF_QK)
    dv = dv.reshape(B, T, DIFF_HEADS, DIFF_V)
    y_b = diff_branch(dq, jnp.concatenate([dk, dk_c], axis=1),
                      jnp.concatenate([dv, dv_c], axis=1), prm, lam_init)
    nq, nk, nv = [t.reshape(B, T, NA_HEADS, NA_DIM) for t in (nq, nk, nv)]
    y_c = na_latent(nq, nk, nv, nk_c, nv_c, prm['na_rel_bias'])
    st = st.astype(jnp.float32)
    y_d, _, _ = retention(rq, rk, rv, rg, prm, st[:, 0], st[:, 1])
    return merge_branches((y_a, y_b, y_c, y_d), gates, prm)


def setup_inputs(seed: int = 0) -> dict:
    key = jax.random.key(seed)
    ks = iter(jax.random.split(key, 40))
    f32 = jnp.float32

    def nrm(shape, scale=1.0):
        return jax.random.normal(next(ks), shape, f32) * scale

    ret_base = jnp.log(2.0 ** (5.0 + jnp.arange(RET_HEADS, dtype=f32)) - 1.0)
    return {
        'x_prompt': nrm((BATCH, SEQ, D_MODEL)),
        'x_sample': nrm((DEC_BATCH, DEC_SEQ, D_MODEL)),
        'cache_mla_ckv': nrm((DEC_BATCH, DEPTH, PAST_LEN, MLA_KV_RANK)),
        'cache_mla_krope': nrm((DEC_BATCH, DEPTH, PAST_LEN, MLA_ROPE)),
        'cache_diff_k': nrm((DEC_BATCH, DEPTH, PAST_LEN, DIFF_HEADS, 2, DIFF_QK)),
        'cache_diff_v': nrm((DEC_BATCH, DEPTH, PAST_LEN, DIFF_HEADS, DIFF_V)),
        'cache_na_k': nrm((DEC_BATCH, DEPTH, PAST_LEN, NA_HEADS, NA_DIM)),
        'cache_na_v': nrm((DEC_BATCH, DEPTH, PAST_LEN, NA_HEADS, NA_DIM)),
        'state_ret': nrm((DEC_BATCH, DEPTH, 2, RET_HEADS, RET_QK, RET_V), 0.5),
        'c': nrm((DEC_BATCH, D_MODEL)),
        'c_ctx': nrm((D_MODEL,)),
        'w_ada': nrm((DEPTH, D_MODEL, 6 * D_MODEL), 0.5 * D_MODEL ** -0.5),
        'b_ada': nrm((DEPTH, 6 * D_MODEL), 0.02),
        'g_attn': 1.0 + nrm((DEPTH, D_MODEL), 0.02),
        'g_ffn': 1.0 + nrm((DEPTH, D_MODEL), 0.02),
        'w_in': nrm((DEPTH, D_MODEL, IN_WIDTH), D_MODEL ** -0.5),
        'mla_kv_norm': 1.0 + nrm((DEPTH, MLA_KV_RANK), 0.02),
        'w_mla_ukv': nrm((DEPTH, MLA_KV_RANK, MLA_HEADS * (MLA_NOPE + MLA_V)), MLA_KV_RANK ** -0.5),
        'diff_lambda': nrm((DEPTH, 4, DIFF_QK), 0.1),
        'diff_norm': 1.0 + nrm((DEPTH, DIFF_HEADS * DIFF_V), 0.02),
        'na_rel_bias': nrm((DEPTH, NA_HEADS, 2 * NA_ROWS - 1, 2 * NA_COLS - 1), 0.1),
        'ret_decay': ret_base + nrm((DEPTH, 2, RET_HEADS), 0.1),
        'ret_norm': 1.0 + nrm((DEPTH, RET_HEADS * RET_V), 0.02),
        'w_branch': nrm((DEPTH, N_BRANCH, BRANCH_W, D_MODEL), BRANCH_W ** -0.5),
        'w_o': nrm((DEPTH, D_MODEL, D_MODEL), D_MODEL ** -0.5),
        'w_up': nrm((DEPTH, D_MODEL, 2 * D_FF), D_MODEL ** -0.5),
        'conv_w': nrm((DEPTH, CONV_W, 2 * D_FF), CONV_W ** -0.5),
        'conv_b': nrm((DEPTH, 2 * D_FF), 0.02),
        'w_down': nrm((DEPTH, D_FF, D_MODEL), D_FF ** -0.5),
        'g_final': 1.0 + nrm((D_MODEL,), 0.02),
    }


def reference(x_prompt, x_sample, cache_mla_ckv, cache_mla_krope, cache_diff_k, cache_diff_v,
              cache_na_k, cache_na_v, state_ret, c, c_ctx, w_ada, b_ada, g_attn, g_ffn, w_in,
              mla_kv_norm, w_mla_ukv, diff_lambda, diff_norm, na_rel_bias, ret_decay, ret_norm,
              w_branch, w_o, w_up, conv_w, conv_b, w_down, g_final):
    xc, xl = x_prompt, x_sample
    ctx_tensors = []
    for l in range(DEPTH):
        prm = {'w_in': w_in[l], 'mla_kv_norm': mla_kv_norm[l], 'w_mla_ukv': w_mla_ukv[l],
               'diff_lambda': diff_lambda[l], 'diff_norm': diff_norm[l], 'na_rel_bias': na_rel_bias[l],
               'ret_decay': ret_decay[l], 'ret_norm': ret_norm[l], 'w_branch': w_branch[l], 'w_o': w_o[l],
               'w_up': w_up[l], 'conv_w': conv_w[l], 'conv_b': conv_b[l], 'w_down': w_down[l]}
        lam_init = 0.8 - 0.6 * math.exp(-0.3 * l)
        sh1, sc1, ga1, sh2, sc2, ga2 = ada_mod(c_ctx, w_ada[l], b_ada[l])
        y, ctx_l = context_mixers(modulate(xc, g_attn[l], sh1, sc1), prm, lam_init)
        xc = xc + ga1 * y
        xc = xc + ga2 * conv_ffn(modulate(xc, g_ffn[l], sh2, sc2), prm)
        ctx_tensors.append(ctx_l)
        cache_l = (cache_mla_ckv[:, l], cache_mla_krope[:, l], cache_diff_k[:, l], cache_diff_v[:, l],
                   cache_na_k[:, l], cache_na_v[:, l], state_ret[:, l])
        sh1, sc1, ga1, sh2, sc2, ga2 = ada_mod(c, w_ada[l], b_ada[l])
        y = latent_mixers(modulate(xl, g_attn[l], sh1, sc1), prm, lam_init, cache_l)
        xl = xl + ga1 * y
        xl = xl + ga2 * conv_ffn(modulate(xl, g_ffn[l], sh2, sc2), prm)
    y_prompt = rmsnorm(xc, g_final)
    y_sample = rmsnorm(xl, g_final)
    (state_mla_ckv, state_mla_krope, state_diff_k, state_diff_v, state_na_k, state_na_v,
     new_state_ret) = [jnp.stack([t[i] for t in ctx_tensors], axis=1) for i in range(7)]
    return (y_prompt, y_sample, state_mla_ckv, state_mla_krope, state_diff_k, state_diff_v,
            state_na_k, state_na_v, new_state_ret)
```

```python
import functools
import math

import jax
import jax.numpy as jnp
import numpy as np
from jax import lax
from jax.experimental import pallas as pl
from jax.experimental.pallas import tpu as pltpu

F32 = jnp.float32
BF16 = jnp.bfloat16

D_MODEL = 2048
GRID_W = 64
ROPE_BASE = 10000.0
EPS = 1e-6
NEG_INF = -1e30
N_BRANCH = 4
BRANCH_W = 512
MLA_HEADS = 8
MLA_NOPE = 64
MLA_ROPE = 32
MLA_V = 64
MLA_KV_RANK = 256
DIFF_QK = 64
DIFF_V = 128
DIFF_HEADS = 4
NA_DIM = 64
NA_HEADS = 8
NA_ROWS = 8
NA_COLS = 16
RET_V = 128
RET_HEADS = 4
RET_QK = 64
D_FF = 5632
CONV_W = 3

LANE = 128
HALF = 64

C_QN = 0
C_QR = 512
C_CKV = 768
C_DQ = 1024
C_DK = 1536
C_DV = 2048
C_NQ = 2560
C_NK = 3072
C_NV = 3584
C_RQ = 4096
C_RK = 4352
C_RV = 4608
C_RG = 5120
C_KR = 5632
C_GATES = 6144
P_WIDTH = C_GATES + N_BRANCH * D_MODEL

VMEM_LIMIT = 56 * 1024 * 1024


def _cparams(sem):
    return pltpu.CompilerParams(dimension_semantics=sem, vmem_limit_bytes=VMEM_LIMIT)


def _dot(a, b):
    return jnp.dot(a, b, preferred_element_type=F32)


def _dot_t(a, b):
    return lax.dot_general(a, b, (((1,), (1,)), ((), ())), preferred_element_type=F32)


def _sigmoid(x):
    return 0.5 * jnp.tanh(0.5 * x) + 0.5


def _silu(x):
    return x * _sigmoid(x)


def _lane_iota(shape):
    return lax.broadcasted_iota(jnp.int32, shape, len(shape) - 1)


def _half_mask(x, e):
    lane = _lane_iota(x.shape)
    keep = (lane < HALF) if e == 0 else (lane >= HALF)
    return jnp.where(keep, x, 0.0)


def _rope(x, cos, sin, dr):
    w = x.shape[-1]
    nf = dr // 4
    lane = _lane_iota(x.shape)
    first = (lane % (2 * nf)) < nf
    rot = jnp.where(first, pltpu.roll(x, w - nf, 1), pltpu.roll(x, nf, 1))
    return x * cos + rot * sin


def _ada_kernel(c_ref, w_ref, b_ref, o_ref):
    a = _silu(c_ref[...]).astype(BF16)
    o_ref[...] = _dot(a, w_ref[...].astype(BF16)) + b_ref[...]


def _ada_call(cond, w_ada, b_ada):
    depth, d, n = w_ada.shape
    rows = cond.shape[0]
    tn = 1024
    return pl.pallas_call(
        _ada_kernel,
        grid=(depth, n // tn),
        in_specs=[
            pl.BlockSpec((rows, d), lambda l, j: (0, 0)),
            pl.BlockSpec((None, d, tn), lambda l, j: (l, 0, j)),
            pl.BlockSpec((None, 1, tn), lambda l, j: (l, 0, j)),
        ],
        out_specs=pl.BlockSpec((None, rows, tn), lambda l, j: (l, 0, j)),
        out_shape=jax.ShapeDtypeStruct((depth, rows, n), F32),
        compiler_params=_cparams(("parallel", "parallel")),
    )(cond, w_ada, b_ada.reshape(depth, 1, n))


NORM_ROWS = 128


def _mm_kernel(*refs, norm, resid):
    refs = list(refs)
    a_ref = refs.pop(0)
    if norm:
        g_ref, sh_ref, sc_ref = refs.pop(0), refs.pop(0), refs.pop(0)
    w_ref = refs.pop(0)
    if resid:
        r_ref, ga_ref = refs.pop(0), refs.pop(0)
    o_ref = refs.pop(0)

    if norm:
        h_ref = refs.pop(0)

        @pl.when(pl.program_id(1) == 0)
        def _():
            g = g_ref[...]
            mul = 1.0 + sc_ref[...]
            add = sh_ref[...]

            def body(c, carry):
                r0 = pl.multiple_of(c * NORM_ROWS, NORM_ROWS)
                x = a_ref[pl.ds(r0, NORM_ROWS), :]
                y = x * lax.rsqrt(jnp.mean(x * x, axis=-1, keepdims=True) + EPS) * g
                h_ref[pl.ds(r0, NORM_ROWS), :] = (y * mul + add).astype(BF16)
                return carry

            lax.fori_loop(0, a_ref.shape[0] // NORM_ROWS, body, 0)

        a = h_ref[...]
    else:
        a = a_ref[...]
    acc = _dot(a, w_ref[...])
    if resid:
        acc = r_ref[...] + ga_ref[...] * acc
    o_ref[...] = acc.astype(o_ref.dtype)


def _mm_call(a, w, *, out_dtype, tm, tn, rows_per_mod, norm=None, resid=None):
    m, k = a.shape
    n = w.shape[1]
    assert m % tm == 0 and n % tn == 0 and rows_per_mod % tm == 0
    tiles_per_mod = rows_per_mod // tm
    in_specs = [pl.BlockSpec((tm, k), lambda i, j: (i, 0))]
    args = [a]
    if norm is not None:
        g, sh, sc = norm
        in_specs += [
            pl.BlockSpec((1, k), lambda i, j: (0, 0)),
            pl.BlockSpec((None, 1, k), lambda i, j: (i // tiles_per_mod, 0, 0)),
            pl.BlockSpec((None, 1, k), lambda i, j: (i // tiles_per_mod, 0, 0)),
        ]
        args += [g, sh, sc]
    in_specs.append(pl.BlockSpec((k, tn), lambda i, j: (0, j)))
    args.append(w)
    if resid is not None:
        x, ga = resid
        in_specs += [
            pl.BlockSpec((tm, tn), lambda i, j: (i, j)),
            pl.BlockSpec((None, 1, tn), lambda i, j: (i // tiles_per_mod, 0, j)),
        ]
        args += [x, ga]
    scratch = [pltpu.VMEM((tm, k), BF16)] if norm is not None else []
    return pl.pallas_call(
        functools.partial(_mm_kernel, norm=norm is not None, resid=resid is not None),
        grid=(m // tm, n // tn),
        in_specs=in_specs,
        out_specs=pl.BlockSpec((tm, tn), lambda i, j: (i, j)),
        out_shape=jax.ShapeDtypeStruct((m, n), out_dtype),
        scratch_shapes=scratch,
        compiler_params=_cparams(("parallel", "arbitrary")),
    )(*args)


def _rope_kernel(x_ref, c_ref, s_ref, o_ref, *, dr):
    o_ref[...] = _rope(x_ref[...].astype(F32), c_ref[...], s_ref[...], dr).astype(o_ref.dtype)


def _rope_call(p, col, width, cos, sin, dr, seq):
    m = p.shape[0]
    tm = 256
    tpb = seq // tm
    cb = col // width
    assert col % width == 0
    return pl.pallas_call(
        functools.partial(_rope_kernel, dr=dr),
        grid=(m // tm,),
        in_specs=[
            pl.BlockSpec((tm, width), lambda i: (i, cb)),
            pl.BlockSpec((tm, width), lambda i: (i % tpb, 0)),
            pl.BlockSpec((tm, width), lambda i: (i % tpb, 0)),
        ],
        out_specs=pl.BlockSpec((tm, width), lambda i: (i, 0)),
        out_shape=jax.ShapeDtypeStruct((m, width), BF16),
        compiler_params=_cparams(("parallel",)),
    )(p, cos, sin)


MLA_K_W = 4 * 256
MLA_V_W = 4 * 128


def _mla_kv_kernel(*refs, norm, rope):
    refs = list(refs)
    c_ref, kr_ref = refs.pop(0), refs.pop(0)
    if rope:
        cos_ref, sin_ref = refs.pop(0), refs.pop(0)
    if norm:
        g_ref = refs.pop(0)
    w_ref = refs.pop(0)
    k_ref, v_ref = refs.pop(0), refs.pop(0)
    c = c_ref[...].astype(F32)
    if norm:
        ckv_ref = refs.pop(0)
        c = c * lax.rsqrt(jnp.mean(c * c, axis=-1, keepdims=True) + EPS) * g_ref[...]
        ckv_ref[...] = c
    kv = _dot(c.astype(BF16), w_ref[...])
    kr = kr_ref[...].astype(F32)
    if rope:
        kr = _rope(kr, cos_ref[...], sin_ref[...], MLA_ROPE)
    k_ref[...] = (kv[:, :MLA_K_W] + jnp.concatenate([kr] * 4, axis=1)).astype(BF16)
    v_ref[...] = kv[:, MLA_K_W:].astype(BF16)


def _mla_kv_call(ckv_src, ckv_col, kr_src, kr_col, w_e, *, g=None, rope=None, seq=None):
    m = ckv_src.shape[0]
    tm = 256
    assert m % tm == 0 and ckv_col % 256 == 0 and kr_col % 256 == 0
    in_specs = [
        pl.BlockSpec((tm, 256), lambda i: (i, ckv_col // 256)),
        pl.BlockSpec((tm, 256), lambda i: (i, kr_col // 256)),
    ]
    args = [ckv_src, kr_src]
    if rope is not None:
        tpb = seq // tm
        in_specs += [pl.BlockSpec((tm, 256), lambda i: (i % tpb, 0))] * 2
        args += list(rope)
    if g is not None:
        in_specs.append(pl.BlockSpec((1, 256), lambda i: (0, 0)))
        args.append(g)
    in_specs.append(pl.BlockSpec(w_e.shape, lambda i: (0, 0)))
    args.append(w_e)
    out_specs = [pl.BlockSpec((tm, MLA_K_W), lambda i: (i, 0)), pl.BlockSpec((tm, MLA_V_W), lambda i: (i, 0))]
    out_shape = [jax.ShapeDtypeStruct((m, MLA_K_W), BF16), jax.ShapeDtypeStruct((m, MLA_V_W), BF16)]
    if g is not None:
        out_specs.append(pl.BlockSpec((tm, 256), lambda i: (i, 0)))
        out_shape.append(jax.ShapeDtypeStruct((m, 256), F32))
    return pl.pallas_call(
        functools.partial(_mla_kv_kernel, norm=g is not None, rope=rope is not None),
        grid=(m // tm,),
        in_specs=in_specs,
        out_specs=out_specs,
        out_shape=out_shape,
        compiler_params=_cparams(("parallel",)),
    )(*args)


def _softmax_parts(scores):
    m = scores[0].max(axis=-1, keepdims=True)
    for s in scores[1:]:
        m = jnp.maximum(m, s.max(axis=-1, keepdims=True))
    ps = [jnp.exp(s - m) for s in scores]
    l = ps[0].sum(axis=-1, keepdims=True)
    for p in ps[1:]:
        l = l + p.sum(axis=-1, keepdims=True)
    return ps, 1.0 / l


def _mha_kernel(*refs, mla, nseg, scale):
    refs = list(refs)
    qn_ref = refs.pop(0)
    qr_ref = refs.pop(0) if mla else None
    kv_refs = [(refs.pop(0), refs.pop(0)) for _ in range(nseg)]
    o_ref = refs.pop(0)
    kw = 256 if mla else LANE
    lane = _lane_iota((qn_ref.shape[0], LANE))
    for j in range(4):
        outs = []
        qn_pair = qn_ref[:, j * LANE:(j + 1) * LANE].astype(F32)
        for e in range(2):
            h = 2 * j + e
            q = _half_mask(qn_pair, e)
            if mla:
                qr = qr_ref[:, (h // 4) * LANE:(h // 4 + 1) * LANE].astype(F32)
                qr = jnp.where(lane // MLA_ROPE == h % 4, qr, 0.0)
                q = jnp.concatenate([q, qr], axis=1)
            q = q.astype(BF16)
            scores = [_dot_t(q, k_ref[:, j * kw:(j + 1) * kw].astype(BF16)) * scale for k_ref, _ in kv_refs]
            ps, inv = _softmax_parts(scores)
            o = _dot(ps[0].astype(BF16), kv_refs[0][1][:, j * LANE:(j + 1) * LANE].astype(BF16))
            for p, (_, v_ref) in zip(ps[1:], kv_refs[1:]):
                o = o + _dot(p.astype(BF16), v_ref[:, j * LANE:(j + 1) * LANE].astype(BF16))
            outs.append(o * inv)
        o_ref[:, j * LANE:(j + 1) * LANE] = jnp.where(lane < HALF, outs[0], outs[1]).astype(o_ref.dtype)


def _mha_call(qn, qn_col, qr, qr_col, segs, *, nb, tq, t, scale):
    mla = qr is not None
    kw = MLA_K_W if mla else 512
    nq = t // tq
    in_specs = [pl.BlockSpec((tq, 512), lambda b, i: (b * nq + i, qn_col // 512))]
    args = [qn]
    if mla:
        in_specs.append(pl.BlockSpec((tq, 256), lambda b, i: (b * nq + i, qr_col // 256)))
        args.append(qr)
    for k_arr, k_col, v_arr, v_col, s in segs:
        assert k_col % kw == 0 and v_col % 512 == 0
        in_specs.append(pl.BlockSpec((s, kw), functools.partial(lambda b, i, c: (b, c), c=k_col // kw)))
        in_specs.append(pl.BlockSpec((s, 512), functools.partial(lambda b, i, c: (b, c), c=v_col // 512)))
        args += [k_arr, v_arr]
    return pl.pallas_call(
        functools.partial(_mha_kernel, mla=mla, nseg=len(segs), scale=scale),
        grid=(nb, nq),
        in_specs=in_specs,
        out_specs=pl.BlockSpec((tq, 512), lambda b, i: (b * nq + i, 0)),
        out_shape=jax.ShapeDtypeStruct((nb * t, 512), BF16),
        compiler_params=_cparams(("parallel", "parallel")),
    )(*args)


def _diff_kernel(*refs, nseg, lam_init):
    refs = list(refs)
    q_ref = refs.pop(0)
    kv_refs = [(refs.pop(0), refs.pop(0)) for _ in range(nseg)]
    lp_ref, g_ref, o_ref = refs
    lp = lp_ref[...]
    lam = (jnp.exp(jnp.sum(lp[0:1] * lp[1:2], axis=-1, keepdims=True))
           - jnp.exp(jnp.sum(lp[2:3] * lp[3:4], axis=-1, keepdims=True)) + lam_init)
    scale = DIFF_QK ** -0.5
    for h in range(DIFF_HEADS):
        q_pair = q_ref[:, h * LANE:(h + 1) * LANE].astype(F32)
        probs = []
        for c in range(2):
            q = _half_mask(q_pair, c).astype(BF16)
            scores = [_dot_t(q, k_ref[:, h * LANE:(h + 1) * LANE].astype(BF16)) * scale for k_ref, _ in kv_refs]
            ps, inv = _softmax_parts(scores)
            probs.append([p * inv for p in ps])
        o = None
        for s in range(nseg):
            a = (probs[0][s] - lam * probs[1][s]).astype(BF16)
            t = _dot(a, kv_refs[s][1][:, h * LANE:(h + 1) * LANE].astype(BF16))
            o = t if o is None else o + t
        o = o * lax.rsqrt(jnp.mean(o * o, axis=-1, keepdims=True) + EPS) * g_ref[:, h * LANE:(h + 1) * LANE]
        o_ref[:, h * LANE:(h + 1) * LANE] = (o * (1.0 - lam_init)).astype(o_ref.dtype)


def _diff_call(q, q_col, segs, lp, g, *, nb, tq, t, lam_init):
    nq = t // tq
    in_specs = [pl.BlockSpec((tq, 512), lambda b, i: (b * nq + i, q_col // 512))]
    args = [q]
    for k_arr, k_col, v_arr, v_col, s in segs:
        in_specs.append(pl.BlockSpec((s, 512), functools.partial(lambda b, i, c: (b, c), c=k_col // 512)))
        in_specs.append(pl.BlockSpec((s, 512), functools.partial(lambda b, i, c: (b, c), c=v_col // 512)))
        args += [k_arr, v_arr]
    in_specs += [pl.BlockSpec((4, DIFF_QK), lambda b, i: (0, 0)), pl.BlockSpec((1, 512), lambda b, i: (0, 0))]
    args += [lp, g]
    return pl.pallas_call(
        functools.partial(_diff_kernel, nseg=len(segs), lam_init=lam_init),
        grid=(nb, nq),
        in_specs=in_specs,
        out_specs=pl.BlockSpec((tq, 512), lambda b, i: (b * nq + i, 0)),
        out_shape=jax.ShapeDtypeStruct((nb * t, 512), BF16),
        compiler_params=_cparams(("parallel", "parallel")),
    )(*args)


def _na_kernel(q_ref, k_ref, v_ref, kc_ref, vc_ref, b_ref, o_ref, *, rows):
    r = pl.program_id(1)
    s = jnp.clip(r - NA_ROWS // 2, 0, rows - NA_ROWS)
    start = pl.multiple_of(s * GRID_W, GRID_W)
    n_loc = NA_ROWS * GRID_W
    scale = NA_DIM ** -0.5
    qc = lax.broadcasted_iota(jnp.int32, (GRID_W, n_loc), 0)
    kc = lax.broadcasted_iota(jnp.int32, (GRID_W, n_loc), 1) % GRID_W
    c0 = jnp.clip(qc - NA_COLS // 2, 0, GRID_W - NA_COLS)
    ok = (kc >= c0) & (kc < c0 + NA_COLS)
    lane = _lane_iota((GRID_W, LANE))
    for j in range(4):
        q_pair = q_ref[:, j * LANE:(j + 1) * LANE].astype(F32)
        k_loc = k_ref[pl.ds(start, n_loc), j * LANE:(j + 1) * LANE].astype(BF16)
        v_loc = v_ref[pl.ds(start, n_loc), j * LANE:(j + 1) * LANE].astype(BF16)
        k_ctx = kc_ref[:, j * LANE:(j + 1) * LANE].astype(BF16)
        v_ctx = vc_ref[:, j * LANE:(j + 1) * LANE].astype(BF16)
        outs = []
        for e in range(2):
            q = _half_mask(q_pair, e).astype(BF16)
            s_loc = _dot_t(q, k_loc) * scale + b_ref[2 * j + e]
            s_loc = jnp.where(ok, s_loc, NEG_INF)
            s_ctx = _dot_t(q, k_ctx) * scale
            ps, inv = _softmax_parts([s_loc, s_ctx])
            o = _dot(ps[0].astype(BF16), v_loc) + _dot(ps[1].astype(BF16), v_ctx)
            outs.append(o * inv)
        o_ref[:, j * LANE:(j + 1) * LANE] = jnp.where(lane < HALF, outs[0], outs[1]).astype(o_ref.dtype)


def _na_call(p, kc, vc, bias_tab, *, nb, t, past):
    rows = t // GRID_W
    assert rows >= NA_ROWS
    n_loc = NA_ROWS * GRID_W

    def bias_idx(b, r):
        return (r - jnp.clip(r - NA_ROWS // 2, 0, rows - NA_ROWS), 0, 0, 0)

    return pl.pallas_call(
        functools.partial(_na_kernel, rows=rows),
        grid=(nb, rows),
        in_specs=[
            pl.BlockSpec((GRID_W, 512), lambda b, r: (b * rows + r, C_NQ // 512)),
            pl.BlockSpec((t, 512), lambda b, r: (b, C_NK // 512)),
            pl.BlockSpec((t, 512), lambda b, r: (b, C_NV // 512)),
            pl.BlockSpec((past, 512), lambda b, r: (b, 0)),
            pl.BlockSpec((past, 512), lambda b, r: (b, 0)),
            pl.BlockSpec((None, NA_HEADS, GRID_W, n_loc), bias_idx),
        ],
        out_specs=pl.BlockSpec((GRID_W, 512), lambda b, r: (b * rows + r, 0)),
        out_shape=jax.ShapeDtypeStruct((nb * t, 512), BF16),
        compiler_params=_cparams(("parallel", "parallel")),
    )(p, p, p, kc, vc, bias_tab)


def _log_sigmoid(x):
    return jnp.minimum(x, 0.0) - jnp.log(1.0 + jnp.exp(-jnp.abs(x)))


def _ret_kernel(*refs, has_state, t):
    refs = list(refs)
    q_ref, k_ref, v_ref, rg_ref, dec_ref, gn_ref = [refs.pop(0) for _ in range(6)]
    s0_ref = refs.pop(0) if has_state else None
    o_ref = refs.pop(0)
    tq = q_ref.shape[0]
    q0 = pl.program_id(1) * tq
    lg = _log_sigmoid(dec_ref[...])
    ti = (q0 + lax.broadcasted_iota(jnp.int32, (tq, t), 0)).astype(F32)
    ui = lax.broadcasted_iota(jnp.int32, (tq, t), 1).astype(F32)
    diff = ti - ui
    tcol = (q0 + lax.broadcasted_iota(jnp.int32, (tq, 1), 0)).astype(F32)
    for h in range(RET_HEADS):
        j, e = h // 2, h % 2
        lgf = lg[h:h + 1, 0:1]
        lgb = lg[RET_HEADS + h:RET_HEADS + h + 1, 0:1]
        q = _half_mask(q_ref[:, j * LANE:(j + 1) * LANE].astype(F32), e).astype(BF16)
        a = _dot_t(q, k_ref[:, j * LANE:(j + 1) * LANE].astype(BF16)) * (RET_QK ** -0.5)
        dmat = jnp.exp(jnp.where(diff >= 0, diff * lgf, -diff * lgb))
        y = _dot((a * dmat).astype(BF16), v_ref[:, h * LANE:(h + 1) * LANE].astype(BF16))
        if has_state:
            cf = _dot(q, s0_ref[0, j].astype(BF16))
            cb = _dot(q, s0_ref[1, j].astype(BF16))
            y = y + cf * jnp.exp((tcol + 1.0) * lgf) + cb * jnp.exp((float(t) - tcol) * lgb)
        mu = jnp.mean(y, axis=-1, keepdims=True)
        yc = y - mu
        var = jnp.mean(yc * yc, axis=-1, keepdims=True)
        yn = yc * lax.rsqrt(var + EPS) * gn_ref[:, h * LANE:(h + 1) * LANE]
        o_ref[:, h * LANE:(h + 1) * LANE] = (_silu(rg_ref[:, h * LANE:(h + 1) * LANE].astype(F32)) * yn).astype(o_ref.dtype)


def _ret_call(p, dec, gn, s0, *, nb, tq, t):
    nq = t // tq
    in_specs = [
        pl.BlockSpec((tq, 256), lambda b, i: (b * nq + i, C_RQ // 256)),
        pl.BlockSpec((t, 256), lambda b, i: (b, C_RK // 256)),
        pl.BlockSpec((t, 512), lambda b, i: (b, C_RV // 512)),
        pl.BlockSpec((tq, 512), lambda b, i: (b * nq + i, C_RG // 512)),
        pl.BlockSpec((8, LANE), lambda b, i: (0, 0)),
        pl.BlockSpec((1, 512), lambda b, i: (0, 0)),
    ]
    args = [p, p, p, p, dec, gn]
    if s0 is not None:
        in_specs.append(pl.BlockSpec((None, 2, 2, LANE, LANE), lambda b, i: (b, 0, 0, 0, 0)))
        args.append(s0)
    return pl.pallas_call(
        functools.partial(_ret_kernel, has_state=s0 is not None, t=t),
        grid=(nb, nq),
        in_specs=in_specs,
        out_specs=pl.BlockSpec((tq, 512), lambda b, i: (b * nq + i, 0)),
        out_shape=jax.ShapeDtypeStruct((nb * t, 512), BF16),
        compiler_params=_cparams(("parallel", "parallel")),
    )(*args)


def _ret_state_kernel(k_ref, v_ref, dec_ref, o_ref, *, t):
    lg = _log_sigmoid(dec_ref[...])
    tcol = lax.broadcasted_iota(jnp.int32, (t, 1), 0).astype(F32)
    lane = _lane_iota((t, LANE))
    for j in range(2):
        k_pair = k_ref[:, j * LANE:(j + 1) * LANE].astype(F32) * (RET_QK ** -0.5)
        for d in range(2):
            expo = (float(t) - 1.0 - tcol) if d == 0 else tcol
            w0 = jnp.exp(expo * lg[d * RET_HEADS + 2 * j:d * RET_HEADS + 2 * j + 1, 0:1])
            w1 = jnp.exp(expo * lg[d * RET_HEADS + 2 * j + 1:d * RET_HEADS + 2 * j + 2, 0:1])
            kd = (k_pair * jnp.where(lane < HALF, w0, w1)).T.astype(BF16)
            for e in range(2):
                h = 2 * j + e
                sfull = _dot(kd, v_ref[:, h * LANE:(h + 1) * LANE].astype(BF16))
                o_ref[d, h] = sfull[e * HALF:(e + 1) * HALF, :]


def _ret_state_call(p, dec, *, nb, t):
    return pl.pallas_call(
        functools.partial(_ret_state_kernel, t=t),
        grid=(nb,),
        in_specs=[
            pl.BlockSpec((t, 256), lambda b: (b, C_RK // 256)),
            pl.BlockSpec((t, 512), lambda b: (b, C_RV // 512)),
            pl.BlockSpec((8, LANE), lambda b: (0, 0)),
        ],
        out_specs=pl.BlockSpec((None, 2, RET_HEADS, RET_QK, RET_V), lambda b: (b, 0, 0, 0, 0)),
        out_shape=jax.ShapeDtypeStruct((nb, 2, RET_HEADS, RET_QK, RET_V), F32),
        compiler_params=_cparams(("parallel",)),
    )(p, p, dec)


def _merge_kernel(ya_ref, yb_ref, yc_ref, yd_ref, g0_ref, g1_ref, g2_ref, g3_ref, w_ref, o_ref):
    ys = (ya_ref, yb_ref, yc_ref, yd_ref)
    gs = (g0_ref, g1_ref, g2_ref, g3_ref)
    m = None
    for i in range(N_BRANCH):
        t = _sigmoid(gs[i][...].astype(F32)) * _dot(ys[i][...], w_ref[i])
        m = t if m is None else m + t
    o_ref[...] = m.astype(o_ref.dtype)


def _merge_call(ys, p, w_branch):
    m = p.shape[0]
    tm, tn = 512, 512
    gate_specs = [
        pl.BlockSpec((tm, tn), functools.partial(lambda r, c, i: (r, (C_GATES + i * D_MODEL) // tn + c), i=i))
        for i in range(N_BRANCH)
    ]
    return pl.pallas_call(
        _merge_kernel,
        grid=(m // tm, D_MODEL // tn),
        in_specs=[pl.BlockSpec((tm, BRANCH_W), lambda r, c: (r, 0))] * N_BRANCH + gate_specs
        + [pl.BlockSpec((N_BRANCH, BRANCH_W, tn), lambda r, c: (0, 0, c))],
        out_specs=pl.BlockSpec((tm, tn), lambda r, c: (r, c)),
        out_shape=jax.ShapeDtypeStruct((m, D_MODEL), BF16),
        compiler_params=_cparams(("parallel", "parallel")),
    )(*ys, p, p, p, p, w_branch)


HALO = 16


def _conv_kernel(ua_ref, uap_ref, uan_ref, ub_ref, ubp_ref, ubn_ref, wa_ref, wb_ref, ba_ref, bb_ref, o_ref, *, tiles_per_seq):
    i = pl.program_id(0)
    tm = ua_ref.shape[0]
    has_prev = ((i % tiles_per_seq) != 0).astype(F32)
    has_next = ((i % tiles_per_seq) != tiles_per_seq - 1).astype(F32)
    row = lax.broadcasted_iota(jnp.int32, ua_ref.shape, 0)

    def conv(u_ref, up_ref, un_ref, w_ref, b_ref):
        u = u_ref[...].astype(F32)
        prev = up_ref[HALO - 1:HALO, :].astype(F32) * has_prev
        nxt = un_ref[0:1, :].astype(F32) * has_next
        um1 = jnp.where(row == 0, prev, pltpu.roll(u, 1, 0))
        up1 = jnp.where(row == tm - 1, nxt, pltpu.roll(u, tm - 1, 0))
        return um1 * w_ref[0:1, :] + u * w_ref[1:2, :] + up1 * w_ref[2:3, :] + b_ref[...]

    a = conv(ua_ref, uap_ref, uan_ref, wa_ref, ba_ref)
    b = conv(ub_ref, ubp_ref, ubn_ref, wb_ref, bb_ref)
    o_ref[...] = (_silu(a) * b).astype(o_ref.dtype)


def _conv_call(u, conv_w, conv_b, *, seq):
    m = u.shape[0]
    tm, tc = 256, 512
    assert seq % tm == 0 and D_FF % tc == 0
    nb_half = D_FF // tc
    last_halo = m // HALO - 1
    r = tm // HALO

    def main(off):
        return pl.BlockSpec((tm, tc), lambda i, c: (i, c + off))

    def prev(off):
        return pl.BlockSpec((HALO, tc), lambda i, c: (jnp.maximum(i * r - 1, 0), c + off))

    def nxt(off):
        return pl.BlockSpec((HALO, tc), lambda i, c: (jnp.minimum((i + 1) * r, last_halo), c + off))

    def vec(rows, off):
        return pl.BlockSpec((rows, tc), lambda i, c: (0, c + off))

    cb = conv_b.reshape(1, 2 * D_FF)
    return pl.pallas_call(
        functools.partial(_conv_kernel, tiles_per_seq=seq // tm),
        grid=(m // tm, nb_half),
        in_specs=[main(0), prev(0), nxt(0), main(nb_half), prev(nb_half), nxt(nb_half),
                  vec(CONV_W, 0), vec(CONV_W, nb_half), vec(1, 0), vec(1, nb_half)],
        out_specs=pl.BlockSpec((tm, tc), lambda i, c: (i, c)),
        out_shape=jax.ShapeDtypeStruct((m, D_FF), BF16),
        compiler_params=_cparams(("parallel", "parallel")),
    )(u, u, u, u, u, u, conv_w, conv_w, cb, cb)


def _final_norm_kernel(x_ref, g_ref, o_ref):
    x = x_ref[...]
    o_ref[...] = x * lax.rsqrt(jnp.mean(x * x, axis=-1, keepdims=True) + EPS) * g_ref[...]


def _final_norm_call(x, g):
    m, d = x.shape
    tm = 256
    return pl.pallas_call(
        _final_norm_kernel,
        grid=(m // tm,),
        in_specs=[pl.BlockSpec((tm, d), lambda i: (i, 0)), pl.BlockSpec((1, d), lambda i: (0, 0))],
        out_specs=pl.BlockSpec((tm, d), lambda i: (i, 0)),
        out_shape=jax.ShapeDtypeStruct((m, d), F32),
        compiler_params=_cparams(("parallel",)),
    )(x, g.reshape(1, d))


def _reorder_w_in(w):
    d = w.shape[0]
    mq = w[:, :768].reshape(d, MLA_HEADS, MLA_NOPE + MLA_ROPE)
    qn = mq[:, :, :MLA_NOPE].reshape(d, 512)
    qr = mq[:, :, MLA_NOPE:].reshape(d, 256)
    ckv = w[:, 768:1024]
    mkr = w[:, 1024:1056]
    rest = w[:, 1056:5664]
    gates = w[:, 5664:]
    krblk = jnp.concatenate([jnp.zeros((d, LANE), w.dtype)] + [mkr] * 4, axis=1)
    pad = jnp.zeros((d, C_GATES - C_KR - 256), w.dtype)
    return jnp.concatenate([qn, qr, ckv, rest, krblk, pad, gates], axis=1).astype(BF16)


def _reorder_w_ukv(w):
    r = w.shape[0]
    w3 = w.reshape(r, MLA_HEADS, MLA_NOPE + MLA_V)
    kn = w3[:, :, :MLA_NOPE].reshape(r, 4, LANE)
    vv = w3[:, :, MLA_NOPE:].reshape(r, 512)
    knz = jnp.concatenate([kn, jnp.zeros_like(kn)], axis=2).reshape(r, MLA_K_W)
    return jnp.concatenate([knz, vv], axis=1).astype(BF16)


def _rope_tables(t, dr, reps, lead_zero=0):
    nf = dr // 4
    pos = jnp.arange(t)
    pos = jnp.stack([pos // GRID_W, pos % GRID_W], axis=-1).astype(F32)
    inv = ROPE_BASE ** (-jnp.arange(nf, dtype=F32) / nf)
    ang = pos[:, :, None] * inv
    cos = jnp.cos(ang)
    sin = jnp.sin(ang)
    c = jnp.tile(jnp.concatenate([cos, cos], axis=2).reshape(t, dr), (1, reps))
    s = jnp.tile(jnp.concatenate([-sin, sin], axis=2).reshape(t, dr), (1, reps))
    if lead_zero:
        z = jnp.zeros((t, lead_zero), F32)
        c = jnp.concatenate([z, c], axis=1)
        s = jnp.concatenate([z, s], axis=1)
    return c, s


def _na_bias_table(rel_bias):
    d = np.arange(NA_ROWS)[:, None, None, None]
    kr = np.arange(NA_ROWS)[None, None, :, None]
    qc = np.arange(GRID_W)[None, :, None, None]
    kc = np.arange(GRID_W)[None, None, None, :]
    row_idx = np.broadcast_to(kr - d + NA_ROWS - 1, (NA_ROWS, GRID_W, NA_ROWS, GRID_W))
    col_idx = np.broadcast_to(np.clip(kc - qc, 1 - NA_COLS, NA_COLS - 1) + NA_COLS - 1, row_idx.shape)
    tab = rel_bias[:, row_idx, col_idx].astype(F32)
    return jnp.moveaxis(tab, 0, 1).reshape(NA_ROWS, NA_HEADS, GRID_W, NA_ROWS * GRID_W)


def kernel(x_prompt, x_sample, cache_mla_ckv, cache_mla_krope, cache_diff_k, cache_diff_v, cache_na_k, cache_na_v, state_ret, c, c_ctx, w_ada, b_ada, g_attn, g_ffn, w_in, mla_kv_norm, w_mla_ukv, diff_lambda, diff_norm, na_rel_bias, ret_decay, ret_norm, w_branch, w_o, w_up, conv_w, conv_b, w_down, g_final):
    nbc, seq, d = x_prompt.shape
    nbl, t, _ = x_sample.shape
    past = cache_mla_ckv.shape[2]
    depth = w_in.shape[0]
    mc, ml = nbc * seq, nbl * t

    xc = x_prompt.reshape(mc, d)
    xl = x_sample.reshape(ml, d)

    mod_rows = 16
    cond = jnp.concatenate([c_ctx[None], c, jnp.zeros((mod_rows - 1 - nbl, d), F32)], axis=0)
    mods = _ada_call(cond, w_ada, b_ada).reshape(depth, mod_rows, 6, d)

    cos32q, sin32q = _rope_tables(t, MLA_ROPE, 8)
    cos32k, sin32k = _rope_tables(t, MLA_ROPE, 4, lead_zero=LANE)
    cos64, sin64 = _rope_tables(t, DIFF_QK, 16)

    tm_c = min(1024, mc)
    tm_l = min(1024, t)
    states = []
    for l in range(depth):
        lam_init = 0.8 - 0.6 * math.exp(-0.3 * l)
        w_in_r = _reorder_w_in(w_in[l])
        w_ukv_e = _reorder_w_ukv(w_mla_ukv[l])
        wb = w_branch[l].astype(BF16)
        wo = w_o[l].astype(BF16)
        wup = w_up[l].astype(BF16)
        wdn = w_down[l].astype(BF16)
        kvn = mla_kv_norm[l].reshape(1, MLA_KV_RANK)
        dec = jnp.broadcast_to(ret_decay[l].reshape(2 * RET_HEADS, 1), (2 * RET_HEADS, LANE))
        gn = ret_norm[l].reshape(1, 512)
        dnorm = diff_norm[l].reshape(1, 512)
        g_a = g_attn[l].reshape(1, d)
        g_f = g_ffn[l].reshape(1, d)

        def mod(i):
            return mods[l, 0:1, i][:, None, :], mods[l, 1:1 + nbl, i][:, None, :]

        sh1, sc1, ga1, sh2, sc2, ga2 = [mod(i) for i in range(6)]

        pc = _mm_call(xc, w_in_r, out_dtype=F32, tm=tm_c, tn=1024, rows_per_mod=mc, norm=(g_a, sh1[0], sc1[0]))
        kc_mla, vc_mla, ckv_state = _mla_kv_call(pc, C_CKV, pc, C_KR, w_ukv_e, g=kvn)
        ya = _mha_call(pc, C_QN, pc, C_QR, [(kc_mla, 0, vc_mla, 0, seq)], nb=nbc, tq=seq, t=seq,
                       scale=(MLA_NOPE + MLA_ROPE) ** -0.5)
        yb = _diff_call(pc, C_DQ, [(pc, C_DK, pc, C_DV, seq)], diff_lambda[l], dnorm, nb=nbc, tq=seq, t=seq,
                        lam_init=lam_init)
        yc = _mha_call(pc, C_NQ, None, 0, [(pc, C_NK, pc, C_NV, seq)], nb=nbc, tq=seq, t=seq, scale=NA_DIM ** -0.5)
        yd = _ret_call(pc, dec, gn, None, nb=nbc, tq=seq, t=seq)
        sret = _ret_state_call(pc, dec, nb=nbc, t=seq)
        states.append((
            ckv_state.reshape(nbc, seq, MLA_KV_RANK),
            pc[:, C_KR + LANE:C_KR + LANE + MLA_ROPE].reshape(nbc, seq, MLA_ROPE),
            pc[:, C_DK:C_DK + 512].reshape(nbc, seq, DIFF_HEADS, 2, DIFF_QK),
            pc[:, C_DV:C_DV + 512].reshape(nbc, seq, DIFF_HEADS, DIFF_V),
            pc[:, C_NK:C_NK + 512].reshape(nbc, seq, NA_HEADS, NA_DIM),
            pc[:, C_NV:C_NV + 512].reshape(nbc, seq, NA_HEADS, NA_DIM),
            sret,
        ))
        mrg = _merge_call((ya, yb, yc, yd), pc, wb)
        xc = _mm_call(mrg, wo, out_dtype=F32, tm=tm_c, tn=1024, rows_per_mod=mc, resid=(xc, ga1[0]))
        u = _mm_call(xc, wup, out_dtype=BF16, tm=tm_c, tn=1024, rows_per_mod=mc, norm=(g_f, sh2[0], sc2[0]))
        act = _conv_call(u, conv_w[l], conv_b[l], seq=seq)
        xc = _mm_call(act, wdn, out_dtype=F32, tm=512, tn=1024, rows_per_mod=mc, resid=(xc, ga2[0]))

        plat = _mm_call(xl, w_in_r, out_dtype=BF16, tm=tm_l, tn=1024, rows_per_mod=t, norm=(g_a, sh1[1], sc1[1]))
        kl_mla, vl_mla, _ = _mla_kv_call(plat, C_CKV, plat, C_KR, w_ukv_e, g=kvn, rope=(cos32k, sin32k), seq=t)
        kr_c = cache_mla_krope[:, l].reshape(nbl * past, MLA_ROPE)
        krblk_c = jnp.concatenate([jnp.zeros((nbl * past, LANE), F32)] + [kr_c] * 4, axis=1)
        kp_mla, vp_mla = _mla_kv_call(cache_mla_ckv[:, l].reshape(nbl * past, MLA_KV_RANK), 0, krblk_c, 0, w_ukv_e)
        qr_l = _rope_call(plat, C_QR, 256, cos32q, sin32q, MLA_ROPE, t)
        dqk_l = _rope_call(plat, C_DQ, 1024, cos64, sin64, DIFF_QK, t)
        tq = 256
        ya = _mha_call(plat, C_QN, qr_l, 0, [(kl_mla, 0, vl_mla, 0, t), (kp_mla, 0, vp_mla, 0, past)],
                       nb=nbl, tq=tq, t=t, scale=(MLA_NOPE + MLA_ROPE) ** -0.5)
        dk_c = cache_diff_k[:, l].reshape(nbl * past, 512)
        dv_c = cache_diff_v[:, l].reshape(nbl * past, 512)
        yb = _diff_call(dqk_l, 0, [(dqk_l, 512, plat, C_DV, t), (dk_c, 0, dv_c, 0, past)], diff_lambda[l], dnorm,
                        nb=nbl, tq=tq, t=t, lam_init=lam_init)
        yc = _na_call(plat, cache_na_k[:, l].reshape(nbl * past, 512), cache_na_v[:, l].reshape(nbl * past, 512),
                      _na_bias_table(na_rel_bias[l]), nb=nbl, t=t, past=past)
        s0 = state_ret[:, l].astype(F32).reshape(nbl, 2, 2, LANE, LANE)
        yd = _ret_call(plat, dec, gn, s0, nb=nbl, tq=tq, t=t)
        mrg = _merge_call((ya, yb, yc, yd), plat, wb)
        xl = _mm_call(mrg, wo, out_dtype=F32, tm=tm_l, tn=1024, rows_per_mod=t, resid=(xl, ga1[1]))
        u = _mm_call(xl, wup, out_dtype=BF16, tm=tm_l, tn=1024, rows_per_mod=t, norm=(g_f, sh2[1], sc2[1]))
        act = _conv_call(u, conv_w[l], conv_b[l], seq=t)
        xl = _mm_call(act, wdn, out_dtype=F32, tm=512, tn=1024, rows_per_mod=t, resid=(xl, ga2[1]))

    y_prompt = _final_norm_call(xc, g_final).reshape(nbc, seq, d)
    y_sample = _final_norm_call(xl, g_final).reshape(nbl, t, d)
    outs = [jnp.stack([st[i] for st in states], axis=1) for i in range(7)]
    return (y_prompt, y_sample, *outs)
```

```python
import functools
import math

import jax
import jax.numpy as jnp
import numpy as np
from jax import lax
from jax.experimental import pallas as pl
from jax.experimental.pallas import tpu as pltpu

F32 = jnp.float32
BF16 = jnp.bfloat16

D_MODEL = 2048
GRID_W = 64
ROPE_BASE = 10000.0
EPS = 1e-6
NEG_INF = -1e30
N_BRANCH = 4
BRANCH_W = 512
MLA_HEADS = 8
MLA_NOPE = 64
MLA_ROPE = 32
MLA_V = 64
MLA_KV_RANK = 256
DIFF_QK = 64
DIFF_V = 128
DIFF_HEADS = 4
NA_DIM = 64
NA_HEADS = 8
NA_ROWS = 8
NA_COLS = 16
RET_V = 128
RET_HEADS = 4
RET_QK = 64
D_FF = 5632
CONV_W = 3

LANE = 128
HALF = 64

C_QN = 0
C_QR = 512
C_CKV = 768
C_DQ = 1024
C_DK = 1536
C_DV = 2048
C_NQ = 2560
C_NK = 3072
C_NV = 3584
C_RQ = 4096
C_RK = 4352
C_RV = 4608
C_RG = 5120
C_KR = 5632
C_GATES = 6144
P_WIDTH = C_GATES + N_BRANCH * D_MODEL

VMEM_LIMIT = 56 * 1024 * 1024


def _cparams(sem):
    return pltpu.CompilerParams(dimension_semantics=sem, vmem_limit_bytes=VMEM_LIMIT)


def _dot(a, b):
    return jnp.dot(a, b, preferred_element_type=F32)


def _dot_t(a, b):
    return lax.dot_general(a, b, (((1,), (1,)), ((), ())), preferred_element_type=F32)


def _sigmoid(x):
    return 0.5 * jnp.tanh(0.5 * x) + 0.5


def _silu(x):
    return x * _sigmoid(x)


def _lane_iota(shape):
    return lax.broadcasted_iota(jnp.int32, shape, len(shape) - 1)


def _half_mask(x, e):
    lane = _lane_iota(x.shape)
    keep = (lane < HALF) if e == 0 else (lane >= HALF)
    return jnp.where(keep, x, 0.0)


def _rope(x, cos, sin, dr):
    w = x.shape[-1]
    nf = dr // 4
    lane = _lane_iota(x.shape)
    first = (lane % (2 * nf)) < nf
    rot = jnp.where(first, pltpu.roll(x, w - nf, 1), pltpu.roll(x, nf, 1))
    return x * cos + rot * sin


def _ada_kernel(c_ref, w_ref, b_ref, o_ref):
    a = _silu(c_ref[...]).astype(BF16)
    o_ref[...] = _dot(a, w_ref[...].astype(BF16)) + b_ref[...]


def _ada_call(cond, w_ada, b_ada):
    depth, d, n = w_ada.shape
    rows = cond.shape[0]
    tn = 1024
    return pl.pallas_call(
        _ada_kernel,
        grid=(depth, n // tn),
        in_specs=[
            pl.BlockSpec((rows, d), lambda l, j: (0, 0)),
            pl.BlockSpec((None, d, tn), lambda l, j: (l, 0, j)),
            pl.BlockSpec((None, 1, tn), lambda l, j: (l, 0, j)),
        ],
        out_specs=pl.BlockSpec((None, rows, tn), lambda l, j: (l, 0, j)),
        out_shape=jax.ShapeDtypeStruct((depth, rows, n), F32),
        compiler_params=_cparams(("parallel", "parallel")),
    )(cond, w_ada, b_ada.reshape(depth, 1, n))


NORM_ROWS = 128


def _mm_kernel(*refs, norm, resid):
    refs = list(refs)
    a_ref = refs.pop(0)
    if norm:
        g_ref, sh_ref, sc_ref = refs.pop(0), refs.pop(0), refs.pop(0)
    w_ref = refs.pop(0)
    if resid:
        r_ref, ga_ref = refs.pop(0), refs.pop(0)
    o_ref = refs.pop(0)

    if norm:
        h_ref = refs.pop(0)

        @pl.when(pl.program_id(1) == 0)
        def _():
            g = g_ref[...]
            mul = 1.0 + sc_ref[...]
            add = sh_ref[...]

            def body(c, carry):
                r0 = pl.multiple_of(c * NORM_ROWS, NORM_ROWS)
                x = a_ref[pl.ds(r0, NORM_ROWS), :]
                y = x * lax.rsqrt(jnp.mean(x * x, axis=-1, keepdims=True) + EPS) * g
                h_ref[pl.ds(r0, NORM_ROWS), :] = (y * mul + add).astype(BF16)
                return carry

            lax.fori_loop(0, a_ref.shape[0] // NORM_ROWS, body, 0)

        a = h_ref[...]
    else:
        a = a_ref[...]
    acc = _dot(a, w_ref[...])
    if resid:
        acc = r_ref[...] + ga_ref[...] * acc
    o_ref[...] = acc.astype(o_ref.dtype)


def _mm_call(a, w, *, out_dtype, tm, tn, rows_per_mod, norm=None, resid=None):
    m, k = a.shape
    n = w.shape[1]
    assert m % tm == 0 and n % tn == 0 and rows_per_mod % tm == 0
    tiles_per_mod = rows_per_mod // tm
    in_specs = [pl.BlockSpec((tm, k), lambda i, j: (i, 0))]
    args = [a]
    if norm is not None:
        g, sh, sc = norm
        in_specs += [
            pl.BlockSpec((1, k), lambda i, j: (0, 0)),
            pl.BlockSpec((None, 1, k), lambda i, j: (i // tiles_per_mod, 0, 0)),
            pl.BlockSpec((None, 1, k), lambda i, j: (i // tiles_per_mod, 0, 0)),
        ]
        args += [g, sh, sc]
    in_specs.append(pl.BlockSpec((k, tn), lambda i, j: (0, j)))
    args.append(w)
    if resid is not None:
        x, ga = resid
        in_specs += [
            pl.BlockSpec((tm, tn), lambda i, j: (i, j)),
            pl.BlockSpec((None, 1, tn), lambda i, j: (i // tiles_per_mod, 0, j)),
        ]
        args += [x, ga]
    scratch = [pltpu.VMEM((tm, k), BF16)] if norm is not None else []
    return pl.pallas_call(
        functools.partial(_mm_kernel, norm=norm is not None, resid=resid is not None),
        grid=(m // tm, n // tn),
        in_specs=in_specs,
        out_specs=pl.BlockSpec((tm, tn), lambda i, j: (i, j)),
        out_shape=jax.ShapeDtypeStruct((m, n), out_dtype),
        scratch_shapes=scratch,
        compiler_params=_cparams(("parallel", "arbitrary")),
    )(*args)


def _rope_kernel(x_ref, c_ref, s_ref, o_ref, *, dr):
    o_ref[...] = _rope(x_ref[...].astype(F32), c_ref[...], s_ref[...], dr).astype(o_ref.dtype)


def _rope_call(p, col, width, cos, sin, dr, seq):
    m = p.shape[0]
    tm = 256
    tpb = seq // tm
    cb = col // width
    assert col % width == 0
    return pl.pallas_call(
        functools.partial(_rope_kernel, dr=dr),
        grid=(m // tm,),
        in_specs=[
            pl.BlockSpec((tm, width), lambda i: (i, cb)),
            pl.BlockSpec((tm, width), lambda i: (i % tpb, 0)),
            pl.BlockSpec((tm, width), lambda i: (i % tpb, 0)),
        ],
        out_specs=pl.BlockSpec((tm, width), lambda i: (i, 0)),
        out_shape=jax.ShapeDtypeStruct((m, width), BF16),
        compiler_params=_cparams(("parallel",)),
    )(p, cos, sin)


MLA_K_W = 4 * 256
MLA_V_W = 4 * 128


def _mla_kv_kernel(*refs, norm, rope):
    refs = list(refs)
    c_ref, kr_ref = refs.pop(0), refs.pop(0)
    if rope:
        cos_ref, sin_ref = refs.pop(0), refs.pop(0)
    if norm:
        g_ref = refs.pop(0)
    w_ref = refs.pop(0)
    k_ref, v_ref = refs.pop(0), refs.pop(0)
    c = c_ref[...].astype(F32)
    if norm:
        ckv_ref = refs.pop(0)
        c = c * lax.rsqrt(jnp.mean(c * c, axis=-1, keepdims=True) + EPS) * g_ref[...]
        ckv_ref[...] = c
    kv = _dot(c.astype(BF16), w_ref[...])
    kr = kr_ref[...].astype(F32)
    if rope:
        kr = _rope(kr, cos_ref[...], sin_ref[...], MLA_ROPE)
    k_ref[...] = (kv[:, :MLA_K_W] + jnp.concatenate([kr] * 4, axis=1)).astype(BF16)
    v_ref[...] = kv[:, MLA_K_W:].astype(BF16)


def _mla_kv_call(ckv_src, ckv_col, kr_src, kr_col, w_e, *, g=None, rope=None, seq=None):
    m = ckv_src.shape[0]
    tm = 256
    assert m % tm == 0 and ckv_col % 256 == 0 and kr_col % 256 == 0
    in_specs = [
        pl.BlockSpec((tm, 256), lambda i: (i, ckv_col // 256)),
        pl.BlockSpec((tm, 256), lambda i: (i, kr_col // 256)),
    ]
    args = [ckv_src, kr_src]
    if rope is not None:
        tpb = seq // tm
        in_specs += [pl.BlockSpec((tm, 256), lambda i: (i % tpb, 0))] * 2
        args += list(rope)
    if g is not None:
        in_specs.append(pl.BlockSpec((1, 256), lambda i: (0, 0)))
        args.append(g)
    in_specs.append(pl.BlockSpec(w_e.shape, lambda i: (0, 0)))
    args.append(w_e)
    out_specs = [pl.BlockSpec((tm, MLA_K_W), lambda i: (i, 0)), pl.BlockSpec((tm, MLA_V_W), lambda i: (i, 0))]
    out_shape = [jax.ShapeDtypeStruct((m, MLA_K_W), BF16), jax.ShapeDtypeStruct((m, MLA_V_W), BF16)]
    if g is not None:
        out_specs.append(pl.BlockSpec((tm, 256), lambda i: (i, 0)))
        out_shape.append(jax.ShapeDtypeStruct((m, 256), F32))
    return pl.pallas_call(
        functools.partial(_mla_kv_kernel, norm=g is not None, rope=rope is not None),
        grid=(m // tm,),
        in_specs=in_specs,
        out_specs=out_specs,
        out_shape=out_shape,
        compiler_params=_cparams(("parallel",)),
    )(*args)


def _softmax_parts(scores):
    m = scores[0].max(axis=-1, keepdims=True)
    for s in scores[1:]:
        m = jnp.maximum(m, s.max(axis=-1, keepdims=True))
    ps = [jnp.exp(s - m) for s in scores]
    l = ps[0].sum(axis=-1, keepdims=True)
    for p in ps[1:]:
        l = l + p.sum(axis=-1, keepdims=True)
    return ps, 1.0 / l


def _mha_kernel(*refs, mla, nseg, scale):
    refs = list(refs)
    qn_ref = refs.pop(0)
    qr_ref = refs.pop(0) if mla else None
    kv_refs = [(refs.pop(0), refs.pop(0)) for _ in range(nseg)]
    o_ref = refs.pop(0)
    kw = 256 if mla else LANE
    lane = _lane_iota((qn_ref.shape[0], LANE))
    for j in range(4):
        outs = []
        qn_pair = qn_ref[:, j * LANE:(j + 1) * LANE].astype(F32)
        for e in range(2):
            h = 2 * j + e
            q = _half_mask(qn_pair, e)
            if mla:
                qr = qr_ref[:, (h // 4) * LANE:(h // 4 + 1) * LANE].astype(F32)
                qr = jnp.where(lane // MLA_ROPE == h % 4, qr, 0.0)
                q = jnp.concatenate([q, qr], axis=1)
            q = q.astype(BF16)
            scores = [_dot_t(q, k_ref[:, j * kw:(j + 1) * kw].astype(BF16)) * scale for k_ref, _ in kv_refs]
            ps, inv = _softmax_parts(scores)
            o = _dot(ps[0].astype(BF16), kv_refs[0][1][:, j * LANE:(j + 1) * LANE].astype(BF16))
            for p, (_, v_ref) in zip(ps[1:], kv_refs[1:]):
                o = o + _dot(p.astype(BF16), v_ref[:, j * LANE:(j + 1) * LANE].astype(BF16))
            outs.append(o * inv)
        o_ref[:, j * LANE:(j + 1) * LANE] = jnp.where(lane < HALF, outs[0], outs[1]).astype(o_ref.dtype)


def _mha_call(qn, qn_col, qr, qr_col, segs, *, nb, tq, t, scale):
    mla = qr is not None
    kw = MLA_K_W if mla else 512
    nq = t // tq
    in_specs = [pl.BlockSpec((tq, 512), lambda b, i: (b * nq + i, qn_col // 512))]
    args = [qn]
    if mla:
        in_specs.append(pl.BlockSpec((tq, 256), lambda b, i: (b * nq + i, qr_col // 256)))
        args.append(qr)
    for k_arr, k_col, v_arr, v_col, s in segs:
        assert k_col % kw == 0 and v_col % 512 == 0
        in_specs.append(pl.BlockSpec((s, kw), functools.partial(lambda b, i, c: (b, c), c=k_col // kw)))
        in_specs.append(pl.BlockSpec((s, 512), functools.partial(lambda b, i, c: (b, c), c=v_col // 512)))
        args += [k_arr, v_arr]
    return pl.pallas_call(
        functools.partial(_mha_kernel, mla=mla, nseg=len(segs), scale=scale),
        grid=(nb, nq),
        in_specs=in_specs,
        out_specs=pl.BlockSpec((tq, 512), lambda b, i: (b * nq + i, 0)),
        out_shape=jax.ShapeDtypeStruct((nb * t, 512), BF16),
        compiler_params=_cparams(("parallel", "parallel")),
    )(*args)


def _diff_kernel(*refs, nseg, lam_init):
    refs = list(refs)
    q_ref = refs.pop(0)
    kv_refs = [(refs.pop(0), refs.pop(0)) for _ in range(nseg)]
    lp_ref, g_ref, o_ref = refs
    lp = lp_ref[...]
    lam = (jnp.exp(jnp.sum(lp[0:1] * lp[1:2], axis=-1, keepdims=True))
           - jnp.exp(jnp.sum(lp[2:3] * lp[3:4], axis=-1, keepdims=True)) + lam_init)
    scale = DIFF_QK ** -0.5
    for h in range(DIFF_HEADS):
        q_pair = q_ref[:, h * LANE:(h + 1) * LANE].astype(F32)
        probs = []
        for c in range(2):
            q = _half_mask(q_pair, c).astype(BF16)
            scores = [_dot_t(q, k_ref[:, h * LANE:(h + 1) * LANE].astype(BF16)) * scale for k_ref, _ in kv_refs]
            ps, inv = _softmax_parts(scores)
            probs.append([p * inv for p in ps])
        o = None
        for s in range(nseg):
            a = (probs[0][s] - lam * probs[1][s]).astype(BF16)
            t = _dot(a, kv_refs[s][1][:, h * LANE:(h + 1) * LANE].astype(BF16))
            o = t if o is None else o + t
        o = o * lax.rsqrt(jnp.mean(o * o, axis=-1, keepdims=True) + EPS) * g_ref[:, h * LANE:(h + 1) * LANE]
        o_ref[:, h * LANE:(h + 1) * LANE] = (o * (1.0 - lam_init)).astype(o_ref.dtype)


def _diff_call(q, q_col, segs, lp, g, *, nb, tq, t, lam_init):
    nq = t // tq
    in_specs = [pl.BlockSpec((tq, 512), lambda b, i: (b * nq + i, q_col // 512))]
    args = [q]
    for k_arr, k_col, v_arr, v_col, s in segs:
        in_specs.append(pl.BlockSpec((s, 512), functools.partial(lambda b, i, c: (b, c), c=k_col // 512)))
        in_specs.append(pl.BlockSpec((s, 512), functools.partial(lambda b, i, c: (b, c), c=v_col // 512)))
        args += [k_arr, v_arr]
    in_specs += [pl.BlockSpec((4, DIFF_QK), lambda b, i: (0, 0)), pl.BlockSpec((1, 512), lambda b, i: (0, 0))]
    args += [lp, g]
    return pl.pallas_call(
        functools.partial(_diff_kernel, nseg=len(segs), lam_init=lam_init),
        grid=(nb, nq),
        in_specs=in_specs,
        out_specs=pl.BlockSpec((tq, 512), lambda b, i: (b * nq + i, 0)),
        out_shape=jax.ShapeDtypeStruct((nb * t, 512), BF16),
        compiler_params=_cparams(("parallel", "parallel")),
    )(*args)


NA_QROWS = 4


def _na_plan(rows):
    win = min(rows, NA_QROWS + NA_ROWS)
    kh = min(NA_ROWS, rows)
    cfgs, cfg_idx, starts = [], [], []
    for blk in range(rows // NA_QROWS):
        r0 = blk * NA_QROWS
        s_blk = int(np.clip(r0 - kh // 2, 0, rows - win))
        cfg = tuple((r0 + i - s_blk, int(np.clip(r0 + i - kh // 2, 0, rows - kh)) - s_blk) for i in range(NA_QROWS))
        if cfg not in cfgs:
            cfgs.append(cfg)
        cfg_idx.append(cfgs.index(cfg))
        starts.append(s_blk)
    return win, cfgs, np.array([cfg_idx, starts], np.int32)


def _na_kernel(meta_ref, q_ref, k_ref, v_ref, kc_ref, vc_ref, b_ref, o_ref, *, win):
    start = pl.multiple_of(meta_ref[1, pl.program_id(1)] * GRID_W, GRID_W)
    n_loc = win * GRID_W
    lane = _lane_iota((q_ref.shape[0], LANE))
    for j in range(4):
        q_pair = q_ref[:, j * LANE:(j + 1) * LANE].astype(F32) * (NA_DIM ** -0.5)
        k_loc = k_ref[pl.ds(start, n_loc), j * LANE:(j + 1) * LANE].astype(BF16)
        v_loc = v_ref[pl.ds(start, n_loc), j * LANE:(j + 1) * LANE].astype(BF16)
        k_ctx = kc_ref[:, j * LANE:(j + 1) * LANE].astype(BF16)
        v_ctx = vc_ref[:, j * LANE:(j + 1) * LANE].astype(BF16)
        outs = []
        for e in range(2):
            q = _half_mask(q_pair, e).astype(BF16)
            s_loc = _dot_t(q, k_loc) + b_ref[2 * j + e]
            s_ctx = _dot_t(q, k_ctx)
            ps, inv = _softmax_parts([s_loc, s_ctx])
            o = _dot(ps[0].astype(BF16), v_loc) + _dot(ps[1].astype(BF16), v_ctx)
            outs.append(o * inv)
        o_ref[:, j * LANE:(j + 1) * LANE] = jnp.where(lane < HALF, outs[0], outs[1]).astype(o_ref.dtype)


def _na_call(p, kc, vc, rel_bias, *, nb, t, past):
    rows = t // GRID_W
    assert rows % NA_QROWS == 0 and rows >= NA_ROWS
    win, cfgs, meta = _na_plan(rows)
    bias_tab = _na_bias_table(rel_bias, cfgs, win)
    nblk = rows // NA_QROWS
    tq = NA_QROWS * GRID_W
    n_loc = win * GRID_W
    grid_spec = pltpu.PrefetchScalarGridSpec(
        num_scalar_prefetch=1,
        grid=(nb, nblk),
        in_specs=[
            pl.BlockSpec((tq, 512), lambda b, r, meta: (b * nblk + r, C_NQ // 512)),
            pl.BlockSpec((t, 512), lambda b, r, meta: (b, C_NK // 512)),
            pl.BlockSpec((t, 512), lambda b, r, meta: (b, C_NV // 512)),
            pl.BlockSpec((past, 512), lambda b, r, meta: (b, 0)),
            pl.BlockSpec((past, 512), lambda b, r, meta: (b, 0)),
            pl.BlockSpec((None, NA_HEADS, tq, n_loc), lambda b, r, meta: (meta[0, r], 0, 0, 0)),
        ],
        out_specs=pl.BlockSpec((tq, 512), lambda b, r, meta: (b * nblk + r, 0)),
    )
    return pl.pallas_call(
        functools.partial(_na_kernel, win=win),
        grid_spec=grid_spec,
        out_shape=jax.ShapeDtypeStruct((nb * t, 512), BF16),
        compiler_params=_cparams(("parallel", "arbitrary")),
    )(jnp.asarray(meta), p, p, p, kc, vc, bias_tab)


def _log_sigmoid(x):
    return jnp.minimum(x, 0.0) - jnp.log(1.0 + jnp.exp(-jnp.abs(x)))


def _ret_kernel(*refs, has_state, t):
    refs = list(refs)
    q_ref, k_ref, v_ref, rg_ref, dec_ref, gn_ref = [refs.pop(0) for _ in range(6)]
    s0_ref = refs.pop(0) if has_state else None
    o_ref = refs.pop(0)
    tq = q_ref.shape[0]
    q0 = pl.program_id(1) * tq
    lg = _log_sigmoid(dec_ref[...])
    ti = (q0 + lax.broadcasted_iota(jnp.int32, (tq, t), 0)).astype(F32)
    ui = lax.broadcasted_iota(jnp.int32, (tq, t), 1).astype(F32)
    diff = ti - ui
    tcol = (q0 + lax.broadcasted_iota(jnp.int32, (tq, 1), 0)).astype(F32)
    for h in range(RET_HEADS):
        j, e = h // 2, h % 2
        lgf = lg[h:h + 1, 0:1]
        lgb = lg[RET_HEADS + h:RET_HEADS + h + 1, 0:1]
        q = _half_mask(q_ref[:, j * LANE:(j + 1) * LANE].astype(F32), e).astype(BF16)
        a = _dot_t(q, k_ref[:, j * LANE:(j + 1) * LANE].astype(BF16)) * (RET_QK ** -0.5)
        dmat = jnp.exp(jnp.where(diff >= 0, diff * lgf, -diff * lgb))
        y = _dot((a * dmat).astype(BF16), v_ref[:, h * LANE:(h + 1) * LANE].astype(BF16))
        if has_state:
            cf = _dot(q, s0_ref[0, j].astype(BF16))
            cb = _dot(q, s0_ref[1, j].astype(BF16))
            y = y + cf * jnp.exp((tcol + 1.0) * lgf) + cb * jnp.exp((float(t) - tcol) * lgb)
        mu = jnp.mean(y, axis=-1, keepdims=True)
        yc = y - mu
        var = jnp.mean(yc * yc, axis=-1, keepdims=True)
        yn = yc * lax.rsqrt(var + EPS) * gn_ref[:, h * LANE:(h + 1) * LANE]
        o_ref[:, h * LANE:(h + 1) * LANE] = (_silu(rg_ref[:, h * LANE:(h + 1) * LANE].astype(F32)) * yn).astype(o_ref.dtype)


def _ret_call(p, dec, gn, s0, *, nb, tq, t):
    nq = t // tq
    in_specs = [
        pl.BlockSpec((tq, 256), lambda b, i: (b * nq + i, C_RQ // 256)),
        pl.BlockSpec((t, 256), lambda b, i: (b, C_RK // 256)),
        pl.BlockSpec((t, 512), lambda b, i: (b, C_RV // 512)),
        pl.BlockSpec((tq, 512), lambda b, i: (b * nq + i, C_RG // 512)),
        pl.BlockSpec((8, LANE), lambda b, i: (0, 0)),
        pl.BlockSpec((1, 512), lambda b, i: (0, 0)),
    ]
    args = [p, p, p, p, dec, gn]
    if s0 is not None:
        in_specs.append(pl.BlockSpec((None, 2, 2, LANE, LANE), lambda b, i: (b, 0, 0, 0, 0)))
        args.append(s0)
    return pl.pallas_call(
        functools.partial(_ret_kernel, has_state=s0 is not None, t=t),
        grid=(nb, nq),
        in_specs=in_specs,
        out_specs=pl.BlockSpec((tq, 512), lambda b, i: (b * nq + i, 0)),
        out_shape=jax.ShapeDtypeStruct((nb * t, 512), BF16),
        compiler_params=_cparams(("parallel", "parallel")),
    )(*args)


def _ret_state_kernel(k_ref, v_ref, dec_ref, o_ref, *, t):
    lg = _log_sigmoid(dec_ref[...])
    tcol = lax.broadcasted_iota(jnp.int32, (t, 1), 0).astype(F32)
    lane = _lane_iota((t, LANE))
    for j in range(2):
        k_pair = k_ref[:, j * LANE:(j + 1) * LANE].astype(F32) * (RET_QK ** -0.5)
        for d in range(2):
            expo = (float(t) - 1.0 - tcol) if d == 0 else tcol
            w0 = jnp.exp(expo * lg[d * RET_HEADS + 2 * j:d * RET_HEADS + 2 * j + 1, 0:1])
            w1 = jnp.exp(expo * lg[d * RET_HEADS + 2 * j + 1:d * RET_HEADS + 2 * j + 2, 0:1])
            kd = (k_pair * jnp.where(lane < HALF, w0, w1)).T.astype(BF16)
            for e in range(2):
                h = 2 * j + e
                sfull = _dot(kd, v_ref[:, h * LANE:(h + 1) * LANE].astype(BF16))
                o_ref[d, h] = sfull[e * HALF:(e + 1) * HALF, :]


def _ret_state_call(p, dec, *, nb, t):
    return pl.pallas_call(
        functools.partial(_ret_state_kernel, t=t),
        grid=(nb,),
        in_specs=[
            pl.BlockSpec((t, 256), lambda b: (b, C_RK // 256)),
            pl.BlockSpec((t, 512), lambda b: (b, C_RV // 512)),
            pl.BlockSpec((8, LANE), lambda b: (0, 0)),
        ],
        out_specs=pl.BlockSpec((None, 2, RET_HEADS, RET_QK, RET_V), lambda b: (b, 0, 0, 0, 0)),
        out_shape=jax.ShapeDtypeStruct((nb, 2, RET_HEADS, RET_QK, RET_V), F32),
        compiler_params=_cparams(("parallel",)),
    )(p, p, dec)


def _merge_kernel(ya_ref, yb_ref, yc_ref, yd_ref, g0_ref, g1_ref, g2_ref, g3_ref, w_ref, o_ref):
    ys = (ya_ref, yb_ref, yc_ref, yd_ref)
    gs = (g0_ref, g1_ref, g2_ref, g3_ref)
    m = None
    for i in range(N_BRANCH):
        t = _sigmoid(gs[i][...].astype(F32)) * _dot(ys[i][...], w_ref[i])
        m = t if m is None else m + t
    o_ref[...] = m.astype(o_ref.dtype)


def _merge_call(ys, p, w_branch):
    m = p.shape[0]
    tm, tn = 512, 512
    gate_specs = [
        pl.BlockSpec((tm, tn), functools.partial(lambda r, c, i: (r, (C_GATES + i * D_MODEL) // tn + c), i=i))
        for i in range(N_BRANCH)
    ]
    return pl.pallas_call(
        _merge_kernel,
        grid=(m // tm, D_MODEL // tn),
        in_specs=[pl.BlockSpec((tm, BRANCH_W), lambda r, c: (r, 0))] * N_BRANCH + gate_specs
        + [pl.BlockSpec((N_BRANCH, BRANCH_W, tn), lambda r, c: (0, 0, c))],
        out_specs=pl.BlockSpec((tm, tn), lambda r, c: (r, c)),
        out_shape=jax.ShapeDtypeStruct((m, D_MODEL), BF16),
        compiler_params=_cparams(("parallel", "parallel")),
    )(*ys, p, p, p, p, w_branch)


HALO = 16


CONV_PAD = 8


def _conv_kernel(ua_ref, uap_ref, uan_ref, ub_ref, ubp_ref, ubn_ref, wa_ref, wb_ref, ba_ref, bb_ref, o_ref,
                 sa_ref, sb_ref, *, tiles_per_seq):
    i = pl.program_id(0)
    tm = ua_ref.shape[0]
    has_prev = ((i % tiles_per_seq) != 0).astype(F32)
    has_next = ((i % tiles_per_seq) != tiles_per_seq - 1).astype(F32)

    def conv(u_ref, up_ref, un_ref, w_ref, b_ref, s_ref):
        u = u_ref[...].astype(F32)
        s_ref[CONV_PAD:CONV_PAD + tm, :] = u
        s_ref[CONV_PAD - 1:CONV_PAD, :] = up_ref[HALO - 1:HALO, :].astype(F32) * has_prev
        s_ref[CONV_PAD + tm:CONV_PAD + tm + 1, :] = un_ref[0:1, :].astype(F32) * has_next
        um1 = s_ref[CONV_PAD - 1:CONV_PAD - 1 + tm, :]
        up1 = s_ref[CONV_PAD + 1:CONV_PAD + 1 + tm, :]
        return um1 * w_ref[0:1, :] + u * w_ref[1:2, :] + up1 * w_ref[2:3, :] + b_ref[...]

    a = conv(ua_ref, uap_ref, uan_ref, wa_ref, ba_ref, sa_ref)
    b = conv(ub_ref, ubp_ref, ubn_ref, wb_ref, bb_ref, sb_ref)
    o_ref[...] = (_silu(a) * b).astype(o_ref.dtype)


def _conv_call(u, conv_w, conv_b, *, seq):
    m = u.shape[0]
    tm, tc = min(seq, 512), 1408
    assert seq % tm == 0 and D_FF % tc == 0
    nb_half = D_FF // tc
    last_halo = m // HALO - 1
    r = tm // HALO

    def main(off):
        return pl.BlockSpec((tm, tc), lambda i, c: (i, c + off))

    def prev(off):
        return pl.BlockSpec((HALO, tc), lambda i, c: (jnp.maximum(i * r - 1, 0), c + off))

    def nxt(off):
        return pl.BlockSpec((HALO, tc), lambda i, c: (jnp.minimum((i + 1) * r, last_halo), c + off))

    def vec(rows, off):
        return pl.BlockSpec((rows, tc), lambda i, c: (0, c + off))

    cb = conv_b.reshape(1, 2 * D_FF)
    return pl.pallas_call(
        functools.partial(_conv_kernel, tiles_per_seq=seq // tm),
        grid=(m // tm, nb_half),
        in_specs=[main(0), prev(0), nxt(0), main(nb_half), prev(nb_half), nxt(nb_half),
                  vec(CONV_W, 0), vec(CONV_W, nb_half), vec(1, 0), vec(1, nb_half)],
        out_specs=pl.BlockSpec((tm, tc), lambda i, c: (i, c)),
        out_shape=jax.ShapeDtypeStruct((m, D_FF), BF16),
        scratch_shapes=[pltpu.VMEM((tm + 2 * CONV_PAD, tc), F32)] * 2,
        compiler_params=_cparams(("parallel", "parallel")),
    )(u, u, u, u, u, u, conv_w, conv_w, cb, cb)


def _final_norm_kernel(x_ref, g_ref, o_ref):
    x = x_ref[...]
    o_ref[...] = x * lax.rsqrt(jnp.mean(x * x, axis=-1, keepdims=True) + EPS) * g_ref[...]


def _final_norm_call(x, g):
    m, d = x.shape
    tm = 256
    return pl.pallas_call(
        _final_norm_kernel,
        grid=(m // tm,),
        in_specs=[pl.BlockSpec((tm, d), lambda i: (i, 0)), pl.BlockSpec((1, d), lambda i: (0, 0))],
        out_specs=pl.BlockSpec((tm, d), lambda i: (i, 0)),
        out_shape=jax.ShapeDtypeStruct((m, d), F32),
        compiler_params=_cparams(("parallel",)),
    )(x, g.reshape(1, d))


def _reorder_w_in(w):
    d = w.shape[0]
    mq = w[:, :768].reshape(d, MLA_HEADS, MLA_NOPE + MLA_ROPE)
    qn = mq[:, :, :MLA_NOPE].reshape(d, 512)
    qr = mq[:, :, MLA_NOPE:].reshape(d, 256)
    ckv = w[:, 768:1024]
    mkr = w[:, 1024:1056]
    rest = w[:, 1056:5664]
    gates = w[:, 5664:]
    krblk = jnp.concatenate([jnp.zeros((d, LANE), w.dtype)] + [mkr] * 4, axis=1)
    pad = jnp.zeros((d, C_GATES - C_KR - 256), w.dtype)
    return jnp.concatenate([qn, qr, ckv, rest, krblk, pad, gates], axis=1).astype(BF16)


def _reorder_w_ukv(w):
    r = w.shape[0]
    w3 = w.reshape(r, MLA_HEADS, MLA_NOPE + MLA_V)
    kn = w3[:, :, :MLA_NOPE].reshape(r, 4, LANE)
    vv = w3[:, :, MLA_NOPE:].reshape(r, 512)
    knz = jnp.concatenate([kn, jnp.zeros_like(kn)], axis=2).reshape(r, MLA_K_W)
    return jnp.concatenate([knz, vv], axis=1).astype(BF16)


def _rope_tables(t, dr, reps, lead_zero=0):
    nf = dr // 4
    pos = jnp.arange(t)
    pos = jnp.stack([pos // GRID_W, pos % GRID_W], axis=-1).astype(F32)
    inv = ROPE_BASE ** (-jnp.arange(nf, dtype=F32) / nf)
    ang = pos[:, :, None] * inv
    cos = jnp.cos(ang)
    sin = jnp.sin(ang)
    c = jnp.tile(jnp.concatenate([cos, cos], axis=2).reshape(t, dr), (1, reps))
    s = jnp.tile(jnp.concatenate([-sin, sin], axis=2).reshape(t, dr), (1, reps))
    if lead_zero:
        z = jnp.zeros((t, lead_zero), F32)
        c = jnp.concatenate([z, c], axis=1)
        s = jnp.concatenate([z, s], axis=1)
    return c, s


def _na_bias_table(rel_bias, cfgs, win):
    v = rel_bias.astype(F32)
    pad = GRID_W - NA_COLS
    ext = jnp.concatenate([jnp.repeat(v[..., :1], pad, axis=-1), v, jnp.repeat(v[..., -1:], pad, axis=-1)], axis=-1)
    col_t = jnp.stack([ext[..., GRID_W - 1 - q:2 * GRID_W - 1 - q] for q in range(GRID_W)], axis=2)
    qc = np.arange(GRID_W)[:, None]
    kc = np.arange(GRID_W)[None, :]
    c0 = np.clip(qc - NA_COLS // 2, 0, GRID_W - NA_COLS)
    col_ok = (kc >= c0) & (kc < c0 + NA_COLS)
    col_t = jnp.where(col_ok, col_t, NEG_INF)
    neg = jnp.full((NA_HEADS, GRID_W, GRID_W), NEG_INF, F32)
    kh = min(NA_ROWS, win)
    tabs = []
    for cfg in cfgs:
        per_row = []
        for delta, off in cfg:
            blocks = [col_t[:, kk - delta + NA_ROWS - 1] if off <= kk < off + kh else neg for kk in range(win)]
            per_row.append(jnp.concatenate(blocks, axis=-1))
        tabs.append(jnp.concatenate(per_row, axis=1))
    return jnp.stack(tabs)


def kernel(x_prompt, x_sample, cache_mla_ckv, cache_mla_krope, cache_diff_k, cache_diff_v, cache_na_k, cache_na_v, state_ret, c, c_ctx, w_ada, b_ada, g_attn, g_ffn, w_in, mla_kv_norm, w_mla_ukv, diff_lambda, diff_norm, na_rel_bias, ret_decay, ret_norm, w_branch, w_o, w_up, conv_w, conv_b, w_down, g_final):
    nbc, seq, d = x_prompt.shape
    nbl, t, _ = x_sample.shape
    past = cache_mla_ckv.shape[2]
    depth = w_in.shape[0]
    mc, ml = nbc * seq, nbl * t

    xc = x_prompt.reshape(mc, d)
    xl = x_sample.reshape(ml, d)

    mod_rows = 16
    cond = jnp.concatenate([c_ctx[None], c, jnp.zeros((mod_rows - 1 - nbl, d), F32)], axis=0)
    mods = _ada_call(cond, w_ada, b_ada).reshape(depth, mod_rows, 6, d)

    cos32q, sin32q = _rope_tables(t, MLA_ROPE, 8)
    cos32k, sin32k = _rope_tables(t, MLA_ROPE, 4, lead_zero=LANE)
    cos64, sin64 = _rope_tables(t, DIFF_QK, 16)

    tm_c = min(1024, mc)
    tm_l = min(1024, t)
    states = []
    for l in range(depth):
        lam_init = 0.8 - 0.6 * math.exp(-0.3 * l)
        w_in_r = _reorder_w_in(w_in[l])
        w_ukv_e = _reorder_w_ukv(w_mla_ukv[l])
        wb = w_branch[l].astype(BF16)
        wo = w_o[l].astype(BF16)
        wup = w_up[l].astype(BF16)
        wdn = w_down[l].astype(BF16)
        kvn = mla_kv_norm[l].reshape(1, MLA_KV_RANK)
        dec = jnp.broadcast_to(ret_decay[l].reshape(2 * RET_HEADS, 1), (2 * RET_HEADS, LANE))
        gn = ret_norm[l].reshape(1, 512)
        dnorm = diff_norm[l].reshape(1, 512)
        g_a = g_attn[l].reshape(1, d)
        g_f = g_ffn[l].reshape(1, d)

        def mod(i):
            return mods[l, 0:1, i][:, None, :], mods[l, 1:1 + nbl, i][:, None, :]

        sh1, sc1, ga1, sh2, sc2, ga2 = [mod(i) for i in range(6)]

        pc = _mm_call(xc, w_in_r, out_dtype=F32, tm=tm_c, tn=1024, rows_per_mod=mc, norm=(g_a, sh1[0], sc1[0]))
        kc_mla, vc_mla, ckv_state = _mla_kv_call(pc, C_CKV, pc, C_KR, w_ukv_e, g=kvn)
        ya = _mha_call(pc, C_QN, pc, C_QR, [(kc_mla, 0, vc_mla, 0, seq)], nb=nbc, tq=seq, t=seq,
                       scale=(MLA_NOPE + MLA_ROPE) ** -0.5)
        yb = _diff_call(pc, C_DQ, [(pc, C_DK, pc, C_DV, seq)], diff_lambda[l], dnorm, nb=nbc, tq=seq, t=seq,
                        lam_init=lam_init)
        yc = _mha_call(pc, C_NQ, None, 0, [(pc, C_NK, pc, C_NV, seq)], nb=nbc, tq=seq, t=seq, scale=NA_DIM ** -0.5)
        yd = _ret_call(pc, dec, gn, None, nb=nbc, tq=seq, t=seq)
        sret = _ret_state_call(pc, dec, nb=nbc, t=seq)
        states.append((
            ckv_state.reshape(nbc, seq, MLA_KV_RANK),
            pc[:, C_KR + LANE:C_KR + LANE + MLA_ROPE].reshape(nbc, seq, MLA_ROPE),
            pc[:, C_DK:C_DK + 512].reshape(nbc, seq, DIFF_HEADS, 2, DIFF_QK),
            pc[:, C_DV:C_DV + 512].reshape(nbc, seq, DIFF_HEADS, DIFF_V),
            pc[:, C_NK:C_NK + 512].reshape(nbc, seq, NA_HEADS, NA_DIM),
            pc[:, C_NV:C_NV + 512].reshape(nbc, seq, NA_HEADS, NA_DIM),
            sret,
        ))
        mrg = _merge_call((ya, yb, yc, yd), pc, wb)
        xc = _mm_call(mrg, wo, out_dtype=F32, tm=tm_c, tn=1024, rows_per_mod=mc, resid=(xc, ga1[0]))
        u = _mm_call(xc, wup, out_dtype=BF16, tm=tm_c, tn=1024, rows_per_mod=mc, norm=(g_f, sh2[0], sc2[0]))
        act = _conv_call(u, conv_w[l], conv_b[l], seq=seq)
        xc = _mm_call(act, wdn, out_dtype=F32, tm=512, tn=1024, rows_per_mod=mc, resid=(xc, ga2[0]))

        plat = _mm_call(xl, w_in_r, out_dtype=BF16, tm=tm_l, tn=1024, rows_per_mod=t, norm=(g_a, sh1[1], sc1[1]))
        kl_mla, vl_mla, _ = _mla_kv_call(plat, C_CKV, plat, C_KR, w_ukv_e, g=kvn, rope=(cos32k, sin32k), seq=t)
        kr_c = cache_mla_krope[:, l].reshape(nbl * past, MLA_ROPE)
        krblk_c = jnp.concatenate([jnp.zeros((nbl * past, LANE), F32)] + [kr_c] * 4, axis=1)
        kp_mla, vp_mla = _mla_kv_call(cache_mla_ckv[:, l].reshape(nbl * past, MLA_KV_RANK), 0, krblk_c, 0, w_ukv_e)
        qr_l = _rope_call(plat, C_QR, 256, cos32q, sin32q, MLA_ROPE, t)
        dqk_l = _rope_call(plat, C_DQ, 1024, cos64, sin64, DIFF_QK, t)
        tq = 256
        ya = _mha_call(plat, C_QN, qr_l, 0, [(kl_mla, 0, vl_mla, 0, t), (kp_mla, 0, vp_mla, 0, past)],
                       nb=nbl, tq=tq, t=t, scale=(MLA_NOPE + MLA_ROPE) ** -0.5)
        dk_c = cache_diff_k[:, l].reshape(nbl * past, 512)
        dv_c = cache_diff_v[:, l].reshape(nbl * past, 512)
        yb = _diff_call(dqk_l, 0, [(dqk_l, 512, plat, C_DV, t), (dk_c, 0, dv_c, 0, past)], diff_lambda[l], dnorm,
                        nb=nbl, tq=tq, t=t, lam_init=lam_init)
        yc = _na_call(plat, cache_na_k[:, l].reshape(nbl * past, 512), cache_na_v[:, l].reshape(nbl * past, 512),
                      na_rel_bias[l], nb=nbl, t=t, past=past)
        s0 = state_ret[:, l].astype(F32).reshape(nbl, 2, 2, LANE, LANE)
        yd = _ret_call(plat, dec, gn, s0, nb=nbl, tq=tq, t=t)
        mrg = _merge_call((ya, yb, yc, yd), plat, wb)
        xl = _mm_call(mrg, wo, out_dtype=F32, tm=tm_l, tn=1024, rows_per_mod=t, resid=(xl, ga1[1]))
        u = _mm_call(xl, wup, out_dtype=BF16, tm=tm_l, tn=1024, rows_per_mod=t, norm=(g_f, sh2[1], sc2[1]))
        act = _conv_call(u, conv_w[l], conv_b[l], seq=t)
        xl = _mm_call(act, wdn, out_dtype=F32, tm=512, tn=1024, rows_per_mod=t, resid=(xl, ga2[1]))

    y_prompt = _final_norm_call(xc, g_final).reshape(nbc, seq, d)
    y_sample = _final_norm_call(xl, g_final).reshape(nbl, t, d)
    outs = [jnp.stack([st[i] for st in states], axis=1) for i in range(7)]
    return (y_prompt, y_sample, *outs)
```

```python
import functools
import math

import jax
import jax.numpy as jnp
import numpy as np
from jax import lax
from jax.experimental import pallas as pl
from jax.experimental.pallas import tpu as pltpu

F32 = jnp.float32
BF16 = jnp.bfloat16

D_MODEL = 2048
GRID_W = 64
ROPE_BASE = 10000.0
EPS = 1e-6
NEG_INF = -1e30
N_BRANCH = 4
BRANCH_W = 512
MLA_HEADS = 8
MLA_NOPE = 64
MLA_ROPE = 32
MLA_V = 64
MLA_KV_RANK = 256
DIFF_QK = 64
DIFF_V = 128
DIFF_HEADS = 4
NA_DIM = 64
NA_HEADS = 8
NA_ROWS = 8
NA_COLS = 16
RET_V = 128
RET_HEADS = 4
RET_QK = 64
D_FF = 5632
CONV_W = 3

LANE = 128
HALF = 64

C_QN = 0
C_QR = 512
C_CKV = 768
C_DQ = 1024
C_DK = 1536
C_DV = 2048
C_NQ = 2560
C_NK = 3072
C_NV = 3584
C_RQ = 4096
C_RK = 4352
C_RV = 4608
C_RG = 5120
C_KR = 5632
C_GATES = 6144
P_WIDTH = C_GATES + N_BRANCH * D_MODEL

VMEM_LIMIT = 56 * 1024 * 1024


def _cparams(sem):
    return pltpu.CompilerParams(dimension_semantics=sem, vmem_limit_bytes=VMEM_LIMIT)


def _dot(a, b):
    return jnp.dot(a, b, preferred_element_type=F32)


def _dot_t(a, b):
    return lax.dot_general(a, b, (((1,), (1,)), ((), ())), preferred_element_type=F32)


def _sigmoid(x):
    return 0.5 * jnp.tanh(0.5 * x) + 0.5


def _silu(x):
    return x * _sigmoid(x)


def _lane_iota(shape):
    return lax.broadcasted_iota(jnp.int32, shape, len(shape) - 1)


def _half_mask(x, e):
    lane = _lane_iota(x.shape)
    keep = (lane < HALF) if e == 0 else (lane >= HALF)
    return jnp.where(keep, x, 0.0)


def _rope(x, cos, sin, dr):
    w = x.shape[-1]
    nf = dr // 4
    lane = _lane_iota(x.shape)
    first = (lane % (2 * nf)) < nf
    rot = jnp.where(first, pltpu.roll(x, w - nf, 1), pltpu.roll(x, nf, 1))
    return x * cos + rot * sin


def _ada_kernel(c_ref, w_ref, b_ref, o_ref):
    a = _silu(c_ref[...]).astype(BF16)
    o_ref[...] = _dot(a, w_ref[...].astype(BF16)) + b_ref[...]


def _ada_call(cond, w_ada, b_ada):
    depth, d, n = w_ada.shape
    rows = cond.shape[0]
    tn = 1024
    return pl.pallas_call(
        _ada_kernel,
        grid=(depth, n // tn),
        in_specs=[
            pl.BlockSpec((rows, d), lambda l, j: (0, 0)),
            pl.BlockSpec((None, d, tn), lambda l, j: (l, 0, j)),
            pl.BlockSpec((None, 1, tn), lambda l, j: (l, 0, j)),
        ],
        out_specs=pl.BlockSpec((None, rows, tn), lambda l, j: (l, 0, j)),
        out_shape=jax.ShapeDtypeStruct((depth, rows, n), F32),
        compiler_params=_cparams(("parallel", "parallel")),
    )(cond, w_ada, b_ada.reshape(depth, 1, n))


NORM_ROWS = 128


def _mm_kernel(*refs, norm, resid):
    refs = list(refs)
    a_ref = refs.pop(0)
    if norm:
        g_ref, sh_ref, sc_ref = refs.pop(0), refs.pop(0), refs.pop(0)
    w_ref = refs.pop(0)
    if resid:
        r_ref, ga_ref = refs.pop(0), refs.pop(0)
    o_ref = refs.pop(0)

    if norm:
        h_ref = refs.pop(0)

        @pl.when(pl.program_id(1) == 0)
        def _():
            g = g_ref[...]
            mul = 1.0 + sc_ref[...]
            add = sh_ref[...]

            def body(c, carry):
                r0 = pl.multiple_of(c * NORM_ROWS, NORM_ROWS)
                x = a_ref[pl.ds(r0, NORM_ROWS), :]
                y = x * lax.rsqrt(jnp.mean(x * x, axis=-1, keepdims=True) + EPS) * g
                h_ref[pl.ds(r0, NORM_ROWS), :] = (y * mul + add).astype(BF16)
                return carry

            lax.fori_loop(0, a_ref.shape[0] // NORM_ROWS, body, 0)

        a = h_ref[...]
    else:
        a = a_ref[...]
    acc = _dot(a, w_ref[...])
    if resid:
        acc = r_ref[...] + ga_ref[...] * acc
    o_ref[...] = acc.astype(o_ref.dtype)


def _mm_call(a, w, *, out_dtype, tm, tn, rows_per_mod, norm=None, resid=None):
    m, k = a.shape
    n = w.shape[1]
    assert m % tm == 0 and n % tn == 0 and rows_per_mod % tm == 0
    tiles_per_mod = rows_per_mod // tm
    in_specs = [pl.BlockSpec((tm, k), lambda i, j: (i, 0))]
    args = [a]
    if norm is not None:
        g, sh, sc = norm
        in_specs += [
            pl.BlockSpec((1, k), lambda i, j: (0, 0)),
            pl.BlockSpec((None, 1, k), lambda i, j: (i // tiles_per_mod, 0, 0)),
            pl.BlockSpec((None, 1, k), lambda i, j: (i // tiles_per_mod, 0, 0)),
        ]
        args += [g, sh, sc]
    in_specs.append(pl.BlockSpec((k, tn), lambda i, j: (0, j)))
    args.append(w)
    if resid is not None:
        x, ga = resid
        in_specs += [
            pl.BlockSpec((tm, tn), lambda i, j: (i, j)),
            pl.BlockSpec((None, 1, tn), lambda i, j: (i // tiles_per_mod, 0, j)),
        ]
        args += [x, ga]
    scratch = [pltpu.VMEM((tm, k), BF16)] if norm is not None else []
    return pl.pallas_call(
        functools.partial(_mm_kernel, norm=norm is not None, resid=resid is not None),
        grid=(m // tm, n // tn),
        in_specs=in_specs,
        out_specs=pl.BlockSpec((tm, tn), lambda i, j: (i, j)),
        out_shape=jax.ShapeDtypeStruct((m, n), out_dtype),
        scratch_shapes=scratch,
        compiler_params=_cparams(("parallel", "arbitrary")),
    )(*args)


def _rope_kernel(x_ref, c_ref, s_ref, o_ref, *, dr):
    o_ref[...] = _rope(x_ref[...].astype(F32), c_ref[...], s_ref[...], dr).astype(o_ref.dtype)


def _rope_call(p, col, width, cos, sin, dr, seq):
    m = p.shape[0]
    tm = 256
    tpb = seq // tm
    cb = col // width
    assert col % width == 0
    return pl.pallas_call(
        functools.partial(_rope_kernel, dr=dr),
        grid=(m // tm,),
        in_specs=[
            pl.BlockSpec((tm, width), lambda i: (i, cb)),
            pl.BlockSpec((tm, width), lambda i: (i % tpb, 0)),
            pl.BlockSpec((tm, width), lambda i: (i % tpb, 0)),
        ],
        out_specs=pl.BlockSpec((tm, width), lambda i: (i, 0)),
        out_shape=jax.ShapeDtypeStruct((m, width), BF16),
        compiler_params=_cparams(("parallel",)),
    )(p, cos, sin)


MLA_K_W = 4 * 256
MLA_V_W = 4 * 128


def _mla_kv_kernel(*refs, norm, rope):
    refs = list(refs)
    c_ref, kr_ref = refs.pop(0), refs.pop(0)
    if rope:
        cos_ref, sin_ref = refs.pop(0), refs.pop(0)
    if norm:
        g_ref = refs.pop(0)
    w_ref = refs.pop(0)
    k_ref, v_ref = refs.pop(0), refs.pop(0)
    c = c_ref[...].astype(F32)
    if norm:
        ckv_ref = refs.pop(0)
        c = c * lax.rsqrt(jnp.mean(c * c, axis=-1, keepdims=True) + EPS) * g_ref[...]
        ckv_ref[...] = c
    kv = _dot(c.astype(BF16), w_ref[...])
    kr = kr_ref[...].astype(F32)
    if rope:
        kr = _rope(kr, cos_ref[...], sin_ref[...], MLA_ROPE)
    k_ref[...] = (kv[:, :MLA_K_W] + jnp.concatenate([kr] * 4, axis=1)).astype(BF16)
    v_ref[...] = kv[:, MLA_K_W:].astype(BF16)


def _mla_kv_call(ckv_src, ckv_col, kr_src, kr_col, w_e, *, g=None, rope=None, seq=None):
    m = ckv_src.shape[0]
    tm = 256
    assert m % tm == 0 and ckv_col % 256 == 0 and kr_col % 256 == 0
    in_specs = [
        pl.BlockSpec((tm, 256), lambda i: (i, ckv_col // 256)),
        pl.BlockSpec((tm, 256), lambda i: (i, kr_col // 256)),
    ]
    args = [ckv_src, kr_src]
    if rope is not None:
        tpb = seq // tm
        in_specs += [pl.BlockSpec((tm, 256), lambda i: (i % tpb, 0))] * 2
        args += list(rope)
    if g is not None:
        in_specs.append(pl.BlockSpec((1, 256), lambda i: (0, 0)))
        args.append(g)
    in_specs.append(pl.BlockSpec(w_e.shape, lambda i: (0, 0)))
    args.append(w_e)
    out_specs = [pl.BlockSpec((tm, MLA_K_W), lambda i: (i, 0)), pl.BlockSpec((tm, MLA_V_W), lambda i: (i, 0))]
    out_shape = [jax.ShapeDtypeStruct((m, MLA_K_W), BF16), jax.ShapeDtypeStruct((m, MLA_V_W), BF16)]
    if g is not None:
        out_specs.append(pl.BlockSpec((tm, 256), lambda i: (i, 0)))
        out_shape.append(jax.ShapeDtypeStruct((m, 256), F32))
    return pl.pallas_call(
        functools.partial(_mla_kv_kernel, norm=g is not None, rope=rope is not None),
        grid=(m // tm,),
        in_specs=in_specs,
        out_specs=out_specs,
        out_shape=out_shape,
        compiler_params=_cparams(("parallel",)),
    )(*args)


LOG2E = 1.4426950408889634


def _softmax_parts(scores, scale=1.0):
    m = scores[0].max(axis=-1, keepdims=True)
    for s in scores[1:]:
        m = jnp.maximum(m, s.max(axis=-1, keepdims=True))
    ps = [jnp.exp2((s - m) * (scale * LOG2E)) for s in scores]
    l = ps[0].sum(axis=-1, keepdims=True)
    for p in ps[1:]:
        l = l + p.sum(axis=-1, keepdims=True)
    return ps, 1.0 / l


def _mha_kernel(*refs, mla, nseg, scale):
    refs = list(refs)
    qn_ref = refs.pop(0)
    qr_ref = refs.pop(0) if mla else None
    kv_refs = [(refs.pop(0), refs.pop(0)) for _ in range(nseg)]
    o_ref = refs.pop(0)
    kw = 256 if mla else LANE
    lane = _lane_iota((qn_ref.shape[0], LANE))
    for j in range(4):
        outs = []
        qn_pair = qn_ref[:, j * LANE:(j + 1) * LANE].astype(F32)
        for e in range(2):
            h = 2 * j + e
            q = _half_mask(qn_pair, e)
            if mla:
                qr = qr_ref[:, (h // 4) * LANE:(h // 4 + 1) * LANE].astype(F32)
                qr = jnp.where(lane // MLA_ROPE == h % 4, qr, 0.0)
                q = jnp.concatenate([q, qr], axis=1)
            q = q.astype(BF16)
            scores = [_dot_t(q, k_ref[:, j * kw:(j + 1) * kw].astype(BF16)) for k_ref, _ in kv_refs]
            ps, inv = _softmax_parts(scores, scale)
            o = _dot(ps[0].astype(BF16), kv_refs[0][1][:, j * LANE:(j + 1) * LANE].astype(BF16))
            for p, (_, v_ref) in zip(ps[1:], kv_refs[1:]):
                o = o + _dot(p.astype(BF16), v_ref[:, j * LANE:(j + 1) * LANE].astype(BF16))
            outs.append(o * inv)
        o_ref[:, j * LANE:(j + 1) * LANE] = jnp.where(lane < HALF, outs[0], outs[1]).astype(o_ref.dtype)


def _mha_call(qn, qn_col, qr, qr_col, segs, *, nb, tq, t, scale):
    mla = qr is not None
    kw = MLA_K_W if mla else 512
    nq = t // tq
    in_specs = [pl.BlockSpec((tq, 512), lambda b, i: (b * nq + i, qn_col // 512))]
    args = [qn]
    if mla:
        in_specs.append(pl.BlockSpec((tq, 256), lambda b, i: (b * nq + i, qr_col // 256)))
        args.append(qr)
    for k_arr, k_col, v_arr, v_col, s in segs:
        assert k_col % kw == 0 and v_col % 512 == 0
        in_specs.append(pl.BlockSpec((s, kw), functools.partial(lambda b, i, c: (b, c), c=k_col // kw)))
        in_specs.append(pl.BlockSpec((s, 512), functools.partial(lambda b, i, c: (b, c), c=v_col // 512)))
        args += [k_arr, v_arr]
    return pl.pallas_call(
        functools.partial(_mha_kernel, mla=mla, nseg=len(segs), scale=scale),
        grid=(nb, nq),
        in_specs=in_specs,
        out_specs=pl.BlockSpec((tq, 512), lambda b, i: (b * nq + i, 0)),
        out_shape=jax.ShapeDtypeStruct((nb * t, 512), BF16),
        compiler_params=_cparams(("parallel", "parallel")),
    )(*args)


def _diff_kernel(*refs, nseg, lam_init):
    refs = list(refs)
    q_ref = refs.pop(0)
    kv_refs = [(refs.pop(0), refs.pop(0)) for _ in range(nseg)]
    lp_ref, g_ref, o_ref = refs
    lp = lp_ref[...]
    lam = (jnp.exp(jnp.sum(lp[0:1] * lp[1:2], axis=-1, keepdims=True))
           - jnp.exp(jnp.sum(lp[2:3] * lp[3:4], axis=-1, keepdims=True)) + lam_init)
    scale = DIFF_QK ** -0.5
    for h in range(DIFF_HEADS):
        q_pair = q_ref[:, h * LANE:(h + 1) * LANE].astype(F32)
        probs = []
        for c in range(2):
            q = _half_mask(q_pair, c).astype(BF16)
            scores = [_dot_t(q, k_ref[:, h * LANE:(h + 1) * LANE].astype(BF16)) for k_ref, _ in kv_refs]
            ps, inv = _softmax_parts(scores, scale)
            probs.append([p * inv for p in ps])
        o = None
        for s in range(nseg):
            a = (probs[0][s] - lam * probs[1][s]).astype(BF16)
            t = _dot(a, kv_refs[s][1][:, h * LANE:(h + 1) * LANE].astype(BF16))
            o = t if o is None else o + t
        o = o * lax.rsqrt(jnp.mean(o * o, axis=-1, keepdims=True) + EPS) * g_ref[:, h * LANE:(h + 1) * LANE]
        o_ref[:, h * LANE:(h + 1) * LANE] = (o * (1.0 - lam_init)).astype(o_ref.dtype)


def _diff_call(q, q_col, segs, lp, g, *, nb, tq, t, lam_init):
    nq = t // tq
    in_specs = [pl.BlockSpec((tq, 512), lambda b, i: (b * nq + i, q_col // 512))]
    args = [q]
    for k_arr, k_col, v_arr, v_col, s in segs:
        in_specs.append(pl.BlockSpec((s, 512), functools.partial(lambda b, i, c: (b, c), c=k_col // 512)))
        in_specs.append(pl.BlockSpec((s, 512), functools.partial(lambda b, i, c: (b, c), c=v_col // 512)))
        args += [k_arr, v_arr]
    in_specs += [pl.BlockSpec((4, DIFF_QK), lambda b, i: (0, 0)), pl.BlockSpec((1, 512), lambda b, i: (0, 0))]
    args += [lp, g]
    return pl.pallas_call(
        functools.partial(_diff_kernel, nseg=len(segs), lam_init=lam_init),
        grid=(nb, nq),
        in_specs=in_specs,
        out_specs=pl.BlockSpec((tq, 512), lambda b, i: (b * nq + i, 0)),
        out_shape=jax.ShapeDtypeStruct((nb * t, 512), BF16),
        compiler_params=_cparams(("parallel", "parallel")),
    )(*args)


NA_QROWS = 4


def _na_plan(rows):
    win = min(rows, NA_QROWS + NA_ROWS)
    kh = min(NA_ROWS, rows)
    cfgs, cfg_idx, starts = [], [], []
    for blk in range(rows // NA_QROWS):
        r0 = blk * NA_QROWS
        s_blk = int(np.clip(r0 - kh // 2, 0, rows - win))
        cfg = tuple((r0 + i - s_blk, int(np.clip(r0 + i - kh // 2, 0, rows - kh)) - s_blk) for i in range(NA_QROWS))
        if cfg not in cfgs:
            cfgs.append(cfg)
        cfg_idx.append(cfgs.index(cfg))
        starts.append(s_blk)
    return win, cfgs, np.array([cfg_idx, starts], np.int32)


def _na_kernel(meta_ref, q_ref, k_ref, v_ref, kc_ref, vc_ref, b_ref, o_ref, *, win):
    start = pl.multiple_of(meta_ref[1, pl.program_id(1)] * GRID_W, GRID_W)
    n_loc = win * GRID_W
    lane = _lane_iota((q_ref.shape[0], LANE))
    for j in range(4):
        q_pair = q_ref[:, j * LANE:(j + 1) * LANE].astype(F32) * (NA_DIM ** -0.5)
        k_loc = k_ref[pl.ds(start, n_loc), j * LANE:(j + 1) * LANE].astype(BF16)
        v_loc = v_ref[pl.ds(start, n_loc), j * LANE:(j + 1) * LANE].astype(BF16)
        k_ctx = kc_ref[:, j * LANE:(j + 1) * LANE].astype(BF16)
        v_ctx = vc_ref[:, j * LANE:(j + 1) * LANE].astype(BF16)
        outs = []
        for e in range(2):
            q = _half_mask(q_pair, e).astype(BF16)
            s_loc = _dot_t(q, k_loc) + b_ref[2 * j + e]
            s_ctx = _dot_t(q, k_ctx)
            ps, inv = _softmax_parts([s_loc, s_ctx])
            o = _dot(ps[0].astype(BF16), v_loc) + _dot(ps[1].astype(BF16), v_ctx)
            outs.append(o * inv)
        o_ref[:, j * LANE:(j + 1) * LANE] = jnp.where(lane < HALF, outs[0], outs[1]).astype(o_ref.dtype)


def _na_call(p, kc, vc, rel_bias, *, nb, t, past):
    rows = t // GRID_W
    assert rows % NA_QROWS == 0 and rows >= NA_ROWS
    win, cfgs, meta = _na_plan(rows)
    bias_tab = _na_bias_table(rel_bias, cfgs, win)
    nblk = rows // NA_QROWS
    tq = NA_QROWS * GRID_W
    n_loc = win * GRID_W
    grid_spec = pltpu.PrefetchScalarGridSpec(
        num_scalar_prefetch=1,
        grid=(nb, nblk),
        in_specs=[
            pl.BlockSpec((tq, 512), lambda b, r, meta: (b * nblk + r, C_NQ // 512)),
            pl.BlockSpec((t, 512), lambda b, r, meta: (b, C_NK // 512)),
            pl.BlockSpec((t, 512), lambda b, r, meta: (b, C_NV // 512)),
            pl.BlockSpec((past, 512), lambda b, r, meta: (b, 0)),
            pl.BlockSpec((past, 512), lambda b, r, meta: (b, 0)),
            pl.BlockSpec((None, NA_HEADS, tq, n_loc), lambda b, r, meta: (meta[0, r], 0, 0, 0)),
        ],
        out_specs=pl.BlockSpec((tq, 512), lambda b, r, meta: (b * nblk + r, 0)),
    )
    return pl.pallas_call(
        functools.partial(_na_kernel, win=win),
        grid_spec=grid_spec,
        out_shape=jax.ShapeDtypeStruct((nb * t, 512), BF16),
        compiler_params=_cparams(("parallel", "arbitrary")),
    )(jnp.asarray(meta), p, p, p, kc, vc, bias_tab)


def _log_sigmoid(x):
    return jnp.minimum(x, 0.0) - jnp.log(1.0 + jnp.exp(-jnp.abs(x)))


def _ret_kernel(*refs, has_state, t):
    refs = list(refs)
    q_ref, k_ref, v_ref, rg_ref, dec_ref, gn_ref = [refs.pop(0) for _ in range(6)]
    s0_ref = refs.pop(0) if has_state else None
    o_ref = refs.pop(0)
    tq = q_ref.shape[0]
    q0 = pl.program_id(1) * tq
    lg = _log_sigmoid(dec_ref[...])
    ti = (q0 + lax.broadcasted_iota(jnp.int32, (tq, t), 0)).astype(F32)
    ui = lax.broadcasted_iota(jnp.int32, (tq, t), 1).astype(F32)
    diff = ti - ui
    tcol = (q0 + lax.broadcasted_iota(jnp.int32, (tq, 1), 0)).astype(F32)
    for h in range(RET_HEADS):
        j, e = h // 2, h % 2
        lgf = lg[h:h + 1, 0:1]
        lgb = lg[RET_HEADS + h:RET_HEADS + h + 1, 0:1]
        qm = _half_mask(q_ref[:, j * LANE:(j + 1) * LANE].astype(F32), e)
        q = qm.astype(BF16)
        a = _dot_t((qm * (RET_QK ** -0.5)).astype(BF16), k_ref[:, j * LANE:(j + 1) * LANE].astype(BF16))
        dmat = jnp.exp2(diff * jnp.where(diff >= 0, lgf * LOG2E, -lgb * LOG2E))
        y = _dot((a * dmat).astype(BF16), v_ref[:, h * LANE:(h + 1) * LANE].astype(BF16))
        if has_state:
            cf = _dot(q, s0_ref[0, j].astype(BF16))
            cb = _dot(q, s0_ref[1, j].astype(BF16))
            y = y + cf * jnp.exp((tcol + 1.0) * lgf) + cb * jnp.exp((float(t) - tcol) * lgb)
        mu = jnp.mean(y, axis=-1, keepdims=True)
        yc = y - mu
        var = jnp.mean(yc * yc, axis=-1, keepdims=True)
        yn = yc * lax.rsqrt(var + EPS) * gn_ref[:, h * LANE:(h + 1) * LANE]
        o_ref[:, h * LANE:(h + 1) * LANE] = (_silu(rg_ref[:, h * LANE:(h + 1) * LANE].astype(F32)) * yn).astype(o_ref.dtype)


def _ret_call(p, dec, gn, s0, *, nb, tq, t):
    nq = t // tq
    in_specs = [
        pl.BlockSpec((tq, 256), lambda b, i: (b * nq + i, C_RQ // 256)),
        pl.BlockSpec((t, 256), lambda b, i: (b, C_RK // 256)),
        pl.BlockSpec((t, 512), lambda b, i: (b, C_RV // 512)),
        pl.BlockSpec((tq, 512), lambda b, i: (b * nq + i, C_RG // 512)),
        pl.BlockSpec((8, LANE), lambda b, i: (0, 0)),
        pl.BlockSpec((1, 512), lambda b, i: (0, 0)),
    ]
    args = [p, p, p, p, dec, gn]
    if s0 is not None:
        in_specs.append(pl.BlockSpec((None, 2, 2, LANE, LANE), lambda b, i: (b, 0, 0, 0, 0)))
        args.append(s0)
    return pl.pallas_call(
        functools.partial(_ret_kernel, has_state=s0 is not None, t=t),
        grid=(nb, nq),
        in_specs=in_specs,
        out_specs=pl.BlockSpec((tq, 512), lambda b, i: (b * nq + i, 0)),
        out_shape=jax.ShapeDtypeStruct((nb * t, 512), BF16),
        compiler_params=_cparams(("parallel", "parallel")),
    )(*args)


def _ret_state_kernel(k_ref, v_ref, dec_ref, o_ref, *, t):
    lg = _log_sigmoid(dec_ref[...])
    tcol = lax.broadcasted_iota(jnp.int32, (t, 1), 0).astype(F32)
    lane = _lane_iota((t, LANE))
    for j in range(2):
        k_pair = k_ref[:, j * LANE:(j + 1) * LANE].astype(F32) * (RET_QK ** -0.5)
        for d in range(2):
            expo = (float(t) - 1.0 - tcol) if d == 0 else tcol
            w0 = jnp.exp(expo * lg[d * RET_HEADS + 2 * j:d * RET_HEADS + 2 * j + 1, 0:1])
            w1 = jnp.exp(expo * lg[d * RET_HEADS + 2 * j + 1:d * RET_HEADS + 2 * j + 2, 0:1])
            kd = (k_pair * jnp.where(lane < HALF, w0, w1)).T.astype(BF16)
            for e in range(2):
                h = 2 * j + e
                sfull = _dot(kd, v_ref[:, h * LANE:(h + 1) * LANE].astype(BF16))
                o_ref[d, h] = sfull[e * HALF:(e + 1) * HALF, :]


def _ret_state_call(p, dec, *, nb, t):
    return pl.pallas_call(
        functools.partial(_ret_state_kernel, t=t),
        grid=(nb,),
        in_specs=[
            pl.BlockSpec((t, 256), lambda b: (b, C_RK // 256)),
            pl.BlockSpec((t, 512), lambda b: (b, C_RV // 512)),
            pl.BlockSpec((8, LANE), lambda b: (0, 0)),
        ],
        out_specs=pl.BlockSpec((None, 2, RET_HEADS, RET_QK, RET_V), lambda b: (b, 0, 0, 0, 0)),
        out_shape=jax.ShapeDtypeStruct((nb, 2, RET_HEADS, RET_QK, RET_V), F32),
        compiler_params=_cparams(("parallel",)),
    )(p, p, dec)


def _merge_kernel(ya_ref, yb_ref, yc_ref, yd_ref, g0_ref, g1_ref, g2_ref, g3_ref, w_ref, o_ref):
    ys = (ya_ref, yb_ref, yc_ref, yd_ref)
    gs = (g0_ref, g1_ref, g2_ref, g3_ref)
    m = None
    for i in range(N_BRANCH):
        t = _sigmoid(gs[i][...].astype(F32)) * _dot(ys[i][...], w_ref[i])
        m = t if m is None else m + t
    o_ref[...] = m.astype(o_ref.dtype)


def _merge_call(ys, p, w_branch):
    m = p.shape[0]
    tm, tn = min(1024, m), 512
    gate_specs = [
        pl.BlockSpec((tm, tn), functools.partial(lambda r, c, i: (r, (C_GATES + i * D_MODEL) // tn + c), i=i))
        for i in range(N_BRANCH)
    ]
    return pl.pallas_call(
        _merge_kernel,
        grid=(m // tm, D_MODEL // tn),
        in_specs=[pl.BlockSpec((tm, BRANCH_W), lambda r, c: (r, 0))] * N_BRANCH + gate_specs
        + [pl.BlockSpec((N_BRANCH, BRANCH_W, tn), lambda r, c: (0, 0, c))],
        out_specs=pl.BlockSpec((tm, tn), lambda r, c: (r, c)),
        out_shape=jax.ShapeDtypeStruct((m, D_MODEL), BF16),
        compiler_params=_cparams(("parallel", "parallel")),
    )(*ys, p, p, p, p, w_branch)


HALO = 16


SUB = 8


def _conv_kernel(ua_ref, uap_ref, uan_ref, ub_ref, ubp_ref, ubn_ref, wa_ref, wb_ref, ba_ref, bb_ref, o_ref, *, tiles_per_seq):
    i = pl.program_id(0)
    tm, tc = ua_ref.shape
    has_prev = ((i % tiles_per_seq) != 0).astype(F32)
    has_next = ((i % tiles_per_seq) != tiles_per_seq - 1).astype(F32)
    sub = lax.broadcasted_iota(jnp.int32, (tm // SUB, SUB, tc), 1)

    def conv(u_ref, up_ref, un_ref, w_ref, b_ref):
        u = u_ref[...].astype(F32).reshape(tm // SUB, SUB, tc)
        prev = jnp.broadcast_to((up_ref[HALO - 1:HALO, :].astype(F32) * has_prev)[None], (1, SUB, tc))
        nxt = jnp.broadcast_to((un_ref[0:1, :].astype(F32) * has_next)[None], (1, SUB, tc))
        r_dn = pltpu.roll(u, 1, 1)
        r_up = pltpu.roll(u, SUB - 1, 1)
        um1 = jnp.where(sub == 0, jnp.concatenate([prev, r_dn[:-1]], axis=0), r_dn)
        up1 = jnp.where(sub == SUB - 1, jnp.concatenate([r_up[1:], nxt], axis=0), r_up)
        w = w_ref[...]
        out = um1 * w[0:1][None] + u * w[1:2][None] + up1 * w[2:3][None] + b_ref[...][None]
        return out.reshape(tm, tc)

    a = conv(ua_ref, uap_ref, uan_ref, wa_ref, ba_ref)
    b = conv(ub_ref, ubp_ref, ubn_ref, wb_ref, bb_ref)
    o_ref[...] = (_silu(a) * b).astype(o_ref.dtype)


def _conv_call(u, conv_w, conv_b, *, seq):
    m = u.shape[0]
    tm, tc = min(seq, 512), 1408
    assert seq % tm == 0 and D_FF % tc == 0
    nb_half = D_FF // tc
    last_halo = m // HALO - 1
    r = tm // HALO

    def main(off):
        return pl.BlockSpec((tm, tc), lambda i, c: (i, c + off))

    def prev(off):
        return pl.BlockSpec((HALO, tc), lambda i, c: (jnp.maximum(i * r - 1, 0), c + off))

    def nxt(off):
        return pl.BlockSpec((HALO, tc), lambda i, c: (jnp.minimum((i + 1) * r, last_halo), c + off))

    def vec(rows, off):
        return pl.BlockSpec((rows, tc), lambda i, c: (0, c + off))

    cb = conv_b.reshape(1, 2 * D_FF)
    return pl.pallas_call(
        functools.partial(_conv_kernel, tiles_per_seq=seq // tm),
        grid=(m // tm, nb_half),
        in_specs=[main(0), prev(0), nxt(0), main(nb_half), prev(nb_half), nxt(nb_half),
                  vec(CONV_W, 0), vec(CONV_W, nb_half), vec(1, 0), vec(1, nb_half)],
        out_specs=pl.BlockSpec((tm, tc), lambda i, c: (i, c)),
        out_shape=jax.ShapeDtypeStruct((m, D_FF), BF16),
        compiler_params=_cparams(("parallel", "parallel")),
    )(u, u, u, u, u, u, conv_w, conv_w, cb, cb)


def _final_norm_kernel(x_ref, g_ref, o_ref):
    x = x_ref[...]
    o_ref[...] = x * lax.rsqrt(jnp.mean(x * x, axis=-1, keepdims=True) + EPS) * g_ref[...]


def _final_norm_call(x, g):
    m, d = x.shape
    tm = 256
    return pl.pallas_call(
        _final_norm_kernel,
        grid=(m // tm,),
        in_specs=[pl.BlockSpec((tm, d), lambda i: (i, 0)), pl.BlockSpec((1, d), lambda i: (0, 0))],
        out_specs=pl.BlockSpec((tm, d), lambda i: (i, 0)),
        out_shape=jax.ShapeDtypeStruct((m, d), F32),
        compiler_params=_cparams(("parallel",)),
    )(x, g.reshape(1, d))


def _reorder_w_in(w):
    d = w.shape[0]
    w = w.astype(BF16)
    mq = w[:, :768].reshape(d, MLA_HEADS, MLA_NOPE + MLA_ROPE)
    qn = mq[:, :, :MLA_NOPE].reshape(d, 512)
    qr = mq[:, :, MLA_NOPE:].reshape(d, 256)
    ckv = w[:, 768:1024]
    mkr = w[:, 1024:1056]
    rest = w[:, 1056:5664]
    gates = w[:, 5664:]
    krblk = jnp.concatenate([jnp.zeros((d, LANE), w.dtype)] + [mkr] * 4, axis=1)
    pad = jnp.zeros((d, C_GATES - C_KR - 256), w.dtype)
    return jnp.concatenate([qn, qr, ckv, rest, krblk, pad, gates], axis=1).astype(BF16)


def _reorder_w_ukv(w):
    r = w.shape[0]
    w3 = w.reshape(r, MLA_HEADS, MLA_NOPE + MLA_V)
    kn = w3[:, :, :MLA_NOPE].reshape(r, 4, LANE)
    vv = w3[:, :, MLA_NOPE:].reshape(r, 512)
    knz = jnp.concatenate([kn, jnp.zeros_like(kn)], axis=2).reshape(r, MLA_K_W)
    return jnp.concatenate([knz, vv], axis=1).astype(BF16)


def _rope_tables(t, dr, reps, lead_zero=0):
    nf = dr // 4
    pos = jnp.arange(t)
    pos = jnp.stack([pos // GRID_W, pos % GRID_W], axis=-1).astype(F32)
    inv = ROPE_BASE ** (-jnp.arange(nf, dtype=F32) / nf)
    ang = pos[:, :, None] * inv
    cos = jnp.cos(ang)
    sin = jnp.sin(ang)
    c = jnp.tile(jnp.concatenate([cos, cos], axis=2).reshape(t, dr), (1, reps))
    s = jnp.tile(jnp.concatenate([-sin, sin], axis=2).reshape(t, dr), (1, reps))
    if lead_zero:
        z = jnp.zeros((t, lead_zero), F32)
        c = jnp.concatenate([z, c], axis=1)
        s = jnp.concatenate([z, s], axis=1)
    return c, s


def _na_bias_table(rel_bias, cfgs, win):
    v = rel_bias.astype(F32)
    pad = GRID_W - NA_COLS
    ext = jnp.concatenate([jnp.repeat(v[..., :1], pad, axis=-1), v, jnp.repeat(v[..., -1:], pad, axis=-1)], axis=-1)
    col_t = jnp.stack([ext[..., GRID_W - 1 - q:2 * GRID_W - 1 - q] for q in range(GRID_W)], axis=2)
    qc = np.arange(GRID_W)[:, None]
    kc = np.arange(GRID_W)[None, :]
    c0 = np.clip(qc - NA_COLS // 2, 0, GRID_W - NA_COLS)
    col_ok = (kc >= c0) & (kc < c0 + NA_COLS)
    col_t = jnp.where(col_ok, col_t, NEG_INF)
    neg = jnp.full((NA_HEADS, GRID_W, GRID_W), NEG_INF, F32)
    kh = min(NA_ROWS, win)
    tabs = []
    for cfg in cfgs:
        per_row = []
        for delta, off in cfg:
            blocks = [col_t[:, kk - delta + NA_ROWS - 1] if off <= kk < off + kh else neg for kk in range(win)]
            per_row.append(jnp.concatenate(blocks, axis=-1))
        tabs.append(jnp.concatenate(per_row, axis=1))
    return jnp.stack(tabs)


def kernel(x_prompt, x_sample, cache_mla_ckv, cache_mla_krope, cache_diff_k, cache_diff_v, cache_na_k, cache_na_v, state_ret, c, c_ctx, w_ada, b_ada, g_attn, g_ffn, w_in, mla_kv_norm, w_mla_ukv, diff_lambda, diff_norm, na_rel_bias, ret_decay, ret_norm, w_branch, w_o, w_up, conv_w, conv_b, w_down, g_final):
    nbc, seq, d = x_prompt.shape
    nbl, t, _ = x_sample.shape
    past = cache_mla_ckv.shape[2]
    depth = w_in.shape[0]
    mc, ml = nbc * seq, nbl * t

    xc = x_prompt.reshape(mc, d)
    xl = x_sample.reshape(ml, d)

    mod_rows = 16
    cond = jnp.concatenate([c_ctx[None], c, jnp.zeros((mod_rows - 1 - nbl, d), F32)], axis=0)
    mods = _ada_call(cond, w_ada, b_ada).reshape(depth, mod_rows, 6, d)

    cos32q, sin32q = _rope_tables(t, MLA_ROPE, 8)
    cos32k, sin32k = _rope_tables(t, MLA_ROPE, 4, lead_zero=LANE)
    cos64, sin64 = _rope_tables(t, DIFF_QK, 16)

    tm_c = min(1024, mc)
    tm_l = min(1024, t)
    states = []
    for l in range(depth):
        lam_init = 0.8 - 0.6 * math.exp(-0.3 * l)
        w_in_r = _reorder_w_in(w_in[l])
        w_ukv_e = _reorder_w_ukv(w_mla_ukv[l])
        wb = w_branch[l].astype(BF16)
        wo = w_o[l].astype(BF16)
        wup = w_up[l].astype(BF16)
        wdn = w_down[l].astype(BF16)
        kvn = mla_kv_norm[l].reshape(1, MLA_KV_RANK)
        dec = jnp.broadcast_to(ret_decay[l].reshape(2 * RET_HEADS, 1), (2 * RET_HEADS, LANE))
        gn = ret_norm[l].reshape(1, 512)
        dnorm = diff_norm[l].reshape(1, 512)
        g_a = g_attn[l].reshape(1, d)
        g_f = g_ffn[l].reshape(1, d)

        def mod(i):
            return mods[l, 0:1, i][:, None, :], mods[l, 1:1 + nbl, i][:, None, :]

        sh1, sc1, ga1, sh2, sc2, ga2 = [mod(i) for i in range(6)]

        pc = _mm_call(xc, w_in_r, out_dtype=F32, tm=tm_c, tn=1024, rows_per_mod=mc, norm=(g_a, sh1[0], sc1[0]))
        kc_mla, vc_mla, ckv_state = _mla_kv_call(pc, C_CKV, pc, C_KR, w_ukv_e, g=kvn)
        ya = _mha_call(pc, C_QN, pc, C_QR, [(kc_mla, 0, vc_mla, 0, seq)], nb=nbc, tq=seq, t=seq,
                       scale=(MLA_NOPE + MLA_ROPE) ** -0.5)
        yb = _diff_call(pc, C_DQ, [(pc, C_DK, pc, C_DV, seq)], diff_lambda[l], dnorm, nb=nbc, tq=seq, t=seq,
                        lam_init=lam_init)
        yc = _mha_call(pc, C_NQ, None, 0, [(pc, C_NK, pc, C_NV, seq)], nb=nbc, tq=seq, t=seq, scale=NA_DIM ** -0.5)
        yd = _ret_call(pc, dec, gn, None, nb=nbc, tq=seq, t=seq)
        sret = _ret_state_call(pc, dec, nb=nbc, t=seq)
        states.append((
            ckv_state.reshape(nbc, seq, MLA_KV_RANK),
            pc[:, C_KR + LANE:C_KR + LANE + MLA_ROPE].reshape(nbc, seq, MLA_ROPE),
            pc[:, C_DK:C_DK + 512].reshape(nbc, seq, DIFF_HEADS, 2, DIFF_QK),
            pc[:, C_DV:C_DV + 512].reshape(nbc, seq, DIFF_HEADS, DIFF_V),
            pc[:, C_NK:C_NK + 512].reshape(nbc, seq, NA_HEADS, NA_DIM),
            pc[:, C_NV:C_NV + 512].reshape(nbc, seq, NA_HEADS, NA_DIM),
            sret,
        ))
        mrg = _merge_call((ya, yb, yc, yd), pc, wb)
        xc = _mm_call(mrg, wo, out_dtype=F32, tm=tm_c, tn=1024, rows_per_mod=mc, resid=(xc, ga1[0]))
        u = _mm_call(xc, wup, out_dtype=BF16, tm=tm_c, tn=1024, rows_per_mod=mc, norm=(g_f, sh2[0], sc2[0]))
        act = _conv_call(u, conv_w[l], conv_b[l], seq=seq)
        xc = _mm_call(act, wdn, out_dtype=F32, tm=512, tn=1024, rows_per_mod=mc, resid=(xc, ga2[0]))

        plat = _mm_call(xl, w_in_r, out_dtype=BF16, tm=tm_l, tn=1024, rows_per_mod=t, norm=(g_a, sh1[1], sc1[1]))
        kl_mla, vl_mla, _ = _mla_kv_call(plat, C_CKV, plat, C_KR, w_ukv_e, g=kvn, rope=(cos32k, sin32k), seq=t)
        kr_c = cache_mla_krope[:, l].reshape(nbl * past, MLA_ROPE)
        krblk_c = jnp.concatenate([jnp.zeros((nbl * past, LANE), F32)] + [kr_c] * 4, axis=1)
        kp_mla, vp_mla = _mla_kv_call(cache_mla_ckv[:, l].reshape(nbl * past, MLA_KV_RANK), 0, krblk_c, 0, w_ukv_e)
        qr_l = _rope_call(plat, C_QR, 256, cos32q, sin32q, MLA_ROPE, t)
        dqk_l = _rope_call(plat, C_DQ, 1024, cos64, sin64, DIFF_QK, t)
        tq = min(512, t)
        ya = _mha_call(plat, C_QN, qr_l, 0, [(kl_mla, 0, vl_mla, 0, t), (kp_mla, 0, vp_mla, 0, past)],
                       nb=nbl, tq=tq, t=t, scale=(MLA_NOPE + MLA_ROPE) ** -0.5)
        dk_c = cache_diff_k[:, l].reshape(nbl * past, 512)
        dv_c = cache_diff_v[:, l].reshape(nbl * past, 512)
        yb = _diff_call(dqk_l, 0, [(dqk_l, 512, plat, C_DV, t), (dk_c, 0, dv_c, 0, past)], diff_lambda[l], dnorm,
                        nb=nbl, tq=tq, t=t, lam_init=lam_init)
        yc = _na_call(plat, cache_na_k[:, l].reshape(nbl * past, 512), cache_na_v[:, l].reshape(nbl * past, 512),
                      na_rel_bias[l], nb=nbl, t=t, past=past)
        s0 = state_ret[:, l].astype(F32).reshape(nbl, 2, 2, LANE, LANE)
        yd = _ret_call(plat, dec, gn, s0, nb=nbl, tq=tq, t=t)
        mrg = _merge_call((ya, yb, yc, yd), plat, wb)
        xl = _mm_call(mrg, wo, out_dtype=F32, tm=tm_l, tn=1024, rows_per_mod=t, resid=(xl, ga1[1]))
        u = _mm_call(xl, wup, out_dtype=BF16, tm=tm_l, tn=1024, rows_per_mod=t, norm=(g_f, sh2[1], sc2[1]))
        act = _conv_call(u, conv_w[l], conv_b[l], seq=t)
        xl = _mm_call(act, wdn, out_dtype=F32, tm=512, tn=1024, rows_per_mod=t, resid=(xl, ga2[1]))

    y_prompt = _final_norm_call(xc, g_final).reshape(nbc, seq, d)
    y_sample = _final_norm_call(xl, g_final).reshape(nbl, t, d)
    outs = [jnp.stack([st[i] for st in states], axis=1) for i in range(7)]
    return (y_prompt, y_sample, *outs)
```

```python
import functools
import math

import jax
import jax.numpy as jnp
import numpy as np
from jax import lax
from jax.experimental import pallas as pl
from jax.experimental.pallas import tpu as pltpu

F32 = jnp.float32
BF16 = jnp.bfloat16

D_MODEL = 2048
GRID_W = 64
ROPE_BASE = 10000.0
EPS = 1e-6
NEG_INF = -1e30
N_BRANCH = 4
BRANCH_W = 512
MLA_HEADS = 8
MLA_NOPE = 64
MLA_ROPE = 32
MLA_V = 64
MLA_KV_RANK = 256
DIFF_QK = 64
DIFF_V = 128
DIFF_HEADS = 4
NA_DIM = 64
NA_HEADS = 8
NA_ROWS = 8
NA_COLS = 16
RET_V = 128
RET_HEADS = 4
RET_QK = 64
D_FF = 5632
CONV_W = 3

LANE = 128
HALF = 64

C_QN = 0
C_QR = 512
C_CKV = 768
C_DQ = 1024
C_DK = 1536
C_DV = 2048
C_NQ = 2560
C_NK = 3072
C_NV = 3584
C_RQ = 4096
C_RK = 4352
C_RV = 4608
C_RG = 5120
C_KR = 5632
C_GATES = 6144
P_WIDTH = C_GATES + N_BRANCH * D_MODEL

VMEM_LIMIT = 56 * 1024 * 1024


def _cparams(sem):
    return pltpu.CompilerParams(dimension_semantics=sem, vmem_limit_bytes=VMEM_LIMIT)


def _dot(a, b):
    return jnp.dot(a, b, preferred_element_type=F32)


def _dot_t(a, b):
    return lax.dot_general(a, b, (((1,), (1,)), ((), ())), preferred_element_type=F32)


def _sigmoid(x):
    return 0.5 * jnp.tanh(0.5 * x) + 0.5


def _silu(x):
    h = 0.5 * x
    return h * (jnp.tanh(h) + 1.0)


def _lane_iota(shape):
    return lax.broadcasted_iota(jnp.int32, shape, len(shape) - 1)


def _half_mask(x, e):
    lane = _lane_iota(x.shape)
    keep = (lane < HALF) if e == 0 else (lane >= HALF)
    return jnp.where(keep, x, 0.0)


def _rope(x, cos, sin, dr):
    w = x.shape[-1]
    nf = dr // 4
    lane = _lane_iota(x.shape)
    first = (lane % (2 * nf)) < nf
    rot = jnp.where(first, pltpu.roll(x, w - nf, 1), pltpu.roll(x, nf, 1))
    return x * cos + rot * sin


def _ada_kernel(c_ref, w_ref, b_ref, o_ref):
    a = _silu(c_ref[...]).astype(BF16)
    o_ref[...] = _dot(a, w_ref[...].astype(BF16)) + b_ref[...]


def _ada_call(cond, w_ada, b_ada):
    depth, d, n = w_ada.shape
    rows = cond.shape[0]
    tn = 1024
    return pl.pallas_call(
        _ada_kernel,
        grid=(depth, n // tn),
        in_specs=[
            pl.BlockSpec((rows, d), lambda l, j: (0, 0)),
            pl.BlockSpec((None, d, tn), lambda l, j: (l, 0, j)),
            pl.BlockSpec((None, 1, tn), lambda l, j: (l, 0, j)),
        ],
        out_specs=pl.BlockSpec((None, rows, tn), lambda l, j: (l, 0, j)),
        out_shape=jax.ShapeDtypeStruct((depth, rows, n), F32),
        compiler_params=_cparams(("parallel", "parallel")),
    )(cond, w_ada, b_ada.reshape(depth, 1, n))


NORM_ROWS = 128


def _mm_kernel(*refs, norm, resid):
    refs = list(refs)
    a_ref = refs.pop(0)
    if norm:
        g_ref, sh_ref, sc_ref = refs.pop(0), refs.pop(0), refs.pop(0)
    w_ref = refs.pop(0)
    if resid:
        r_ref, ga_ref = refs.pop(0), refs.pop(0)
    o_ref = refs.pop(0)

    if norm:
        h_ref = refs.pop(0)

        @pl.when(pl.program_id(1) == 0)
        def _():
            g = g_ref[...]
            mul = 1.0 + sc_ref[...]
            add = sh_ref[...]

            def body(c, carry):
                r0 = pl.multiple_of(c * NORM_ROWS, NORM_ROWS)
                x = a_ref[pl.ds(r0, NORM_ROWS), :]
                y = x * lax.rsqrt(jnp.mean(x * x, axis=-1, keepdims=True) + EPS) * g
                h_ref[pl.ds(r0, NORM_ROWS), :] = (y * mul + add).astype(BF16)
                return carry

            lax.fori_loop(0, a_ref.shape[0] // NORM_ROWS, body, 0)

        a = h_ref[...]
    else:
        a = a_ref[...]
    acc = _dot(a, w_ref[...])
    if resid:
        acc = r_ref[...] + ga_ref[...] * acc
    o_ref[...] = acc.astype(o_ref.dtype)


def _mm_call(a, w, *, out_dtype, tm, tn, rows_per_mod, norm=None, resid=None):
    m, k = a.shape
    n = w.shape[1]
    assert m % tm == 0 and n % tn == 0 and rows_per_mod % tm == 0
    tiles_per_mod = rows_per_mod // tm
    in_specs = [pl.BlockSpec((tm, k), lambda i, j: (i, 0))]
    args = [a]
    if norm is not None:
        g, sh, sc = norm
        in_specs += [
            pl.BlockSpec((1, k), lambda i, j: (0, 0)),
            pl.BlockSpec((None, 1, k), lambda i, j: (i // tiles_per_mod, 0, 0)),
            pl.BlockSpec((None, 1, k), lambda i, j: (i // tiles_per_mod, 0, 0)),
        ]
        args += [g, sh, sc]
    in_specs.append(pl.BlockSpec((k, tn), lambda i, j: (0, j)))
    args.append(w)
    if resid is not None:
        x, ga = resid
        in_specs += [
            pl.BlockSpec((tm, tn), lambda i, j: (i, j)),
            pl.BlockSpec((None, 1, tn), lambda i, j: (i // tiles_per_mod, 0, j)),
        ]
        args += [x, ga]
    scratch = [pltpu.VMEM((tm, k), BF16)] if norm is not None else []
    return pl.pallas_call(
        functools.partial(_mm_kernel, norm=norm is not None, resid=resid is not None),
        grid=(m // tm, n // tn),
        in_specs=in_specs,
        out_specs=pl.BlockSpec((tm, tn), lambda i, j: (i, j)),
        out_shape=jax.ShapeDtypeStruct((m, n), out_dtype),
        scratch_shapes=scratch,
        compiler_params=_cparams(("parallel", "arbitrary")),
    )(*args)


def _rope_kernel(x_ref, c_ref, s_ref, o_ref, *, dr):
    o_ref[...] = _rope(x_ref[...].astype(F32), c_ref[...], s_ref[...], dr).astype(o_ref.dtype)


def _rope_call(p, col, width, cos, sin, dr, seq):
    m = p.shape[0]
    tm = 256
    tpb = seq // tm
    cb = col // width
    assert col % width == 0
    return pl.pallas_call(
        functools.partial(_rope_kernel, dr=dr),
        grid=(m // tm,),
        in_specs=[
            pl.BlockSpec((tm, width), lambda i: (i, cb)),
            pl.BlockSpec((tm, width), lambda i: (i % tpb, 0)),
            pl.BlockSpec((tm, width), lambda i: (i % tpb, 0)),
        ],
        out_specs=pl.BlockSpec((tm, width), lambda i: (i, 0)),
        out_shape=jax.ShapeDtypeStruct((m, width), BF16),
        compiler_params=_cparams(("parallel",)),
    )(p, cos, sin)


MLA_K_W = 4 * 256
MLA_V_W = 4 * 128


def _mla_kv_kernel(*refs, norm, rope):
    refs = list(refs)
    c_ref, kr_ref = refs.pop(0), refs.pop(0)
    if rope:
        cos_ref, sin_ref = refs.pop(0), refs.pop(0)
    if norm:
        g_ref = refs.pop(0)
    w_ref = refs.pop(0)
    k_ref, v_ref = refs.pop(0), refs.pop(0)
    c = c_ref[...].astype(F32)
    if norm:
        ckv_ref = refs.pop(0)
        c = c * lax.rsqrt(jnp.mean(c * c, axis=-1, keepdims=True) + EPS) * g_ref[...]
        ckv_ref[...] = c
    kv = _dot(c.astype(BF16), w_ref[...])
    kr = kr_ref[...].astype(F32)
    if rope:
        kr = _rope(kr, cos_ref[...], sin_ref[...], MLA_ROPE)
    k_ref[...] = (kv[:, :MLA_K_W] + jnp.concatenate([kr] * 4, axis=1)).astype(BF16)
    v_ref[...] = kv[:, MLA_K_W:].astype(BF16)


def _mla_kv_call(ckv_src, ckv_col, kr_src, kr_col, w_e, *, g=None, rope=None, seq=None):
    m = ckv_src.shape[0]
    tm = 256
    assert m % tm == 0 and ckv_col % 256 == 0 and kr_col % 256 == 0
    in_specs = [
        pl.BlockSpec((tm, 256), lambda i: (i, ckv_col // 256)),
        pl.BlockSpec((tm, 256), lambda i: (i, kr_col // 256)),
    ]
    args = [ckv_src, kr_src]
    if rope is not None:
        tpb = seq // tm
        in_specs += [pl.BlockSpec((tm, 256), lambda i: (i % tpb, 0))] * 2
        args += list(rope)
    if g is not None:
        in_specs.append(pl.BlockSpec((1, 256), lambda i: (0, 0)))
        args.append(g)
    in_specs.append(pl.BlockSpec(w_e.shape, lambda i: (0, 0)))
    args.append(w_e)
    out_specs = [pl.BlockSpec((tm, MLA_K_W), lambda i: (i, 0)), pl.BlockSpec((tm, MLA_V_W), lambda i: (i, 0))]
    out_shape = [jax.ShapeDtypeStruct((m, MLA_K_W), BF16), jax.ShapeDtypeStruct((m, MLA_V_W), BF16)]
    if g is not None:
        out_specs.append(pl.BlockSpec((tm, 256), lambda i: (i, 0)))
        out_shape.append(jax.ShapeDtypeStruct((m, 256), F32))
    return pl.pallas_call(
        functools.partial(_mla_kv_kernel, norm=g is not None, rope=rope is not None),
        grid=(m // tm,),
        in_specs=in_specs,
        out_specs=out_specs,
        out_shape=out_shape,
        compiler_params=_cparams(("parallel",)),
    )(*args)


LOG2E = 1.4426950408889634


def _softmax_parts(scores, scale=1.0):
    m = scores[0].max(axis=-1, keepdims=True)
    for s in scores[1:]:
        m = jnp.maximum(m, s.max(axis=-1, keepdims=True))
    ps = [jnp.exp2((s - m) * (scale * LOG2E)) for s in scores]
    l = ps[0].sum(axis=-1, keepdims=True)
    for p in ps[1:]:
        l = l + p.sum(axis=-1, keepdims=True)
    return ps, 1.0 / l


def _mha_kernel(*refs, mla, nseg, scale):
    refs = list(refs)
    qn_ref = refs.pop(0)
    qr_ref = refs.pop(0) if mla else None
    kv_refs = [(refs.pop(0), refs.pop(0)) for _ in range(nseg)]
    o_ref = refs.pop(0)
    kw = 256 if mla else LANE
    lane = _lane_iota((qn_ref.shape[0], LANE))
    for j in range(4):
        outs = []
        qn_pair = qn_ref[:, j * LANE:(j + 1) * LANE].astype(F32)
        for e in range(2):
            h = 2 * j + e
            q = _half_mask(qn_pair, e)
            if mla:
                qr = qr_ref[:, (h // 4) * LANE:(h // 4 + 1) * LANE].astype(F32)
                qr = jnp.where(lane // MLA_ROPE == h % 4, qr, 0.0)
                q = jnp.concatenate([q, qr], axis=1)
            q = q.astype(BF16)
            scores = [_dot_t(q, k_ref[:, j * kw:(j + 1) * kw].astype(BF16)) for k_ref, _ in kv_refs]
            ps, inv = _softmax_parts(scores, scale)
            o = _dot(ps[0].astype(BF16), kv_refs[0][1][:, j * LANE:(j + 1) * LANE].astype(BF16))
            for p, (_, v_ref) in zip(ps[1:], kv_refs[1:]):
                o = o + _dot(p.astype(BF16), v_ref[:, j * LANE:(j + 1) * LANE].astype(BF16))
            outs.append(o * inv)
        o_ref[:, j * LANE:(j + 1) * LANE] = jnp.where(lane < HALF, outs[0], outs[1]).astype(o_ref.dtype)


def _mha_call(qn, qn_col, qr, qr_col, segs, *, nb, tq, t, scale):
    mla = qr is not None
    kw = MLA_K_W if mla else 512
    nq = t // tq
    in_specs = [pl.BlockSpec((tq, 512), lambda b, i: (b * nq + i, qn_col // 512))]
    args = [qn]
    if mla:
        in_specs.append(pl.BlockSpec((tq, 256), lambda b, i: (b * nq + i, qr_col // 256)))
        args.append(qr)
    for k_arr, k_col, v_arr, v_col, s in segs:
        assert k_col % kw == 0 and v_col % 512 == 0
        in_specs.append(pl.BlockSpec((s, kw), functools.partial(lambda b, i, c: (b, c), c=k_col // kw)))
        in_specs.append(pl.BlockSpec((s, 512), functools.partial(lambda b, i, c: (b, c), c=v_col // 512)))
        args += [k_arr, v_arr]
    return pl.pallas_call(
        functools.partial(_mha_kernel, mla=mla, nseg=len(segs), scale=scale),
        grid=(nb, nq),
        in_specs=in_specs,
        out_specs=pl.BlockSpec((tq, 512), lambda b, i: (b * nq + i, 0)),
        out_shape=jax.ShapeDtypeStruct((nb * t, 512), BF16),
        compiler_params=_cparams(("parallel", "parallel")),
    )(*args)


def _diff_kernel(*refs, nseg, lam_init):
    refs = list(refs)
    q_ref = refs.pop(0)
    kv_refs = [(refs.pop(0), refs.pop(0)) for _ in range(nseg)]
    lp_ref, g_ref, o_ref = refs
    lp = lp_ref[...]
    lam = (jnp.exp(jnp.sum(lp[0:1] * lp[1:2], axis=-1, keepdims=True))
           - jnp.exp(jnp.sum(lp[2:3] * lp[3:4], axis=-1, keepdims=True)) + lam_init)
    scale = DIFF_QK ** -0.5
    for h in range(DIFF_HEADS):
        q_pair = q_ref[:, h * LANE:(h + 1) * LANE].astype(F32)
        probs = []
        for c in range(2):
            q = _half_mask(q_pair, c).astype(BF16)
            scores = [_dot_t(q, k_ref[:, h * LANE:(h + 1) * LANE].astype(BF16)) for k_ref, _ in kv_refs]
            probs.append(_softmax_parts(scores, scale))
        (ps0, inv0), (ps1, inv1) = probs
        c1 = lam * inv1
        o = None
        for s in range(nseg):
            a = (ps0[s] * inv0 - ps1[s] * c1).astype(BF16)
            t = _dot(a, kv_refs[s][1][:, h * LANE:(h + 1) * LANE].astype(BF16))
            o = t if o is None else o + t
        o = o * lax.rsqrt(jnp.mean(o * o, axis=-1, keepdims=True) + EPS) * g_ref[:, h * LANE:(h + 1) * LANE]
        o_ref[:, h * LANE:(h + 1) * LANE] = (o * (1.0 - lam_init)).astype(o_ref.dtype)


def _diff_call(q, q_col, segs, lp, g, *, nb, tq, t, lam_init):
    nq = t // tq
    in_specs = [pl.BlockSpec((tq, 512), lambda b, i: (b * nq + i, q_col // 512))]
    args = [q]
    for k_arr, k_col, v_arr, v_col, s in segs:
        in_specs.append(pl.BlockSpec((s, 512), functools.partial(lambda b, i, c: (b, c), c=k_col // 512)))
        in_specs.append(pl.BlockSpec((s, 512), functools.partial(lambda b, i, c: (b, c), c=v_col // 512)))
        args += [k_arr, v_arr]
    in_specs += [pl.BlockSpec((4, DIFF_QK), lambda b, i: (0, 0)), pl.BlockSpec((1, 512), lambda b, i: (0, 0))]
    args += [lp, g]
    return pl.pallas_call(
        functools.partial(_diff_kernel, nseg=len(segs), lam_init=lam_init),
        grid=(nb, nq),
        in_specs=in_specs,
        out_specs=pl.BlockSpec((tq, 512), lambda b, i: (b * nq + i, 0)),
        out_shape=jax.ShapeDtypeStruct((nb * t, 512), BF16),
        compiler_params=_cparams(("parallel", "parallel")),
    )(*args)


NA_QROWS = 4


def _na_plan(rows):
    win = min(rows, NA_QROWS + NA_ROWS)
    kh = min(NA_ROWS, rows)
    cfgs, cfg_idx, starts = [], [], []
    for blk in range(rows // NA_QROWS):
        r0 = blk * NA_QROWS
        s_blk = int(np.clip(r0 - kh // 2, 0, rows - win))
        cfg = tuple((r0 + i - s_blk, int(np.clip(r0 + i - kh // 2, 0, rows - kh)) - s_blk) for i in range(NA_QROWS))
        if cfg not in cfgs:
            cfgs.append(cfg)
        cfg_idx.append(cfgs.index(cfg))
        starts.append(s_blk)
    return win, cfgs, np.array([cfg_idx, starts], np.int32)


def _na_kernel(meta_ref, q_ref, k_ref, v_ref, kc_ref, vc_ref, b_ref, o_ref, *, win):
    start = pl.multiple_of(meta_ref[1, pl.program_id(1)] * GRID_W, GRID_W)
    n_loc = win * GRID_W
    lane = _lane_iota((q_ref.shape[0], LANE))
    for j in range(4):
        q_pair = q_ref[:, j * LANE:(j + 1) * LANE].astype(F32) * (NA_DIM ** -0.5)
        k_loc = k_ref[pl.ds(start, n_loc), j * LANE:(j + 1) * LANE].astype(BF16)
        v_loc = v_ref[pl.ds(start, n_loc), j * LANE:(j + 1) * LANE].astype(BF16)
        k_ctx = kc_ref[:, j * LANE:(j + 1) * LANE].astype(BF16)
        v_ctx = vc_ref[:, j * LANE:(j + 1) * LANE].astype(BF16)
        outs = []
        for e in range(2):
            q = _half_mask(q_pair, e).astype(BF16)
            s_loc = _dot_t(q, k_loc) + b_ref[2 * j + e]
            s_ctx = _dot_t(q, k_ctx)
            ps, inv = _softmax_parts([s_loc, s_ctx])
            o = _dot(ps[0].astype(BF16), v_loc) + _dot(ps[1].astype(BF16), v_ctx)
            outs.append(o * inv)
        o_ref[:, j * LANE:(j + 1) * LANE] = jnp.where(lane < HALF, outs[0], outs[1]).astype(o_ref.dtype)


def _na_call(p, kc, vc, rel_bias, *, nb, t, past):
    rows = t // GRID_W
    assert rows % NA_QROWS == 0 and rows >= NA_ROWS
    win, cfgs, meta = _na_plan(rows)
    bias_tab = _na_bias_table(rel_bias, cfgs, win)
    nblk = rows // NA_QROWS
    tq = NA_QROWS * GRID_W
    n_loc = win * GRID_W
    grid_spec = pltpu.PrefetchScalarGridSpec(
        num_scalar_prefetch=1,
        grid=(nb, nblk),
        in_specs=[
            pl.BlockSpec((tq, 512), lambda b, r, meta: (b * nblk + r, C_NQ // 512)),
            pl.BlockSpec((t, 512), lambda b, r, meta: (b, C_NK // 512)),
            pl.BlockSpec((t, 512), lambda b, r, meta: (b, C_NV // 512)),
            pl.BlockSpec((past, 512), lambda b, r, meta: (b, 0)),
            pl.BlockSpec((past, 512), lambda b, r, meta: (b, 0)),
            pl.BlockSpec((None, NA_HEADS, tq, n_loc), lambda b, r, meta: (meta[0, r], 0, 0, 0)),
        ],
        out_specs=pl.BlockSpec((tq, 512), lambda b, r, meta: (b * nblk + r, 0)),
    )
    return pl.pallas_call(
        functools.partial(_na_kernel, win=win),
        grid_spec=grid_spec,
        out_shape=jax.ShapeDtypeStruct((nb * t, 512), BF16),
        compiler_params=_cparams(("parallel", "arbitrary")),
    )(jnp.asarray(meta), p, p, p, kc, vc, bias_tab)


def _log_sigmoid(x):
    return jnp.minimum(x, 0.0) - jnp.log(1.0 + jnp.exp(-jnp.abs(x)))


def _ret_kernel(*refs, has_state, t):
    refs = list(refs)
    q_ref, k_ref, v_ref, rg_ref, dec_ref, gn_ref = [refs.pop(0) for _ in range(6)]
    s0_ref = refs.pop(0) if has_state else None
    o_ref = refs.pop(0)
    tq = q_ref.shape[0]
    q0 = pl.program_id(1) * tq
    lg = _log_sigmoid(dec_ref[...])
    ti = (q0 + lax.broadcasted_iota(jnp.int32, (tq, t), 0)).astype(F32)
    ui = lax.broadcasted_iota(jnp.int32, (tq, t), 1).astype(F32)
    diff = ti - ui
    tcol = (q0 + lax.broadcasted_iota(jnp.int32, (tq, 1), 0)).astype(F32)
    for h in range(RET_HEADS):
        j, e = h // 2, h % 2
        lgf = lg[h:h + 1, 0:1]
        lgb = lg[RET_HEADS + h:RET_HEADS + h + 1, 0:1]
        qm = _half_mask(q_ref[:, j * LANE:(j + 1) * LANE].astype(F32), e)
        q = qm.astype(BF16)
        a = _dot_t((qm * (RET_QK ** -0.5)).astype(BF16), k_ref[:, j * LANE:(j + 1) * LANE].astype(BF16))
        dmat = jnp.exp2(diff * jnp.where(diff >= 0, lgf * LOG2E, -lgb * LOG2E))
        y = _dot((a * dmat).astype(BF16), v_ref[:, h * LANE:(h + 1) * LANE].astype(BF16))
        if has_state:
            cf = _dot(q, s0_ref[0, j].astype(BF16))
            cb = _dot(q, s0_ref[1, j].astype(BF16))
            y = y + cf * jnp.exp((tcol + 1.0) * lgf) + cb * jnp.exp((float(t) - tcol) * lgb)
        mu = jnp.mean(y, axis=-1, keepdims=True)
        yc = y - mu
        var = jnp.mean(yc * yc, axis=-1, keepdims=True)
        yn = yc * lax.rsqrt(var + EPS) * gn_ref[:, h * LANE:(h + 1) * LANE]
        o_ref[:, h * LANE:(h + 1) * LANE] = (_silu(rg_ref[:, h * LANE:(h + 1) * LANE].astype(F32)) * yn).astype(o_ref.dtype)


def _ret_call(p, dec, gn, s0, *, nb, tq, t):
    nq = t // tq
    in_specs = [
        pl.BlockSpec((tq, 256), lambda b, i: (b * nq + i, C_RQ // 256)),
        pl.BlockSpec((t, 256), lambda b, i: (b, C_RK // 256)),
        pl.BlockSpec((t, 512), lambda b, i: (b, C_RV // 512)),
        pl.BlockSpec((tq, 512), lambda b, i: (b * nq + i, C_RG // 512)),
        pl.BlockSpec((8, LANE), lambda b, i: (0, 0)),
        pl.BlockSpec((1, 512), lambda b, i: (0, 0)),
    ]
    args = [p, p, p, p, dec, gn]
    if s0 is not None:
        in_specs.append(pl.BlockSpec((None, 2, 2, LANE, LANE), lambda b, i: (b, 0, 0, 0, 0)))
        args.append(s0)
    return pl.pallas_call(
        functools.partial(_ret_kernel, has_state=s0 is not None, t=t),
        grid=(nb, nq),
        in_specs=in_specs,
        out_specs=pl.BlockSpec((tq, 512), lambda b, i: (b * nq + i, 0)),
        out_shape=jax.ShapeDtypeStruct((nb * t, 512), BF16),
        compiler_params=_cparams(("parallel", "parallel")),
    )(*args)


def _ret_state_kernel(k_ref, v_ref, dec_ref, o_ref, *, t):
    lg = _log_sigmoid(dec_ref[...])
    tcol = lax.broadcasted_iota(jnp.int32, (t, 1), 0).astype(F32)
    lane = _lane_iota((t, LANE))
    for j in range(2):
        k_pair = k_ref[:, j * LANE:(j + 1) * LANE].astype(F32) * (RET_QK ** -0.5)
        for d in range(2):
            expo = (float(t) - 1.0 - tcol) if d == 0 else tcol
            w0 = jnp.exp(expo * lg[d * RET_HEADS + 2 * j:d * RET_HEADS + 2 * j + 1, 0:1])
            w1 = jnp.exp(expo * lg[d * RET_HEADS + 2 * j + 1:d * RET_HEADS + 2 * j + 2, 0:1])
            kd = (k_pair * jnp.where(lane < HALF, w0, w1)).T.astype(BF16)
            for e in range(2):
                h = 2 * j + e
                sfull = _dot(kd, v_ref[:, h * LANE:(h + 1) * LANE].astype(BF16))
                o_ref[d, h] = sfull[e * HALF:(e + 1) * HALF, :]


def _ret_state_call(p, dec, *, nb, t):
    return pl.pallas_call(
        functools.partial(_ret_state_kernel, t=t),
        grid=(nb,),
        in_specs=[
            pl.BlockSpec((t, 256), lambda b: (b, C_RK // 256)),
            pl.BlockSpec((t, 512), lambda b: (b, C_RV // 512)),
            pl.BlockSpec((8, LANE), lambda b: (0, 0)),
        ],
        out_specs=pl.BlockSpec((None, 2, RET_HEADS, RET_QK, RET_V), lambda b: (b, 0, 0, 0, 0)),
        out_shape=jax.ShapeDtypeStruct((nb, 2, RET_HEADS, RET_QK, RET_V), F32),
        compiler_params=_cparams(("parallel",)),
    )(p, p, dec)


def _merge_kernel(ya_ref, yb_ref, yc_ref, yd_ref, g0_ref, g1_ref, g2_ref, g3_ref, w_ref, o_ref):
    ys = (ya_ref, yb_ref, yc_ref, yd_ref)
    gs = (g0_ref, g1_ref, g2_ref, g3_ref)
    m = None
    for i in range(N_BRANCH):
        t = _sigmoid(gs[i][...].astype(F32)) * _dot(ys[i][...], w_ref[i])
        m = t if m is None else m + t
    o_ref[...] = m.astype(o_ref.dtype)


def _merge_call(ys, p, w_branch):
    m = p.shape[0]
    tm, tn = min(1024, m), 512
    gate_specs = [
        pl.BlockSpec((tm, tn), functools.partial(lambda r, c, i: (r, (C_GATES + i * D_MODEL) // tn + c), i=i))
        for i in range(N_BRANCH)
    ]
    return pl.pallas_call(
        _merge_kernel,
        grid=(m // tm, D_MODEL // tn),
        in_specs=[pl.BlockSpec((tm, BRANCH_W), lambda r, c: (r, 0))] * N_BRANCH + gate_specs
        + [pl.BlockSpec((N_BRANCH, BRANCH_W, tn), lambda r, c: (0, 0, c))],
        out_specs=pl.BlockSpec((tm, tn), lambda r, c: (r, c)),
        out_shape=jax.ShapeDtypeStruct((m, D_MODEL), BF16),
        compiler_params=_cparams(("parallel", "parallel")),
    )(*ys, p, p, p, p, w_branch)


HALO = 16


SUB = 8


SHIFT_BLK = 256


def _shift_matrix():
    i = np.arange(SHIFT_BLK)
    s = np.zeros((2 * SHIFT_BLK, SHIFT_BLK), np.float32)
    s[i[1:], i[1:] - 1] = 1.0
    s[SHIFT_BLK + i[:-1], i[:-1] + 1] = 1.0
    return jnp.asarray(s, BF16)


def _conv_kernel(s_ref, ua_ref, uap_ref, uan_ref, ub_ref, ubp_ref, ubn_ref, wa_ref, wb_ref, ba_ref, bb_ref, o_ref, *, tiles_per_seq):
    i = pl.program_id(0)
    tm, tc = ua_ref.shape
    nblk = tm // SHIFT_BLK
    has_prev = ((i % tiles_per_seq) != 0).astype(F32)
    has_next = ((i % tiles_per_seq) != tiles_per_seq - 1).astype(F32)
    sub = lax.broadcasted_iota(jnp.int32, (SUB, tc), 0)
    shift = s_ref[...]

    def conv(u_ref, up_ref, un_ref, w_ref, b_ref, r):
        r0 = r * SHIFT_BLK
        ub = u_ref[r0:r0 + SHIFT_BLK, :]
        sh = _dot(shift, ub)
        if r == 0:
            prev = up_ref[HALO - 1:HALO, :].astype(F32) * has_prev
        else:
            prev = u_ref[r0 - 1:r0, :].astype(F32)
        if r == nblk - 1:
            nxt = un_ref[0:1, :].astype(F32) * has_next
        else:
            nxt = u_ref[r0 + SHIFT_BLK:r0 + SHIFT_BLK + 1, :].astype(F32)
        dn = sh[:SHIFT_BLK]
        up = sh[SHIFT_BLK:]
        um1 = jnp.concatenate([jnp.where(sub == 0, prev, dn[:SUB]), dn[SUB:]], axis=0)
        up1 = jnp.concatenate([up[:-SUB], jnp.where(sub == SUB - 1, nxt, up[-SUB:])], axis=0)
        return um1 * w_ref[0:1, :] + ub.astype(F32) * w_ref[1:2, :] + up1 * w_ref[2:3, :] + b_ref[...]

    for r in range(nblk):
        a = conv(ua_ref, uap_ref, uan_ref, wa_ref, ba_ref, r)
        b = conv(ub_ref, ubp_ref, ubn_ref, wb_ref, bb_ref, r)
        o_ref[r * SHIFT_BLK:(r + 1) * SHIFT_BLK, :] = (_silu(a) * b).astype(o_ref.dtype)


def _conv_call(u, conv_w, conv_b, *, seq):
    m = u.shape[0]
    tm, tc = min(seq, 512), 1408
    assert seq % tm == 0 and D_FF % tc == 0
    nb_half = D_FF // tc
    last_halo = m // HALO - 1
    r = tm // HALO

    def main(off):
        return pl.BlockSpec((tm, tc), lambda i, c: (i, c + off))

    def prev(off):
        return pl.BlockSpec((HALO, tc), lambda i, c: (jnp.maximum(i * r - 1, 0), c + off))

    def nxt(off):
        return pl.BlockSpec((HALO, tc), lambda i, c: (jnp.minimum((i + 1) * r, last_halo), c + off))

    def vec(rows, off):
        return pl.BlockSpec((rows, tc), lambda i, c: (0, c + off))

    cb = conv_b.reshape(1, 2 * D_FF)
    return pl.pallas_call(
        functools.partial(_conv_kernel, tiles_per_seq=seq // tm),
        grid=(m // tm, nb_half),
        in_specs=[pl.BlockSpec((2 * SHIFT_BLK, SHIFT_BLK), lambda i, c: (0, 0)),
                  main(0), prev(0), nxt(0), main(nb_half), prev(nb_half), nxt(nb_half),
                  vec(CONV_W, 0), vec(CONV_W, nb_half), vec(1, 0), vec(1, nb_half)],
        out_specs=pl.BlockSpec((tm, tc), lambda i, c: (i, c)),
        out_shape=jax.ShapeDtypeStruct((m, D_FF), BF16),
        compiler_params=_cparams(("parallel", "parallel")),
    )(_shift_matrix(), u, u, u, u, u, u, conv_w, conv_w, cb, cb)


def _final_norm_kernel(x_ref, g_ref, o_ref):
    x = x_ref[...]
    o_ref[...] = x * lax.rsqrt(jnp.mean(x * x, axis=-1, keepdims=True) + EPS) * g_ref[...]


def _final_norm_call(x, g):
    m, d = x.shape
    tm = 256
    return pl.pallas_call(
        _final_norm_kernel,
        grid=(m // tm,),
        in_specs=[pl.BlockSpec((tm, d), lambda i: (i, 0)), pl.BlockSpec((1, d), lambda i: (0, 0))],
        out_specs=pl.BlockSpec((tm, d), lambda i: (i, 0)),
        out_shape=jax.ShapeDtypeStruct((m, d), F32),
        compiler_params=_cparams(("parallel",)),
    )(x, g.reshape(1, d))


def _reorder_w_in(w):
    d = w.shape[0]
    mq = w[:, :768].reshape(d, MLA_HEADS, MLA_NOPE + MLA_ROPE)
    qn = mq[:, :, :MLA_NOPE].reshape(d, 512)
    qr = mq[:, :, MLA_NOPE:].reshape(d, 256)
    ckv = w[:, 768:1024]
    mkr = w[:, 1024:1056]
    rest = w[:, 1056:5664]
    gates = w[:, 5664:]
    krblk = jnp.concatenate([jnp.zeros((d, LANE), w.dtype)] + [mkr] * 4, axis=1)
    pad = jnp.zeros((d, C_GATES - C_KR - 256), w.dtype)
    return jnp.concatenate([qn, qr, ckv, rest, krblk, pad, gates], axis=1).astype(BF16)


def _reorder_w_ukv(w):
    r = w.shape[0]
    w3 = w.reshape(r, MLA_HEADS, MLA_NOPE + MLA_V)
    kn = w3[:, :, :MLA_NOPE].reshape(r, 4, LANE)
    vv = w3[:, :, MLA_NOPE:].reshape(r, 512)
    knz = jnp.concatenate([kn, jnp.zeros_like(kn)], axis=2).reshape(r, MLA_K_W)
    return jnp.concatenate([knz, vv], axis=1).astype(BF16)


def _rope_tables(t, dr, reps, lead_zero=0):
    nf = dr // 4
    pos = jnp.arange(t)
    pos = jnp.stack([pos // GRID_W, pos % GRID_W], axis=-1).astype(F32)
    inv = ROPE_BASE ** (-jnp.arange(nf, dtype=F32) / nf)
    ang = pos[:, :, None] * inv
    cos = jnp.cos(ang)
    sin = jnp.sin(ang)
    c = jnp.tile(jnp.concatenate([cos, cos], axis=2).reshape(t, dr), (1, reps))
    s = jnp.tile(jnp.concatenate([-sin, sin], axis=2).reshape(t, dr), (1, reps))
    if lead_zero:
        z = jnp.zeros((t, lead_zero), F32)
        c = jnp.concatenate([z, c], axis=1)
        s = jnp.concatenate([z, s], axis=1)
    return c, s


def _na_bias_table(rel_bias, cfgs, win):
    v = rel_bias.astype(F32)
    pad = GRID_W - NA_COLS
    ext = jnp.concatenate([jnp.repeat(v[..., :1], pad, axis=-1), v, jnp.repeat(v[..., -1:], pad, axis=-1)], axis=-1)
    col_t = jnp.stack([ext[..., GRID_W - 1 - q:2 * GRID_W - 1 - q] for q in range(GRID_W)], axis=2)
    qc = np.arange(GRID_W)[:, None]
    kc = np.arange(GRID_W)[None, :]
    c0 = np.clip(qc - NA_COLS // 2, 0, GRID_W - NA_COLS)
    col_ok = (kc >= c0) & (kc < c0 + NA_COLS)
    col_t = jnp.where(col_ok, col_t, NEG_INF)
    neg = jnp.full((NA_HEADS, GRID_W, GRID_W), NEG_INF, F32)
    kh = min(NA_ROWS, win)
    tabs = []
    for cfg in cfgs:
        per_row = []
        for delta, off in cfg:
            blocks = [col_t[:, kk - delta + NA_ROWS - 1] if off <= kk < off + kh else neg for kk in range(win)]
            per_row.append(jnp.concatenate(blocks, axis=-1))
        tabs.append(jnp.concatenate(per_row, axis=1))
    return jnp.stack(tabs)


def kernel(x_prompt, x_sample, cache_mla_ckv, cache_mla_krope, cache_diff_k, cache_diff_v, cache_na_k, cache_na_v, state_ret, c, c_ctx, w_ada, b_ada, g_attn, g_ffn, w_in, mla_kv_norm, w_mla_ukv, diff_lambda, diff_norm, na_rel_bias, ret_decay, ret_norm, w_branch, w_o, w_up, conv_w, conv_b, w_down, g_final):
    nbc, seq, d = x_prompt.shape
    nbl, t, _ = x_sample.shape
    past = cache_mla_ckv.shape[2]
    depth = w_in.shape[0]
    mc, ml = nbc * seq, nbl * t

    xc = x_prompt.reshape(mc, d)
    xl = x_sample.reshape(ml, d)

    mod_rows = 16
    cond = jnp.concatenate([c_ctx[None], c, jnp.zeros((mod_rows - 1 - nbl, d), F32)], axis=0)
    mods = _ada_call(cond, w_ada, b_ada).reshape(depth, mod_rows, 6, d)

    cos32q, sin32q = _rope_tables(t, MLA_ROPE, 8)
    cos32k, sin32k = _rope_tables(t, MLA_ROPE, 4, lead_zero=LANE)
    cos64, sin64 = _rope_tables(t, DIFF_QK, 16)

    tm_c = min(1024, mc)
    tm_l = min(1024, t)
    states = []
    for l in range(depth):
        lam_init = 0.8 - 0.6 * math.exp(-0.3 * l)
        w_in_r = _reorder_w_in(w_in[l])
        w_ukv_e = _reorder_w_ukv(w_mla_ukv[l])
        wb = w_branch[l].astype(BF16)
        wo = w_o[l].astype(BF16)
        wup = w_up[l].astype(BF16)
        wdn = w_down[l].astype(BF16)
        kvn = mla_kv_norm[l].reshape(1, MLA_KV_RANK)
        dec = jnp.broadcast_to(ret_decay[l].reshape(2 * RET_HEADS, 1), (2 * RET_HEADS, LANE))
        gn = ret_norm[l].reshape(1, 512)
        dnorm = diff_norm[l].reshape(1, 512)
        g_a = g_attn[l].reshape(1, d)
        g_f = g_ffn[l].reshape(1, d)

        def mod(i):
            return mods[l, 0:1, i][:, None, :], mods[l, 1:1 + nbl, i][:, None, :]

        sh1, sc1, ga1, sh2, sc2, ga2 = [mod(i) for i in range(6)]

        pc = _mm_call(xc, w_in_r, out_dtype=F32, tm=tm_c, tn=1024, rows_per_mod=mc, norm=(g_a, sh1[0], sc1[0]))
        kc_mla, vc_mla, ckv_state = _mla_kv_call(pc, C_CKV, pc, C_KR, w_ukv_e, g=kvn)
        ya = _mha_call(pc, C_QN, pc, C_QR, [(kc_mla, 0, vc_mla, 0, seq)], nb=nbc, tq=seq, t=seq,
                       scale=(MLA_NOPE + MLA_ROPE) ** -0.5)
        yb = _diff_call(pc, C_DQ, [(pc, C_DK, pc, C_DV, seq)], diff_lambda[l], dnorm, nb=nbc, tq=seq, t=seq,
                        lam_init=lam_init)
        yc = _mha_call(pc, C_NQ, None, 0, [(pc, C_NK, pc, C_NV, seq)], nb=nbc, tq=seq, t=seq, scale=NA_DIM ** -0.5)
        yd = _ret_call(pc, dec, gn, None, nb=nbc, tq=seq, t=seq)
        sret = _ret_state_call(pc, dec, nb=nbc, t=seq)
        states.append((
            ckv_state.reshape(nbc, seq, MLA_KV_RANK),
            pc[:, C_KR + LANE:C_KR + LANE + MLA_ROPE].reshape(nbc, seq, MLA_ROPE),
            pc[:, C_DK:C_DK + 512].reshape(nbc, seq, DIFF_HEADS, 2, DIFF_QK),
            pc[:, C_DV:C_DV + 512].reshape(nbc, seq, DIFF_HEADS, DIFF_V),
            pc[:, C_NK:C_NK + 512].reshape(nbc, seq, NA_HEADS, NA_DIM),
            pc[:, C_NV:C_NV + 512].reshape(nbc, seq, NA_HEADS, NA_DIM),
            sret,
        ))
        mrg = _merge_call((ya, yb, yc, yd), pc, wb)
        xc = _mm_call(mrg, wo, out_dtype=F32, tm=tm_c, tn=1024, rows_per_mod=mc, resid=(xc, ga1[0]))
        u = _mm_call(xc, wup, out_dtype=BF16, tm=tm_c, tn=1024, rows_per_mod=mc, norm=(g_f, sh2[0], sc2[0]))
        act = _conv_call(u, conv_w[l], conv_b[l], seq=seq)
        xc = _mm_call(act, wdn, out_dtype=F32, tm=512, tn=1024, rows_per_mod=mc, resid=(xc, ga2[0]))

        plat = _mm_call(xl, w_in_r, out_dtype=BF16, tm=tm_l, tn=1024, rows_per_mod=t, norm=(g_a, sh1[1], sc1[1]))
        kl_mla, vl_mla, _ = _mla_kv_call(plat, C_CKV, plat, C_KR, w_ukv_e, g=kvn, rope=(cos32k, sin32k), seq=t)
        kr_c = cache_mla_krope[:, l].reshape(nbl * past, MLA_ROPE)
        krblk_c = jnp.concatenate([jnp.zeros((nbl * past, LANE), F32)] + [kr_c] * 4, axis=1)
        kp_mla, vp_mla = _mla_kv_call(cache_mla_ckv[:, l].reshape(nbl * past, MLA_KV_RANK), 0, krblk_c, 0, w_ukv_e)
        qr_l = _rope_call(plat, C_QR, 256, cos32q, sin32q, MLA_ROPE, t)
        dqk_l = _rope_call(plat, C_DQ, 1024, cos64, sin64, DIFF_QK, t)
        tq = min(512, t)
        ya = _mha_call(plat, C_QN, qr_l, 0, [(kl_mla, 0, vl_mla, 0, t), (kp_mla, 0, vp_mla, 0, past)],
                       nb=nbl, tq=tq, t=t, scale=(MLA_NOPE + MLA_ROPE) ** -0.5)
        dk_c = cache_diff_k[:, l].reshape(nbl * past, 512)
        dv_c = cache_diff_v[:, l].reshape(nbl * past, 512)
        yb = _diff_call(dqk_l, 0, [(dqk_l, 512, plat, C_DV, t), (dk_c, 0, dv_c, 0, past)], diff_lambda[l], dnorm,
                        nb=nbl, tq=tq, t=t, lam_init=lam_init)
        yc = _na_call(plat, cache_na_k[:, l].reshape(nbl * past, 512), cache_na_v[:, l].reshape(nbl * past, 512),
                      na_rel_bias[l], nb=nbl, t=t, past=past)
        s0 = state_ret[:, l].astype(F32).reshape(nbl, 2, 2, LANE, LANE)
        yd = _ret_call(plat, dec, gn, s0, nb=nbl, tq=tq, t=t)
        mrg = _merge_call((ya, yb, yc, yd), plat, wb)
        xl = _mm_call(mrg, wo, out_dtype=F32, tm=tm_l, tn=1024, rows_per_mod=t, resid=(xl, ga1[1]))
        u = _mm_call(xl, wup, out_dtype=BF16, tm=tm_l, tn=1024, rows_per_mod=t, norm=(g_f, sh2[1], sc2[1]))
        act = _conv_call(u, conv_w[l], conv_b[l], seq=t)
        xl = _mm_call(act, wdn, out_dtype=F32, tm=512, tn=1024, rows_per_mod=t, resid=(xl, ga2[1]))

    y_prompt = _final_norm_call(xc, g_final).reshape(nbc, seq, d)
    y_sample = _final_norm_call(xl, g_final).reshape(nbl, t, d)
    outs = [jnp.stack([st[i] for st in states], axis=1) for i in range(7)]
    return (y_prompt, y_sample, *outs)
```

```python
import functools
import math

import jax
import jax.numpy as jnp
import numpy as np
from jax import lax
from jax.experimental import pallas as pl
from jax.experimental.pallas import tpu as pltpu

F32 = jnp.float32
BF16 = jnp.bfloat16

D_MODEL = 2048
GRID_W = 64
ROPE_BASE = 10000.0
EPS = 1e-6
NEG_INF = -1e30
N_BRANCH = 4
BRANCH_W = 512
MLA_HEADS = 8
MLA_NOPE = 64
MLA_ROPE = 32
MLA_V = 64
MLA_KV_RANK = 256
DIFF_QK = 64
DIFF_V = 128
DIFF_HEADS = 4
NA_DIM = 64
NA_HEADS = 8
NA_ROWS = 8
NA_COLS = 16
RET_V = 128
RET_HEADS = 4
RET_QK = 64
D_FF = 5632
CONV_W = 3

LANE = 128
HALF = 64

C_QN = 0
C_QR = 512
C_CKV = 768
C_DQ = 1024
C_DK = 1536
C_DV = 2048
C_NQ = 2560
C_NK = 3072
C_NV = 3584
C_RQ = 4096
C_RK = 4352
C_RV = 4608
C_RG = 5120
C_KR = 5632
C_GATES = 6144
P_WIDTH = C_GATES + N_BRANCH * D_MODEL

VMEM_LIMIT = 56 * 1024 * 1024


def _cparams(sem):
    return pltpu.CompilerParams(dimension_semantics=sem, vmem_limit_bytes=VMEM_LIMIT)


def _dot(a, b):
    return jnp.dot(a, b, preferred_element_type=F32)


def _dot_t(a, b):
    return lax.dot_general(a, b, (((1,), (1,)), ((), ())), preferred_element_type=F32)


def _sigmoid(x):
    return 0.5 * jnp.tanh(0.5 * x) + 0.5


def _silu(x):
    h = 0.5 * x
    return h * (jnp.tanh(h) + 1.0)


def _lane_iota(shape):
    return lax.broadcasted_iota(jnp.int32, shape, len(shape) - 1)


def _half_mask(x, e):
    lane = _lane_iota(x.shape)
    keep = (lane < HALF) if e == 0 else (lane >= HALF)
    return jnp.where(keep, x, 0.0)


def _rope(x, cos, sin, dr):
    w = x.shape[-1]
    nf = dr // 4
    lane = _lane_iota(x.shape)
    first = (lane % (2 * nf)) < nf
    rot = jnp.where(first, pltpu.roll(x, w - nf, 1), pltpu.roll(x, nf, 1))
    return x * cos + rot * sin


def _ada_kernel(c_ref, w_ref, b_ref, o_ref):
    a = _silu(c_ref[...]).astype(BF16)
    o_ref[...] = _dot(a, w_ref[...].astype(BF16)) + b_ref[...]


def _ada_call(cond, w_ada, b_ada):
    depth, d, n = w_ada.shape
    rows = cond.shape[0]
    tn = 1024
    return pl.pallas_call(
        _ada_kernel,
        grid=(depth, n // tn),
        in_specs=[
            pl.BlockSpec((rows, d), lambda l, j: (0, 0)),
            pl.BlockSpec((None, d, tn), lambda l, j: (l, 0, j)),
            pl.BlockSpec((None, 1, tn), lambda l, j: (l, 0, j)),
        ],
        out_specs=pl.BlockSpec((None, rows, tn), lambda l, j: (l, 0, j)),
        out_shape=jax.ShapeDtypeStruct((depth, rows, n), F32),
        compiler_params=_cparams(("parallel", "parallel")),
    )(cond, w_ada, b_ada.reshape(depth, 1, n))


NORM_ROWS = 128


def _mm_kernel(*refs, norm, resid, out_norm):
    refs = list(refs)
    a_ref = refs.pop(0)
    if norm:
        g_ref, sh_ref, sc_ref = refs.pop(0), refs.pop(0), refs.pop(0)
    w_ref = refs.pop(0)
    if resid:
        r_ref, ga_ref = refs.pop(0), refs.pop(0)
    if out_norm:
        go_ref = refs.pop(0)
    o_ref = refs.pop(0)

    if norm:
        h_ref = refs.pop(0)

        @pl.when(pl.program_id(1) == 0)
        def _():
            g = g_ref[...]
            mul = 1.0 + sc_ref[...]
            add = sh_ref[...]

            def body(c, carry):
                r0 = pl.multiple_of(c * NORM_ROWS, NORM_ROWS)
                x = a_ref[pl.ds(r0, NORM_ROWS), :]
                y = x * lax.rsqrt(jnp.mean(x * x, axis=-1, keepdims=True) + EPS) * g
                h_ref[pl.ds(r0, NORM_ROWS), :] = (y * mul + add).astype(BF16)
                return carry

            lax.fori_loop(0, a_ref.shape[0] // NORM_ROWS, body, 0)

        a = h_ref[...]
    else:
        a = a_ref[...]
    acc = _dot(a, w_ref[...])
    if resid:
        acc = r_ref[...] + ga_ref[...] * acc
    if out_norm:
        acc = acc * lax.rsqrt(jnp.mean(acc * acc, axis=-1, keepdims=True) + EPS) * go_ref[...]
    o_ref[...] = acc.astype(o_ref.dtype)


def _mm_call(a, w, *, out_dtype, tm, tn, rows_per_mod, norm=None, resid=None, out_norm=None):
    m, k = a.shape
    n = w.shape[1]
    assert m % tm == 0 and n % tn == 0 and rows_per_mod % tm == 0
    assert out_norm is None or tn == n
    tiles_per_mod = rows_per_mod // tm
    in_specs = [pl.BlockSpec((tm, k), lambda i, j: (i, 0))]
    args = [a]
    if norm is not None:
        g, sh, sc = norm
        in_specs += [
            pl.BlockSpec((1, k), lambda i, j: (0, 0)),
            pl.BlockSpec((None, 1, k), lambda i, j: (i // tiles_per_mod, 0, 0)),
            pl.BlockSpec((None, 1, k), lambda i, j: (i // tiles_per_mod, 0, 0)),
        ]
        args += [g, sh, sc]
    if tn == n:
        in_specs.append(pl.BlockSpec((k, tn), lambda i, j: (0, 0), pipeline_mode=pl.Buffered(1)))
    else:
        in_specs.append(pl.BlockSpec((k, tn), lambda i, j: (0, j)))
    args.append(w)
    if resid is not None:
        x, ga = resid
        in_specs += [
            pl.BlockSpec((tm, tn), lambda i, j: (i, j)),
            pl.BlockSpec((None, 1, tn), lambda i, j: (i // tiles_per_mod, 0, j)),
        ]
        args += [x, ga]
    if out_norm is not None:
        in_specs.append(pl.BlockSpec((1, n), lambda i, j: (0, 0)))
        args.append(out_norm)
    scratch = [pltpu.VMEM((tm, k), BF16)] if norm is not None else []
    return pl.pallas_call(
        functools.partial(_mm_kernel, norm=norm is not None, resid=resid is not None, out_norm=out_norm is not None),
        grid=(m // tm, n // tn),
        in_specs=in_specs,
        out_specs=pl.BlockSpec((tm, tn), lambda i, j: (i, j)),
        out_shape=jax.ShapeDtypeStruct((m, n), out_dtype),
        scratch_shapes=scratch,
        compiler_params=_cparams(("parallel", "arbitrary")),
    )(*args)


def _rope_kernel(x_ref, c_ref, s_ref, o_ref, *, dr):
    o_ref[...] = _rope(x_ref[...].astype(F32), c_ref[...], s_ref[...], dr).astype(o_ref.dtype)


def _rope_call(p, col, width, cos, sin, dr, seq):
    m = p.shape[0]
    tm = 256
    tpb = seq // tm
    cb = col // width
    assert col % width == 0
    return pl.pallas_call(
        functools.partial(_rope_kernel, dr=dr),
        grid=(tpb, m // seq),
        in_specs=[
            pl.BlockSpec((tm, width), lambda i, b: (b * tpb + i, cb)),
            pl.BlockSpec((tm, width), lambda i, b: (i, 0)),
            pl.BlockSpec((tm, width), lambda i, b: (i, 0)),
        ],
        out_specs=pl.BlockSpec((tm, width), lambda i, b: (b * tpb + i, 0)),
        out_shape=jax.ShapeDtypeStruct((m, width), BF16),
        compiler_params=_cparams(("parallel", "parallel")),
    )(p, cos, sin)


MLA_K_W = 4 * 256
MLA_V_W = 4 * 128


def _mla_kv_kernel(*refs, norm, rope):
    refs = list(refs)
    c_ref, kr_ref = refs.pop(0), refs.pop(0)
    if rope:
        cos_ref, sin_ref = refs.pop(0), refs.pop(0)
    if norm:
        g_ref = refs.pop(0)
    w_ref = refs.pop(0)
    k_ref, v_ref = refs.pop(0), refs.pop(0)
    c = c_ref[...].astype(F32)
    if norm:
        ckv_ref = refs.pop(0)
        c = c * lax.rsqrt(jnp.mean(c * c, axis=-1, keepdims=True) + EPS) * g_ref[...]
        ckv_ref[...] = c
    kv = _dot(c.astype(BF16), w_ref[...])
    kr = kr_ref[...].astype(F32)
    if rope:
        kr = _rope(kr, cos_ref[...], sin_ref[...], MLA_ROPE)
    k_ref[...] = (kv[:, :MLA_K_W] + jnp.concatenate([kr] * 4, axis=1)).astype(BF16)
    v_ref[...] = kv[:, MLA_K_W:].astype(BF16)


def _mla_kv_call(ckv_src, ckv_col, kr_src, kr_col, w_e, *, g=None, rope=None, seq=None):
    m = ckv_src.shape[0]
    tm = 256
    assert m % tm == 0 and ckv_col % 256 == 0 and kr_col % 256 == 0
    in_specs = [
        pl.BlockSpec((tm, 256), lambda i: (i, ckv_col // 256)),
        pl.BlockSpec((tm, 256), lambda i: (i, kr_col // 256)),
    ]
    args = [ckv_src, kr_src]
    if rope is not None:
        tpb = seq // tm
        in_specs += [pl.BlockSpec((tm, 256), lambda i: (i % tpb, 0))] * 2
        args += list(rope)
    if g is not None:
        in_specs.append(pl.BlockSpec((1, 256), lambda i: (0, 0)))
        args.append(g)
    in_specs.append(pl.BlockSpec(w_e.shape, lambda i: (0, 0)))
    args.append(w_e)
    out_specs = [pl.BlockSpec((tm, MLA_K_W), lambda i: (i, 0)), pl.BlockSpec((tm, MLA_V_W), lambda i: (i, 0))]
    out_shape = [jax.ShapeDtypeStruct((m, MLA_K_W), BF16), jax.ShapeDtypeStruct((m, MLA_V_W), BF16)]
    if g is not None:
        out_specs.append(pl.BlockSpec((tm, 256), lambda i: (i, 0)))
        out_shape.append(jax.ShapeDtypeStruct((m, 256), F32))
    return pl.pallas_call(
        functools.partial(_mla_kv_kernel, norm=g is not None, rope=rope is not None),
        grid=(m // tm,),
        in_specs=in_specs,
        out_specs=out_specs,
        out_shape=out_shape,
        compiler_params=_cparams(("parallel",)),
    )(*args)


LOG2E = 1.4426950408889634


def _exp_parts(scores, scale=1.0):
    m = scores[0].max(axis=-1, keepdims=True)
    for s in scores[1:]:
        m = jnp.maximum(m, s.max(axis=-1, keepdims=True))
    return [jnp.exp2((s - m) * (scale * LOG2E)) for s in scores]


def _softmax_parts(scores, scale=1.0):
    ps = _exp_parts(scores, scale)
    l = ps[0].sum(axis=-1, keepdims=True)
    for p in ps[1:]:
        l = l + p.sum(axis=-1, keepdims=True)
    return ps, 1.0 / l


def _pv_normalised(ps, vs):
    acc = None
    for p, v in zip(ps, vs):
        t = _dot(p.astype(BF16), jnp.concatenate([v, jnp.ones_like(v)], axis=1))
        acc = t if acc is None else acc + t
    return acc[:, :LANE] / acc[:, LANE:]


def _mha_kernel(*refs, mla, nseg, scale):
    refs = list(refs)
    qn_ref = refs.pop(0)
    qr_ref = refs.pop(0) if mla else None
    kv_refs = [(refs.pop(0), refs.pop(0)) for _ in range(nseg)]
    o_ref = refs.pop(0)
    kw = 256 if mla else LANE
    lane = _lane_iota((qn_ref.shape[0], LANE))
    for j in range(4):
        outs = []
        qn_pair = qn_ref[:, j * LANE:(j + 1) * LANE].astype(F32)
        for e in range(2):
            h = 2 * j + e
            q = _half_mask(qn_pair, e)
            if mla:
                qr = qr_ref[:, (h // 4) * LANE:(h // 4 + 1) * LANE].astype(F32)
                qr = jnp.where(lane // MLA_ROPE == h % 4, qr, 0.0)
                q = jnp.concatenate([q, qr], axis=1)
            q = q.astype(BF16)
            scores = [_dot_t(q, k_ref[:, j * kw:(j + 1) * kw].astype(BF16)) for k_ref, _ in kv_refs]
            ps = _exp_parts(scores, scale)
            outs.append(_pv_normalised(ps, [v_ref[:, j * LANE:(j + 1) * LANE].astype(BF16) for _, v_ref in kv_refs]))
        o_ref[:, j * LANE:(j + 1) * LANE] = jnp.where(lane < HALF, outs[0], outs[1]).astype(o_ref.dtype)


def _mha_call(qn, qn_col, qr, qr_col, segs, *, nb, tq, t, scale):
    mla = qr is not None
    kw = MLA_K_W if mla else 512
    nq = t // tq
    in_specs = [pl.BlockSpec((tq, 512), lambda b, i: (b * nq + i, qn_col // 512))]
    args = [qn]
    if mla:
        in_specs.append(pl.BlockSpec((tq, 256), lambda b, i: (b * nq + i, qr_col // 256)))
        args.append(qr)
    for k_arr, k_col, v_arr, v_col, s in segs:
        assert k_col % kw == 0 and v_col % 512 == 0
        in_specs.append(pl.BlockSpec((s, kw), functools.partial(lambda b, i, c: (b, c), c=k_col // kw)))
        in_specs.append(pl.BlockSpec((s, 512), functools.partial(lambda b, i, c: (b, c), c=v_col // 512)))
        args += [k_arr, v_arr]
    return pl.pallas_call(
        functools.partial(_mha_kernel, mla=mla, nseg=len(segs), scale=scale),
        grid=(nb, nq),
        in_specs=in_specs,
        out_specs=pl.BlockSpec((tq, 512), lambda b, i: (b * nq + i, 0)),
        out_shape=jax.ShapeDtypeStruct((nb * t, 512), BF16),
        compiler_params=_cparams(("parallel", "parallel")),
    )(*args)


def _diff_kernel(*refs, nseg, lam_init):
    refs = list(refs)
    q_ref = refs.pop(0)
    kv_refs = [(refs.pop(0), refs.pop(0)) for _ in range(nseg)]
    lp_ref, g_ref, o_ref = refs
    lp = lp_ref[...]
    lam = (jnp.exp(jnp.sum(lp[0:1] * lp[1:2], axis=-1, keepdims=True))
           - jnp.exp(jnp.sum(lp[2:3] * lp[3:4], axis=-1, keepdims=True)) + lam_init)
    scale = DIFF_QK ** -0.5
    for h in range(DIFF_HEADS):
        q_pair = q_ref[:, h * LANE:(h + 1) * LANE].astype(F32)
        probs = []
        for c in range(2):
            q = _half_mask(q_pair, c).astype(BF16)
            scores = [_dot_t(q, k_ref[:, h * LANE:(h + 1) * LANE].astype(BF16)) for k_ref, _ in kv_refs]
            probs.append(_softmax_parts(scores, scale))
        (ps0, inv0), (ps1, inv1) = probs
        c1 = lam * inv1
        o = None
        for s in range(nseg):
            a = (ps0[s] * inv0 - ps1[s] * c1).astype(BF16)
            t = _dot(a, kv_refs[s][1][:, h * LANE:(h + 1) * LANE].astype(BF16))
            o = t if o is None else o + t
        o = o * lax.rsqrt(jnp.mean(o * o, axis=-1, keepdims=True) + EPS) * g_ref[:, h * LANE:(h + 1) * LANE]
        o_ref[:, h * LANE:(h + 1) * LANE] = (o * (1.0 - lam_init)).astype(o_ref.dtype)


def _diff_call(q, q_col, segs, lp, g, *, nb, tq, t, lam_init):
    nq = t // tq
    in_specs = [pl.BlockSpec((tq, 512), lambda b, i: (b * nq + i, q_col // 512))]
    args = [q]
    for k_arr, k_col, v_arr, v_col, s in segs:
        in_specs.append(pl.BlockSpec((s, 512), functools.partial(lambda b, i, c: (b, c), c=k_col // 512)))
        in_specs.append(pl.BlockSpec((s, 512), functools.partial(lambda b, i, c: (b, c), c=v_col // 512)))
        args += [k_arr, v_arr]
    in_specs += [pl.BlockSpec((4, DIFF_QK), lambda b, i: (0, 0)), pl.BlockSpec((1, 512), lambda b, i: (0, 0))]
    args += [lp, g]
    return pl.pallas_call(
        functools.partial(_diff_kernel, nseg=len(segs), lam_init=lam_init),
        grid=(nb, nq),
        in_specs=in_specs,
        out_specs=pl.BlockSpec((tq, 512), lambda b, i: (b * nq + i, 0)),
        out_shape=jax.ShapeDtypeStruct((nb * t, 512), BF16),
        compiler_params=_cparams(("parallel", "parallel")),
    )(*args)


NA_QROWS = 4


def _na_plan(rows):
    win = min(rows, NA_QROWS + NA_ROWS)
    kh = min(NA_ROWS, rows)
    cfgs, cfg_idx, starts = [], [], []
    for blk in range(rows // NA_QROWS):
        r0 = blk * NA_QROWS
        s_blk = int(np.clip(r0 - kh // 2, 0, rows - win))
        cfg = tuple((r0 + i - s_blk, int(np.clip(r0 + i - kh // 2, 0, rows - kh)) - s_blk) for i in range(NA_QROWS))
        if cfg not in cfgs:
            cfgs.append(cfg)
        cfg_idx.append(cfgs.index(cfg))
        starts.append(s_blk)
    return win, cfgs, np.array([cfg_idx, starts], np.int32)


def _na_kernel(meta_ref, q_ref, k_ref, v_ref, kc_ref, vc_ref, b_ref, o_ref, *, win):
    start = pl.multiple_of(meta_ref[1, pl.program_id(1)] * GRID_W, GRID_W)
    n_loc = win * GRID_W
    lane = _lane_iota((q_ref.shape[0], LANE))
    for j in range(4):
        q_pair = q_ref[:, j * LANE:(j + 1) * LANE].astype(F32) * (NA_DIM ** -0.5)
        k_loc = k_ref[pl.ds(start, n_loc), j * LANE:(j + 1) * LANE].astype(BF16)
        v_loc = v_ref[pl.ds(start, n_loc), j * LANE:(j + 1) * LANE].astype(BF16)
        k_ctx = kc_ref[:, j * LANE:(j + 1) * LANE].astype(BF16)
        v_ctx = vc_ref[:, j * LANE:(j + 1) * LANE].astype(BF16)
        outs = []
        for e in range(2):
            q = _half_mask(q_pair, e).astype(BF16)
            s_loc = _dot_t(q, k_loc) + b_ref[2 * j + e]
            s_ctx = _dot_t(q, k_ctx)
            outs.append(_pv_normalised(_exp_parts([s_loc, s_ctx]), [v_loc, v_ctx]))
        o_ref[:, j * LANE:(j + 1) * LANE] = jnp.where(lane < HALF, outs[0], outs[1]).astype(o_ref.dtype)


def _na_call(p, kc, vc, rel_bias, *, nb, t, past):
    rows = t // GRID_W
    assert rows % NA_QROWS == 0 and rows >= NA_ROWS
    win, cfgs, meta = _na_plan(rows)
    bias_tab = _na_bias_table(rel_bias, cfgs, win)
    nblk = rows // NA_QROWS
    tq = NA_QROWS * GRID_W
    n_loc = win * GRID_W
    grid_spec = pltpu.PrefetchScalarGridSpec(
        num_scalar_prefetch=1,
        grid=(nb, nblk),
        in_specs=[
            pl.BlockSpec((tq, 512), lambda b, r, meta: (b * nblk + r, C_NQ // 512)),
            pl.BlockSpec((t, 512), lambda b, r, meta: (b, C_NK // 512)),
            pl.BlockSpec((t, 512), lambda b, r, meta: (b, C_NV // 512)),
            pl.BlockSpec((past, 512), lambda b, r, meta: (b, 0)),
            pl.BlockSpec((past, 512), lambda b, r, meta: (b, 0)),
            pl.BlockSpec((None, NA_HEADS, tq, n_loc), lambda b, r, meta: (meta[0, r], 0, 0, 0)),
        ],
        out_specs=pl.BlockSpec((tq, 512), lambda b, r, meta: (b * nblk + r, 0)),
    )
    return pl.pallas_call(
        functools.partial(_na_kernel, win=win),
        grid_spec=grid_spec,
        out_shape=jax.ShapeDtypeStruct((nb * t, 512), BF16),
        compiler_params=_cparams(("parallel", "arbitrary")),
    )(jnp.asarray(meta), p, p, p, kc, vc, bias_tab)


def _log_sigmoid(x):
    return jnp.minimum(x, 0.0) - jnp.log(1.0 + jnp.exp(-jnp.abs(x)))


def _ret_kernel(*refs, has_state, t):
    refs = list(refs)
    q_ref, k_ref, v_ref, rg_ref, dec_ref, gn_ref = [refs.pop(0) for _ in range(6)]
    s0_ref = refs.pop(0) if has_state else None
    o_ref = refs.pop(0)
    tq = q_ref.shape[0]
    q0 = pl.program_id(1) * tq
    lg = _log_sigmoid(dec_ref[...])
    ti = (q0 + lax.broadcasted_iota(jnp.int32, (tq, t), 0)).astype(F32)
    ui = lax.broadcasted_iota(jnp.int32, (tq, t), 1).astype(F32)
    diff = ti - ui
    tcol = (q0 + lax.broadcasted_iota(jnp.int32, (tq, 1), 0)).astype(F32)
    for h in range(RET_HEADS):
        j, e = h // 2, h % 2
        lgf = lg[h:h + 1, 0:1]
        lgb = lg[RET_HEADS + h:RET_HEADS + h + 1, 0:1]
        qm = _half_mask(q_ref[:, j * LANE:(j + 1) * LANE].astype(F32), e)
        q = qm.astype(BF16)
        a = _dot_t((qm * (RET_QK ** -0.5)).astype(BF16), k_ref[:, j * LANE:(j + 1) * LANE].astype(BF16))
        dmat = jnp.exp2(diff * jnp.where(diff >= 0, lgf * LOG2E, -lgb * LOG2E))
        y = _dot((a * dmat).astype(BF16), v_ref[:, h * LANE:(h + 1) * LANE].astype(BF16))
        if has_state:
            cf = _dot(q, s0_ref[0, j].astype(BF16))
            cb = _dot(q, s0_ref[1, j].astype(BF16))
            y = y + cf * jnp.exp((tcol + 1.0) * lgf) + cb * jnp.exp((float(t) - tcol) * lgb)
        mu = jnp.mean(y, axis=-1, keepdims=True)
        yc = y - mu
        var = jnp.mean(yc * yc, axis=-1, keepdims=True)
        yn = yc * lax.rsqrt(var + EPS) * gn_ref[:, h * LANE:(h + 1) * LANE]
        o_ref[:, h * LANE:(h + 1) * LANE] = (_silu(rg_ref[:, h * LANE:(h + 1) * LANE].astype(F32)) * yn).astype(o_ref.dtype)


def _ret_call(p, dec, gn, s0, *, nb, tq, t):
    nq = t // tq
    in_specs = [
        pl.BlockSpec((tq, 256), lambda b, i: (b * nq + i, C_RQ // 256)),
        pl.BlockSpec((t, 256), lambda b, i: (b, C_RK // 256)),
        pl.BlockSpec((t, 512), lambda b, i: (b, C_RV // 512)),
        pl.BlockSpec((tq, 512), lambda b, i: (b * nq + i, C_RG // 512)),
        pl.BlockSpec((8, LANE), lambda b, i: (0, 0)),
        pl.BlockSpec((1, 512), lambda b, i: (0, 0)),
    ]
    args = [p, p, p, p, dec, gn]
    if s0 is not None:
        in_specs.append(pl.BlockSpec((None, 2, 2, LANE, LANE), lambda b, i: (b, 0, 0, 0, 0)))
        args.append(s0)
    return pl.pallas_call(
        functools.partial(_ret_kernel, has_state=s0 is not None, t=t),
        grid=(nb, nq),
        in_specs=in_specs,
        out_specs=pl.BlockSpec((tq, 512), lambda b, i: (b * nq + i, 0)),
        out_shape=jax.ShapeDtypeStruct((nb * t, 512), BF16),
        compiler_params=_cparams(("parallel", "parallel")),
    )(*args)


def _ret_state_kernel(k_ref, v_ref, dec_ref, o_ref, *, t):
    lg = _log_sigmoid(dec_ref[...])
    tcol = lax.broadcasted_iota(jnp.int32, (t, 1), 0).astype(F32)
    lane = _lane_iota((t, LANE))
    for j in range(2):
        k_pair = k_ref[:, j * LANE:(j + 1) * LANE].astype(F32) * (RET_QK ** -0.5)
        for d in range(2):
            expo = (float(t) - 1.0 - tcol) if d == 0 else tcol
            w0 = jnp.exp(expo * lg[d * RET_HEADS + 2 * j:d * RET_HEADS + 2 * j + 1, 0:1])
            w1 = jnp.exp(expo * lg[d * RET_HEADS + 2 * j + 1:d * RET_HEADS + 2 * j + 2, 0:1])
            kd = (k_pair * jnp.where(lane < HALF, w0, w1)).T.astype(BF16)
            for e in range(2):
                h = 2 * j + e
                sfull = _dot(kd, v_ref[:, h * LANE:(h + 1) * LANE].astype(BF16))
                o_ref[d, h] = sfull[e * HALF:(e + 1) * HALF, :]


def _ret_state_call(p, dec, *, nb, t):
    return pl.pallas_call(
        functools.partial(_ret_state_kernel, t=t),
        grid=(nb,),
        in_specs=[
            pl.BlockSpec((t, 256), lambda b: (b, C_RK // 256)),
            pl.BlockSpec((t, 512), lambda b: (b, C_RV // 512)),
            pl.BlockSpec((8, LANE), lambda b: (0, 0)),
        ],
        out_specs=pl.BlockSpec((None, 2, RET_HEADS, RET_QK, RET_V), lambda b: (b, 0, 0, 0, 0)),
        out_shape=jax.ShapeDtypeStruct((nb, 2, RET_HEADS, RET_QK, RET_V), F32),
        compiler_params=_cparams(("parallel",)),
    )(p, p, dec)


def _merge_kernel(ya_ref, yb_ref, yc_ref, yd_ref, g0_ref, g1_ref, g2_ref, g3_ref, w_ref, o_ref):
    ys = (ya_ref, yb_ref, yc_ref, yd_ref)
    gs = (g0_ref, g1_ref, g2_ref, g3_ref)
    m = None
    for i in range(N_BRANCH):
        t = (jnp.tanh(gs[i][...].astype(F32)) + 1.0) * _dot(ys[i][...], w_ref[i])
        m = t if m is None else m + t
    o_ref[...] = m.astype(o_ref.dtype)


def _merge_call(ys, p, w_branch):
    m = p.shape[0]
    tm, tn = min(1024, m), 512
    gate_specs = [
        pl.BlockSpec((tm, tn), functools.partial(lambda r, c, i: (r, (C_GATES + i * D_MODEL) // tn + c), i=i))
        for i in range(N_BRANCH)
    ]
    return pl.pallas_call(
        _merge_kernel,
        grid=(m // tm, D_MODEL // tn),
        in_specs=[pl.BlockSpec((tm, BRANCH_W), lambda r, c: (r, 0))] * N_BRANCH + gate_specs
        + [pl.BlockSpec((N_BRANCH, BRANCH_W, tn), lambda r, c: (0, 0, c))],
        out_specs=pl.BlockSpec((tm, tn), lambda r, c: (r, c)),
        out_shape=jax.ShapeDtypeStruct((m, D_MODEL), BF16),
        compiler_params=_cparams(("parallel", "parallel")),
    )(*ys, p, p, p, p, w_branch)


HALO = 16


SUB = 8


SHIFT_BLK = 256


def _shift_matrix():
    i = np.arange(SHIFT_BLK)
    s = np.zeros((2 * SHIFT_BLK, SHIFT_BLK), np.float32)
    s[i[1:], i[1:] - 1] = 1.0
    s[SHIFT_BLK + i[:-1], i[:-1] + 1] = 1.0
    return jnp.asarray(s, BF16)


def _conv_kernel(s_ref, ua_ref, uap_ref, uan_ref, ub_ref, ubp_ref, ubn_ref, wa_ref, wb_ref, ba_ref, bb_ref, o_ref, *, tiles_per_seq):
    i = pl.program_id(0)
    tm, tc = ua_ref.shape
    nblk = tm // SHIFT_BLK
    has_prev = ((i % tiles_per_seq) != 0).astype(F32)
    has_next = ((i % tiles_per_seq) != tiles_per_seq - 1).astype(F32)
    sub = lax.broadcasted_iota(jnp.int32, (SUB, tc), 0)
    shift = s_ref[...]

    def conv(u_ref, up_ref, un_ref, w_ref, b_ref, r):
        r0 = r * SHIFT_BLK
        ub = u_ref[r0:r0 + SHIFT_BLK, :]
        sh = _dot(shift, ub)
        if r == 0:
            prev = up_ref[HALO - 1:HALO, :].astype(F32) * has_prev
        else:
            prev = u_ref[r0 - 1:r0, :].astype(F32)
        if r == nblk - 1:
            nxt = un_ref[0:1, :].astype(F32) * has_next
        else:
            nxt = u_ref[r0 + SHIFT_BLK:r0 + SHIFT_BLK + 1, :].astype(F32)
        dn = sh[:SHIFT_BLK]
        up = sh[SHIFT_BLK:]
        um1 = jnp.concatenate([jnp.where(sub == 0, prev, dn[:SUB]), dn[SUB:]], axis=0)
        up1 = jnp.concatenate([up[:-SUB], jnp.where(sub == SUB - 1, nxt, up[-SUB:])], axis=0)
        return um1 * w_ref[0:1, :] + ub.astype(F32) * w_ref[1:2, :] + up1 * w_ref[2:3, :] + b_ref[...]

    for r in range(nblk):
        a = conv(ua_ref, uap_ref, uan_ref, wa_ref, ba_ref, r)
        b = conv(ub_ref, ubp_ref, ubn_ref, wb_ref, bb_ref, r)
        o_ref[r * SHIFT_BLK:(r + 1) * SHIFT_BLK, :] = (_silu(a) * b).astype(o_ref.dtype)


def _conv_call(u, conv_w, conv_b, *, seq):
    m = u.shape[0]
    tm, tc = min(seq, 512), 1408
    assert seq % tm == 0 and D_FF % tc == 0
    nb_half = D_FF // tc
    last_halo = m // HALO - 1
    r = tm // HALO

    def main(off):
        return pl.BlockSpec((tm, tc), lambda i, c: (i, c + off))

    def prev(off):
        return pl.BlockSpec((HALO, tc), lambda i, c: (jnp.maximum(i * r - 1, 0), c + off))

    def nxt(off):
        return pl.BlockSpec((HALO, tc), lambda i, c: (jnp.minimum((i + 1) * r, last_halo), c + off))

    def vec(rows, off):
        return pl.BlockSpec((rows, tc), lambda i, c: (0, c + off))

    cb = conv_b.reshape(1, 2 * D_FF)
    return pl.pallas_call(
        functools.partial(_conv_kernel, tiles_per_seq=seq // tm),
        grid=(m // tm, nb_half),
        in_specs=[pl.BlockSpec((2 * SHIFT_BLK, SHIFT_BLK), lambda i, c: (0, 0)),
                  main(0), prev(0), nxt(0), main(nb_half), prev(nb_half), nxt(nb_half),
                  vec(CONV_W, 0), vec(CONV_W, nb_half), vec(1, 0), vec(1, nb_half)],
        out_specs=pl.BlockSpec((tm, tc), lambda i, c: (i, c)),
        out_shape=jax.ShapeDtypeStruct((m, D_FF), BF16),
        compiler_params=_cparams(("parallel", "parallel")),
    )(_shift_matrix(), u, u, u, u, u, u, conv_w, conv_w, cb, cb)


def _final_norm_kernel(x_ref, g_ref, o_ref):
    x = x_ref[...]
    o_ref[...] = x * lax.rsqrt(jnp.mean(x * x, axis=-1, keepdims=True) + EPS) * g_ref[...]


def _final_norm_call(x, g):
    m, d = x.shape
    tm = 256
    return pl.pallas_call(
        _final_norm_kernel,
        grid=(m // tm,),
        in_specs=[pl.BlockSpec((tm, d), lambda i: (i, 0)), pl.BlockSpec((1, d), lambda i: (0, 0))],
        out_specs=pl.BlockSpec((tm, d), lambda i: (i, 0)),
        out_shape=jax.ShapeDtypeStruct((m, d), F32),
        compiler_params=_cparams(("parallel",)),
    )(x, g.reshape(1, d))


def _reorder_w_in(w):
    d = w.shape[0]
    mq = w[:, :768].reshape(d, MLA_HEADS, MLA_NOPE + MLA_ROPE)
    qn = mq[:, :, :MLA_NOPE].reshape(d, 512)
    qr = mq[:, :, MLA_NOPE:].reshape(d, 256)
    ckv = w[:, 768:1024]
    mkr = w[:, 1024:1056]
    rest = w[:, 1056:5664]
    gates = w[:, 5664:] * 0.5
    krblk = jnp.concatenate([jnp.zeros((d, LANE), w.dtype)] + [mkr] * 4, axis=1)
    pad = jnp.zeros((d, C_GATES - C_KR - 256), w.dtype)
    return jnp.concatenate([qn, qr, ckv, rest, krblk, pad, gates], axis=1).astype(BF16)


def _reorder_w_ukv(w):
    r = w.shape[0]
    w3 = w.reshape(r, MLA_HEADS, MLA_NOPE + MLA_V)
    kn = w3[:, :, :MLA_NOPE].reshape(r, 4, LANE)
    vv = w3[:, :, MLA_NOPE:].reshape(r, 512)
    knz = jnp.concatenate([kn, jnp.zeros_like(kn)], axis=2).reshape(r, MLA_K_W)
    return jnp.concatenate([knz, vv], axis=1).astype(BF16)


def _rope_tables(t, dr, reps, lead_zero=0):
    nf = dr // 4
    pos = jnp.arange(t)
    pos = jnp.stack([pos // GRID_W, pos % GRID_W], axis=-1).astype(F32)
    inv = ROPE_BASE ** (-jnp.arange(nf, dtype=F32) / nf)
    ang = pos[:, :, None] * inv
    cos = jnp.cos(ang)
    sin = jnp.sin(ang)
    c = jnp.tile(jnp.concatenate([cos, cos], axis=2).reshape(t, dr), (1, reps))
    s = jnp.tile(jnp.concatenate([-sin, sin], axis=2).reshape(t, dr), (1, reps))
    if lead_zero:
        z = jnp.zeros((t, lead_zero), F32)
        c = jnp.concatenate([z, c], axis=1)
        s = jnp.concatenate([z, s], axis=1)
    return c, s


def _na_bias_table(rel_bias, cfgs, win):
    v = rel_bias.astype(F32)
    pad = GRID_W - NA_COLS
    ext = jnp.concatenate([jnp.repeat(v[..., :1], pad, axis=-1), v, jnp.repeat(v[..., -1:], pad, axis=-1)], axis=-1)
    col_t = jnp.stack([ext[..., GRID_W - 1 - q:2 * GRID_W - 1 - q] for q in range(GRID_W)], axis=2)
    qc = np.arange(GRID_W)[:, None]
    kc = np.arange(GRID_W)[None, :]
    c0 = np.clip(qc - NA_COLS // 2, 0, GRID_W - NA_COLS)
    col_ok = (kc >= c0) & (kc < c0 + NA_COLS)
    col_t = jnp.where(col_ok, col_t, NEG_INF)
    neg = jnp.full((NA_HEADS, GRID_W, GRID_W), NEG_INF, F32)
    kh = min(NA_ROWS, win)
    tabs = []
    for cfg in cfgs:
        per_row = []
        for delta, off in cfg:
            blocks = [col_t[:, kk - delta + NA_ROWS - 1] if off <= kk < off + kh else neg for kk in range(win)]
            per_row.append(jnp.concatenate(blocks, axis=-1))
        tabs.append(jnp.concatenate(per_row, axis=1))
    return jnp.stack(tabs)


def kernel(x_prompt, x_sample, cache_mla_ckv, cache_mla_krope, cache_diff_k, cache_diff_v, cache_na_k, cache_na_v, state_ret, c, c_ctx, w_ada, b_ada, g_attn, g_ffn, w_in, mla_kv_norm, w_mla_ukv, diff_lambda, diff_norm, na_rel_bias, ret_decay, ret_norm, w_branch, w_o, w_up, conv_w, conv_b, w_down, g_final):
    nbc, seq, d = x_prompt.shape
    nbl, t, _ = x_sample.shape
    past = cache_mla_ckv.shape[2]
    depth = w_in.shape[0]
    mc, ml = nbc * seq, nbl * t

    xc = x_prompt.reshape(mc, d)
    xl = x_sample.reshape(ml, d)

    mod_rows = 16
    cond = jnp.concatenate([c_ctx[None], c, jnp.zeros((mod_rows - 1 - nbl, d), F32)], axis=0)
    mods = _ada_call(cond, w_ada, b_ada).reshape(depth, mod_rows, 6, d)

    cos32q, sin32q = _rope_tables(t, MLA_ROPE, 8)
    cos32k, sin32k = _rope_tables(t, MLA_ROPE, 4, lead_zero=LANE)
    cos64, sin64 = _rope_tables(t, DIFF_QK, 16)

    tm_c = min(1024, mc)
    tm_l = min(1024, t)
    states = []
    for l in range(depth):
        lam_init = 0.8 - 0.6 * math.exp(-0.3 * l)
        w_in_r = _reorder_w_in(w_in[l])
        w_ukv_e = _reorder_w_ukv(w_mla_ukv[l])
        wb = (w_branch[l] * 0.5).astype(BF16)
        wo = w_o[l].astype(BF16)
        wup = w_up[l].astype(BF16)
        wdn = w_down[l].astype(BF16)
        kvn = mla_kv_norm[l].reshape(1, MLA_KV_RANK)
        dec = jnp.broadcast_to(ret_decay[l].reshape(2 * RET_HEADS, 1), (2 * RET_HEADS, LANE))
        gn = ret_norm[l].reshape(1, 512)
        dnorm = diff_norm[l].reshape(1, 512)
        g_out = g_final.reshape(1, d) if l == depth - 1 else None
        g_a = g_attn[l].reshape(1, d)
        g_f = g_ffn[l].reshape(1, d)

        def mod(i):
            return mods[l, 0:1, i][:, None, :], mods[l, 1:1 + nbl, i][:, None, :]

        sh1, sc1, ga1, sh2, sc2, ga2 = [mod(i) for i in range(6)]

        pc = _mm_call(xc, w_in_r, out_dtype=F32, tm=tm_c, tn=1024, rows_per_mod=mc, norm=(g_a, sh1[0], sc1[0]))
        kc_mla, vc_mla, ckv_state = _mla_kv_call(pc, C_CKV, pc, C_KR, w_ukv_e, g=kvn)
        ya = _mha_call(pc, C_QN, pc, C_QR, [(kc_mla, 0, vc_mla, 0, seq)], nb=nbc, tq=seq, t=seq,
                       scale=(MLA_NOPE + MLA_ROPE) ** -0.5)
        yb = _diff_call(pc, C_DQ, [(pc, C_DK, pc, C_DV, seq)], diff_lambda[l], dnorm, nb=nbc, tq=seq, t=seq,
                        lam_init=lam_init)
        yc = _mha_call(pc, C_NQ, None, 0, [(pc, C_NK, pc, C_NV, seq)], nb=nbc, tq=seq, t=seq, scale=NA_DIM ** -0.5)
        yd = _ret_call(pc, dec, gn, None, nb=nbc, tq=seq, t=seq)
        sret = _ret_state_call(pc, dec, nb=nbc, t=seq)
        states.append((
            ckv_state.reshape(nbc, seq, MLA_KV_RANK),
            pc[:, C_KR + LANE:C_KR + LANE + MLA_ROPE].reshape(nbc, seq, MLA_ROPE),
            pc[:, C_DK:C_DK + 512].reshape(nbc, seq, DIFF_HEADS, 2, DIFF_QK),
            pc[:, C_DV:C_DV + 512].reshape(nbc, seq, DIFF_HEADS, DIFF_V),
            pc[:, C_NK:C_NK + 512].reshape(nbc, seq, NA_HEADS, NA_DIM),
            pc[:, C_NV:C_NV + 512].reshape(nbc, seq, NA_HEADS, NA_DIM),
            sret,
        ))
        mrg = _merge_call((ya, yb, yc, yd), pc, wb)
        xc = _mm_call(mrg, wo, out_dtype=F32, tm=tm_c, tn=1024, rows_per_mod=mc, resid=(xc, ga1[0]))
        u = _mm_call(xc, wup, out_dtype=BF16, tm=tm_c, tn=1024, rows_per_mod=mc, norm=(g_f, sh2[0], sc2[0]))
        act = _conv_call(u, conv_w[l], conv_b[l], seq=seq)
        xc = _mm_call(act, wdn, out_dtype=F32, tm=256, tn=d, rows_per_mod=mc, resid=(xc, ga2[0]), out_norm=g_out)

        plat = _mm_call(xl, w_in_r, out_dtype=BF16, tm=tm_l, tn=1024, rows_per_mod=t, norm=(g_a, sh1[1], sc1[1]))
        kl_mla, vl_mla, _ = _mla_kv_call(plat, C_CKV, plat, C_KR, w_ukv_e, g=kvn, rope=(cos32k, sin32k), seq=t)
        kr_c = cache_mla_krope[:, l].reshape(nbl * past, MLA_ROPE)
        krblk_c = jnp.concatenate([jnp.zeros((nbl * past, LANE), F32)] + [kr_c] * 4, axis=1)
        kp_mla, vp_mla = _mla_kv_call(cache_mla_ckv[:, l].reshape(nbl * past, MLA_KV_RANK), 0, krblk_c, 0, w_ukv_e)
        qr_l = _rope_call(plat, C_QR, 256, cos32q, sin32q, MLA_ROPE, t)
        dqk_l = _rope_call(plat, C_DQ, 1024, cos64, sin64, DIFF_QK, t)
        tq = min(512, t)
        ya = _mha_call(plat, C_QN, qr_l, 0, [(kl_mla, 0, vl_mla, 0, t), (kp_mla, 0, vp_mla, 0, past)],
                       nb=nbl, tq=tq, t=t, scale=(MLA_NOPE + MLA_ROPE) ** -0.5)
        dk_c = cache_diff_k[:, l].reshape(nbl * past, 512)
        dv_c = cache_diff_v[:, l].reshape(nbl * past, 512)
        yb = _diff_call(dqk_l, 0, [(dqk_l, 512, plat, C_DV, t), (dk_c, 0, dv_c, 0, past)], diff_lambda[l], dnorm,
                        nb=nbl, tq=tq, t=t, lam_init=lam_init)
        yc = _na_call(plat, cache_na_k[:, l].reshape(nbl * past, 512), cache_na_v[:, l].reshape(nbl * past, 512),
                      na_rel_bias[l], nb=nbl, t=t, past=past)
        s0 = state_ret[:, l].astype(F32).reshape(nbl, 2, 2, LANE, LANE)
        yd = _ret_call(plat, dec, gn, s0, nb=nbl, tq=tq, t=t)
        mrg = _merge_call((ya, yb, yc, yd), plat, wb)
        xl = _mm_call(mrg, wo, out_dtype=F32, tm=tm_l, tn=1024, rows_per_mod=t, resid=(xl, ga1[1]))
        u = _mm_call(xl, wup, out_dtype=BF16, tm=tm_l, tn=1024, rows_per_mod=t, norm=(g_f, sh2[1], sc2[1]))
        act = _conv_call(u, conv_w[l], conv_b[l], seq=t)
        xl = _mm_call(act, wdn, out_dtype=F32, tm=256, tn=d, rows_per_mod=t, resid=(xl, ga2[1]), out_norm=g_out)

    y_prompt = xc.reshape(nbc, seq, d)
    y_sample = xl.reshape(nbl, t, d)
    outs = [jnp.stack([st[i] for st in states], axis=1) for i in range(7)]
    return (y_prompt, y_sample, *outs)
```

```python
import functools
import math

import jax
import jax.numpy as jnp
import numpy as np
from jax import lax
from jax.experimental import pallas as pl
from jax.experimental.pallas import tpu as pltpu

F32 = jnp.float32
BF16 = jnp.bfloat16

D_MODEL = 2048
GRID_W = 64
ROPE_BASE = 10000.0
EPS = 1e-6
NEG_INF = -1e30
N_BRANCH = 4
BRANCH_W = 512
MLA_HEADS = 8
MLA_NOPE = 64
MLA_ROPE = 32
MLA_V = 64
MLA_KV_RANK = 256
DIFF_QK = 64
DIFF_V = 128
DIFF_HEADS = 4
NA_DIM = 64
NA_HEADS = 8
NA_ROWS = 8
NA_COLS = 16
RET_V = 128
RET_HEADS = 4
RET_QK = 64
D_FF = 5632
CONV_W = 3

LANE = 128
HALF = 64

C_QN = 0
C_QR = 512
C_CKV = 768
C_DQ = 1024
C_DK = 1536
C_DV = 2048
C_NQ = 2560
C_NK = 3072
C_NV = 3584
C_RQ = 4096
C_RK = 4352
C_RV = 4608
C_RG = 5120
C_KR = 5632
C_GATES = 6144
P_WIDTH = C_GATES + N_BRANCH * D_MODEL

VMEM_LIMIT = 56 * 1024 * 1024


def _cparams(sem):
    return pltpu.CompilerParams(dimension_semantics=sem, vmem_limit_bytes=VMEM_LIMIT)


def _dot(a, b):
    return jnp.dot(a, b, preferred_element_type=F32)


def _dot_t(a, b):
    return lax.dot_general(a, b, (((1,), (1,)), ((), ())), preferred_element_type=F32)


def _sigmoid(x):
    return 0.5 * jnp.tanh(0.5 * x) + 0.5


def _silu(x):
    h = 0.5 * x
    return h * (jnp.tanh(h) + 1.0)


def _lane_iota(shape):
    return lax.broadcasted_iota(jnp.int32, shape, len(shape) - 1)


def _half_mask(x, e):
    lane = _lane_iota(x.shape)
    keep = (lane < HALF) if e == 0 else (lane >= HALF)
    return jnp.where(keep, x, 0.0)


def _rope(x, cos, sin, dr):
    w = x.shape[-1]
    nf = dr // 4
    lane = _lane_iota(x.shape)
    first = (lane % (2 * nf)) < nf
    rot = jnp.where(first, pltpu.roll(x, w - nf, 1), pltpu.roll(x, nf, 1))
    return x * cos + rot * sin


def _ada_kernel(c_ref, w_ref, b_ref, o_ref):
    a = _silu(c_ref[...]).astype(BF16)
    o_ref[...] = _dot(a, w_ref[...].astype(BF16)) + b_ref[...]


def _ada_call(cond, w_ada, b_ada):
    depth, d, n = w_ada.shape
    rows = cond.shape[0]
    tn = 1024
    return pl.pallas_call(
        _ada_kernel,
        grid=(depth, n // tn),
        in_specs=[
            pl.BlockSpec((rows, d), lambda l, j: (0, 0)),
            pl.BlockSpec((None, d, tn), lambda l, j: (l, 0, j)),
            pl.BlockSpec((None, 1, tn), lambda l, j: (l, 0, j)),
        ],
        out_specs=pl.BlockSpec((None, rows, tn), lambda l, j: (l, 0, j)),
        out_shape=jax.ShapeDtypeStruct((depth, rows, n), F32),
        compiler_params=_cparams(("parallel", "parallel")),
    )(cond, w_ada, b_ada.reshape(depth, 1, n))


NORM_ROWS = 128


def _mm_kernel(*refs, norm, resid, out_norm):
    refs = list(refs)
    a_ref = refs.pop(0)
    if norm:
        g_ref, sh_ref, sc_ref = refs.pop(0), refs.pop(0), refs.pop(0)
    w_ref = refs.pop(0)
    if resid:
        r_ref, ga_ref = refs.pop(0), refs.pop(0)
    if out_norm:
        go_ref = refs.pop(0)
    o_ref = refs.pop(0)

    if norm:
        h_ref = refs.pop(0)

        @pl.when(pl.program_id(1) == 0)
        def _():
            g = g_ref[...]
            mul = 1.0 + sc_ref[...]
            add = sh_ref[...]

            def body(c, carry):
                r0 = pl.multiple_of(c * NORM_ROWS, NORM_ROWS)
                x = a_ref[pl.ds(r0, NORM_ROWS), :]
                y = x * lax.rsqrt(jnp.mean(x * x, axis=-1, keepdims=True) + EPS) * g
                h_ref[pl.ds(r0, NORM_ROWS), :] = (y * mul + add).astype(BF16)
                return carry

            lax.fori_loop(0, a_ref.shape[0] // NORM_ROWS, body, 0)

        a = h_ref[...]
    else:
        a = a_ref[...]
    acc = _dot(a, w_ref[...])
    if resid:
        acc = r_ref[...] + ga_ref[...] * acc
    if out_norm:
        acc = acc * lax.rsqrt(jnp.mean(acc * acc, axis=-1, keepdims=True) + EPS) * go_ref[...]
    o_ref[...] = acc.astype(o_ref.dtype)


def _mm_call(a, w, *, out_dtype, tm, tn, rows_per_mod, norm=None, resid=None, out_norm=None):
    m, k = a.shape
    n = w.shape[1]
    assert m % tm == 0 and n % tn == 0 and rows_per_mod % tm == 0
    assert out_norm is None or tn == n
    tiles_per_mod = rows_per_mod // tm
    in_specs = [pl.BlockSpec((tm, k), lambda i, j: (i, 0))]
    args = [a]
    if norm is not None:
        g, sh, sc = norm
        in_specs += [
            pl.BlockSpec((1, k), lambda i, j: (0, 0)),
            pl.BlockSpec((None, 1, k), lambda i, j: (i // tiles_per_mod, 0, 0)),
            pl.BlockSpec((None, 1, k), lambda i, j: (i // tiles_per_mod, 0, 0)),
        ]
        args += [g, sh, sc]
    if tn == n:
        in_specs.append(pl.BlockSpec((k, tn), lambda i, j: (0, 0), pipeline_mode=pl.Buffered(1)))
    else:
        in_specs.append(pl.BlockSpec((k, tn), lambda i, j: (0, j)))
    args.append(w)
    if resid is not None:
        x, ga = resid
        in_specs += [
            pl.BlockSpec((tm, tn), lambda i, j: (i, j)),
            pl.BlockSpec((None, 1, tn), lambda i, j: (i // tiles_per_mod, 0, j)),
        ]
        args += [x, ga]
    if out_norm is not None:
        in_specs.append(pl.BlockSpec((1, n), lambda i, j: (0, 0)))
        args.append(out_norm)
    scratch = [pltpu.VMEM((tm, k), BF16)] if norm is not None else []
    return pl.pallas_call(
        functools.partial(_mm_kernel, norm=norm is not None, resid=resid is not None, out_norm=out_norm is not None),
        grid=(m // tm, n // tn),
        in_specs=in_specs,
        out_specs=pl.BlockSpec((tm, tn), lambda i, j: (i, j)),
        out_shape=jax.ShapeDtypeStruct((m, n), out_dtype),
        scratch_shapes=scratch,
        compiler_params=_cparams(("parallel", "arbitrary")),
    )(*args)


def _rope_kernel(x_ref, c_ref, s_ref, o_ref, *, dr):
    o_ref[...] = _rope(x_ref[...].astype(F32), c_ref[...], s_ref[...], dr).astype(o_ref.dtype)


def _rope_call(p, col, width, cos, sin, dr, seq):
    m = p.shape[0]
    tm = 256
    tpb = seq // tm
    cb = col // width
    assert col % width == 0
    return pl.pallas_call(
        functools.partial(_rope_kernel, dr=dr),
        grid=(tpb, m // seq),
        in_specs=[
            pl.BlockSpec((tm, width), lambda i, b: (b * tpb + i, cb)),
            pl.BlockSpec((tm, width), lambda i, b: (i, 0)),
            pl.BlockSpec((tm, width), lambda i, b: (i, 0)),
        ],
        out_specs=pl.BlockSpec((tm, width), lambda i, b: (b * tpb + i, 0)),
        out_shape=jax.ShapeDtypeStruct((m, width), BF16),
        compiler_params=_cparams(("parallel", "parallel")),
    )(p, cos, sin)


MLA_K_W = 4 * 256
MLA_V_W = 4 * 128


def _mla_kv_kernel(*refs, norm, rope):
    refs = list(refs)
    c_ref, kr_ref = refs.pop(0), refs.pop(0)
    if rope:
        cos_ref, sin_ref = refs.pop(0), refs.pop(0)
    if norm:
        g_ref = refs.pop(0)
    w_ref = refs.pop(0)
    k_ref, v_ref = refs.pop(0), refs.pop(0)
    c = c_ref[...].astype(F32)
    if norm:
        ckv_ref = refs.pop(0)
        c = c * lax.rsqrt(jnp.mean(c * c, axis=-1, keepdims=True) + EPS) * g_ref[...]
        ckv_ref[...] = c
    kv = _dot(c.astype(BF16), w_ref[...])
    kr = kr_ref[...].astype(F32)
    if rope:
        kr = _rope(kr, cos_ref[...], sin_ref[...], MLA_ROPE)
    k_ref[...] = (kv[:, :MLA_K_W] + jnp.concatenate([kr] * 4, axis=1)).astype(BF16)
    v_ref[...] = kv[:, MLA_K_W:].astype(BF16)


def _mla_kv_call(ckv_src, ckv_col, kr_src, kr_col, w_e, *, g=None, rope=None, seq=None):
    m = ckv_src.shape[0]
    tm = 256
    assert m % tm == 0 and ckv_col % 256 == 0 and kr_col % 256 == 0
    in_specs = [
        pl.BlockSpec((tm, 256), lambda i: (i, ckv_col // 256)),
        pl.BlockSpec((tm, 256), lambda i: (i, kr_col // 256)),
    ]
    args = [ckv_src, kr_src]
    if rope is not None:
        tpb = seq // tm
        in_specs += [pl.BlockSpec((tm, 256), lambda i: (i % tpb, 0))] * 2
        args += list(rope)
    if g is not None:
        in_specs.append(pl.BlockSpec((1, 256), lambda i: (0, 0)))
        args.append(g)
    in_specs.append(pl.BlockSpec(w_e.shape, lambda i: (0, 0)))
    args.append(w_e)
    out_specs = [pl.BlockSpec((tm, MLA_K_W), lambda i: (i, 0)), pl.BlockSpec((tm, MLA_V_W), lambda i: (i, 0))]
    out_shape = [jax.ShapeDtypeStruct((m, MLA_K_W), BF16), jax.ShapeDtypeStruct((m, MLA_V_W), BF16)]
    if g is not None:
        out_specs.append(pl.BlockSpec((tm, 256), lambda i: (i, 0)))
        out_shape.append(jax.ShapeDtypeStruct((m, 256), F32))
    return pl.pallas_call(
        functools.partial(_mla_kv_kernel, norm=g is not None, rope=rope is not None),
        grid=(m // tm,),
        in_specs=in_specs,
        out_specs=out_specs,
        out_shape=out_shape,
        compiler_params=_cparams(("parallel",)),
    )(*args)


LOG2E = 1.4426950408889634


def _exp_parts(scores, scale=1.0):
    m = scores[0].max(axis=-1, keepdims=True)
    for s in scores[1:]:
        m = jnp.maximum(m, s.max(axis=-1, keepdims=True))
    return [jnp.exp2((s - m) * (scale * LOG2E)) for s in scores]


def _softmax_parts(scores, scale=1.0):
    ps = _exp_parts(scores, scale)
    l = ps[0].sum(axis=-1, keepdims=True)
    for p in ps[1:]:
        l = l + p.sum(axis=-1, keepdims=True)
    return ps, 1.0 / l


def _pv_normalised(ps, vs):
    acc = None
    for p, v in zip(ps, vs):
        t = _dot(p.astype(BF16), jnp.concatenate([v, jnp.ones_like(v)], axis=1))
        acc = t if acc is None else acc + t
    return acc[:, :LANE] / acc[:, LANE:]


def _mha_kernel(*refs, mla, nseg, scale):
    refs = list(refs)
    qn_ref = refs.pop(0)
    qr_ref = refs.pop(0) if mla else None
    kv_refs = [(refs.pop(0), refs.pop(0)) for _ in range(nseg)]
    o_ref = refs.pop(0)
    kw = 256 if mla else LANE
    lane = _lane_iota((qn_ref.shape[0], LANE))
    nheads = 8

    def head_scores(h):
        j, e = h // 2, h % 2
        q = _half_mask(qn_ref[:, j * LANE:(j + 1) * LANE].astype(F32), e)
        if mla:
            qr = qr_ref[:, (h // 4) * LANE:(h // 4 + 1) * LANE].astype(F32)
            qr = jnp.where(lane // MLA_ROPE == h % 4, qr, 0.0)
            q = jnp.concatenate([q, qr], axis=1)
        q = q.astype(BF16)
        return [_dot_t(q, k_ref[:, j * kw:(j + 1) * kw].astype(BF16)) for k_ref, _ in kv_refs]

    outs = []

    def head_values(h, ps):
        j = h // 2
        outs.append(_pv_normalised(ps, [v_ref[:, j * LANE:(j + 1) * LANE].astype(BF16) for _, v_ref in kv_refs]))
        if h % 2 == 1:
            o_ref[:, j * LANE:(j + 1) * LANE] = jnp.where(lane < HALF, outs[0], outs[1]).astype(o_ref.dtype)
            outs.clear()

    scores = head_scores(0)
    pending = None
    for h in range(nheads):
        nxt = head_scores(h + 1) if h + 1 < nheads else None
        ps = _exp_parts(scores, scale)
        if pending is not None:
            head_values(*pending)
        pending = (h, ps)
        scores = nxt
    head_values(*pending)


def _mha_call(qn, qn_col, qr, qr_col, segs, *, nb, tq, t, scale):
    mla = qr is not None
    kw = MLA_K_W if mla else 512
    nq = t // tq
    in_specs = [pl.BlockSpec((tq, 512), lambda b, i: (b * nq + i, qn_col // 512))]
    args = [qn]
    if mla:
        in_specs.append(pl.BlockSpec((tq, 256), lambda b, i: (b * nq + i, qr_col // 256)))
        args.append(qr)
    for k_arr, k_col, v_arr, v_col, s in segs:
        assert k_col % kw == 0 and v_col % 512 == 0
        in_specs.append(pl.BlockSpec((s, kw), functools.partial(lambda b, i, c: (b, c), c=k_col // kw)))
        in_specs.append(pl.BlockSpec((s, 512), functools.partial(lambda b, i, c: (b, c), c=v_col // 512)))
        args += [k_arr, v_arr]
    return pl.pallas_call(
        functools.partial(_mha_kernel, mla=mla, nseg=len(segs), scale=scale),
        grid=(nb, nq),
        in_specs=in_specs,
        out_specs=pl.BlockSpec((tq, 512), lambda b, i: (b * nq + i, 0)),
        out_shape=jax.ShapeDtypeStruct((nb * t, 512), BF16),
        compiler_params=_cparams(("parallel", "parallel")),
    )(*args)


def _diff_kernel(*refs, nseg, lam_init):
    refs = list(refs)
    q_ref = refs.pop(0)
    kv_refs = [(refs.pop(0), refs.pop(0)) for _ in range(nseg)]
    lp_ref, g_ref, o_ref = refs
    lp = lp_ref[...]
    lam = (jnp.exp(jnp.sum(lp[0:1] * lp[1:2], axis=-1, keepdims=True))
           - jnp.exp(jnp.sum(lp[2:3] * lp[3:4], axis=-1, keepdims=True)) + lam_init)
    scale = DIFF_QK ** -0.5

    def head_scores(h):
        q_pair = q_ref[:, h * LANE:(h + 1) * LANE].astype(F32)
        return [[_dot_t(_half_mask(q_pair, c).astype(BF16), k_ref[:, h * LANE:(h + 1) * LANE].astype(BF16))
                 for k_ref, _ in kv_refs] for c in range(2)]

    scores = head_scores(0)
    for h in range(DIFF_HEADS):
        nxt = head_scores(h + 1) if h + 1 < DIFF_HEADS else None
        (ps0, inv0), (ps1, inv1) = [_softmax_parts(sc, scale) for sc in scores]
        o = None
        for s in range(nseg):
            a = (ps0[s] * inv0 - lam * (ps1[s] * inv1)).astype(BF16)
            t = _dot(a, kv_refs[s][1][:, h * LANE:(h + 1) * LANE].astype(BF16))
            o = t if o is None else o + t
        o = o * lax.rsqrt(jnp.mean(o * o, axis=-1, keepdims=True) + EPS) * g_ref[:, h * LANE:(h + 1) * LANE]
        o_ref[:, h * LANE:(h + 1) * LANE] = (o * (1.0 - lam_init)).astype(o_ref.dtype)
        scores = nxt


def _diff_call(q, q_col, segs, lp, g, *, nb, tq, t, lam_init):
    nq = t // tq
    in_specs = [pl.BlockSpec((tq, 512), lambda b, i: (b * nq + i, q_col // 512))]
    args = [q]
    for k_arr, k_col, v_arr, v_col, s in segs:
        in_specs.append(pl.BlockSpec((s, 512), functools.partial(lambda b, i, c: (b, c), c=k_col // 512)))
        in_specs.append(pl.BlockSpec((s, 512), functools.partial(lambda b, i, c: (b, c), c=v_col // 512)))
        args += [k_arr, v_arr]
    in_specs += [pl.BlockSpec((4, DIFF_QK), lambda b, i: (0, 0)), pl.BlockSpec((1, 512), lambda b, i: (0, 0))]
    args += [lp, g]
    return pl.pallas_call(
        functools.partial(_diff_kernel, nseg=len(segs), lam_init=lam_init),
        grid=(nb, nq),
        in_specs=in_specs,
        out_specs=pl.BlockSpec((tq, 512), lambda b, i: (b * nq + i, 0)),
        out_shape=jax.ShapeDtypeStruct((nb * t, 512), BF16),
        compiler_params=_cparams(("parallel", "parallel")),
    )(*args)


NA_QROWS = 4


def _na_plan(rows):
    win = min(rows, NA_QROWS + NA_ROWS)
    kh = min(NA_ROWS, rows)
    cfgs, cfg_idx, starts = [], [], []
    for blk in range(rows // NA_QROWS):
        r0 = blk * NA_QROWS
        s_blk = int(np.clip(r0 - kh // 2, 0, rows - win))
        cfg = tuple((r0 + i - s_blk, int(np.clip(r0 + i - kh // 2, 0, rows - kh)) - s_blk) for i in range(NA_QROWS))
        if cfg not in cfgs:
            cfgs.append(cfg)
        cfg_idx.append(cfgs.index(cfg))
        starts.append(s_blk)
    return win, cfgs, np.array([cfg_idx, starts], np.int32)


def _na_kernel(meta_ref, q_ref, k_ref, v_ref, kc_ref, vc_ref, b_ref, o_ref, *, win):
    start = pl.multiple_of(meta_ref[1, pl.program_id(1)] * GRID_W, GRID_W)
    n_loc = win * GRID_W
    lane = _lane_iota((q_ref.shape[0], LANE))

    def head_scores(h):
        j, e = h // 2, h % 2
        q = _half_mask(q_ref[:, j * LANE:(j + 1) * LANE].astype(F32) * (NA_DIM ** -0.5), e).astype(BF16)
        s_loc = _dot_t(q, k_ref[pl.ds(start, n_loc), j * LANE:(j + 1) * LANE].astype(BF16))
        s_ctx = _dot_t(q, kc_ref[:, j * LANE:(j + 1) * LANE].astype(BF16))
        return [s_loc + b_ref[h], s_ctx]

    scores = head_scores(0)
    outs = []
    for h in range(NA_HEADS):
        nxt = head_scores(h + 1) if h + 1 < NA_HEADS else None
        j = h // 2
        v_loc = v_ref[pl.ds(start, n_loc), j * LANE:(j + 1) * LANE].astype(BF16)
        v_ctx = vc_ref[:, j * LANE:(j + 1) * LANE].astype(BF16)
        outs.append(_pv_normalised(_exp_parts(scores), [v_loc, v_ctx]))
        if h % 2 == 1:
            o_ref[:, j * LANE:(j + 1) * LANE] = jnp.where(lane < HALF, outs[0], outs[1]).astype(o_ref.dtype)
            outs = []
        scores = nxt


def _na_call(p, kc, vc, rel_bias, *, nb, t, past):
    rows = t // GRID_W
    assert rows % NA_QROWS == 0 and rows >= NA_ROWS
    win, cfgs, meta = _na_plan(rows)
    bias_tab = _na_bias_table(rel_bias, cfgs, win)
    nblk = rows // NA_QROWS
    tq = NA_QROWS * GRID_W
    n_loc = win * GRID_W
    grid_spec = pltpu.PrefetchScalarGridSpec(
        num_scalar_prefetch=1,
        grid=(nb, nblk),
        in_specs=[
            pl.BlockSpec((tq, 512), lambda b, r, meta: (b * nblk + r, C_NQ // 512)),
            pl.BlockSpec((t, 512), lambda b, r, meta: (b, C_NK // 512)),
            pl.BlockSpec((t, 512), lambda b, r, meta: (b, C_NV // 512)),
            pl.BlockSpec((past, 512), lambda b, r, meta: (b, 0)),
            pl.BlockSpec((past, 512), lambda b, r, meta: (b, 0)),
            pl.BlockSpec((None, NA_HEADS, tq, n_loc), lambda b, r, meta: (meta[0, r], 0, 0, 0)),
        ],
        out_specs=pl.BlockSpec((tq, 512), lambda b, r, meta: (b * nblk + r, 0)),
    )
    return pl.pallas_call(
        functools.partial(_na_kernel, win=win),
        grid_spec=grid_spec,
        out_shape=jax.ShapeDtypeStruct((nb * t, 512), BF16),
        compiler_params=_cparams(("parallel", "arbitrary")),
    )(jnp.asarray(meta), p, p, p, kc, vc, bias_tab)


def _log_sigmoid(x):
    return jnp.minimum(x, 0.0) - jnp.log(1.0 + jnp.exp(-jnp.abs(x)))


def _ret_kernel(*refs, has_state, t):
    refs = list(refs)
    q_ref, k_ref, v_ref, rg_ref, dec_ref, gn_ref = [refs.pop(0) for _ in range(6)]
    s0_ref = refs.pop(0) if has_state else None
    o_ref = refs.pop(0)
    tq = q_ref.shape[0]
    q0 = pl.program_id(1) * tq
    lg = _log_sigmoid(dec_ref[...])
    ti = (q0 + lax.broadcasted_iota(jnp.int32, (tq, t), 0)).astype(F32)
    ui = lax.broadcasted_iota(jnp.int32, (tq, t), 1).astype(F32)
    diff = ti - ui
    tcol = (q0 + lax.broadcasted_iota(jnp.int32, (tq, 1), 0)).astype(F32)

    def head_scores(h):
        j, e = h // 2, h % 2
        qm = _half_mask(q_ref[:, j * LANE:(j + 1) * LANE].astype(F32), e)
        return qm, _dot_t((qm * (RET_QK ** -0.5)).astype(BF16), k_ref[:, j * LANE:(j + 1) * LANE].astype(BF16))

    cur = head_scores(0)
    for h in range(RET_HEADS):
        nxt = head_scores(h + 1) if h + 1 < RET_HEADS else None
        j = h // 2
        qm, a = cur
        cur = nxt
        lgf = lg[h:h + 1, 0:1]
        lgb = lg[RET_HEADS + h:RET_HEADS + h + 1, 0:1]
        q = qm.astype(BF16)
        dmat = jnp.exp2(diff * jnp.where(diff >= 0, lgf * LOG2E, -lgb * LOG2E))
        y = _dot((a * dmat).astype(BF16), v_ref[:, h * LANE:(h + 1) * LANE].astype(BF16))
        if has_state:
            cf = _dot(q, s0_ref[0, j].astype(BF16))
            cb = _dot(q, s0_ref[1, j].astype(BF16))
            y = y + cf * jnp.exp((tcol + 1.0) * lgf) + cb * jnp.exp((float(t) - tcol) * lgb)
        mu = jnp.mean(y, axis=-1, keepdims=True)
        yc = y - mu
        var = jnp.mean(yc * yc, axis=-1, keepdims=True)
        yn = yc * lax.rsqrt(var + EPS) * gn_ref[:, h * LANE:(h + 1) * LANE]
        o_ref[:, h * LANE:(h + 1) * LANE] = (_silu(rg_ref[:, h * LANE:(h + 1) * LANE].astype(F32)) * yn).astype(o_ref.dtype)


def _ret_call(p, dec, gn, s0, *, nb, tq, t):
    nq = t // tq
    in_specs = [
        pl.BlockSpec((tq, 256), lambda b, i: (b * nq + i, C_RQ // 256)),
        pl.BlockSpec((t, 256), lambda b, i: (b, C_RK // 256)),
        pl.BlockSpec((t, 512), lambda b, i: (b, C_RV // 512)),
        pl.BlockSpec((tq, 512), lambda b, i: (b * nq + i, C_RG // 512)),
        pl.BlockSpec((8, LANE), lambda b, i: (0, 0)),
        pl.BlockSpec((1, 512), lambda b, i: (0, 0)),
    ]
    args = [p, p, p, p, dec, gn]
    if s0 is not None:
        in_specs.append(pl.BlockSpec((None, 2, 2, LANE, LANE), lambda b, i: (b, 0, 0, 0, 0)))
        args.append(s0)
    return pl.pallas_call(
        functools.partial(_ret_kernel, has_state=s0 is not None, t=t),
        grid=(nb, nq),
        in_specs=in_specs,
        out_specs=pl.BlockSpec((tq, 512), lambda b, i: (b * nq + i, 0)),
        out_shape=jax.ShapeDtypeStruct((nb * t, 512), BF16),
        compiler_params=_cparams(("parallel", "parallel")),
    )(*args)


def _ret_state_kernel(k_ref, v_ref, dec_ref, o_ref, *, t):
    lg = _log_sigmoid(dec_ref[...])
    tcol = lax.broadcasted_iota(jnp.int32, (t, 1), 0).astype(F32)
    lane = _lane_iota((t, LANE))
    for j in range(2):
        k_pair = k_ref[:, j * LANE:(j + 1) * LANE].astype(F32) * (RET_QK ** -0.5)
        for d in range(2):
            expo = (float(t) - 1.0 - tcol) if d == 0 else tcol
            w0 = jnp.exp(expo * lg[d * RET_HEADS + 2 * j:d * RET_HEADS + 2 * j + 1, 0:1])
            w1 = jnp.exp(expo * lg[d * RET_HEADS + 2 * j + 1:d * RET_HEADS + 2 * j + 2, 0:1])
            kd = (k_pair * jnp.where(lane < HALF, w0, w1)).T.astype(BF16)
            for e in range(2):
                h = 2 * j + e
                sfull = _dot(kd, v_ref[:, h * LANE:(h + 1) * LANE].astype(BF16))
                o_ref[d, h] = sfull[e * HALF:(e + 1) * HALF, :]


def _ret_state_call(p, dec, *, nb, t):
    return pl.pallas_call(
        functools.partial(_ret_state_kernel, t=t),
        grid=(nb,),
        in_specs=[
            pl.BlockSpec((t, 256), lambda b: (b, C_RK // 256)),
            pl.BlockSpec((t, 512), lambda b: (b, C_RV // 512)),
            pl.BlockSpec((8, LANE), lambda b: (0, 0)),
        ],
        out_specs=pl.BlockSpec((None, 2, RET_HEADS, RET_QK, RET_V), lambda b: (b, 0, 0, 0, 0)),
        out_shape=jax.ShapeDtypeStruct((nb, 2, RET_HEADS, RET_QK, RET_V), F32),
        compiler_params=_cparams(("parallel",)),
    )(p, p, dec)


def _merge_kernel(ya_ref, yb_ref, yc_ref, yd_ref, g0_ref, g1_ref, g2_ref, g3_ref, w_ref, o_ref):
    ys = (ya_ref, yb_ref, yc_ref, yd_ref)
    gs = (g0_ref, g1_ref, g2_ref, g3_ref)
    m = None
    for i in range(N_BRANCH):
        t = (jnp.tanh(gs[i][...].astype(F32)) + 1.0) * _dot(ys[i][...], w_ref[i])
        m = t if m is None else m + t
    o_ref[...] = m.astype(o_ref.dtype)


def _merge_call(ys, p, w_branch):
    m = p.shape[0]
    tm, tn = min(1024, m), 512
    gate_specs = [
        pl.BlockSpec((tm, tn), functools.partial(lambda r, c, i: (r, (C_GATES + i * D_MODEL) // tn + c), i=i))
        for i in range(N_BRANCH)
    ]
    return pl.pallas_call(
        _merge_kernel,
        grid=(m // tm, D_MODEL // tn),
        in_specs=[pl.BlockSpec((tm, BRANCH_W), lambda r, c: (r, 0))] * N_BRANCH + gate_specs
        + [pl.BlockSpec((N_BRANCH, BRANCH_W, tn), lambda r, c: (0, 0, c))],
        out_specs=pl.BlockSpec((tm, tn), lambda r, c: (r, c)),
        out_shape=jax.ShapeDtypeStruct((m, D_MODEL), BF16),
        compiler_params=_cparams(("parallel", "parallel")),
    )(*ys, p, p, p, p, w_branch)


HALO = 16


SUB = 8


SHIFT_BLK = 256


def _shift_matrix():
    i = np.arange(SHIFT_BLK)
    s = np.zeros((2 * SHIFT_BLK, SHIFT_BLK), np.float32)
    s[i[1:], i[1:] - 1] = 1.0
    s[SHIFT_BLK + i[:-1], i[:-1] + 1] = 1.0
    return jnp.asarray(s, BF16)


def _conv_kernel(s_ref, ua_ref, uap_ref, uan_ref, ub_ref, ubp_ref, ubn_ref, wa_ref, wb_ref, ba_ref, bb_ref, o_ref, *, tiles_per_seq):
    i = pl.program_id(0)
    tm, tc = ua_ref.shape
    nblk = tm // SHIFT_BLK
    has_prev = ((i % tiles_per_seq) != 0).astype(F32)
    has_next = ((i % tiles_per_seq) != tiles_per_seq - 1).astype(F32)
    sub = lax.broadcasted_iota(jnp.int32, (SUB, tc), 0)
    shift = s_ref[...]

    def conv(u_ref, up_ref, un_ref, w_ref, b_ref, r):
        r0 = r * SHIFT_BLK
        ub = u_ref[r0:r0 + SHIFT_BLK, :]
        sh = _dot(shift, ub)
        if r == 0:
            prev = up_ref[HALO - 1:HALO, :].astype(F32) * has_prev
        else:
            prev = u_ref[r0 - 1:r0, :].astype(F32)
        if r == nblk - 1:
            nxt = un_ref[0:1, :].astype(F32) * has_next
        else:
            nxt = u_ref[r0 + SHIFT_BLK:r0 + SHIFT_BLK + 1, :].astype(F32)
        dn = sh[:SHIFT_BLK]
        up = sh[SHIFT_BLK:]
        um1 = jnp.concatenate([jnp.where(sub == 0, prev, dn[:SUB]), dn[SUB:]], axis=0)
        up1 = jnp.concatenate([up[:-SUB], jnp.where(sub == SUB - 1, nxt, up[-SUB:])], axis=0)
        return um1 * w_ref[0:1, :] + ub.astype(F32) * w_ref[1:2, :] + up1 * w_ref[2:3, :] + b_ref[...]

    for r in range(nblk):
        a = conv(ua_ref, uap_ref, uan_ref, wa_ref, ba_ref, r)
        b = conv(ub_ref, ubp_ref, ubn_ref, wb_ref, bb_ref, r)
        o_ref[r * SHIFT_BLK:(r + 1) * SHIFT_BLK, :] = (_silu(a) * b).astype(o_ref.dtype)


def _conv_call(u, conv_w, conv_b, *, seq):
    m = u.shape[0]
    tm, tc = min(seq, 512), 1408
    assert seq % tm == 0 and D_FF % tc == 0
    nb_half = D_FF // tc
    last_halo = m // HALO - 1
    r = tm // HALO

    def main(off):
        return pl.BlockSpec((tm, tc), lambda i, c: (i, c + off))

    def prev(off):
        return pl.BlockSpec((HALO, tc), lambda i, c: (jnp.maximum(i * r - 1, 0), c + off))

    def nxt(off):
        return pl.BlockSpec((HALO, tc), lambda i, c: (jnp.minimum((i + 1) * r, last_halo), c + off))

    def vec(rows, off):
        return pl.BlockSpec((rows, tc), lambda i, c: (0, c + off))

    cb = conv_b.reshape(1, 2 * D_FF)
    return pl.pallas_call(
        functools.partial(_conv_kernel, tiles_per_seq=seq // tm),
        grid=(m // tm, nb_half),
        in_specs=[pl.BlockSpec((2 * SHIFT_BLK, SHIFT_BLK), lambda i, c: (0, 0)),
                  main(0), prev(0), nxt(0), main(nb_half), prev(nb_half), nxt(nb_half),
                  vec(CONV_W, 0), vec(CONV_W, nb_half), vec(1, 0), vec(1, nb_half)],
        out_specs=pl.BlockSpec((tm, tc), lambda i, c: (i, c)),
        out_shape=jax.ShapeDtypeStruct((m, D_FF), BF16),
        compiler_params=_cparams(("parallel", "parallel")),
    )(_shift_matrix(), u, u, u, u, u, u, conv_w, conv_w, cb, cb)


def _final_norm_kernel(x_ref, g_ref, o_ref):
    x = x_ref[...]
    o_ref[...] = x * lax.rsqrt(jnp.mean(x * x, axis=-1, keepdims=True) + EPS) * g_ref[...]


def _final_norm_call(x, g):
    m, d = x.shape
    tm = 256
    return pl.pallas_call(
        _final_norm_kernel,
        grid=(m // tm,),
        in_specs=[pl.BlockSpec((tm, d), lambda i: (i, 0)), pl.BlockSpec((1, d), lambda i: (0, 0))],
        out_specs=pl.BlockSpec((tm, d), lambda i: (i, 0)),
        out_shape=jax.ShapeDtypeStruct((m, d), F32),
        compiler_params=_cparams(("parallel",)),
    )(x, g.reshape(1, d))


def _reorder_w_in(w):
    d = w.shape[0]
    mq = w[:, :768].reshape(d, MLA_HEADS, MLA_NOPE + MLA_ROPE)
    qn = mq[:, :, :MLA_NOPE].reshape(d, 512)
    qr = mq[:, :, MLA_NOPE:].reshape(d, 256)
    ckv = w[:, 768:1024]
    mkr = w[:, 1024:1056]
    rest = w[:, 1056:5664]
    gates = w[:, 5664:] * 0.5
    krblk = jnp.concatenate([jnp.zeros((d, LANE), w.dtype)] + [mkr] * 4, axis=1)
    pad = jnp.zeros((d, C_GATES - C_KR - 256), w.dtype)
    return jnp.concatenate([qn, qr, ckv, rest, krblk, pad, gates], axis=1).astype(BF16)


def _reorder_w_ukv(w):
    r = w.shape[0]
    w3 = w.reshape(r, MLA_HEADS, MLA_NOPE + MLA_V)
    kn = w3[:, :, :MLA_NOPE].reshape(r, 4, LANE)
    vv = w3[:, :, MLA_NOPE:].reshape(r, 512)
    knz = jnp.concatenate([kn, jnp.zeros_like(kn)], axis=2).reshape(r, MLA_K_W)
    return jnp.concatenate([knz, vv], axis=1).astype(BF16)


def _rope_tables(t, dr, reps, lead_zero=0):
    nf = dr // 4
    pos = jnp.arange(t)
    pos = jnp.stack([pos // GRID_W, pos % GRID_W], axis=-1).astype(F32)
    inv = ROPE_BASE ** (-jnp.arange(nf, dtype=F32) / nf)
    ang = pos[:, :, None] * inv
    cos = jnp.cos(ang)
    sin = jnp.sin(ang)
    c = jnp.tile(jnp.concatenate([cos, cos], axis=2).reshape(t, dr), (1, reps))
    s = jnp.tile(jnp.concatenate([-sin, sin], axis=2).reshape(t, dr), (1, reps))
    if lead_zero:
        z = jnp.zeros((t, lead_zero), F32)
        c = jnp.concatenate([z, c], axis=1)
        s = jnp.concatenate([z, s], axis=1)
    return c, s


def _na_bias_table(rel_bias, cfgs, win):
    v = rel_bias.astype(F32)
    pad = GRID_W - NA_COLS
    ext = jnp.concatenate([jnp.repeat(v[..., :1], pad, axis=-1), v, jnp.repeat(v[..., -1:], pad, axis=-1)], axis=-1)
    col_t = jnp.stack([ext[..., GRID_W - 1 - q:2 * GRID_W - 1 - q] for q in range(GRID_W)], axis=2)
    qc = np.arange(GRID_W)[:, None]
    kc = np.arange(GRID_W)[None, :]
    c0 = np.clip(qc - NA_COLS // 2, 0, GRID_W - NA_COLS)
    col_ok = (kc >= c0) & (kc < c0 + NA_COLS)
    col_t = jnp.where(col_ok, col_t, NEG_INF)
    neg = jnp.full((NA_HEADS, GRID_W, GRID_W), NEG_INF, F32)
    kh = min(NA_ROWS, win)
    tabs = []
    for cfg in cfgs:
        per_row = []
        for delta, off in cfg:
            blocks = [col_t[:, kk - delta + NA_ROWS - 1] if off <= kk < off + kh else neg for kk in range(win)]
            per_row.append(jnp.concatenate(blocks, axis=-1))
        tabs.append(jnp.concatenate(per_row, axis=1))
    return jnp.stack(tabs)


def kernel(x_prompt, x_sample, cache_mla_ckv, cache_mla_krope, cache_diff_k, cache_diff_v, cache_na_k, cache_na_v, state_ret, c, c_ctx, w_ada, b_ada, g_attn, g_ffn, w_in, mla_kv_norm, w_mla_ukv, diff_lambda, diff_norm, na_rel_bias, ret_decay, ret_norm, w_branch, w_o, w_up, conv_w, conv_b, w_down, g_final):
    nbc, seq, d = x_prompt.shape
    nbl, t, _ = x_sample.shape
    past = cache_mla_ckv.shape[2]
    depth = w_in.shape[0]
    mc, ml = nbc * seq, nbl * t

    xc = x_prompt.reshape(mc, d)
    xl = x_sample.reshape(ml, d)

    mod_rows = 16
    cond = jnp.concatenate([c_ctx[None], c, jnp.zeros((mod_rows - 1 - nbl, d), F32)], axis=0)
    mods = _ada_call(cond, w_ada, b_ada).reshape(depth, mod_rows, 6, d)

    cos32q, sin32q = _rope_tables(t, MLA_ROPE, 8)
    cos32k, sin32k = _rope_tables(t, MLA_ROPE, 4, lead_zero=LANE)
    cos64, sin64 = _rope_tables(t, DIFF_QK, 16)

    tm_c = min(1024, mc)
    tm_l = min(1024, t)
    states = []
    for l in range(depth):
        lam_init = 0.8 - 0.6 * math.exp(-0.3 * l)
        w_in_r = _reorder_w_in(w_in[l])
        w_ukv_e = _reorder_w_ukv(w_mla_ukv[l])
        wb = (w_branch[l] * 0.5).astype(BF16)
        wo = w_o[l].astype(BF16)
        wup = w_up[l].astype(BF16)
        wdn = w_down[l].astype(BF16)
        kvn = mla_kv_norm[l].reshape(1, MLA_KV_RANK)
        dec = jnp.broadcast_to(ret_decay[l].reshape(2 * RET_HEADS, 1), (2 * RET_HEADS, LANE))
        gn = ret_norm[l].reshape(1, 512)
        dnorm = diff_norm[l].reshape(1, 512)
        g_out = g_final.reshape(1, d) if l == depth - 1 else None
        g_a = g_attn[l].reshape(1, d)
        g_f = g_ffn[l].reshape(1, d)

        def mod(i):
            return mods[l, 0:1, i][:, None, :], mods[l, 1:1 + nbl, i][:, None, :]

        sh1, sc1, ga1, sh2, sc2, ga2 = [mod(i) for i in range(6)]

        pc = _mm_call(xc, w_in_r, out_dtype=F32, tm=tm_c, tn=1024, rows_per_mod=mc, norm=(g_a, sh1[0], sc1[0]))
        kc_mla, vc_mla, ckv_state = _mla_kv_call(pc, C_CKV, pc, C_KR, w_ukv_e, g=kvn)
        ya = _mha_call(pc, C_QN, pc, C_QR, [(kc_mla, 0, vc_mla, 0, seq)], nb=nbc, tq=seq, t=seq,
                       scale=(MLA_NOPE + MLA_ROPE) ** -0.5)
        yb = _diff_call(pc, C_DQ, [(pc, C_DK, pc, C_DV, seq)], diff_lambda[l], dnorm, nb=nbc, tq=seq, t=seq,
                        lam_init=lam_init)
        yc = _mha_call(pc, C_NQ, None, 0, [(pc, C_NK, pc, C_NV, seq)], nb=nbc, tq=seq, t=seq, scale=NA_DIM ** -0.5)
        yd = _ret_call(pc, dec, gn, None, nb=nbc, tq=seq, t=seq)
        sret = _ret_state_call(pc, dec, nb=nbc, t=seq)
        states.append((
            ckv_state.reshape(nbc, seq, MLA_KV_RANK),
            pc[:, C_KR + LANE:C_KR + LANE + MLA_ROPE].reshape(nbc, seq, MLA_ROPE),
            pc[:, C_DK:C_DK + 512].reshape(nbc, seq, DIFF_HEADS, 2, DIFF_QK),
            pc[:, C_DV:C_DV + 512].reshape(nbc, seq, DIFF_HEADS, DIFF_V),
            pc[:, C_NK:C_NK + 512].reshape(nbc, seq, NA_HEADS, NA_DIM),
            pc[:, C_NV:C_NV + 512].reshape(nbc, seq, NA_HEADS, NA_DIM),
            sret,
        ))
        mrg = _merge_call((ya, yb, yc, yd), pc, wb)
        xc = _mm_call(mrg, wo, out_dtype=F32, tm=tm_c, tn=1024, rows_per_mod=mc, resid=(xc, ga1[0]))
        u = _mm_call(xc, wup, out_dtype=BF16, tm=tm_c, tn=1024, rows_per_mod=mc, norm=(g_f, sh2[0], sc2[0]))
        act = _conv_call(u, conv_w[l], conv_b[l], seq=seq)
        xc = _mm_call(act, wdn, out_dtype=F32, tm=256, tn=d, rows_per_mod=mc, resid=(xc, ga2[0]), out_norm=g_out)

        plat = _mm_call(xl, w_in_r, out_dtype=BF16, tm=tm_l, tn=1024, rows_per_mod=t, norm=(g_a, sh1[1], sc1[1]))
        kl_mla, vl_mla, _ = _mla_kv_call(plat, C_CKV, plat, C_KR, w_ukv_e, g=kvn, rope=(cos32k, sin32k), seq=t)
        kr_c = cache_mla_krope[:, l].reshape(nbl * past, MLA_ROPE)
        krblk_c = jnp.concatenate([jnp.zeros((nbl * past, LANE), F32)] + [kr_c] * 4, axis=1)
        kp_mla, vp_mla = _mla_kv_call(cache_mla_ckv[:, l].reshape(nbl * past, MLA_KV_RANK), 0, krblk_c, 0, w_ukv_e)
        qr_l = _rope_call(plat, C_QR, 256, cos32q, sin32q, MLA_ROPE, t)
        dqk_l = _rope_call(plat, C_DQ, 1024, cos64, sin64, DIFF_QK, t)
        tq = min(512, t)
        ya = _mha_call(plat, C_QN, qr_l, 0, [(kl_mla, 0, vl_mla, 0, t), (kp_mla, 0, vp_mla, 0, past)],
                       nb=nbl, tq=tq, t=t, scale=(MLA_NOPE + MLA_ROPE) ** -0.5)
        dk_c = cache_diff_k[:, l].reshape(nbl * past, 512)
        dv_c = cache_diff_v[:, l].reshape(nbl * past, 512)
        yb = _diff_call(dqk_l, 0, [(dqk_l, 512, plat, C_DV, t), (dk_c, 0, dv_c, 0, past)], diff_lambda[l], dnorm,
                        nb=nbl, tq=tq, t=t, lam_init=lam_init)
        yc = _na_call(plat, cache_na_k[:, l].reshape(nbl * past, 512), cache_na_v[:, l].reshape(nbl * past, 512),
                      na_rel_bias[l], nb=nbl, t=t, past=past)
        s0 = state_ret[:, l].astype(F32).reshape(nbl, 2, 2, LANE, LANE)
        yd = _ret_call(plat, dec, gn, s0, nb=nbl, tq=tq, t=t)
        mrg = _merge_call((ya, yb, yc, yd), plat, wb)
        xl = _mm_call(mrg, wo, out_dtype=F32, tm=tm_l, tn=1024, rows_per_mod=t, resid=(xl, ga1[1]))
        u = _mm_call(xl, wup, out_dtype=BF16, tm=tm_l, tn=1024, rows_per_mod=t, norm=(g_f, sh2[1], sc2[1]))
        act = _conv_call(u, conv_w[l], conv_b[l], seq=t)
        xl = _mm_call(act, wdn, out_dtype=F32, tm=256, tn=d, rows_per_mod=t, resid=(xl, ga2[1]), out_norm=g_out)

    y_prompt = xc.reshape(nbc, seq, d)
    y_sample = xl.reshape(nbl, t, d)
    outs = [jnp.stack([st[i] for st in states], axis=1) for i in range(7)]
    return (y_prompt, y_sample, *outs)
```

```python
import functools
import math

import jax
import jax.numpy as jnp
import numpy as np
from jax import lax
from jax.experimental import pallas as pl
from jax.experimental.pallas import tpu as pltpu

F32 = jnp.float32
BF16 = jnp.bfloat16

D_MODEL = 2048
GRID_W = 64
ROPE_BASE = 10000.0
EPS = 1e-6
NEG_INF = -1e30
N_BRANCH = 4
BRANCH_W = 512
MLA_HEADS = 8
MLA_NOPE = 64
MLA_ROPE = 32
MLA_V = 64
MLA_KV_RANK = 256
DIFF_QK = 64
DIFF_V = 128
DIFF_HEADS = 4
NA_DIM = 64
NA_HEADS = 8
NA_ROWS = 8
NA_COLS = 16
RET_V = 128
RET_HEADS = 4
RET_QK = 64
D_FF = 5632
CONV_W = 3

LANE = 128
HALF = 64

C_QN = 0
C_QR = 512
C_CKV = 768
C_DQ = 1024
C_DK = 1536
C_DV = 2048
C_NQ = 2560
C_NK = 3072
C_NV = 3584
C_RQ = 4096
C_RK = 4352
C_RV = 4608
C_RG = 5120
C_KR = 5632
C_GATES = 6144
P_WIDTH = C_GATES + N_BRANCH * D_MODEL

VMEM_LIMIT = 56 * 1024 * 1024


def _cparams(sem):
    return pltpu.CompilerParams(dimension_semantics=sem, vmem_limit_bytes=VMEM_LIMIT)


def _dot(a, b):
    return jnp.dot(a, b, preferred_element_type=F32)


def _dot_t(a, b):
    return lax.dot_general(a, b, (((1,), (1,)), ((), ())), preferred_element_type=F32)


def _sigmoid(x):
    return 0.5 * jnp.tanh(0.5 * x) + 0.5


def _silu(x):
    h = 0.5 * x
    return h * (jnp.tanh(h) + 1.0)


def _lane_iota(shape):
    return lax.broadcasted_iota(jnp.int32, shape, len(shape) - 1)


def _half_mask(x, e):
    lane = _lane_iota(x.shape)
    keep = (lane < HALF) if e == 0 else (lane >= HALF)
    return jnp.where(keep, x, 0.0)


def _rope(x, cos, sin, dr):
    w = x.shape[-1]
    nf = dr // 4
    lane = _lane_iota(x.shape)
    first = (lane % (2 * nf)) < nf
    rot = jnp.where(first, pltpu.roll(x, w - nf, 1), pltpu.roll(x, nf, 1))
    return x * cos + rot * sin


def _ada_kernel(c_ref, w_ref, b_ref, o_ref):
    a = _silu(c_ref[...]).astype(BF16)
    o_ref[...] = _dot(a, w_ref[...].astype(BF16)) + b_ref[...]


def _ada_call(cond, w_ada, b_ada):
    depth, d, n = w_ada.shape
    rows = cond.shape[0]
    tn = 1024
    return pl.pallas_call(
        _ada_kernel,
        grid=(depth, n // tn),
        in_specs=[
            pl.BlockSpec((rows, d), lambda l, j: (0, 0)),
            pl.BlockSpec((None, d, tn), lambda l, j: (l, 0, j)),
            pl.BlockSpec((None, 1, tn), lambda l, j: (l, 0, j)),
        ],
        out_specs=pl.BlockSpec((None, rows, tn), lambda l, j: (l, 0, j)),
        out_shape=jax.ShapeDtypeStruct((depth, rows, n), F32),
        compiler_params=_cparams(("parallel", "parallel")),
    )(cond, w_ada, b_ada.reshape(depth, 1, n))


NORM_ROWS = 128


def _mm_kernel(*refs, norm, resid, out_norm):
    refs = list(refs)
    a_ref = refs.pop(0)
    if norm:
        g_ref, sh_ref, sc_ref = refs.pop(0), refs.pop(0), refs.pop(0)
    w_ref = refs.pop(0)
    if resid:
        r_ref, ga_ref = refs.pop(0), refs.pop(0)
    if out_norm:
        go_ref = refs.pop(0)
    o_ref = refs.pop(0)

    if norm:
        h_ref = refs.pop(0)

        @pl.when(pl.program_id(1) == 0)
        def _():
            g = g_ref[...]
            mul = 1.0 + sc_ref[...]
            add = sh_ref[...]

            def body(c, carry):
                r0 = pl.multiple_of(c * NORM_ROWS, NORM_ROWS)
                x = a_ref[pl.ds(r0, NORM_ROWS), :]
                y = x * lax.rsqrt(jnp.mean(x * x, axis=-1, keepdims=True) + EPS) * g
                h_ref[pl.ds(r0, NORM_ROWS), :] = (y * mul + add).astype(BF16)
                return carry

            lax.fori_loop(0, a_ref.shape[0] // NORM_ROWS, body, 0)

        a = h_ref[...]
    else:
        a = a_ref[...]
    acc = _dot(a, w_ref[...])
    if resid:
        acc = r_ref[...] + ga_ref[...] * acc
    if out_norm:
        acc = acc * lax.rsqrt(jnp.mean(acc * acc, axis=-1, keepdims=True) + EPS) * go_ref[...]
    o_ref[...] = acc.astype(o_ref.dtype)


def _mm_call(a, w, *, out_dtype, tm, tn, rows_per_mod, norm=None, resid=None, out_norm=None):
    m, k = a.shape
    n = w.shape[1]
    assert m % tm == 0 and n % tn == 0 and rows_per_mod % tm == 0
    assert out_norm is None or tn == n
    tiles_per_mod = rows_per_mod // tm
    in_specs = [pl.BlockSpec((tm, k), lambda i, j: (i, 0))]
    args = [a]
    if norm is not None:
        g, sh, sc = norm
        in_specs += [
            pl.BlockSpec((1, k), lambda i, j: (0, 0)),
            pl.BlockSpec((None, 1, k), lambda i, j: (i // tiles_per_mod, 0, 0)),
            pl.BlockSpec((None, 1, k), lambda i, j: (i // tiles_per_mod, 0, 0)),
        ]
        args += [g, sh, sc]
    if tn == n:
        in_specs.append(pl.BlockSpec((k, tn), lambda i, j: (0, 0), pipeline_mode=pl.Buffered(1)))
    else:
        in_specs.append(pl.BlockSpec((k, tn), lambda i, j: (0, j)))
    args.append(w)
    if resid is not None:
        x, ga = resid
        in_specs += [
            pl.BlockSpec((tm, tn), lambda i, j: (i, j)),
            pl.BlockSpec((None, 1, tn), lambda i, j: (i // tiles_per_mod, 0, j)),
        ]
        args += [x, ga]
    if out_norm is not None:
        in_specs.append(pl.BlockSpec((1, n), lambda i, j: (0, 0)))
        args.append(out_norm)
    scratch = [pltpu.VMEM((tm, k), BF16)] if norm is not None else []
    return pl.pallas_call(
        functools.partial(_mm_kernel, norm=norm is not None, resid=resid is not None, out_norm=out_norm is not None),
        grid=(m // tm, n // tn),
        in_specs=in_specs,
        out_specs=pl.BlockSpec((tm, tn), lambda i, j: (i, j)),
        out_shape=jax.ShapeDtypeStruct((m, n), out_dtype),
        scratch_shapes=scratch,
        compiler_params=_cparams(("parallel", "arbitrary")),
    )(*args)


def _rope_kernel(x_ref, c_ref, s_ref, o_ref, *, dr):
    o_ref[...] = _rope(x_ref[...].astype(F32), c_ref[...], s_ref[...], dr).astype(o_ref.dtype)


def _rope_call(p, col, width, cos, sin, dr, seq):
    m = p.shape[0]
    tm = 256
    tpb = seq // tm
    cb = col // width
    assert col % width == 0
    return pl.pallas_call(
        functools.partial(_rope_kernel, dr=dr),
        grid=(tpb, m // seq),
        in_specs=[
            pl.BlockSpec((tm, width), lambda i, b: (b * tpb + i, cb)),
            pl.BlockSpec((tm, width), lambda i, b: (i, 0)),
            pl.BlockSpec((tm, width), lambda i, b: (i, 0)),
        ],
        out_specs=pl.BlockSpec((tm, width), lambda i, b: (b * tpb + i, 0)),
        out_shape=jax.ShapeDtypeStruct((m, width), BF16),
        compiler_params=_cparams(("parallel", "parallel")),
    )(p, cos, sin)


MLA_K_W = 4 * 256
MLA_V_W = 4 * 128


def _mla_kv_kernel(*refs, norm, rope):
    refs = list(refs)
    c_ref, kr_ref = refs.pop(0), refs.pop(0)
    if rope:
        cos_ref, sin_ref = refs.pop(0), refs.pop(0)
    if norm:
        g_ref = refs.pop(0)
    w_ref = refs.pop(0)
    k_ref, v_ref = refs.pop(0), refs.pop(0)
    c = c_ref[...].astype(F32)
    if norm:
        ckv_ref = refs.pop(0)
        c = c * lax.rsqrt(jnp.mean(c * c, axis=-1, keepdims=True) + EPS) * g_ref[...]
        ckv_ref[...] = c
    kv = _dot(c.astype(BF16), w_ref[...])
    kr = kr_ref[...].astype(F32)
    if rope:
        kr = _rope(kr, cos_ref[...], sin_ref[...], MLA_ROPE)
    k_ref[...] = (kv[:, :MLA_K_W] + jnp.concatenate([kr] * 4, axis=1)).astype(BF16)
    v_ref[...] = kv[:, MLA_K_W:].astype(BF16)


def _mla_kv_call(ckv_src, ckv_col, kr_src, kr_col, w_e, *, g=None, rope=None, seq=None):
    m = ckv_src.shape[0]
    tm = 256
    assert m % tm == 0 and ckv_col % 256 == 0 and kr_col % 256 == 0
    in_specs = [
        pl.BlockSpec((tm, 256), lambda i: (i, ckv_col // 256)),
        pl.BlockSpec((tm, 256), lambda i: (i, kr_col // 256)),
    ]
    args = [ckv_src, kr_src]
    if rope is not None:
        tpb = seq // tm
        in_specs += [pl.BlockSpec((tm, 256), lambda i: (i % tpb, 0))] * 2
        args += list(rope)
    if g is not None:
        in_specs.append(pl.BlockSpec((1, 256), lambda i: (0, 0)))
        args.append(g)
    in_specs.append(pl.BlockSpec(w_e.shape, lambda i: (0, 0)))
    args.append(w_e)
    out_specs = [pl.BlockSpec((tm, MLA_K_W), lambda i: (i, 0)), pl.BlockSpec((tm, MLA_V_W), lambda i: (i, 0))]
    out_shape = [jax.ShapeDtypeStruct((m, MLA_K_W), BF16), jax.ShapeDtypeStruct((m, MLA_V_W), BF16)]
    if g is not None:
        out_specs.append(pl.BlockSpec((tm, 256), lambda i: (i, 0)))
        out_shape.append(jax.ShapeDtypeStruct((m, 256), F32))
    return pl.pallas_call(
        functools.partial(_mla_kv_kernel, norm=g is not None, rope=rope is not None),
        grid=(m // tm,),
        in_specs=in_specs,
        out_specs=out_specs,
        out_shape=out_shape,
        compiler_params=_cparams(("parallel",)),
    )(*args)


LOG2E = 1.4426950408889634


def _exp_parts(scores, scale=1.0):
    m = scores[0].max(axis=-1, keepdims=True)
    for s in scores[1:]:
        m = jnp.maximum(m, s.max(axis=-1, keepdims=True))
    return [jnp.exp2((s - m) * (scale * LOG2E)) for s in scores]


def _softmax_parts(scores, scale=1.0):
    ps = _exp_parts(scores, scale)
    l = ps[0].sum(axis=-1, keepdims=True)
    for p in ps[1:]:
        l = l + p.sum(axis=-1, keepdims=True)
    return ps, 1.0 / l


def _pv_normalised(ps, vs):
    acc = None
    for p, v in zip(ps, vs):
        t = _dot(p.astype(BF16), jnp.concatenate([v, jnp.ones_like(v)], axis=1))
        acc = t if acc is None else acc + t
    return acc[:, :LANE] / acc[:, LANE:]


def _mha_kernel(*refs, mla, nseg, scale):
    refs = list(refs)
    qn_ref = refs.pop(0)
    qr_ref = refs.pop(0) if mla else None
    kv_refs = [(refs.pop(0), refs.pop(0)) for _ in range(nseg)]
    o_ref = refs.pop(0)
    kw = 256 if mla else LANE
    lane = _lane_iota((qn_ref.shape[0], LANE))
    nheads = 8

    def head_scores(h):
        j, e = h // 2, h % 2
        q = _half_mask(qn_ref[:, j * LANE:(j + 1) * LANE].astype(F32), e)
        if mla:
            qr = qr_ref[:, (h // 4) * LANE:(h // 4 + 1) * LANE].astype(F32)
            qr = jnp.where(lane // MLA_ROPE == h % 4, qr, 0.0)
            q = jnp.concatenate([q, qr], axis=1)
        q = q.astype(BF16)
        return [_dot_t(q, k_ref[:, j * kw:(j + 1) * kw].astype(BF16)) for k_ref, _ in kv_refs]

    outs = []

    def head_values(h, ps):
        j = h // 2
        outs.append(_pv_normalised(ps, [v_ref[:, j * LANE:(j + 1) * LANE].astype(BF16) for _, v_ref in kv_refs]))
        if h % 2 == 1:
            o_ref[:, j * LANE:(j + 1) * LANE] = jnp.where(lane < HALF, outs[0], outs[1]).astype(o_ref.dtype)
            outs.clear()

    scores = head_scores(0)
    pending = None
    for h in range(nheads):
        nxt = head_scores(h + 1) if h + 1 < nheads else None
        ps = _exp_parts(scores, scale)
        if pending is not None:
            head_values(*pending)
        pending = (h, ps)
        scores = nxt
    head_values(*pending)


def _mha_call(qn, qn_col, qr, qr_col, segs, *, nb, tq, t, scale):
    mla = qr is not None
    kw = MLA_K_W if mla else 512
    nq = t // tq
    in_specs = [pl.BlockSpec((tq, 512), lambda b, i: (b * nq + i, qn_col // 512))]
    args = [qn]
    if mla:
        in_specs.append(pl.BlockSpec((tq, 256), lambda b, i: (b * nq + i, qr_col // 256)))
        args.append(qr)
    for k_arr, k_col, v_arr, v_col, s in segs:
        assert k_col % kw == 0 and v_col % 512 == 0
        in_specs.append(pl.BlockSpec((s, kw), functools.partial(lambda b, i, c: (b, c), c=k_col // kw)))
        in_specs.append(pl.BlockSpec((s, 512), functools.partial(lambda b, i, c: (b, c), c=v_col // 512)))
        args += [k_arr, v_arr]
    return pl.pallas_call(
        functools.partial(_mha_kernel, mla=mla, nseg=len(segs), scale=scale),
        grid=(nb, nq),
        in_specs=in_specs,
        out_specs=pl.BlockSpec((tq, 512), lambda b, i: (b * nq + i, 0)),
        out_shape=jax.ShapeDtypeStruct((nb * t, 512), BF16),
        compiler_params=_cparams(("parallel", "parallel")),
    )(*args)


def _diff_kernel(*refs, nseg, lam_init):
    refs = list(refs)
    q_ref = refs.pop(0)
    kv_refs = [(refs.pop(0), refs.pop(0)) for _ in range(nseg)]
    lp_ref, g_ref, o_ref = refs
    lp = lp_ref[...]
    lam = (jnp.exp(jnp.sum(lp[0:1] * lp[1:2], axis=-1, keepdims=True))
           - jnp.exp(jnp.sum(lp[2:3] * lp[3:4], axis=-1, keepdims=True)) + lam_init)
    scale = DIFF_QK ** -0.5

    def head_scores(h):
        q_pair = q_ref[:, h * LANE:(h + 1) * LANE].astype(F32)
        return [[_dot_t(_half_mask(q_pair, c).astype(BF16), k_ref[:, h * LANE:(h + 1) * LANE].astype(BF16))
                 for k_ref, _ in kv_refs] for c in range(2)]

    scores = head_scores(0)
    for h in range(DIFF_HEADS):
        nxt = head_scores(h + 1) if h + 1 < DIFF_HEADS else None
        (ps0, inv0), (ps1, inv1) = [_softmax_parts(sc, scale) for sc in scores]
        o = None
        for s in range(nseg):
            a = (ps0[s] * inv0 - lam * (ps1[s] * inv1)).astype(BF16)
            t = _dot(a, kv_refs[s][1][:, h * LANE:(h + 1) * LANE].astype(BF16))
            o = t if o is None else o + t
        o = o * lax.rsqrt(jnp.mean(o * o, axis=-1, keepdims=True) + EPS) * g_ref[:, h * LANE:(h + 1) * LANE]
        o_ref[:, h * LANE:(h + 1) * LANE] = (o * (1.0 - lam_init)).astype(o_ref.dtype)
        scores = nxt


def _diff_call(q, q_col, segs, lp, g, *, nb, tq, t, lam_init):
    nq = t // tq
    in_specs = [pl.BlockSpec((tq, 512), lambda b, i: (b * nq + i, q_col // 512))]
    args = [q]
    for k_arr, k_col, v_arr, v_col, s in segs:
        in_specs.append(pl.BlockSpec((s, 512), functools.partial(lambda b, i, c: (b, c), c=k_col // 512)))
        in_specs.append(pl.BlockSpec((s, 512), functools.partial(lambda b, i, c: (b, c), c=v_col // 512)))
        args += [k_arr, v_arr]
    in_specs += [pl.BlockSpec((4, DIFF_QK), lambda b, i: (0, 0)), pl.BlockSpec((1, 512), lambda b, i: (0, 0))]
    args += [lp, g]
    return pl.pallas_call(
        functools.partial(_diff_kernel, nseg=len(segs), lam_init=lam_init),
        grid=(nb, nq),
        in_specs=in_specs,
        out_specs=pl.BlockSpec((tq, 512), lambda b, i: (b * nq + i, 0)),
        out_shape=jax.ShapeDtypeStruct((nb * t, 512), BF16),
        compiler_params=_cparams(("parallel", "parallel")),
    )(*args)


NA_QROWS = 4


def _na_plan(rows):
    win = min(rows, NA_QROWS + NA_ROWS)
    kh = min(NA_ROWS, rows)
    cfgs, cfg_idx, starts = [], [], []
    for blk in range(rows // NA_QROWS):
        r0 = blk * NA_QROWS
        s_blk = int(np.clip(r0 - kh // 2, 0, rows - win))
        cfg = tuple((r0 + i - s_blk, int(np.clip(r0 + i - kh // 2, 0, rows - kh)) - s_blk) for i in range(NA_QROWS))
        if cfg not in cfgs:
            cfgs.append(cfg)
        cfg_idx.append(cfgs.index(cfg))
        starts.append(s_blk)
    return win, cfgs, np.array([cfg_idx, starts], np.int32)


def _na_kernel(meta_ref, q_ref, k_ref, v_ref, kc_ref, vc_ref, b_ref, o_ref, *, win):
    start = pl.multiple_of(meta_ref[1, pl.program_id(1)] * GRID_W, GRID_W)
    n_loc = win * GRID_W
    lane = _lane_iota((q_ref.shape[0], LANE))

    def head_scores(h):
        j, e = h // 2, h % 2
        q = _half_mask(q_ref[:, j * LANE:(j + 1) * LANE].astype(F32) * (NA_DIM ** -0.5), e).astype(BF16)
        s_loc = _dot_t(q, k_ref[pl.ds(start, n_loc), j * LANE:(j + 1) * LANE].astype(BF16))
        s_ctx = _dot_t(q, kc_ref[:, j * LANE:(j + 1) * LANE].astype(BF16))
        return [s_loc + b_ref[h], s_ctx]

    scores = head_scores(0)
    outs = []
    for h in range(NA_HEADS):
        nxt = head_scores(h + 1) if h + 1 < NA_HEADS else None
        j = h // 2
        v_loc = v_ref[pl.ds(start, n_loc), j * LANE:(j + 1) * LANE].astype(BF16)
        v_ctx = vc_ref[:, j * LANE:(j + 1) * LANE].astype(BF16)
        outs.append(_pv_normalised(_exp_parts(scores), [v_loc, v_ctx]))
        if h % 2 == 1:
            o_ref[:, j * LANE:(j + 1) * LANE] = jnp.where(lane < HALF, outs[0], outs[1]).astype(o_ref.dtype)
            outs = []
        scores = nxt


def _na_call(p, kc, vc, rel_bias, *, nb, t, past):
    rows = t // GRID_W
    assert rows % NA_QROWS == 0 and rows >= NA_ROWS
    win, cfgs, meta = _na_plan(rows)
    bias_tab = _na_bias_table(rel_bias, cfgs, win)
    nblk = rows // NA_QROWS
    tq = NA_QROWS * GRID_W
    n_loc = win * GRID_W
    grid_spec = pltpu.PrefetchScalarGridSpec(
        num_scalar_prefetch=1,
        grid=(nb, nblk),
        in_specs=[
            pl.BlockSpec((tq, 512), lambda b, r, meta: (b * nblk + r, C_NQ // 512)),
            pl.BlockSpec((t, 512), lambda b, r, meta: (b, C_NK // 512)),
            pl.BlockSpec((t, 512), lambda b, r, meta: (b, C_NV // 512)),
            pl.BlockSpec((past, 512), lambda b, r, meta: (b, 0)),
            pl.BlockSpec((past, 512), lambda b, r, meta: (b, 0)),
            pl.BlockSpec((None, NA_HEADS, tq, n_loc), lambda b, r, meta: (meta[0, r], 0, 0, 0)),
        ],
        out_specs=pl.BlockSpec((tq, 512), lambda b, r, meta: (b * nblk + r, 0)),
    )
    return pl.pallas_call(
        functools.partial(_na_kernel, win=win),
        grid_spec=grid_spec,
        out_shape=jax.ShapeDtypeStruct((nb * t, 512), BF16),
        compiler_params=_cparams(("parallel", "arbitrary")),
    )(jnp.asarray(meta), p, p, p, kc, vc, bias_tab)


def _log_sigmoid(x):
    return jnp.minimum(x, 0.0) - jnp.log(1.0 + jnp.exp(-jnp.abs(x)))


def _ret_kernel(*refs, has_state, t, whole_seq=False):
    refs = list(refs)
    q_ref, k_ref, v_ref, rg_ref, dec_ref, gn_ref = [refs.pop(0) for _ in range(6)]
    s0_ref = refs.pop(0) if has_state else None
    o_ref = refs.pop(0)
    tq = q_ref.shape[0]
    q0 = 0 if whole_seq else pl.program_id(1) * tq
    lg = _log_sigmoid(dec_ref[...])
    ti = (q0 + lax.broadcasted_iota(jnp.int32, (tq, t), 0)).astype(F32)
    ui = lax.broadcasted_iota(jnp.int32, (tq, t), 1).astype(F32)
    diff = ti - ui
    tcol = (q0 + lax.broadcasted_iota(jnp.int32, (tq, 1), 0)).astype(F32)

    def head_scores(h):
        j, e = h // 2, h % 2
        qm = _half_mask(q_ref[:, j * LANE:(j + 1) * LANE].astype(F32), e)
        return qm, _dot_t((qm * (RET_QK ** -0.5)).astype(BF16), k_ref[:, j * LANE:(j + 1) * LANE].astype(BF16))

    cur = head_scores(0)
    for h in range(RET_HEADS):
        nxt = head_scores(h + 1) if h + 1 < RET_HEADS else None
        j = h // 2
        qm, a = cur
        cur = nxt
        lgf = lg[h:h + 1, 0:1]
        lgb = lg[RET_HEADS + h:RET_HEADS + h + 1, 0:1]
        q = qm.astype(BF16)
        dmat = jnp.exp2(diff * jnp.where(diff >= 0, lgf * LOG2E, -lgb * LOG2E))
        y = _dot((a * dmat).astype(BF16), v_ref[:, h * LANE:(h + 1) * LANE].astype(BF16))
        if has_state:
            cf = _dot(q, s0_ref[0, j].astype(BF16))
            cb = _dot(q, s0_ref[1, j].astype(BF16))
            y = y + cf * jnp.exp((tcol + 1.0) * lgf) + cb * jnp.exp((float(t) - tcol) * lgb)
        mu = jnp.mean(y, axis=-1, keepdims=True)
        yc = y - mu
        var = jnp.mean(yc * yc, axis=-1, keepdims=True)
        yn = yc * lax.rsqrt(var + EPS) * gn_ref[:, h * LANE:(h + 1) * LANE]
        o_ref[:, h * LANE:(h + 1) * LANE] = (_silu(rg_ref[:, h * LANE:(h + 1) * LANE].astype(F32)) * yn).astype(o_ref.dtype)


def _ret_call(p, dec, gn, s0, *, nb, tq, t):
    nq = t // tq
    in_specs = [
        pl.BlockSpec((tq, 256), lambda b, i: (b * nq + i, C_RQ // 256)),
        pl.BlockSpec((t, 256), lambda b, i: (b, C_RK // 256)),
        pl.BlockSpec((t, 512), lambda b, i: (b, C_RV // 512)),
        pl.BlockSpec((tq, 512), lambda b, i: (b * nq + i, C_RG // 512)),
        pl.BlockSpec((8, LANE), lambda b, i: (0, 0)),
        pl.BlockSpec((1, 512), lambda b, i: (0, 0)),
    ]
    args = [p, p, p, p, dec, gn]
    if s0 is not None:
        in_specs.append(pl.BlockSpec((None, 2, 2, LANE, LANE), lambda b, i: (b, 0, 0, 0, 0)))
        args.append(s0)
    return pl.pallas_call(
        functools.partial(_ret_kernel, has_state=s0 is not None, t=t),
        grid=(nb, nq),
        in_specs=in_specs,
        out_specs=pl.BlockSpec((tq, 512), lambda b, i: (b * nq + i, 0)),
        out_shape=jax.ShapeDtypeStruct((nb * t, 512), BF16),
        compiler_params=_cparams(("parallel", "parallel")),
    )(*args)


def _ret_state_kernel(k_ref, v_ref, dec_ref, o_ref, *, t):
    lg = _log_sigmoid(dec_ref[...])
    tcol = lax.broadcasted_iota(jnp.int32, (t, 1), 0).astype(F32)
    lane = _lane_iota((t, LANE))
    for j in range(2):
        k_pair = k_ref[:, j * LANE:(j + 1) * LANE].astype(F32) * (RET_QK ** -0.5)
        for d in range(2):
            expo = (float(t) - 1.0 - tcol) if d == 0 else tcol
            w0 = jnp.exp(expo * lg[d * RET_HEADS + 2 * j:d * RET_HEADS + 2 * j + 1, 0:1])
            w1 = jnp.exp(expo * lg[d * RET_HEADS + 2 * j + 1:d * RET_HEADS + 2 * j + 2, 0:1])
            kd = (k_pair * jnp.where(lane < HALF, w0, w1)).T.astype(BF16)
            for e in range(2):
                h = 2 * j + e
                sfull = _dot(kd, v_ref[:, h * LANE:(h + 1) * LANE].astype(BF16))
                o_ref[d, h] = sfull[e * HALF:(e + 1) * HALF, :]


def _ret_state_call(p, dec, *, nb, t):
    return pl.pallas_call(
        functools.partial(_ret_state_kernel, t=t),
        grid=(nb,),
        in_specs=[
            pl.BlockSpec((t, 256), lambda b: (b, C_RK // 256)),
            pl.BlockSpec((t, 512), lambda b: (b, C_RV // 512)),
            pl.BlockSpec((8, LANE), lambda b: (0, 0)),
        ],
        out_specs=pl.BlockSpec((None, 2, RET_HEADS, RET_QK, RET_V), lambda b: (b, 0, 0, 0, 0)),
        out_shape=jax.ShapeDtypeStruct((nb, 2, RET_HEADS, RET_QK, RET_V), F32),
        compiler_params=_cparams(("parallel",)),
    )(p, p, dec)


def _ctx_mix_kernel(p_ref, w_ref, kvn_ref, lp_ref, dn_ref, dec_ref, gn_ref,
                    ya_ref, yb_ref, yc_ref, yd_ref, ckv_ref, sret_ref, k_scr, v_scr, *, seq, lam_init):
    def col(c, w):
        return p_ref.at[:, c:c + w]

    _mla_kv_kernel(col(C_CKV, 256), col(C_KR, 256), kvn_ref, w_ref, k_scr, v_scr, ckv_ref, norm=True, rope=False)
    _mha_kernel(col(C_QN, 512), col(C_QR, 256), k_scr, v_scr, ya_ref, mla=True, nseg=1,
                scale=(MLA_NOPE + MLA_ROPE) ** -0.5)
    _diff_kernel(col(C_DQ, 512), col(C_DK, 512), col(C_DV, 512), lp_ref, dn_ref, yb_ref, nseg=1, lam_init=lam_init)
    _mha_kernel(col(C_NQ, 512), col(C_NK, 512), col(C_NV, 512), yc_ref, mla=False, nseg=1, scale=NA_DIM ** -0.5)
    _ret_kernel(col(C_RQ, 256), col(C_RK, 256), col(C_RV, 512), col(C_RG, 512), dec_ref, gn_ref, yd_ref,
                has_state=False, t=seq, whole_seq=True)
    _ret_state_kernel(col(C_RK, 256), col(C_RV, 512), dec_ref, sret_ref, t=seq)


def _ctx_mix_call(p, w_ukv_e, kvn, lp, dnorm, dec, gn, *, nb, seq, lam_init):
    def const(shape):
        return pl.BlockSpec(shape, lambda b: (0,) * len(shape))

    def rows(width, dtype):
        return pl.BlockSpec((seq, width), lambda b: (b, 0)), jax.ShapeDtypeStruct((nb * seq, width), dtype)

    outs = [rows(512, BF16)] * 4 + [rows(MLA_KV_RANK, F32)]
    out_specs = [o[0] for o in outs] + [pl.BlockSpec((None, 2, RET_HEADS, RET_QK, RET_V), lambda b: (b, 0, 0, 0, 0))]
    out_shape = [o[1] for o in outs] + [jax.ShapeDtypeStruct((nb, 2, RET_HEADS, RET_QK, RET_V), F32)]
    return pl.pallas_call(
        functools.partial(_ctx_mix_kernel, seq=seq, lam_init=lam_init),
        grid=(nb,),
        in_specs=[pl.BlockSpec((seq, C_GATES), lambda b: (b, 0)), const(w_ukv_e.shape), const((1, MLA_KV_RANK)),
                  const((4, DIFF_QK)), const((1, 512)), const((8, LANE)), const((1, 512))],
        out_specs=out_specs,
        out_shape=out_shape,
        scratch_shapes=[pltpu.VMEM((seq, MLA_K_W), BF16), pltpu.VMEM((seq, MLA_V_W), BF16)],
        compiler_params=_cparams(("parallel",)),
    )(p, w_ukv_e, kvn, lp, dnorm, dec, gn)


def _merge_kernel(ya_ref, yb_ref, yc_ref, yd_ref, g0_ref, g1_ref, g2_ref, g3_ref, w_ref, o_ref):
    ys = (ya_ref, yb_ref, yc_ref, yd_ref)
    gs = (g0_ref, g1_ref, g2_ref, g3_ref)
    m = None
    for i in range(N_BRANCH):
        t = (jnp.tanh(gs[i][...].astype(F32)) + 1.0) * _dot(ys[i][...], w_ref[i])
        m = t if m is None else m + t
    o_ref[...] = m.astype(o_ref.dtype)


def _merge_call(ys, p, w_branch):
    m = p.shape[0]
    tm, tn = min(1024, m), 512
    gate_specs = [
        pl.BlockSpec((tm, tn), functools.partial(lambda r, c, i: (r, (C_GATES + i * D_MODEL) // tn + c), i=i))
        for i in range(N_BRANCH)
    ]
    return pl.pallas_call(
        _merge_kernel,
        grid=(m // tm, D_MODEL // tn),
        in_specs=[pl.BlockSpec((tm, BRANCH_W), lambda r, c: (r, 0))] * N_BRANCH + gate_specs
        + [pl.BlockSpec((N_BRANCH, BRANCH_W, tn), lambda r, c: (0, 0, c))],
        out_specs=pl.BlockSpec((tm, tn), lambda r, c: (r, c)),
        out_shape=jax.ShapeDtypeStruct((m, D_MODEL), BF16),
        compiler_params=_cparams(("parallel", "parallel")),
    )(*ys, p, p, p, p, w_branch)


HALO = 16


SUB = 8


SHIFT_BLK = 256


def _shift_matrix():
    i = np.arange(SHIFT_BLK)
    s = np.zeros((2 * SHIFT_BLK, SHIFT_BLK), np.float32)
    s[i[1:], i[1:] - 1] = 1.0
    s[SHIFT_BLK + i[:-1], i[:-1] + 1] = 1.0
    return jnp.asarray(s, BF16)


def _conv_kernel(s_ref, ua_ref, uap_ref, uan_ref, ub_ref, ubp_ref, ubn_ref, wa_ref, wb_ref, ba_ref, bb_ref, o_ref, *, tiles_per_seq):
    i = pl.program_id(0)
    tm, tc = ua_ref.shape
    nblk = tm // SHIFT_BLK
    has_prev = ((i % tiles_per_seq) != 0).astype(F32)
    has_next = ((i % tiles_per_seq) != tiles_per_seq - 1).astype(F32)
    sub = lax.broadcasted_iota(jnp.int32, (SUB, tc), 0)
    shift = s_ref[...]

    def conv(u_ref, up_ref, un_ref, w_ref, b_ref, r):
        r0 = r * SHIFT_BLK
        ub = u_ref[r0:r0 + SHIFT_BLK, :]
        sh = _dot(shift, ub)
        if r == 0:
            prev = up_ref[HALO - 1:HALO, :].astype(F32) * has_prev
        else:
            prev = u_ref[r0 - 1:r0, :].astype(F32)
        if r == nblk - 1:
            nxt = un_ref[0:1, :].astype(F32) * has_next
        else:
            nxt = u_ref[r0 + SHIFT_BLK:r0 + SHIFT_BLK + 1, :].astype(F32)
        dn = sh[:SHIFT_BLK]
        up = sh[SHIFT_BLK:]
        um1 = jnp.concatenate([jnp.where(sub == 0, prev, dn[:SUB]), dn[SUB:]], axis=0)
        up1 = jnp.concatenate([up[:-SUB], jnp.where(sub == SUB - 1, nxt, up[-SUB:])], axis=0)
        return um1 * w_ref[0:1, :] + ub.astype(F32) * w_ref[1:2, :] + up1 * w_ref[2:3, :] + b_ref[...]

    for r in range(nblk):
        a = conv(ua_ref, uap_ref, uan_ref, wa_ref, ba_ref, r)
        b = conv(ub_ref, ubp_ref, ubn_ref, wb_ref, bb_ref, r)
        o_ref[r * SHIFT_BLK:(r + 1) * SHIFT_BLK, :] = (_silu(a) * b).astype(o_ref.dtype)


def _conv_call(u, conv_w, conv_b, *, seq):
    m = u.shape[0]
    tm, tc = min(seq, 512), 1408
    assert seq % tm == 0 and D_FF % tc == 0
    nb_half = D_FF // tc
    last_halo = m // HALO - 1
    r = tm // HALO

    def main(off):
        return pl.BlockSpec((tm, tc), lambda i, c: (i, c + off))

    def prev(off):
        return pl.BlockSpec((HALO, tc), lambda i, c: (jnp.maximum(i * r - 1, 0), c + off))

    def nxt(off):
        return pl.BlockSpec((HALO, tc), lambda i, c: (jnp.minimum((i + 1) * r, last_halo), c + off))

    def vec(rows, off):
        return pl.BlockSpec((rows, tc), lambda i, c: (0, c + off))

    cb = conv_b.reshape(1, 2 * D_FF)
    return pl.pallas_call(
        functools.partial(_conv_kernel, tiles_per_seq=seq // tm),
        grid=(m // tm, nb_half),
        in_specs=[pl.BlockSpec((2 * SHIFT_BLK, SHIFT_BLK), lambda i, c: (0, 0)),
                  main(0), prev(0), nxt(0), main(nb_half), prev(nb_half), nxt(nb_half),
                  vec(CONV_W, 0), vec(CONV_W, nb_half), vec(1, 0), vec(1, nb_half)],
        out_specs=pl.BlockSpec((tm, tc), lambda i, c: (i, c)),
        out_shape=jax.ShapeDtypeStruct((m, D_FF), BF16),
        compiler_params=_cparams(("parallel", "parallel")),
    )(_shift_matrix(), u, u, u, u, u, u, conv_w, conv_w, cb, cb)


def _final_norm_kernel(x_ref, g_ref, o_ref):
    x = x_ref[...]
    o_ref[...] = x * lax.rsqrt(jnp.mean(x * x, axis=-1, keepdims=True) + EPS) * g_ref[...]


def _final_norm_call(x, g):
    m, d = x.shape
    tm = 256
    return pl.pallas_call(
        _final_norm_kernel,
        grid=(m // tm,),
        in_specs=[pl.BlockSpec((tm, d), lambda i: (i, 0)), pl.BlockSpec((1, d), lambda i: (0, 0))],
        out_specs=pl.BlockSpec((tm, d), lambda i: (i, 0)),
        out_shape=jax.ShapeDtypeStruct((m, d), F32),
        compiler_params=_cparams(("parallel",)),
    )(x, g.reshape(1, d))


def _reorder_w_in(w):
    d = w.shape[0]
    mq = w[:, :768].reshape(d, MLA_HEADS, MLA_NOPE + MLA_ROPE)
    qn = mq[:, :, :MLA_NOPE].reshape(d, 512)
    qr = mq[:, :, MLA_NOPE:].reshape(d, 256)
    ckv = w[:, 768:1024]
    mkr = w[:, 1024:1056]
    rest = w[:, 1056:5664]
    gates = w[:, 5664:] * 0.5
    krblk = jnp.concatenate([jnp.zeros((d, LANE), w.dtype)] + [mkr] * 4, axis=1)
    pad = jnp.zeros((d, C_GATES - C_KR - 256), w.dtype)
    return jnp.concatenate([qn, qr, ckv, rest, krblk, pad, gates], axis=1).astype(BF16)


def _reorder_w_ukv(w):
    r = w.shape[0]
    w3 = w.reshape(r, MLA_HEADS, MLA_NOPE + MLA_V)
    kn = w3[:, :, :MLA_NOPE].reshape(r, 4, LANE)
    vv = w3[:, :, MLA_NOPE:].reshape(r, 512)
    knz = jnp.concatenate([kn, jnp.zeros_like(kn)], axis=2).reshape(r, MLA_K_W)
    return jnp.concatenate([knz, vv], axis=1).astype(BF16)


def _rope_tables(t, dr, reps, lead_zero=0):
    nf = dr // 4
    pos = jnp.arange(t)
    pos = jnp.stack([pos // GRID_W, pos % GRID_W], axis=-1).astype(F32)
    inv = ROPE_BASE ** (-jnp.arange(nf, dtype=F32) / nf)
    ang = pos[:, :, None] * inv
    cos = jnp.cos(ang)
    sin = jnp.sin(ang)
    c = jnp.tile(jnp.concatenate([cos, cos], axis=2).reshape(t, dr), (1, reps))
    s = jnp.tile(jnp.concatenate([-sin, sin], axis=2).reshape(t, dr), (1, reps))
    if lead_zero:
        z = jnp.zeros((t, lead_zero), F32)
        c = jnp.concatenate([z, c], axis=1)
        s = jnp.concatenate([z, s], axis=1)
    return c, s


def _na_bias_table(rel_bias, cfgs, win):
    v = rel_bias.astype(F32)
    pad = GRID_W - NA_COLS
    ext = jnp.concatenate([jnp.repeat(v[..., :1], pad, axis=-1), v, jnp.repeat(v[..., -1:], pad, axis=-1)], axis=-1)
    col_t = jnp.stack([ext[..., GRID_W - 1 - q:2 * GRID_W - 1 - q] for q in range(GRID_W)], axis=2)
    qc = np.arange(GRID_W)[:, None]
    kc = np.arange(GRID_W)[None, :]
    c0 = np.clip(qc - NA_COLS // 2, 0, GRID_W - NA_COLS)
    col_ok = (kc >= c0) & (kc < c0 + NA_COLS)
    col_t = jnp.where(col_ok, col_t, NEG_INF)
    neg = jnp.full((NA_HEADS, GRID_W, GRID_W), NEG_INF, F32)
    kh = min(NA_ROWS, win)
    tabs = []
    for cfg in cfgs:
        per_row = []
        for delta, off in cfg:
            blocks = [col_t[:, kk - delta + NA_ROWS - 1] if off <= kk < off + kh else neg for kk in range(win)]
            per_row.append(jnp.concatenate(blocks, axis=-1))
        tabs.append(jnp.concatenate(per_row, axis=1))
    return jnp.stack(tabs)


def kernel(x_prompt, x_sample, cache_mla_ckv, cache_mla_krope, cache_diff_k, cache_diff_v, cache_na_k, cache_na_v, state_ret, c, c_ctx, w_ada, b_ada, g_attn, g_ffn, w_in, mla_kv_norm, w_mla_ukv, diff_lambda, diff_norm, na_rel_bias, ret_decay, ret_norm, w_branch, w_o, w_up, conv_w, conv_b, w_down, g_final):
    nbc, seq, d = x_prompt.shape
    nbl, t, _ = x_sample.shape
    past = cache_mla_ckv.shape[2]
    depth = w_in.shape[0]
    mc, ml = nbc * seq, nbl * t

    xc = x_prompt.reshape(mc, d)
    xl = x_sample.reshape(ml, d)

    mod_rows = 16
    cond = jnp.concatenate([c_ctx[None], c, jnp.zeros((mod_rows - 1 - nbl, d), F32)], axis=0)
    mods = _ada_call(cond, w_ada, b_ada).reshape(depth, mod_rows, 6, d)

    cos32q, sin32q = _rope_tables(t, MLA_ROPE, 8)
    cos32k, sin32k = _rope_tables(t, MLA_ROPE, 4, lead_zero=LANE)
    cos64, sin64 = _rope_tables(t, DIFF_QK, 16)

    tm_c = min(1024, mc)
    tm_l = min(1024, t)
    states = []
    for l in range(depth):
        lam_init = 0.8 - 0.6 * math.exp(-0.3 * l)
        w_in_r = _reorder_w_in(w_in[l])
        w_ukv_e = _reorder_w_ukv(w_mla_ukv[l])
        wb = (w_branch[l] * 0.5).astype(BF16)
        wo = w_o[l].astype(BF16)
        wup = w_up[l].astype(BF16)
        wdn = w_down[l].astype(BF16)
        kvn = mla_kv_norm[l].reshape(1, MLA_KV_RANK)
        dec = jnp.broadcast_to(ret_decay[l].reshape(2 * RET_HEADS, 1), (2 * RET_HEADS, LANE))
        gn = ret_norm[l].reshape(1, 512)
        dnorm = diff_norm[l].reshape(1, 512)
        g_out = g_final.reshape(1, d) if l == depth - 1 else None
        g_a = g_attn[l].reshape(1, d)
        g_f = g_ffn[l].reshape(1, d)

        def mod(i):
            return mods[l, 0:1, i][:, None, :], mods[l, 1:1 + nbl, i][:, None, :]

        sh1, sc1, ga1, sh2, sc2, ga2 = [mod(i) for i in range(6)]

        pc = _mm_call(xc, w_in_r, out_dtype=F32, tm=tm_c, tn=1024, rows_per_mod=mc, norm=(g_a, sh1[0], sc1[0]))
        ya, yb, yc, yd, ckv_state, sret = _ctx_mix_call(pc, w_ukv_e, kvn, diff_lambda[l], dnorm, dec, gn,
                                                        nb=nbc, seq=seq, lam_init=lam_init)
        states.append((
            ckv_state.reshape(nbc, seq, MLA_KV_RANK),
            pc[:, C_KR + LANE:C_KR + LANE + MLA_ROPE].reshape(nbc, seq, MLA_ROPE),
            pc[:, C_DK:C_DK + 512].reshape(nbc, seq, DIFF_HEADS, 2, DIFF_QK),
            pc[:, C_DV:C_DV + 512].reshape(nbc, seq, DIFF_HEADS, DIFF_V),
            pc[:, C_NK:C_NK + 512].reshape(nbc, seq, NA_HEADS, NA_DIM),
            pc[:, C_NV:C_NV + 512].reshape(nbc, seq, NA_HEADS, NA_DIM),
            sret,
        ))
        mrg = _merge_call((ya, yb, yc, yd), pc, wb)
        xc = _mm_call(mrg, wo, out_dtype=F32, tm=512, tn=d, rows_per_mod=mc, resid=(xc, ga1[0]))
        u = _mm_call(xc, wup, out_dtype=BF16, tm=tm_c, tn=1024, rows_per_mod=mc, norm=(g_f, sh2[0], sc2[0]))
        act = _conv_call(u, conv_w[l], conv_b[l], seq=seq)
        xc = _mm_call(act, wdn, out_dtype=F32, tm=256, tn=d, rows_per_mod=mc, resid=(xc, ga2[0]), out_norm=g_out)

        plat = _mm_call(xl, w_in_r, out_dtype=BF16, tm=tm_l, tn=2048, rows_per_mod=t, norm=(g_a, sh1[1], sc1[1]))
        kl_mla, vl_mla, _ = _mla_kv_call(plat, C_CKV, plat, C_KR, w_ukv_e, g=kvn, rope=(cos32k, sin32k), seq=t)
        kr_c = cache_mla_krope[:, l].reshape(nbl * past, MLA_ROPE)
        krblk_c = jnp.concatenate([jnp.zeros((nbl * past, LANE), F32)] + [kr_c] * 4, axis=1)
        kp_mla, vp_mla = _mla_kv_call(cache_mla_ckv[:, l].reshape(nbl * past, MLA_KV_RANK), 0, krblk_c, 0, w_ukv_e)
        qr_l = _rope_call(plat, C_QR, 256, cos32q, sin32q, MLA_ROPE, t)
        dqk_l = _rope_call(plat, C_DQ, 1024, cos64, sin64, DIFF_QK, t)
        tq = min(512, t)
        ya = _mha_call(plat, C_QN, qr_l, 0, [(kl_mla, 0, vl_mla, 0, t), (kp_mla, 0, vp_mla, 0, past)],
                       nb=nbl, tq=tq, t=t, scale=(MLA_NOPE + MLA_ROPE) ** -0.5)
        dk_c = cache_diff_k[:, l].reshape(nbl * past, 512)
        dv_c = cache_diff_v[:, l].reshape(nbl * past, 512)
        yb = _diff_call(dqk_l, 0, [(dqk_l, 512, plat, C_DV, t), (dk_c, 0, dv_c, 0, past)], diff_lambda[l], dnorm,
                        nb=nbl, tq=tq, t=t, lam_init=lam_init)
        yc = _na_call(plat, cache_na_k[:, l].reshape(nbl * past, 512), cache_na_v[:, l].reshape(nbl * past, 512),
                      na_rel_bias[l], nb=nbl, t=t, past=past)
        s0 = state_ret[:, l].astype(F32).reshape(nbl, 2, 2, LANE, LANE)
        yd = _ret_call(plat, dec, gn, s0, nb=nbl, tq=tq, t=t)
        mrg = _merge_call((ya, yb, yc, yd), plat, wb)
        xl = _mm_call(mrg, wo, out_dtype=F32, tm=512, tn=d, rows_per_mod=t, resid=(xl, ga1[1]))
        u = _mm_call(xl, wup, out_dtype=BF16, tm=tm_l, tn=1024, rows_per_mod=t, norm=(g_f, sh2[1], sc2[1]))
        act = _conv_call(u, conv_w[l], conv_b[l], seq=t)
        xl = _mm_call(act, wdn, out_dtype=F32, tm=256, tn=d, rows_per_mod=t, resid=(xl, ga2[1]), out_norm=g_out)

    y_prompt = xc.reshape(nbc, seq, d)
    y_sample = xl.reshape(nbl, t, d)
    outs = [jnp.stack([st[i] for st in states], axis=1) for i in range(7)]
    return (y_prompt, y_sample, *outs)
```

```python
import functools
import math

import jax
import jax.numpy as jnp
import numpy as np
from jax import lax
from jax.experimental import pallas as pl
from jax.experimental.pallas import tpu as pltpu

F32 = jnp.float32
BF16 = jnp.bfloat16

D_MODEL = 2048
GRID_W = 64
ROPE_BASE = 10000.0
EPS = 1e-6
NEG_INF = -1e30
N_BRANCH = 4
BRANCH_W = 512
MLA_HEADS = 8
MLA_NOPE = 64
MLA_ROPE = 32
MLA_V = 64
MLA_KV_RANK = 256
DIFF_QK = 64
DIFF_V = 128
DIFF_HEADS = 4
NA_DIM = 64
NA_HEADS = 8
NA_ROWS = 8
NA_COLS = 16
RET_V = 128
RET_HEADS = 4
RET_QK = 64
D_FF = 5632
CONV_W = 3

LANE = 128
HALF = 64

C_QN = 0
C_QR = 512
C_CKV = 768
C_DQ = 1024
C_DK = 1536
C_DV = 2048
C_NQ = 2560
C_NK = 3072
C_NV = 3584
C_RQ = 4096
C_RK = 4352
C_RV = 4608
C_RG = 5120
C_KR = 5632
C_GATES = 6144
P_WIDTH = C_GATES + N_BRANCH * D_MODEL

VMEM_LIMIT = 56 * 1024 * 1024


def _cparams(sem):
    return pltpu.CompilerParams(dimension_semantics=sem, vmem_limit_bytes=VMEM_LIMIT)


def _dot(a, b):
    return jnp.dot(a, b, preferred_element_type=F32)


def _dot_t(a, b):
    return lax.dot_general(a, b, (((1,), (1,)), ((), ())), preferred_element_type=F32)


def _sigmoid(x):
    return 0.5 * jnp.tanh(0.5 * x) + 0.5


def _silu(x):
    h = 0.5 * x
    return h * (jnp.tanh(h) + 1.0)


def _lane_iota(shape):
    return lax.broadcasted_iota(jnp.int32, shape, len(shape) - 1)


def _half_mask(x, e):
    lane = _lane_iota(x.shape)
    keep = (lane < HALF) if e == 0 else (lane >= HALF)
    return jnp.where(keep, x, 0.0)


def _rope(x, cos, sin, dr):
    w = x.shape[-1]
    nf = dr // 4
    lane = _lane_iota(x.shape)
    first = (lane % (2 * nf)) < nf
    rot = jnp.where(first, pltpu.roll(x, w - nf, 1), pltpu.roll(x, nf, 1))
    return x * cos + rot * sin


def _ada_kernel(c_ref, w_ref, b_ref, o_ref):
    a = _silu(c_ref[...]).astype(BF16)
    o_ref[...] = _dot(a, w_ref[...].astype(BF16)) + b_ref[...]


def _ada_call(cond, w_ada, b_ada):
    depth, d, n = w_ada.shape
    rows = cond.shape[0]
    tn = 1024
    return pl.pallas_call(
        _ada_kernel,
        grid=(depth, n // tn),
        in_specs=[
            pl.BlockSpec((rows, d), lambda l, j: (0, 0)),
            pl.BlockSpec((None, d, tn), lambda l, j: (l, 0, j)),
            pl.BlockSpec((None, 1, tn), lambda l, j: (l, 0, j)),
        ],
        out_specs=pl.BlockSpec((None, rows, tn), lambda l, j: (l, 0, j)),
        out_shape=jax.ShapeDtypeStruct((depth, rows, n), F32),
        compiler_params=_cparams(("parallel", "parallel")),
    )(cond, w_ada, b_ada.reshape(depth, 1, n))


NORM_ROWS = 128


def _mm_kernel(*refs, norm, resid, out_norm):
    refs = list(refs)
    a_ref = refs.pop(0)
    if norm:
        g_ref, sh_ref, sc_ref = refs.pop(0), refs.pop(0), refs.pop(0)
    w_ref = refs.pop(0)
    if resid:
        r_ref, ga_ref = refs.pop(0), refs.pop(0)
    if out_norm:
        go_ref = refs.pop(0)
    o_ref = refs.pop(0)

    if norm:
        h_ref = refs.pop(0)

        @pl.when(pl.program_id(1) == 0)
        def _():
            g = g_ref[...]
            mul = 1.0 + sc_ref[...]
            add = sh_ref[...]

            def body(c, carry):
                r0 = pl.multiple_of(c * NORM_ROWS, NORM_ROWS)
                x = a_ref[pl.ds(r0, NORM_ROWS), :]
                y = x * lax.rsqrt(jnp.mean(x * x, axis=-1, keepdims=True) + EPS) * g
                h_ref[pl.ds(r0, NORM_ROWS), :] = (y * mul + add).astype(BF16)
                return carry

            lax.fori_loop(0, a_ref.shape[0] // NORM_ROWS, body, 0)

        a = h_ref[...]
    else:
        a = a_ref[...]
    acc = _dot(a, w_ref[...])
    if resid:
        acc = r_ref[...] + ga_ref[...] * acc
    if out_norm:
        acc = acc * lax.rsqrt(jnp.mean(acc * acc, axis=-1, keepdims=True) + EPS) * go_ref[...]
    o_ref[...] = acc.astype(o_ref.dtype)


def _mm_call(a, w, *, out_dtype, tm, tn, rows_per_mod, norm=None, resid=None, out_norm=None):
    m, k = a.shape
    n = w.shape[1]
    assert m % tm == 0 and n % tn == 0 and rows_per_mod % tm == 0
    assert out_norm is None or tn == n
    tiles_per_mod = rows_per_mod // tm
    in_specs = [pl.BlockSpec((tm, k), lambda i, j: (i, 0))]
    args = [a]
    if norm is not None:
        g, sh, sc = norm
        in_specs += [
            pl.BlockSpec((1, k), lambda i, j: (0, 0)),
            pl.BlockSpec((None, 1, k), lambda i, j: (i // tiles_per_mod, 0, 0)),
            pl.BlockSpec((None, 1, k), lambda i, j: (i // tiles_per_mod, 0, 0)),
        ]
        args += [g, sh, sc]
    if tn == n:
        in_specs.append(pl.BlockSpec((k, tn), lambda i, j: (0, 0), pipeline_mode=pl.Buffered(1)))
    else:
        in_specs.append(pl.BlockSpec((k, tn), lambda i, j: (0, j)))
    args.append(w)
    if resid is not None:
        x, ga = resid
        in_specs += [
            pl.BlockSpec((tm, tn), lambda i, j: (i, j)),
            pl.BlockSpec((None, 1, tn), lambda i, j: (i // tiles_per_mod, 0, j)),
        ]
        args += [x, ga]
    if out_norm is not None:
        in_specs.append(pl.BlockSpec((1, n), lambda i, j: (0, 0)))
        args.append(out_norm)
    scratch = [pltpu.VMEM((tm, k), BF16)] if norm is not None else []
    return pl.pallas_call(
        functools.partial(_mm_kernel, norm=norm is not None, resid=resid is not None, out_norm=out_norm is not None),
        grid=(m // tm, n // tn),
        in_specs=in_specs,
        out_specs=pl.BlockSpec((tm, tn), lambda i, j: (i, j)),
        out_shape=jax.ShapeDtypeStruct((m, n), out_dtype),
        scratch_shapes=scratch,
        compiler_params=_cparams(("parallel", "arbitrary")),
    )(*args)


def _rope_kernel(x_ref, c_ref, s_ref, o_ref, *, dr):
    o_ref[...] = _rope(x_ref[...].astype(F32), c_ref[...], s_ref[...], dr).astype(o_ref.dtype)


def _rope_call(p, col, width, cos, sin, dr, seq):
    m = p.shape[0]
    tm = 256
    tpb = seq // tm
    cb = col // width
    assert col % width == 0
    return pl.pallas_call(
        functools.partial(_rope_kernel, dr=dr),
        grid=(tpb, m // seq),
        in_specs=[
            pl.BlockSpec((tm, width), lambda i, b: (b * tpb + i, cb)),
            pl.BlockSpec((tm, width), lambda i, b: (i, 0)),
            pl.BlockSpec((tm, width), lambda i, b: (i, 0)),
        ],
        out_specs=pl.BlockSpec((tm, width), lambda i, b: (b * tpb + i, 0)),
        out_shape=jax.ShapeDtypeStruct((m, width), BF16),
        compiler_params=_cparams(("parallel", "parallel")),
    )(p, cos, sin)


MLA_K_W = 4 * 256
MLA_V_W = 4 * 128


def _mla_kv_kernel(*refs, norm, rope):
    refs = list(refs)
    c_ref, kr_ref = refs.pop(0), refs.pop(0)
    if rope:
        cos_ref, sin_ref = refs.pop(0), refs.pop(0)
    if norm:
        g_ref = refs.pop(0)
    w_ref = refs.pop(0)
    k_ref, v_ref = refs.pop(0), refs.pop(0)
    c = c_ref[...].astype(F32)
    if norm:
        ckv_ref = refs.pop(0)
        c = c * lax.rsqrt(jnp.mean(c * c, axis=-1, keepdims=True) + EPS) * g_ref[...]
        ckv_ref[...] = c
    kv = _dot(c.astype(BF16), w_ref[...])
    kr = kr_ref[...].astype(F32)
    if rope:
        kr = _rope(kr, cos_ref[...], sin_ref[...], MLA_ROPE)
    k_ref[...] = (kv[:, :MLA_K_W] + jnp.concatenate([kr] * 4, axis=1)).astype(BF16)
    v_ref[...] = kv[:, MLA_K_W:].astype(BF16)


def _mla_kv_call(ckv_src, ckv_col, kr_src, kr_col, w_e, *, g=None, rope=None, seq=None):
    m = ckv_src.shape[0]
    tm = 256
    assert m % tm == 0 and ckv_col % 256 == 0 and kr_col % 256 == 0
    in_specs = [
        pl.BlockSpec((tm, 256), lambda i: (i, ckv_col // 256)),
        pl.BlockSpec((tm, 256), lambda i: (i, kr_col // 256)),
    ]
    args = [ckv_src, kr_src]
    if rope is not None:
        tpb = seq // tm
        in_specs += [pl.BlockSpec((tm, 256), lambda i: (i % tpb, 0))] * 2
        args += list(rope)
    if g is not None:
        in_specs.append(pl.BlockSpec((1, 256), lambda i: (0, 0)))
        args.append(g)
    in_specs.append(pl.BlockSpec(w_e.shape, lambda i: (0, 0)))
    args.append(w_e)
    out_specs = [pl.BlockSpec((tm, MLA_K_W), lambda i: (i, 0)), pl.BlockSpec((tm, MLA_V_W), lambda i: (i, 0))]
    out_shape = [jax.ShapeDtypeStruct((m, MLA_K_W), BF16), jax.ShapeDtypeStruct((m, MLA_V_W), BF16)]
    if g is not None:
        out_specs.append(pl.BlockSpec((tm, 256), lambda i: (i, 0)))
        out_shape.append(jax.ShapeDtypeStruct((m, 256), F32))
    return pl.pallas_call(
        functools.partial(_mla_kv_kernel, norm=g is not None, rope=rope is not None),
        grid=(m // tm,),
        in_specs=in_specs,
        out_specs=out_specs,
        out_shape=out_shape,
        compiler_params=_cparams(("parallel",)),
    )(*args)


LOG2E = 1.4426950408889634


def _exp_parts(scores, scale=1.0):
    m = scores[0].max(axis=-1, keepdims=True)
    for s in scores[1:]:
        m = jnp.maximum(m, s.max(axis=-1, keepdims=True))
    return [jnp.exp2((s - m) * (scale * LOG2E)) for s in scores]


def _softmax_parts(scores, scale=1.0):
    ps = _exp_parts(scores, scale)
    l = ps[0].sum(axis=-1, keepdims=True)
    for p in ps[1:]:
        l = l + p.sum(axis=-1, keepdims=True)
    return ps, 1.0 / l


def _pv_normalised(ps, vs):
    acc = None
    for p, v in zip(ps, vs):
        t = _dot(p.astype(BF16), jnp.concatenate([v, jnp.ones_like(v)], axis=1))
        acc = t if acc is None else acc + t
    return acc[:, :LANE] / acc[:, LANE:]


def _mha_kernel(*refs, mla, nseg, scale):
    refs = list(refs)
    qn_ref = refs.pop(0)
    qr_ref = refs.pop(0) if mla else None
    kv_refs = [(refs.pop(0), refs.pop(0)) for _ in range(nseg)]
    o_ref = refs.pop(0)
    kw = 256 if mla else LANE
    lane = _lane_iota((qn_ref.shape[0], LANE))
    nheads = 8

    def head_scores(h):
        j, e = h // 2, h % 2
        q = _half_mask(qn_ref[:, j * LANE:(j + 1) * LANE].astype(F32), e)
        if mla:
            qr = qr_ref[:, (h // 4) * LANE:(h // 4 + 1) * LANE].astype(F32)
            qr = jnp.where(lane // MLA_ROPE == h % 4, qr, 0.0)
            q = jnp.concatenate([q, qr], axis=1)
        q = q.astype(BF16)
        return [_dot_t(q, k_ref[:, j * kw:(j + 1) * kw].astype(BF16)) for k_ref, _ in kv_refs]

    outs = []

    def head_values(h, ps):
        j = h // 2
        outs.append(_pv_normalised(ps, [v_ref[:, j * LANE:(j + 1) * LANE].astype(BF16) for _, v_ref in kv_refs]))
        if h % 2 == 1:
            o_ref[:, j * LANE:(j + 1) * LANE] = jnp.where(lane < HALF, outs[0], outs[1]).astype(o_ref.dtype)
            outs.clear()

    scores = head_scores(0)
    pending = None
    for h in range(nheads):
        nxt = head_scores(h + 1) if h + 1 < nheads else None
        ps = _exp_parts(scores, scale)
        if pending is not None:
            head_values(*pending)
        pending = (h, ps)
        scores = nxt
    head_values(*pending)


def _mha_call(qn, qn_col, qr, qr_col, segs, *, nb, tq, t, scale):
    mla = qr is not None
    kw = MLA_K_W if mla else 512
    nq = t // tq
    in_specs = [pl.BlockSpec((tq, 512), lambda b, i: (b * nq + i, qn_col // 512))]
    args = [qn]
    if mla:
        in_specs.append(pl.BlockSpec((tq, 256), lambda b, i: (b * nq + i, qr_col // 256)))
        args.append(qr)
    for k_arr, k_col, v_arr, v_col, s in segs:
        assert k_col % kw == 0 and v_col % 512 == 0
        in_specs.append(pl.BlockSpec((s, kw), functools.partial(lambda b, i, c: (b, c), c=k_col // kw)))
        in_specs.append(pl.BlockSpec((s, 512), functools.partial(lambda b, i, c: (b, c), c=v_col // 512)))
        args += [k_arr, v_arr]
    return pl.pallas_call(
        functools.partial(_mha_kernel, mla=mla, nseg=len(segs), scale=scale),
        grid=(nb, nq),
        in_specs=in_specs,
        out_specs=pl.BlockSpec((tq, 512), lambda b, i: (b * nq + i, 0)),
        out_shape=jax.ShapeDtypeStruct((nb * t, 512), BF16),
        compiler_params=_cparams(("parallel", "parallel")),
    )(*args)


def _diff_kernel(*refs, nseg, lam_init):
    refs = list(refs)
    q_ref = refs.pop(0)
    kv_refs = [(refs.pop(0), refs.pop(0)) for _ in range(nseg)]
    lp_ref, g_ref, o_ref = refs
    lp = lp_ref[...]
    lam = (jnp.exp(jnp.sum(lp[0:1] * lp[1:2], axis=-1, keepdims=True))
           - jnp.exp(jnp.sum(lp[2:3] * lp[3:4], axis=-1, keepdims=True)) + lam_init)
    scale = DIFF_QK ** -0.5

    def head_scores(h):
        q_pair = q_ref[:, h * LANE:(h + 1) * LANE].astype(F32)
        return [[_dot_t(_half_mask(q_pair, c).astype(BF16), k_ref[:, h * LANE:(h + 1) * LANE].astype(BF16))
                 for k_ref, _ in kv_refs] for c in range(2)]

    scores = head_scores(0)
    for h in range(DIFF_HEADS):
        nxt = head_scores(h + 1) if h + 1 < DIFF_HEADS else None
        (ps0, inv0), (ps1, inv1) = [_softmax_parts(sc, scale) for sc in scores]
        o = None
        for s in range(nseg):
            a = (ps0[s] * inv0 - lam * (ps1[s] * inv1)).astype(BF16)
            t = _dot(a, kv_refs[s][1][:, h * LANE:(h + 1) * LANE].astype(BF16))
            o = t if o is None else o + t
        o = o * lax.rsqrt(jnp.mean(o * o, axis=-1, keepdims=True) + EPS) * g_ref[:, h * LANE:(h + 1) * LANE]
        o_ref[:, h * LANE:(h + 1) * LANE] = (o * (1.0 - lam_init)).astype(o_ref.dtype)
        scores = nxt


def _diff_call(q, q_col, segs, lp, g, *, nb, tq, t, lam_init):
    nq = t // tq
    in_specs = [pl.BlockSpec((tq, 512), lambda b, i: (b * nq + i, q_col // 512))]
    args = [q]
    for k_arr, k_col, v_arr, v_col, s in segs:
        in_specs.append(pl.BlockSpec((s, 512), functools.partial(lambda b, i, c: (b, c), c=k_col // 512)))
        in_specs.append(pl.BlockSpec((s, 512), functools.partial(lambda b, i, c: (b, c), c=v_col // 512)))
        args += [k_arr, v_arr]
    in_specs += [pl.BlockSpec((4, DIFF_QK), lambda b, i: (0, 0)), pl.BlockSpec((1, 512), lambda b, i: (0, 0))]
    args += [lp, g]
    return pl.pallas_call(
        functools.partial(_diff_kernel, nseg=len(segs), lam_init=lam_init),
        grid=(nb, nq),
        in_specs=in_specs,
        out_specs=pl.BlockSpec((tq, 512), lambda b, i: (b * nq + i, 0)),
        out_shape=jax.ShapeDtypeStruct((nb * t, 512), BF16),
        compiler_params=_cparams(("parallel", "parallel")),
    )(*args)


NA_QROWS = 4


def _na_plan(rows):
    win = min(rows, NA_QROWS + NA_ROWS)
    kh = min(NA_ROWS, rows)
    cfgs, cfg_idx, starts = [], [], []
    for blk in range(rows // NA_QROWS):
        r0 = blk * NA_QROWS
        s_blk = int(np.clip(r0 - kh // 2, 0, rows - win))
        cfg = tuple((r0 + i - s_blk, int(np.clip(r0 + i - kh // 2, 0, rows - kh)) - s_blk) for i in range(NA_QROWS))
        if cfg not in cfgs:
            cfgs.append(cfg)
        cfg_idx.append(cfgs.index(cfg))
        starts.append(s_blk)
    return win, cfgs, np.array([cfg_idx, starts], np.int32)


def _na_kernel(meta_ref, q_ref, k_ref, v_ref, kc_ref, vc_ref, b_ref, o_ref, *, win):
    start = pl.multiple_of(meta_ref[1, pl.program_id(1)] * GRID_W, GRID_W)
    n_loc = win * GRID_W
    lane = _lane_iota((q_ref.shape[0], LANE))

    def head_scores(h):
        j, e = h // 2, h % 2
        q = _half_mask(q_ref[:, j * LANE:(j + 1) * LANE].astype(F32) * (NA_DIM ** -0.5), e).astype(BF16)
        s_loc = _dot_t(q, k_ref[pl.ds(start, n_loc), j * LANE:(j + 1) * LANE].astype(BF16))
        s_ctx = _dot_t(q, kc_ref[:, j * LANE:(j + 1) * LANE].astype(BF16))
        return [s_loc + b_ref[h], s_ctx]

    scores = head_scores(0)
    outs = []
    for h in range(NA_HEADS):
        nxt = head_scores(h + 1) if h + 1 < NA_HEADS else None
        j = h // 2
        v_loc = v_ref[pl.ds(start, n_loc), j * LANE:(j + 1) * LANE].astype(BF16)
        v_ctx = vc_ref[:, j * LANE:(j + 1) * LANE].astype(BF16)
        outs.append(_pv_normalised(_exp_parts(scores), [v_loc, v_ctx]))
        if h % 2 == 1:
            o_ref[:, j * LANE:(j + 1) * LANE] = jnp.where(lane < HALF, outs[0], outs[1]).astype(o_ref.dtype)
            outs = []
        scores = nxt


def _na_call(p, kc, vc, rel_bias, *, nb, t, past):
    rows = t // GRID_W
    assert rows % NA_QROWS == 0 and rows >= NA_ROWS
    win, cfgs, meta = _na_plan(rows)
    bias_tab = _na_bias_table(rel_bias, cfgs, win)
    nblk = rows // NA_QROWS
    tq = NA_QROWS * GRID_W
    n_loc = win * GRID_W
    grid_spec = pltpu.PrefetchScalarGridSpec(
        num_scalar_prefetch=1,
        grid=(nb, nblk),
        in_specs=[
            pl.BlockSpec((tq, 512), lambda b, r, meta: (b * nblk + r, C_NQ // 512)),
            pl.BlockSpec((t, 512), lambda b, r, meta: (b, C_NK // 512)),
            pl.BlockSpec((t, 512), lambda b, r, meta: (b, C_NV // 512)),
            pl.BlockSpec((past, 512), lambda b, r, meta: (b, 0)),
            pl.BlockSpec((past, 512), lambda b, r, meta: (b, 0)),
            pl.BlockSpec((None, NA_HEADS, tq, n_loc), lambda b, r, meta: (meta[0, r], 0, 0, 0)),
        ],
        out_specs=pl.BlockSpec((tq, 512), lambda b, r, meta: (b * nblk + r, 0)),
    )
    return pl.pallas_call(
        functools.partial(_na_kernel, win=win),
        grid_spec=grid_spec,
        out_shape=jax.ShapeDtypeStruct((nb * t, 512), BF16),
        compiler_params=_cparams(("parallel", "arbitrary")),
    )(jnp.asarray(meta), p, p, p, kc, vc, bias_tab)


def _log_sigmoid(x):
    return jnp.minimum(x, 0.0) - jnp.log(1.0 + jnp.exp(-jnp.abs(x)))


def _ret_kernel(*refs, has_state, t, whole_seq=False):
    refs = list(refs)
    q_ref, k_ref, v_ref, rg_ref, dec_ref, gn_ref = [refs.pop(0) for _ in range(6)]
    s0_ref = refs.pop(0) if has_state else None
    o_ref = refs.pop(0)
    tq = q_ref.shape[0]
    q0 = 0 if whole_seq else pl.program_id(1) * tq
    lg = _log_sigmoid(dec_ref[...])
    ti = (q0 + lax.broadcasted_iota(jnp.int32, (tq, t), 0)).astype(F32)
    ui = lax.broadcasted_iota(jnp.int32, (tq, t), 1).astype(F32)
    diff = ti - ui
    tcol = (q0 + lax.broadcasted_iota(jnp.int32, (tq, 1), 0)).astype(F32)

    def head_scores(h):
        j, e = h // 2, h % 2
        qm = _half_mask(q_ref[:, j * LANE:(j + 1) * LANE].astype(F32), e)
        return qm, _dot_t((qm * (RET_QK ** -0.5)).astype(BF16), k_ref[:, j * LANE:(j + 1) * LANE].astype(BF16))

    cur = head_scores(0)
    for h in range(RET_HEADS):
        nxt = head_scores(h + 1) if h + 1 < RET_HEADS else None
        j = h // 2
        qm, a = cur
        cur = nxt
        lgf = lg[h:h + 1, 0:1]
        lgb = lg[RET_HEADS + h:RET_HEADS + h + 1, 0:1]
        q = qm.astype(BF16)
        dmat = jnp.exp2(diff * jnp.where(diff >= 0, lgf * LOG2E, -lgb * LOG2E))
        y = _dot((a * dmat).astype(BF16), v_ref[:, h * LANE:(h + 1) * LANE].astype(BF16))
        if has_state:
            cf = _dot(q, s0_ref[0, j].astype(BF16))
            cb = _dot(q, s0_ref[1, j].astype(BF16))
            y = y + cf * jnp.exp((tcol + 1.0) * lgf) + cb * jnp.exp((float(t) - tcol) * lgb)
        mu = jnp.mean(y, axis=-1, keepdims=True)
        yc = y - mu
        var = jnp.mean(yc * yc, axis=-1, keepdims=True)
        yn = yc * lax.rsqrt(var + EPS) * gn_ref[:, h * LANE:(h + 1) * LANE]
        o_ref[:, h * LANE:(h + 1) * LANE] = (_silu(rg_ref[:, h * LANE:(h + 1) * LANE].astype(F32)) * yn).astype(o_ref.dtype)


def _ret_call(p, dec, gn, s0, *, nb, tq, t):
    nq = t // tq
    in_specs = [
        pl.BlockSpec((tq, 256), lambda b, i: (b * nq + i, C_RQ // 256)),
        pl.BlockSpec((t, 256), lambda b, i: (b, C_RK // 256)),
        pl.BlockSpec((t, 512), lambda b, i: (b, C_RV // 512)),
        pl.BlockSpec((tq, 512), lambda b, i: (b * nq + i, C_RG // 512)),
        pl.BlockSpec((8, LANE), lambda b, i: (0, 0)),
        pl.BlockSpec((1, 512), lambda b, i: (0, 0)),
    ]
    args = [p, p, p, p, dec, gn]
    if s0 is not None:
        in_specs.append(pl.BlockSpec((None, 2, 2, LANE, LANE), lambda b, i: (b, 0, 0, 0, 0)))
        args.append(s0)
    return pl.pallas_call(
        functools.partial(_ret_kernel, has_state=s0 is not None, t=t),
        grid=(nb, nq),
        in_specs=in_specs,
        out_specs=pl.BlockSpec((tq, 512), lambda b, i: (b * nq + i, 0)),
        out_shape=jax.ShapeDtypeStruct((nb * t, 512), BF16),
        compiler_params=_cparams(("parallel", "parallel")),
    )(*args)


def _ret_state_kernel(k_ref, v_ref, dec_ref, o_ref, *, t):
    lg = _log_sigmoid(dec_ref[...])
    tcol = lax.broadcasted_iota(jnp.int32, (t, 1), 0).astype(F32)
    lane = _lane_iota((t, LANE))
    for j in range(2):
        k_pair = k_ref[:, j * LANE:(j + 1) * LANE].astype(F32) * (RET_QK ** -0.5)
        for d in range(2):
            expo = (float(t) - 1.0 - tcol) if d == 0 else tcol
            w0 = jnp.exp(expo * lg[d * RET_HEADS + 2 * j:d * RET_HEADS + 2 * j + 1, 0:1])
            w1 = jnp.exp(expo * lg[d * RET_HEADS + 2 * j + 1:d * RET_HEADS + 2 * j + 2, 0:1])
            kd = (k_pair * jnp.where(lane < HALF, w0, w1)).T.astype(BF16)
            for e in range(2):
                h = 2 * j + e
                sfull = _dot(kd, v_ref[:, h * LANE:(h + 1) * LANE].astype(BF16))
                o_ref[d, h] = sfull[e * HALF:(e + 1) * HALF, :]


def _ret_state_call(p, dec, *, nb, t):
    return pl.pallas_call(
        functools.partial(_ret_state_kernel, t=t),
        grid=(nb,),
        in_specs=[
            pl.BlockSpec((t, 256), lambda b: (b, C_RK // 256)),
            pl.BlockSpec((t, 512), lambda b: (b, C_RV // 512)),
            pl.BlockSpec((8, LANE), lambda b: (0, 0)),
        ],
        out_specs=pl.BlockSpec((None, 2, RET_HEADS, RET_QK, RET_V), lambda b: (b, 0, 0, 0, 0)),
        out_shape=jax.ShapeDtypeStruct((nb, 2, RET_HEADS, RET_QK, RET_V), F32),
        compiler_params=_cparams(("parallel",)),
    )(p, p, dec)


def _ctx_mix_kernel(p_ref, w_ref, kvn_ref, lp_ref, dn_ref, dec_ref, gn_ref,
                    ya_ref, yb_ref, yc_ref, yd_ref, ckv_ref, sret_ref, k_scr, v_scr, *, seq, lam_init):
    def col(c, w):
        return p_ref.at[:, c:c + w]

    _mla_kv_kernel(col(C_CKV, 256), col(C_KR, 256), kvn_ref, w_ref, k_scr, v_scr, ckv_ref, norm=True, rope=False)
    _mha_kernel(col(C_QN, 512), col(C_QR, 256), k_scr, v_scr, ya_ref, mla=True, nseg=1,
                scale=(MLA_NOPE + MLA_ROPE) ** -0.5)
    _diff_kernel(col(C_DQ, 512), col(C_DK, 512), col(C_DV, 512), lp_ref, dn_ref, yb_ref, nseg=1, lam_init=lam_init)
    _mha_kernel(col(C_NQ, 512), col(C_NK, 512), col(C_NV, 512), yc_ref, mla=False, nseg=1, scale=NA_DIM ** -0.5)
    _ret_kernel(col(C_RQ, 256), col(C_RK, 256), col(C_RV, 512), col(C_RG, 512), dec_ref, gn_ref, yd_ref,
                has_state=False, t=seq, whole_seq=True)
    _ret_state_kernel(col(C_RK, 256), col(C_RV, 512), dec_ref, sret_ref, t=seq)


def _ctx_mix_call(p, w_ukv_e, kvn, lp, dnorm, dec, gn, *, nb, seq, lam_init):
    def const(shape):
        return pl.BlockSpec(shape, lambda b: (0,) * len(shape))

    def rows(width, dtype):
        return pl.BlockSpec((seq, width), lambda b: (b, 0)), jax.ShapeDtypeStruct((nb * seq, width), dtype)

    outs = [rows(512, BF16)] * 4 + [rows(MLA_KV_RANK, F32)]
    out_specs = [o[0] for o in outs] + [pl.BlockSpec((None, 2, RET_HEADS, RET_QK, RET_V), lambda b: (b, 0, 0, 0, 0))]
    out_shape = [o[1] for o in outs] + [jax.ShapeDtypeStruct((nb, 2, RET_HEADS, RET_QK, RET_V), F32)]
    return pl.pallas_call(
        functools.partial(_ctx_mix_kernel, seq=seq, lam_init=lam_init),
        grid=(nb,),
        in_specs=[pl.BlockSpec((seq, C_GATES), lambda b: (b, 0)), const(w_ukv_e.shape), const((1, MLA_KV_RANK)),
                  const((4, DIFF_QK)), const((1, 512)), const((8, LANE)), const((1, 512))],
        out_specs=out_specs,
        out_shape=out_shape,
        scratch_shapes=[pltpu.VMEM((seq, MLA_K_W), BF16), pltpu.VMEM((seq, MLA_V_W), BF16)],
        compiler_params=_cparams(("parallel",)),
    )(p, w_ukv_e, kvn, lp, dnorm, dec, gn)


MERGE_CHUNK = 512


def _merge_kernel(ya_ref, yb_ref, yc_ref, yd_ref, g0_ref, g1_ref, g2_ref, g3_ref, wb_ref, wo_ref, x_ref, ga_ref,
                  o_ref, m_ref):
    ys = (ya_ref, yb_ref, yc_ref, yd_ref)
    gs = (g0_ref, g1_ref, g2_ref, g3_ref)
    for c0 in range(0, D_MODEL, MERGE_CHUNK):
        cols = slice(c0, c0 + MERGE_CHUNK)
        m = None
        for i in range(N_BRANCH):
            t = _sigmoid(gs[i][:, cols].astype(F32)) * _dot(ys[i][...], wb_ref[i, :, cols])
            m = t if m is None else m + t
        m_ref[:, cols] = m.astype(BF16)
    o_ref[...] = x_ref[...] + ga_ref[...] * _dot(m_ref[...], wo_ref[...])


def _merge_call(ys, p, w_branch, w_o, x, ga, *, rows_per_mod):
    m = p.shape[0]
    tm = 256
    assert m % tm == 0 and rows_per_mod % tm == 0
    tiles_per_mod = rows_per_mod // tm
    first_gate_block = C_GATES // D_MODEL
    assert C_GATES % D_MODEL == 0
    gate_specs = [
        pl.BlockSpec((tm, D_MODEL), functools.partial(lambda r, i: (r, first_gate_block + i), i=i))
        for i in range(N_BRANCH)
    ]
    return pl.pallas_call(
        _merge_kernel,
        grid=(m // tm,),
        in_specs=[pl.BlockSpec((tm, BRANCH_W), lambda r: (r, 0))] * N_BRANCH + gate_specs + [
            pl.BlockSpec((N_BRANCH, BRANCH_W, D_MODEL), lambda r: (0, 0, 0), pipeline_mode=pl.Buffered(1)),
            pl.BlockSpec((D_MODEL, D_MODEL), lambda r: (0, 0), pipeline_mode=pl.Buffered(1)),
            pl.BlockSpec((tm, D_MODEL), lambda r: (r, 0)),
            pl.BlockSpec((None, 1, D_MODEL), lambda r: (r // tiles_per_mod, 0, 0)),
        ],
        out_specs=pl.BlockSpec((tm, D_MODEL), lambda r: (r, 0)),
        out_shape=jax.ShapeDtypeStruct((m, D_MODEL), F32),
        scratch_shapes=[pltpu.VMEM((tm, D_MODEL), BF16)],
        compiler_params=_cparams(("parallel",)),
    )(*ys, p, p, p, p, w_branch, w_o, x, ga)


HALO = 16


SUB = 8


SHIFT_BLK = 256


def _shift_matrix():
    i = np.arange(SHIFT_BLK)
    s = np.zeros((2 * SHIFT_BLK, SHIFT_BLK), np.float32)
    s[i[1:], i[1:] - 1] = 1.0
    s[SHIFT_BLK + i[:-1], i[:-1] + 1] = 1.0
    return jnp.asarray(s, BF16)


def _conv_kernel(s_ref, ua_ref, uap_ref, uan_ref, ub_ref, ubp_ref, ubn_ref, wa_ref, wb_ref, ba_ref, bb_ref, o_ref, *, tiles_per_seq):
    i = pl.program_id(0)
    tm, tc = ua_ref.shape
    nblk = tm // SHIFT_BLK
    has_prev = ((i % tiles_per_seq) != 0).astype(F32)
    has_next = ((i % tiles_per_seq) != tiles_per_seq - 1).astype(F32)
    sub = lax.broadcasted_iota(jnp.int32, (SUB, tc), 0)
    shift = s_ref[...]

    def conv(u_ref, up_ref, un_ref, w_ref, b_ref, r):
        r0 = r * SHIFT_BLK
        ub = u_ref[r0:r0 + SHIFT_BLK, :]
        sh = _dot(shift, ub)
        if r == 0:
            prev = up_ref[HALO - 1:HALO, :].astype(F32) * has_prev
        else:
            prev = u_ref[r0 - 1:r0, :].astype(F32)
        if r == nblk - 1:
            nxt = un_ref[0:1, :].astype(F32) * has_next
        else:
            nxt = u_ref[r0 + SHIFT_BLK:r0 + SHIFT_BLK + 1, :].astype(F32)
        dn = sh[:SHIFT_BLK]
        up = sh[SHIFT_BLK:]
        um1 = jnp.concatenate([jnp.where(sub == 0, prev, dn[:SUB]), dn[SUB:]], axis=0)
        up1 = jnp.concatenate([up[:-SUB], jnp.where(sub == SUB - 1, nxt, up[-SUB:])], axis=0)
        return um1 * w_ref[0:1, :] + ub.astype(F32) * w_ref[1:2, :] + up1 * w_ref[2:3, :] + b_ref[...]

    for r in range(nblk):
        a = conv(ua_ref, uap_ref, uan_ref, wa_ref, ba_ref, r)
        b = conv(ub_ref, ubp_ref, ubn_ref, wb_ref, bb_ref, r)
        o_ref[r * SHIFT_BLK:(r + 1) * SHIFT_BLK, :] = (_silu(a) * b).astype(o_ref.dtype)


def _conv_call(u, conv_w, conv_b, *, seq):
    m = u.shape[0]
    tm, tc = min(seq, 512), 1408
    assert seq % tm == 0 and D_FF % tc == 0
    nb_half = D_FF // tc
    last_halo = m // HALO - 1
    r = tm // HALO

    def main(off):
        return pl.BlockSpec((tm, tc), lambda i, c: (i, c + off))

    def prev(off):
        return pl.BlockSpec((HALO, tc), lambda i, c: (jnp.maximum(i * r - 1, 0), c + off))

    def nxt(off):
        return pl.BlockSpec((HALO, tc), lambda i, c: (jnp.minimum((i + 1) * r, last_halo), c + off))

    def vec(rows, off):
        return pl.BlockSpec((rows, tc), lambda i, c: (0, c + off))

    cb = conv_b.reshape(1, 2 * D_FF)
    return pl.pallas_call(
        functools.partial(_conv_kernel, tiles_per_seq=seq // tm),
        grid=(m // tm, nb_half),
        in_specs=[pl.BlockSpec((2 * SHIFT_BLK, SHIFT_BLK), lambda i, c: (0, 0)),
                  main(0), prev(0), nxt(0), main(nb_half), prev(nb_half), nxt(nb_half),
                  vec(CONV_W, 0), vec(CONV_W, nb_half), vec(1, 0), vec(1, nb_half)],
        out_specs=pl.BlockSpec((tm, tc), lambda i, c: (i, c)),
        out_shape=jax.ShapeDtypeStruct((m, D_FF), BF16),
        compiler_params=_cparams(("parallel", "parallel")),
    )(_shift_matrix(), u, u, u, u, u, u, conv_w, conv_w, cb, cb)


def _final_norm_kernel(x_ref, g_ref, o_ref):
    x = x_ref[...]
    o_ref[...] = x * lax.rsqrt(jnp.mean(x * x, axis=-1, keepdims=True) + EPS) * g_ref[...]


def _final_norm_call(x, g):
    m, d = x.shape
    tm = 256
    return pl.pallas_call(
        _final_norm_kernel,
        grid=(m // tm,),
        in_specs=[pl.BlockSpec((tm, d), lambda i: (i, 0)), pl.BlockSpec((1, d), lambda i: (0, 0))],
        out_specs=pl.BlockSpec((tm, d), lambda i: (i, 0)),
        out_shape=jax.ShapeDtypeStruct((m, d), F32),
        compiler_params=_cparams(("parallel",)),
    )(x, g.reshape(1, d))


def _reorder_w_in(w):
    d = w.shape[0]
    mq = w[:, :768].reshape(d, MLA_HEADS, MLA_NOPE + MLA_ROPE)
    qn = mq[:, :, :MLA_NOPE].reshape(d, 512)
    qr = mq[:, :, MLA_NOPE:].reshape(d, 256)
    ckv = w[:, 768:1024]
    mkr = w[:, 1024:1056]
    rest = w[:, 1056:5664]
    gates = w[:, 5664:]
    krblk = jnp.concatenate([jnp.zeros((d, LANE), w.dtype)] + [mkr] * 4, axis=1)
    pad = jnp.zeros((d, C_GATES - C_KR - 256), w.dtype)
    return jnp.concatenate([qn, qr, ckv, rest, krblk, pad, gates], axis=1).astype(BF16)


def _reorder_w_ukv(w):
    r = w.shape[0]
    w3 = w.reshape(r, MLA_HEADS, MLA_NOPE + MLA_V)
    kn = w3[:, :, :MLA_NOPE].reshape(r, 4, LANE)
    vv = w3[:, :, MLA_NOPE:].reshape(r, 512)
    knz = jnp.concatenate([kn, jnp.zeros_like(kn)], axis=2).reshape(r, MLA_K_W)
    return jnp.concatenate([knz, vv], axis=1).astype(BF16)


def _rope_tables(t, dr, reps, lead_zero=0):
    nf = dr // 4
    pos = jnp.arange(t)
    pos = jnp.stack([pos // GRID_W, pos % GRID_W], axis=-1).astype(F32)
    inv = ROPE_BASE ** (-jnp.arange(nf, dtype=F32) / nf)
    ang = pos[:, :, None] * inv
    cos = jnp.cos(ang)
    sin = jnp.sin(ang)
    c = jnp.tile(jnp.concatenate([cos, cos], axis=2).reshape(t, dr), (1, reps))
    s = jnp.tile(jnp.concatenate([-sin, sin], axis=2).reshape(t, dr), (1, reps))
    if lead_zero:
        z = jnp.zeros((t, lead_zero), F32)
        c = jnp.concatenate([z, c], axis=1)
        s = jnp.concatenate([z, s], axis=1)
    return c, s


def _na_bias_table(rel_bias, cfgs, win):
    v = rel_bias.astype(F32)
    pad = GRID_W - NA_COLS
    ext = jnp.concatenate([jnp.repeat(v[..., :1], pad, axis=-1), v, jnp.repeat(v[..., -1:], pad, axis=-1)], axis=-1)
    col_t = jnp.stack([ext[..., GRID_W - 1 - q:2 * GRID_W - 1 - q] for q in range(GRID_W)], axis=2)
    qc = np.arange(GRID_W)[:, None]
    kc = np.arange(GRID_W)[None, :]
    c0 = np.clip(qc - NA_COLS // 2, 0, GRID_W - NA_COLS)
    col_ok = (kc >= c0) & (kc < c0 + NA_COLS)
    col_t = jnp.where(col_ok, col_t, NEG_INF)
    neg = jnp.full((NA_HEADS, GRID_W, GRID_W), NEG_INF, F32)
    kh = min(NA_ROWS, win)
    tabs = []
    for cfg in cfgs:
        per_row = []
        for delta, off in cfg:
            blocks = [col_t[:, kk - delta + NA_ROWS - 1] if off <= kk < off + kh else neg for kk in range(win)]
            per_row.append(jnp.concatenate(blocks, axis=-1))
        tabs.append(jnp.concatenate(per_row, axis=1))
    return jnp.stack(tabs)


def kernel(x_prompt, x_sample, cache_mla_ckv, cache_mla_krope, cache_diff_k, cache_diff_v, cache_na_k, cache_na_v, state_ret, c, c_ctx, w_ada, b_ada, g_attn, g_ffn, w_in, mla_kv_norm, w_mla_ukv, diff_lambda, diff_norm, na_rel_bias, ret_decay, ret_norm, w_branch, w_o, w_up, conv_w, conv_b, w_down, g_final):
    nbc, seq, d = x_prompt.shape
    nbl, t, _ = x_sample.shape
    past = cache_mla_ckv.shape[2]
    depth = w_in.shape[0]
    mc, ml = nbc * seq, nbl * t

    xc = x_prompt.reshape(mc, d)
    xl = x_sample.reshape(ml, d)

    mod_rows = 16
    cond = jnp.concatenate([c_ctx[None], c, jnp.zeros((mod_rows - 1 - nbl, d), F32)], axis=0)
    mods = _ada_call(cond, w_ada, b_ada).reshape(depth, mod_rows, 6, d)

    cos32q, sin32q = _rope_tables(t, MLA_ROPE, 8)
    cos32k, sin32k = _rope_tables(t, MLA_ROPE, 4, lead_zero=LANE)
    cos64, sin64 = _rope_tables(t, DIFF_QK, 16)

    tm_c = min(1024, mc)
    tm_l = min(1024, t)
    states = []
    for l in range(depth):
        lam_init = 0.8 - 0.6 * math.exp(-0.3 * l)
        w_in_r = _reorder_w_in(w_in[l])
        w_ukv_e = _reorder_w_ukv(w_mla_ukv[l])
        wb = w_branch[l].astype(BF16)
        wo = w_o[l].astype(BF16)
        wup = w_up[l].astype(BF16)
        wdn = w_down[l].astype(BF16)
        kvn = mla_kv_norm[l].reshape(1, MLA_KV_RANK)
        dec = jnp.broadcast_to(ret_decay[l].reshape(2 * RET_HEADS, 1), (2 * RET_HEADS, LANE))
        gn = ret_norm[l].reshape(1, 512)
        dnorm = diff_norm[l].reshape(1, 512)
        g_out = g_final.reshape(1, d) if l == depth - 1 else None
        g_a = g_attn[l].reshape(1, d)
        g_f = g_ffn[l].reshape(1, d)

        def mod(i):
            return mods[l, 0:1, i][:, None, :], mods[l, 1:1 + nbl, i][:, None, :]

        sh1, sc1, ga1, sh2, sc2, ga2 = [mod(i) for i in range(6)]

        pc = _mm_call(xc, w_in_r, out_dtype=F32, tm=tm_c, tn=1024, rows_per_mod=mc, norm=(g_a, sh1[0], sc1[0]))
        ya, yb, yc, yd, ckv_state, sret = _ctx_mix_call(pc, w_ukv_e, kvn, diff_lambda[l], dnorm, dec, gn,
                                                        nb=nbc, seq=seq, lam_init=lam_init)
        states.append((
            ckv_state.reshape(nbc, seq, MLA_KV_RANK),
            pc[:, C_KR + LANE:C_KR + LANE + MLA_ROPE].reshape(nbc, seq, MLA_ROPE),
            pc[:, C_DK:C_DK + 512].reshape(nbc, seq, DIFF_HEADS, 2, DIFF_QK),
            pc[:, C_DV:C_DV + 512].reshape(nbc, seq, DIFF_HEADS, DIFF_V),
            pc[:, C_NK:C_NK + 512].reshape(nbc, seq, NA_HEADS, NA_DIM),
            pc[:, C_NV:C_NV + 512].reshape(nbc, seq, NA_HEADS, NA_DIM),
            sret,
        ))
        xc = _merge_call((ya, yb, yc, yd), pc, wb, wo, xc, ga1[0], rows_per_mod=mc)
        u = _mm_call(xc, wup, out_dtype=BF16, tm=tm_c, tn=1024, rows_per_mod=mc, norm=(g_f, sh2[0], sc2[0]))
        act = _conv_call(u, conv_w[l], conv_b[l], seq=seq)
        xc = _mm_call(act, wdn, out_dtype=F32, tm=256, tn=d, rows_per_mod=mc, resid=(xc, ga2[0]), out_norm=g_out)

        plat = _mm_call(xl, w_in_r, out_dtype=BF16, tm=tm_l, tn=2048, rows_per_mod=t, norm=(g_a, sh1[1], sc1[1]))
        kl_mla, vl_mla, _ = _mla_kv_call(plat, C_CKV, plat, C_KR, w_ukv_e, g=kvn, rope=(cos32k, sin32k), seq=t)
        kr_c = cache_mla_krope[:, l].reshape(nbl * past, MLA_ROPE)
        krblk_c = jnp.concatenate([jnp.zeros((nbl * past, LANE), F32)] + [kr_c] * 4, axis=1)
        kp_mla, vp_mla = _mla_kv_call(cache_mla_ckv[:, l].reshape(nbl * past, MLA_KV_RANK), 0, krblk_c, 0, w_ukv_e)
        qr_l = _rope_call(plat, C_QR, 256, cos32q, sin32q, MLA_ROPE, t)
        dqk_l = _rope_call(plat, C_DQ, 1024, cos64, sin64, DIFF_QK, t)
        tq = min(512, t)
        ya = _mha_call(plat, C_QN, qr_l, 0, [(kl_mla, 0, vl_mla, 0, t), (kp_mla, 0, vp_mla, 0, past)],
                       nb=nbl, tq=tq, t=t, scale=(MLA_NOPE + MLA_ROPE) ** -0.5)
        dk_c = cache_diff_k[:, l].reshape(nbl * past, 512)
        dv_c = cache_diff_v[:, l].reshape(nbl * past, 512)
        yb = _diff_call(dqk_l, 0, [(dqk_l, 512, plat, C_DV, t), (dk_c, 0, dv_c, 0, past)], diff_lambda[l], dnorm,
                        nb=nbl, tq=tq, t=t, lam_init=lam_init)
        yc = _na_call(plat, cache_na_k[:, l].reshape(nbl * past, 512), cache_na_v[:, l].reshape(nbl * past, 512),
                      na_rel_bias[l], nb=nbl, t=t, past=past)
        s0 = state_ret[:, l].astype(F32).reshape(nbl, 2, 2, LANE, LANE)
        yd = _ret_call(plat, dec, gn, s0, nb=nbl, tq=tq, t=t)
        xl = _merge_call((ya, yb, yc, yd), plat, wb, wo, xl, ga1[1], rows_per_mod=t)
        u = _mm_call(xl, wup, out_dtype=BF16, tm=tm_l, tn=1024, rows_per_mod=t, norm=(g_f, sh2[1], sc2[1]))
        act = _conv_call(u, conv_w[l], conv_b[l], seq=t)
        xl = _mm_call(act, wdn, out_dtype=F32, tm=256, tn=d, rows_per_mod=t, resid=(xl, ga2[1]), out_norm=g_out)

    y_prompt = xc.reshape(nbc, seq, d)
    y_sample = xl.reshape(nbl, t, d)
    outs = [jnp.stack([st[i] for st in states], axis=1) for i in range(7)]
    return (y_prompt, y_sample, *outs)
```

```python
import functools
import math

import jax
import jax.numpy as jnp
import numpy as np
from jax import lax
from jax.experimental import pallas as pl
from jax.experimental.pallas import tpu as pltpu

F32 = jnp.float32
BF16 = jnp.bfloat16

D_MODEL = 2048
GRID_W = 64
ROPE_BASE = 10000.0
EPS = 1e-6
NEG_INF = -1e30
N_BRANCH = 4
BRANCH_W = 512
MLA_HEADS = 8
MLA_NOPE = 64
MLA_ROPE = 32
MLA_V = 64
MLA_KV_RANK = 256
DIFF_QK = 64
DIFF_V = 128
DIFF_HEADS = 4
NA_DIM = 64
NA_HEADS = 8
NA_ROWS = 8
NA_COLS = 16
RET_V = 128
RET_HEADS = 4
RET_QK = 64
D_FF = 5632
CONV_W = 3

LANE = 128
HALF = 64

C_QN = 0
C_QR = 512
C_CKV = 768
C_DQ = 1024
C_DK = 1536
C_DV = 2048
C_NQ = 2560
C_NK = 3072
C_NV = 3584
C_RQ = 4096
C_RK = 4352
C_RV = 4608
C_RG = 5120
C_KR = 5632
C_GATES = 6144
P_WIDTH = C_GATES + N_BRANCH * D_MODEL

VMEM_LIMIT = 56 * 1024 * 1024


def _cparams(sem):
    return pltpu.CompilerParams(dimension_semantics=sem, vmem_limit_bytes=VMEM_LIMIT)


def _dot(a, b):
    return jnp.dot(a, b, preferred_element_type=F32)


def _dot_t(a, b):
    return lax.dot_general(a, b, (((1,), (1,)), ((), ())), preferred_element_type=F32)


def _sigmoid(x):
    return 0.5 * jnp.tanh(0.5 * x) + 0.5


def _silu(x):
    h = 0.5 * x
    return h * (jnp.tanh(h) + 1.0)


def _lane_iota(shape):
    return lax.broadcasted_iota(jnp.int32, shape, len(shape) - 1)


def _half_mask(x, e):
    lane = _lane_iota(x.shape)
    keep = (lane < HALF) if e == 0 else (lane >= HALF)
    return jnp.where(keep, x, 0.0)


def _rope(x, cos, sin, dr):
    w = x.shape[-1]
    nf = dr // 4
    lane = _lane_iota(x.shape)
    first = (lane % (2 * nf)) < nf
    rot = jnp.where(first, pltpu.roll(x, w - nf, 1), pltpu.roll(x, nf, 1))
    return x * cos + rot * sin


def _ada_kernel(c_ref, w_ref, b_ref, o_ref):
    a = _silu(c_ref[...]).astype(BF16)
    o_ref[...] = _dot(a, w_ref[...].astype(BF16)) + b_ref[...]


def _ada_call(cond, w_ada, b_ada):
    depth, d, n = w_ada.shape
    rows = cond.shape[0]
    tn = 1024
    return pl.pallas_call(
        _ada_kernel,
        grid=(depth, n // tn),
        in_specs=[
            pl.BlockSpec((rows, d), lambda l, j: (0, 0)),
            pl.BlockSpec((None, d, tn), lambda l, j: (l, 0, j)),
            pl.BlockSpec((None, 1, tn), lambda l, j: (l, 0, j)),
        ],
        out_specs=pl.BlockSpec((None, rows, tn), lambda l, j: (l, 0, j)),
        out_shape=jax.ShapeDtypeStruct((depth, rows, n), F32),
        compiler_params=_cparams(("parallel", "parallel")),
    )(cond, w_ada, b_ada.reshape(depth, 1, n))


NORM_ROWS = 128


def _mm_kernel(*refs, norm, resid, out_norm):
    refs = list(refs)
    a_ref = refs.pop(0)
    if norm:
        g_ref, sh_ref, sc_ref = refs.pop(0), refs.pop(0), refs.pop(0)
    w_ref = refs.pop(0)
    if resid:
        r_ref, ga_ref = refs.pop(0), refs.pop(0)
    if out_norm:
        go_ref = refs.pop(0)
    o_ref = refs.pop(0)

    if norm:
        h_ref = refs.pop(0)

        @pl.when(pl.program_id(1) == 0)
        def _():
            g = g_ref[...]
            mul = 1.0 + sc_ref[...]
            add = sh_ref[...]

            def body(c, carry):
                r0 = pl.multiple_of(c * NORM_ROWS, NORM_ROWS)
                x = a_ref[pl.ds(r0, NORM_ROWS), :]
                y = x * lax.rsqrt(jnp.mean(x * x, axis=-1, keepdims=True) + EPS) * g
                h_ref[pl.ds(r0, NORM_ROWS), :] = (y * mul + add).astype(BF16)
                return carry

            lax.fori_loop(0, a_ref.shape[0] // NORM_ROWS, body, 0)

        a = h_ref[...]
    else:
        a = a_ref[...]
    acc = _dot(a, w_ref[...])
    if resid:
        acc = r_ref[...] + ga_ref[...] * acc
    if out_norm:
        acc = acc * lax.rsqrt(jnp.mean(acc * acc, axis=-1, keepdims=True) + EPS) * go_ref[...]
    o_ref[...] = acc.astype(o_ref.dtype)


def _mm_call(a, w, *, out_dtype, tm, tn, rows_per_mod, norm=None, resid=None, out_norm=None):
    m, k = a.shape
    n = w.shape[1]
    assert m % tm == 0 and n % tn == 0 and rows_per_mod % tm == 0
    assert out_norm is None or tn == n
    tiles_per_mod = rows_per_mod // tm
    in_specs = [pl.BlockSpec((tm, k), lambda i, j: (i, 0))]
    args = [a]
    if norm is not None:
        g, sh, sc = norm
        in_specs += [
            pl.BlockSpec((1, k), lambda i, j: (0, 0)),
            pl.BlockSpec((None, 1, k), lambda i, j: (i // tiles_per_mod, 0, 0)),
            pl.BlockSpec((None, 1, k), lambda i, j: (i // tiles_per_mod, 0, 0)),
        ]
        args += [g, sh, sc]
    if tn == n:
        in_specs.append(pl.BlockSpec((k, tn), lambda i, j: (0, 0), pipeline_mode=pl.Buffered(1)))
    else:
        in_specs.append(pl.BlockSpec((k, tn), lambda i, j: (0, j)))
    args.append(w)
    if resid is not None:
        x, ga = resid
        in_specs += [
            pl.BlockSpec((tm, tn), lambda i, j: (i, j)),
            pl.BlockSpec((None, 1, tn), lambda i, j: (i // tiles_per_mod, 0, j)),
        ]
        args += [x, ga]
    if out_norm is not None:
        in_specs.append(pl.BlockSpec((1, n), lambda i, j: (0, 0)))
        args.append(out_norm)
    scratch = [pltpu.VMEM((tm, k), BF16)] if norm is not None else []
    return pl.pallas_call(
        functools.partial(_mm_kernel, norm=norm is not None, resid=resid is not None, out_norm=out_norm is not None),
        grid=(m // tm, n // tn),
        in_specs=in_specs,
        out_specs=pl.BlockSpec((tm, tn), lambda i, j: (i, j)),
        out_shape=jax.ShapeDtypeStruct((m, n), out_dtype),
        scratch_shapes=scratch,
        compiler_params=_cparams(("parallel", "arbitrary")),
    )(*args)


def _rope_kernel(x_ref, c_ref, s_ref, o_ref, *, dr):
    o_ref[...] = _rope(x_ref[...].astype(F32), c_ref[...], s_ref[...], dr).astype(o_ref.dtype)


def _rope_call(p, col, width, cos, sin, dr, seq):
    m = p.shape[0]
    tm = 256
    tpb = seq // tm
    cb = col // width
    assert col % width == 0
    return pl.pallas_call(
        functools.partial(_rope_kernel, dr=dr),
        grid=(tpb, m // seq),
        in_specs=[
            pl.BlockSpec((tm, width), lambda i, b: (b * tpb + i, cb)),
            pl.BlockSpec((tm, width), lambda i, b: (i, 0)),
            pl.BlockSpec((tm, width), lambda i, b: (i, 0)),
        ],
        out_specs=pl.BlockSpec((tm, width), lambda i, b: (b * tpb + i, 0)),
        out_shape=jax.ShapeDtypeStruct((m, width), BF16),
        compiler_params=_cparams(("parallel", "parallel")),
    )(p, cos, sin)


MLA_K_W = 4 * 256
MLA_V_W = 4 * 128


def _mla_kv_kernel(*refs, norm, rope):
    refs = list(refs)
    c_ref, kr_ref = refs.pop(0), refs.pop(0)
    if rope:
        cos_ref, sin_ref = refs.pop(0), refs.pop(0)
    if norm:
        g_ref = refs.pop(0)
    w_ref = refs.pop(0)
    k_ref, v_ref = refs.pop(0), refs.pop(0)
    c = c_ref[...].astype(F32)
    if norm:
        ckv_ref = refs.pop(0)
        c = c * lax.rsqrt(jnp.mean(c * c, axis=-1, keepdims=True) + EPS) * g_ref[...]
        ckv_ref[...] = c
    kv = _dot(c.astype(BF16), w_ref[...])
    kr = kr_ref[...].astype(F32)
    if rope:
        kr = _rope(kr, cos_ref[...], sin_ref[...], MLA_ROPE)
    k_ref[...] = (kv[:, :MLA_K_W] + jnp.concatenate([kr] * 4, axis=1)).astype(BF16)
    v_ref[...] = kv[:, MLA_K_W:].astype(BF16)


def _mla_kv_call(ckv_src, ckv_col, kr_src, kr_col, w_e, *, g=None, rope=None, seq=None):
    m = ckv_src.shape[0]
    tm = 256
    assert m % tm == 0 and ckv_col % 256 == 0 and kr_col % 256 == 0
    in_specs = [
        pl.BlockSpec((tm, 256), lambda i: (i, ckv_col // 256)),
        pl.BlockSpec((tm, 256), lambda i: (i, kr_col // 256)),
    ]
    args = [ckv_src, kr_src]
    if rope is not None:
        tpb = seq // tm
        in_specs += [pl.BlockSpec((tm, 256), lambda i: (i % tpb, 0))] * 2
        args += list(rope)
    if g is not None:
        in_specs.append(pl.BlockSpec((1, 256), lambda i: (0, 0)))
        args.append(g)
    in_specs.append(pl.BlockSpec(w_e.shape, lambda i: (0, 0)))
    args.append(w_e)
    out_specs = [pl.BlockSpec((tm, MLA_K_W), lambda i: (i, 0)), pl.BlockSpec((tm, MLA_V_W), lambda i: (i, 0))]
    out_shape = [jax.ShapeDtypeStruct((m, MLA_K_W), BF16), jax.ShapeDtypeStruct((m, MLA_V_W), BF16)]
    if g is not None:
        out_specs.append(pl.BlockSpec((tm, 256), lambda i: (i, 0)))
        out_shape.append(jax.ShapeDtypeStruct((m, 256), F32))
    return pl.pallas_call(
        functools.partial(_mla_kv_kernel, norm=g is not None, rope=rope is not None),
        grid=(m // tm,),
        in_specs=in_specs,
        out_specs=out_specs,
        out_shape=out_shape,
        compiler_params=_cparams(("parallel",)),
    )(*args)


LOG2E = 1.4426950408889634


def _exp_parts(scores, scale=1.0):
    m = scores[0].max(axis=-1, keepdims=True)
    for s in scores[1:]:
        m = jnp.maximum(m, s.max(axis=-1, keepdims=True))
    return [jnp.exp2((s - m) * (scale * LOG2E)) for s in scores]


def _softmax_parts(scores, scale=1.0):
    ps = _exp_parts(scores, scale)
    l = ps[0].sum(axis=-1, keepdims=True)
    for p in ps[1:]:
        l = l + p.sum(axis=-1, keepdims=True)
    return ps, 1.0 / l


def _pv_normalised(ps, vs):
    acc = None
    for p, v in zip(ps, vs):
        t = _dot(p.astype(BF16), jnp.concatenate([v, jnp.ones_like(v)], axis=1))
        acc = t if acc is None else acc + t
    return acc[:, :LANE] / acc[:, LANE:]


def _mha_kernel(*refs, mla, nseg, scale, rope=False):
    refs = list(refs)
    qn_ref = refs.pop(0)
    qr_ref = refs.pop(0) if mla else None
    if rope:
        cos_ref, sin_ref = refs.pop(0), refs.pop(0)
    kv_refs = [(refs.pop(0), refs.pop(0)) for _ in range(nseg)]
    o_ref = refs.pop(0)
    kw = 256 if mla else LANE
    lane = _lane_iota((qn_ref.shape[0], LANE))
    nheads = 8
    if mla:
        qr_all = qr_ref[...].astype(F32)
        if rope:
            qr_all = _rope(qr_all, cos_ref[...], sin_ref[...], MLA_ROPE)

    def head_scores(h):
        j, e = h // 2, h % 2
        q = _half_mask(qn_ref[:, j * LANE:(j + 1) * LANE].astype(F32), e)
        if mla:
            qr = qr_all[:, (h // 4) * LANE:(h // 4 + 1) * LANE]
            qr = jnp.where(lane // MLA_ROPE == h % 4, qr, 0.0)
            q = jnp.concatenate([q, qr], axis=1)
        q = q.astype(BF16)
        return [_dot_t(q, k_ref[:, j * kw:(j + 1) * kw].astype(BF16)) for k_ref, _ in kv_refs]

    outs = []

    def head_values(h, ps):
        j = h // 2
        outs.append(_pv_normalised(ps, [v_ref[:, j * LANE:(j + 1) * LANE].astype(BF16) for _, v_ref in kv_refs]))
        if h % 2 == 1:
            o_ref[:, j * LANE:(j + 1) * LANE] = jnp.where(lane < HALF, outs[0], outs[1]).astype(o_ref.dtype)
            outs.clear()

    scores = head_scores(0)
    pending = None
    for h in range(nheads):
        nxt = head_scores(h + 1) if h + 1 < nheads else None
        ps = _exp_parts(scores, scale)
        if pending is not None:
            head_values(*pending)
        pending = (h, ps)
        scores = nxt
    head_values(*pending)


def _mha_call(qn, qn_col, qr, qr_col, segs, *, nb, tq, t, scale, rope=None):
    mla = qr is not None
    kw = MLA_K_W if mla else 512
    nq = t // tq
    in_specs = [pl.BlockSpec((tq, 512), lambda b, i: (b * nq + i, qn_col // 512))]
    args = [qn]
    if mla:
        in_specs.append(pl.BlockSpec((tq, 256), lambda b, i: (b * nq + i, qr_col // 256)))
        args.append(qr)
    if rope is not None:
        in_specs += [pl.BlockSpec((tq, 256), lambda b, i: (i, 0))] * 2
        args += list(rope)
    for k_arr, k_col, v_arr, v_col, s in segs:
        assert k_col % kw == 0 and v_col % 512 == 0
        in_specs.append(pl.BlockSpec((s, kw), functools.partial(lambda b, i, c: (b, c), c=k_col // kw)))
        in_specs.append(pl.BlockSpec((s, 512), functools.partial(lambda b, i, c: (b, c), c=v_col // 512)))
        args += [k_arr, v_arr]
    return pl.pallas_call(
        functools.partial(_mha_kernel, mla=mla, nseg=len(segs), scale=scale, rope=rope is not None),
        grid=(nb, nq),
        in_specs=in_specs,
        out_specs=pl.BlockSpec((tq, 512), lambda b, i: (b * nq + i, 0)),
        out_shape=jax.ShapeDtypeStruct((nb * t, 512), BF16),
        compiler_params=_cparams(("parallel", "parallel")),
    )(*args)


def _diff_kernel(*refs, nseg, lam_init, rope=False):
    refs = list(refs)
    q_ref = refs.pop(0)
    if rope:
        cos_ref, sin_ref = refs.pop(0), refs.pop(0)
    kv_refs = [(refs.pop(0), refs.pop(0)) for _ in range(nseg)]
    lp_ref, g_ref, o_ref = refs
    lp = lp_ref[...]
    lam = (jnp.exp(jnp.sum(lp[0:1] * lp[1:2], axis=-1, keepdims=True))
           - jnp.exp(jnp.sum(lp[2:3] * lp[3:4], axis=-1, keepdims=True)) + lam_init)
    scale = DIFF_QK ** -0.5
    q_all = q_ref[...].astype(F32)
    if rope:
        q_all = _rope(q_all, cos_ref[...], sin_ref[...], DIFF_QK)

    def head_scores(h):
        q_pair = q_all[:, h * LANE:(h + 1) * LANE]
        return [[_dot_t(_half_mask(q_pair, c).astype(BF16), k_ref[:, h * LANE:(h + 1) * LANE].astype(BF16))
                 for k_ref, _ in kv_refs] for c in range(2)]

    scores = head_scores(0)
    for h in range(DIFF_HEADS):
        nxt = head_scores(h + 1) if h + 1 < DIFF_HEADS else None
        (ps0, inv0), (ps1, inv1) = [_softmax_parts(sc, scale) for sc in scores]
        o = None
        for s in range(nseg):
            a = (ps0[s] * inv0 - lam * (ps1[s] * inv1)).astype(BF16)
            t = _dot(a, kv_refs[s][1][:, h * LANE:(h + 1) * LANE].astype(BF16))
            o = t if o is None else o + t
        o = o * lax.rsqrt(jnp.mean(o * o, axis=-1, keepdims=True) + EPS) * g_ref[:, h * LANE:(h + 1) * LANE]
        o_ref[:, h * LANE:(h + 1) * LANE] = (o * (1.0 - lam_init)).astype(o_ref.dtype)
        scores = nxt


def _diff_call(q, q_col, segs, lp, g, *, nb, tq, t, lam_init, rope=None):
    nq = t // tq
    in_specs = [pl.BlockSpec((tq, 512), lambda b, i: (b * nq + i, q_col // 512))]
    args = [q]
    if rope is not None:
        in_specs += [pl.BlockSpec((tq, 512), lambda b, i: (i, 0))] * 2
        args += list(rope)
    for k_arr, k_col, v_arr, v_col, s in segs:
        in_specs.append(pl.BlockSpec((s, 512), functools.partial(lambda b, i, c: (b, c), c=k_col // 512)))
        in_specs.append(pl.BlockSpec((s, 512), functools.partial(lambda b, i, c: (b, c), c=v_col // 512)))
        args += [k_arr, v_arr]
    in_specs += [pl.BlockSpec((4, DIFF_QK), lambda b, i: (0, 0)), pl.BlockSpec((1, 512), lambda b, i: (0, 0))]
    args += [lp, g]
    return pl.pallas_call(
        functools.partial(_diff_kernel, nseg=len(segs), lam_init=lam_init, rope=rope is not None),
        grid=(nb, nq),
        in_specs=in_specs,
        out_specs=pl.BlockSpec((tq, 512), lambda b, i: (b * nq + i, 0)),
        out_shape=jax.ShapeDtypeStruct((nb * t, 512), BF16),
        compiler_params=_cparams(("parallel", "parallel")),
    )(*args)


NA_QROWS = 4


def _na_plan(rows):
    win = min(rows, NA_QROWS + NA_ROWS)
    kh = min(NA_ROWS, rows)
    cfgs, cfg_idx, starts = [], [], []
    for blk in range(rows // NA_QROWS):
        r0 = blk * NA_QROWS
        s_blk = int(np.clip(r0 - kh // 2, 0, rows - win))
        cfg = tuple((r0 + i - s_blk, int(np.clip(r0 + i - kh // 2, 0, rows - kh)) - s_blk) for i in range(NA_QROWS))
        if cfg not in cfgs:
            cfgs.append(cfg)
        cfg_idx.append(cfgs.index(cfg))
        starts.append(s_blk)
    return win, cfgs, np.array([cfg_idx, starts], np.int32)


def _na_kernel(meta_ref, q_ref, k_ref, v_ref, kc_ref, vc_ref, b_ref, o_ref, *, win):
    start = pl.multiple_of(meta_ref[1, pl.program_id(1)] * GRID_W, GRID_W)
    n_loc = win * GRID_W
    lane = _lane_iota((q_ref.shape[0], LANE))

    def head_scores(h):
        j, e = h // 2, h % 2
        q = _half_mask(q_ref[:, j * LANE:(j + 1) * LANE].astype(F32) * (NA_DIM ** -0.5), e).astype(BF16)
        s_loc = _dot_t(q, k_ref[pl.ds(start, n_loc), j * LANE:(j + 1) * LANE].astype(BF16))
        s_ctx = _dot_t(q, kc_ref[:, j * LANE:(j + 1) * LANE].astype(BF16))
        return [s_loc + b_ref[h], s_ctx]

    scores = head_scores(0)
    outs = []
    for h in range(NA_HEADS):
        nxt = head_scores(h + 1) if h + 1 < NA_HEADS else None
        j = h // 2
        v_loc = v_ref[pl.ds(start, n_loc), j * LANE:(j + 1) * LANE].astype(BF16)
        v_ctx = vc_ref[:, j * LANE:(j + 1) * LANE].astype(BF16)
        outs.append(_pv_normalised(_exp_parts(scores), [v_loc, v_ctx]))
        if h % 2 == 1:
            o_ref[:, j * LANE:(j + 1) * LANE] = jnp.where(lane < HALF, outs[0], outs[1]).astype(o_ref.dtype)
            outs = []
        scores = nxt


def _na_call(p, kc, vc, rel_bias, *, nb, t, past):
    rows = t // GRID_W
    assert rows % NA_QROWS == 0 and rows >= NA_ROWS
    win, cfgs, meta = _na_plan(rows)
    bias_tab = _na_bias_table(rel_bias, cfgs, win)
    nblk = rows // NA_QROWS
    tq = NA_QROWS * GRID_W
    n_loc = win * GRID_W
    grid_spec = pltpu.PrefetchScalarGridSpec(
        num_scalar_prefetch=1,
        grid=(nb, nblk),
        in_specs=[
            pl.BlockSpec((tq, 512), lambda b, r, meta: (b * nblk + r, C_NQ // 512)),
            pl.BlockSpec((t, 512), lambda b, r, meta: (b, C_NK // 512)),
            pl.BlockSpec((t, 512), lambda b, r, meta: (b, C_NV // 512)),
            pl.BlockSpec((past, 512), lambda b, r, meta: (b, 0)),
            pl.BlockSpec((past, 512), lambda b, r, meta: (b, 0)),
            pl.BlockSpec((None, NA_HEADS, tq, n_loc), lambda b, r, meta: (meta[0, r], 0, 0, 0)),
        ],
        out_specs=pl.BlockSpec((tq, 512), lambda b, r, meta: (b * nblk + r, 0)),
    )
    return pl.pallas_call(
        functools.partial(_na_kernel, win=win),
        grid_spec=grid_spec,
        out_shape=jax.ShapeDtypeStruct((nb * t, 512), BF16),
        compiler_params=_cparams(("parallel", "arbitrary")),
    )(jnp.asarray(meta), p, p, p, kc, vc, bias_tab)


def _log_sigmoid(x):
    return jnp.minimum(x, 0.0) - jnp.log(1.0 + jnp.exp(-jnp.abs(x)))


def _ret_kernel(*refs, has_state, t, whole_seq=False):
    refs = list(refs)
    q_ref, k_ref, v_ref, rg_ref, dec_ref, gn_ref = [refs.pop(0) for _ in range(6)]
    s0_ref = refs.pop(0) if has_state else None
    o_ref = refs.pop(0)
    tq = q_ref.shape[0]
    q0 = 0 if whole_seq else pl.program_id(1) * tq
    lg = _log_sigmoid(dec_ref[...])
    ti = (q0 + lax.broadcasted_iota(jnp.int32, (tq, t), 0)).astype(F32)
    ui = lax.broadcasted_iota(jnp.int32, (tq, t), 1).astype(F32)
    diff = ti - ui
    tcol = (q0 + lax.broadcasted_iota(jnp.int32, (tq, 1), 0)).astype(F32)

    def head_scores(h):
        j, e = h // 2, h % 2
        qm = _half_mask(q_ref[:, j * LANE:(j + 1) * LANE].astype(F32), e)
        return qm, _dot_t((qm * (RET_QK ** -0.5)).astype(BF16), k_ref[:, j * LANE:(j + 1) * LANE].astype(BF16))

    cur = head_scores(0)
    for h in range(RET_HEADS):
        nxt = head_scores(h + 1) if h + 1 < RET_HEADS else None
        j = h // 2
        qm, a = cur
        cur = nxt
        lgf = lg[h:h + 1, 0:1]
        lgb = lg[RET_HEADS + h:RET_HEADS + h + 1, 0:1]
        q = qm.astype(BF16)
        dmat = jnp.exp2(diff * jnp.where(diff >= 0, lgf * LOG2E, -lgb * LOG2E))
        y = _dot((a * dmat).astype(BF16), v_ref[:, h * LANE:(h + 1) * LANE].astype(BF16))
        if has_state:
            cf = _dot(q, s0_ref[0, j].astype(BF16))
            cb = _dot(q, s0_ref[1, j].astype(BF16))
            y = y + cf * jnp.exp((tcol + 1.0) * lgf) + cb * jnp.exp((float(t) - tcol) * lgb)
        mu = jnp.mean(y, axis=-1, keepdims=True)
        yc = y - mu
        var = jnp.mean(yc * yc, axis=-1, keepdims=True)
        yn = yc * lax.rsqrt(var + EPS) * gn_ref[:, h * LANE:(h + 1) * LANE]
        o_ref[:, h * LANE:(h + 1) * LANE] = (_silu(rg_ref[:, h * LANE:(h + 1) * LANE].astype(F32)) * yn).astype(o_ref.dtype)


def _ret_call(p, dec, gn, s0, *, nb, tq, t):
    nq = t // tq
    in_specs = [
        pl.BlockSpec((tq, 256), lambda b, i: (b * nq + i, C_RQ // 256)),
        pl.BlockSpec((t, 256), lambda b, i: (b, C_RK // 256)),
        pl.BlockSpec((t, 512), lambda b, i: (b, C_RV // 512)),
        pl.BlockSpec((tq, 512), lambda b, i: (b * nq + i, C_RG // 512)),
        pl.BlockSpec((8, LANE), lambda b, i: (0, 0)),
        pl.BlockSpec((1, 512), lambda b, i: (0, 0)),
    ]
    args = [p, p, p, p, dec, gn]
    if s0 is not None:
        in_specs.append(pl.BlockSpec((None, 2, 2, LANE, LANE), lambda b, i: (b, 0, 0, 0, 0)))
        args.append(s0)
    return pl.pallas_call(
        functools.partial(_ret_kernel, has_state=s0 is not None, t=t),
        grid=(nb, nq),
        in_specs=in_specs,
        out_specs=pl.BlockSpec((tq, 512), lambda b, i: (b * nq + i, 0)),
        out_shape=jax.ShapeDtypeStruct((nb * t, 512), BF16),
        compiler_params=_cparams(("parallel", "parallel")),
    )(*args)


def _ret_state_kernel(k_ref, v_ref, dec_ref, o_ref, *, t):
    lg = _log_sigmoid(dec_ref[...])
    tcol = lax.broadcasted_iota(jnp.int32, (t, 1), 0).astype(F32)
    lane = _lane_iota((t, LANE))
    for j in range(2):
        k_pair = k_ref[:, j * LANE:(j + 1) * LANE].astype(F32) * (RET_QK ** -0.5)
        for d in range(2):
            expo = (float(t) - 1.0 - tcol) if d == 0 else tcol
            w0 = jnp.exp(expo * lg[d * RET_HEADS + 2 * j:d * RET_HEADS + 2 * j + 1, 0:1])
            w1 = jnp.exp(expo * lg[d * RET_HEADS + 2 * j + 1:d * RET_HEADS + 2 * j + 2, 0:1])
            kd = (k_pair * jnp.where(lane < HALF, w0, w1)).T.astype(BF16)
            for e in range(2):
                h = 2 * j + e
                sfull = _dot(kd, v_ref[:, h * LANE:(h + 1) * LANE].astype(BF16))
                o_ref[d, h] = sfull[e * HALF:(e + 1) * HALF, :]


def _ret_state_call(p, dec, *, nb, t):
    return pl.pallas_call(
        functools.partial(_ret_state_kernel, t=t),
        grid=(nb,),
        in_specs=[
            pl.BlockSpec((t, 256), lambda b: (b, C_RK // 256)),
            pl.BlockSpec((t, 512), lambda b: (b, C_RV // 512)),
            pl.BlockSpec((8, LANE), lambda b: (0, 0)),
        ],
        out_specs=pl.BlockSpec((None, 2, RET_HEADS, RET_QK, RET_V), lambda b: (b, 0, 0, 0, 0)),
        out_shape=jax.ShapeDtypeStruct((nb, 2, RET_HEADS, RET_QK, RET_V), F32),
        compiler_params=_cparams(("parallel",)),
    )(p, p, dec)


def _ctx_mix_kernel(p_ref, w_ref, kvn_ref, lp_ref, dn_ref, dec_ref, gn_ref,
                    ya_ref, yb_ref, yc_ref, yd_ref, ckv_ref, sret_ref, k_scr, v_scr, *, seq, lam_init):
    def col(c, w):
        return p_ref.at[:, c:c + w]

    _mla_kv_kernel(col(C_CKV, 256), col(C_KR, 256), kvn_ref, w_ref, k_scr, v_scr, ckv_ref, norm=True, rope=False)
    _mha_kernel(col(C_QN, 512), col(C_QR, 256), k_scr, v_scr, ya_ref, mla=True, nseg=1,
                scale=(MLA_NOPE + MLA_ROPE) ** -0.5)
    _diff_kernel(col(C_DQ, 512), col(C_DK, 512), col(C_DV, 512), lp_ref, dn_ref, yb_ref, nseg=1, lam_init=lam_init)
    _mha_kernel(col(C_NQ, 512), col(C_NK, 512), col(C_NV, 512), yc_ref, mla=False, nseg=1, scale=NA_DIM ** -0.5)
    _ret_kernel(col(C_RQ, 256), col(C_RK, 256), col(C_RV, 512), col(C_RG, 512), dec_ref, gn_ref, yd_ref,
                has_state=False, t=seq, whole_seq=True)
    _ret_state_kernel(col(C_RK, 256), col(C_RV, 512), dec_ref, sret_ref, t=seq)


def _ctx_mix_call(p, w_ukv_e, kvn, lp, dnorm, dec, gn, *, nb, seq, lam_init):
    def const(shape):
        return pl.BlockSpec(shape, lambda b: (0,) * len(shape))

    def rows(width, dtype):
        return pl.BlockSpec((seq, width), lambda b: (b, 0)), jax.ShapeDtypeStruct((nb * seq, width), dtype)

    outs = [rows(512, BF16)] * 4 + [rows(MLA_KV_RANK, F32)]
    out_specs = [o[0] for o in outs] + [pl.BlockSpec((None, 2, RET_HEADS, RET_QK, RET_V), lambda b: (b, 0, 0, 0, 0))]
    out_shape = [o[1] for o in outs] + [jax.ShapeDtypeStruct((nb, 2, RET_HEADS, RET_QK, RET_V), F32)]
    return pl.pallas_call(
        functools.partial(_ctx_mix_kernel, seq=seq, lam_init=lam_init),
        grid=(nb,),
        in_specs=[pl.BlockSpec((seq, C_GATES), lambda b: (b, 0)), const(w_ukv_e.shape), const((1, MLA_KV_RANK)),
                  const((4, DIFF_QK)), const((1, 512)), const((8, LANE)), const((1, 512))],
        out_specs=out_specs,
        out_shape=out_shape,
        scratch_shapes=[pltpu.VMEM((seq, MLA_K_W), BF16), pltpu.VMEM((seq, MLA_V_W), BF16)],
        compiler_params=_cparams(("parallel",)),
    )(p, w_ukv_e, kvn, lp, dnorm, dec, gn)


MERGE_CHUNK = 512


def _merge_kernel(ya_ref, yb_ref, yc_ref, yd_ref, g0_ref, g1_ref, g2_ref, g3_ref, wb_ref, wo_ref, x_ref, ga_ref,
                  o_ref, m_ref):
    ys = (ya_ref, yb_ref, yc_ref, yd_ref)
    gs = (g0_ref, g1_ref, g2_ref, g3_ref)
    for c0 in range(0, D_MODEL, MERGE_CHUNK):
        cols = slice(c0, c0 + MERGE_CHUNK)
        m = None
        for i in range(N_BRANCH):
            t = _sigmoid(gs[i][:, cols].astype(F32)) * _dot(ys[i][...], wb_ref[i, :, cols])
            m = t if m is None else m + t
        m_ref[:, cols] = m.astype(BF16)
    o_ref[...] = x_ref[...] + ga_ref[...] * _dot(m_ref[...], wo_ref[...])


def _merge_call(ys, p, w_branch, w_o, x, ga, *, rows_per_mod):
    m = p.shape[0]
    tm = 256
    assert m % tm == 0 and rows_per_mod % tm == 0
    tiles_per_mod = rows_per_mod // tm
    first_gate_block = C_GATES // D_MODEL
    assert C_GATES % D_MODEL == 0
    gate_specs = [
        pl.BlockSpec((tm, D_MODEL), functools.partial(lambda r, i: (r, first_gate_block + i), i=i))
        for i in range(N_BRANCH)
    ]
    return pl.pallas_call(
        _merge_kernel,
        grid=(m // tm,),
        in_specs=[pl.BlockSpec((tm, BRANCH_W), lambda r: (r, 0))] * N_BRANCH + gate_specs + [
            pl.BlockSpec((N_BRANCH, BRANCH_W, D_MODEL), lambda r: (0, 0, 0), pipeline_mode=pl.Buffered(1)),
            pl.BlockSpec((D_MODEL, D_MODEL), lambda r: (0, 0), pipeline_mode=pl.Buffered(1)),
            pl.BlockSpec((tm, D_MODEL), lambda r: (r, 0)),
            pl.BlockSpec((None, 1, D_MODEL), lambda r: (r // tiles_per_mod, 0, 0)),
        ],
        out_specs=pl.BlockSpec((tm, D_MODEL), lambda r: (r, 0)),
        out_shape=jax.ShapeDtypeStruct((m, D_MODEL), F32),
        scratch_shapes=[pltpu.VMEM((tm, D_MODEL), BF16)],
        compiler_params=_cparams(("parallel",)),
    )(*ys, p, p, p, p, w_branch, w_o, x, ga)


HALO = 16


SUB = 8


SHIFT_BLK = 256


def _shift_matrix():
    i = np.arange(SHIFT_BLK)
    s = np.zeros((2 * SHIFT_BLK, SHIFT_BLK), np.float32)
    s[i[1:], i[1:] - 1] = 1.0
    s[SHIFT_BLK + i[:-1], i[:-1] + 1] = 1.0
    return jnp.asarray(s, BF16)


def _conv_kernel(s_ref, ua_ref, uap_ref, uan_ref, ub_ref, ubp_ref, ubn_ref, wa_ref, wb_ref, ba_ref, bb_ref, o_ref, *, tiles_per_seq):
    i = pl.program_id(0)
    tm, tc = ua_ref.shape
    nblk = tm // SHIFT_BLK
    has_prev = ((i % tiles_per_seq) != 0).astype(F32)
    has_next = ((i % tiles_per_seq) != tiles_per_seq - 1).astype(F32)
    sub = lax.broadcasted_iota(jnp.int32, (SUB, tc), 0)
    shift = s_ref[...]

    def conv(u_ref, up_ref, un_ref, w_ref, b_ref, r):
        r0 = r * SHIFT_BLK
        ub = u_ref[r0:r0 + SHIFT_BLK, :]
        sh = _dot(shift, ub)
        if r == 0:
            prev = up_ref[HALO - 1:HALO, :].astype(F32) * has_prev
        else:
            prev = u_ref[r0 - 1:r0, :].astype(F32)
        if r == nblk - 1:
            nxt = un_ref[0:1, :].astype(F32) * has_next
        else:
            nxt = u_ref[r0 + SHIFT_BLK:r0 + SHIFT_BLK + 1, :].astype(F32)
        dn = sh[:SHIFT_BLK]
        up = sh[SHIFT_BLK:]
        um1 = jnp.concatenate([jnp.where(sub == 0, prev, dn[:SUB]), dn[SUB:]], axis=0)
        up1 = jnp.concatenate([up[:-SUB], jnp.where(sub == SUB - 1, nxt, up[-SUB:])], axis=0)
        return um1 * w_ref[0:1, :] + ub.astype(F32) * w_ref[1:2, :] + up1 * w_ref[2:3, :] + b_ref[...]

    for r in range(nblk):
        a = conv(ua_ref, uap_ref, uan_ref, wa_ref, ba_ref, r)
        b = conv(ub_ref, ubp_ref, ubn_ref, wb_ref, bb_ref, r)
        o_ref[r * SHIFT_BLK:(r + 1) * SHIFT_BLK, :] = (_silu(a) * b).astype(o_ref.dtype)


def _conv_call(u, conv_w, conv_b, *, seq):
    m = u.shape[0]
    tm, tc = min(seq, 512), 1408
    assert seq % tm == 0 and D_FF % tc == 0
    nb_half = D_FF // tc
    last_halo = m // HALO - 1
    r = tm // HALO

    def main(off):
        return pl.BlockSpec((tm, tc), lambda i, c: (i, c + off))

    def prev(off):
        return pl.BlockSpec((HALO, tc), lambda i, c: (jnp.maximum(i * r - 1, 0), c + off))

    def nxt(off):
        return pl.BlockSpec((HALO, tc), lambda i, c: (jnp.minimum((i + 1) * r, last_halo), c + off))

    def vec(rows, off):
        return pl.BlockSpec((rows, tc), lambda i, c: (0, c + off))

    cb = conv_b.reshape(1, 2 * D_FF)
    return pl.pallas_call(
        functools.partial(_conv_kernel, tiles_per_seq=seq // tm),
        grid=(m // tm, nb_half),
        in_specs=[pl.BlockSpec((2 * SHIFT_BLK, SHIFT_BLK), lambda i, c: (0, 0)),
                  main(0), prev(0), nxt(0), main(nb_half), prev(nb_half), nxt(nb_half),
                  vec(CONV_W, 0), vec(CONV_W, nb_half), vec(1, 0), vec(1, nb_half)],
        out_specs=pl.BlockSpec((tm, tc), lambda i, c: (i, c)),
        out_shape=jax.ShapeDtypeStruct((m, D_FF), BF16),
        compiler_params=_cparams(("parallel", "parallel")),
    )(_shift_matrix(), u, u, u, u, u, u, conv_w, conv_w, cb, cb)


def _final_norm_kernel(x_ref, g_ref, o_ref):
    x = x_ref[...]
    o_ref[...] = x * lax.rsqrt(jnp.mean(x * x, axis=-1, keepdims=True) + EPS) * g_ref[...]


def _final_norm_call(x, g):
    m, d = x.shape
    tm = 256
    return pl.pallas_call(
        _final_norm_kernel,
        grid=(m // tm,),
        in_specs=[pl.BlockSpec((tm, d), lambda i: (i, 0)), pl.BlockSpec((1, d), lambda i: (0, 0))],
        out_specs=pl.BlockSpec((tm, d), lambda i: (i, 0)),
        out_shape=jax.ShapeDtypeStruct((m, d), F32),
        compiler_params=_cparams(("parallel",)),
    )(x, g.reshape(1, d))


def _reorder_w_in(w):
    d = w.shape[0]
    mq = w[:, :768].reshape(d, MLA_HEADS, MLA_NOPE + MLA_ROPE)
    qn = mq[:, :, :MLA_NOPE].reshape(d, 512)
    qr = mq[:, :, MLA_NOPE:].reshape(d, 256)
    ckv = w[:, 768:1024]
    mkr = w[:, 1024:1056]
    rest = w[:, 1056:5664]
    gates = w[:, 5664:]
    krblk = jnp.concatenate([jnp.zeros((d, LANE), w.dtype)] + [mkr] * 4, axis=1)
    pad = jnp.zeros((d, C_GATES - C_KR - 256), w.dtype)
    return jnp.concatenate([x.astype(BF16) for x in (qn, qr, ckv, rest, krblk, pad, gates)], axis=1)


def _reorder_w_ukv(w):
    r = w.shape[0]
    w3 = w.reshape(r, MLA_HEADS, MLA_NOPE + MLA_V)
    kn = w3[:, :, :MLA_NOPE].reshape(r, 4, LANE)
    vv = w3[:, :, MLA_NOPE:].reshape(r, 512)
    knz = jnp.concatenate([kn, jnp.zeros_like(kn)], axis=2).reshape(r, MLA_K_W)
    return jnp.concatenate([knz, vv], axis=1).astype(BF16)


def _rope_tables(t, dr, reps, lead_zero=0):
    nf = dr // 4
    pos = jnp.arange(t)
    pos = jnp.stack([pos // GRID_W, pos % GRID_W], axis=-1).astype(F32)
    inv = ROPE_BASE ** (-jnp.arange(nf, dtype=F32) / nf)
    ang = pos[:, :, None] * inv
    cos = jnp.cos(ang)
    sin = jnp.sin(ang)
    c = jnp.tile(jnp.concatenate([cos, cos], axis=2).reshape(t, dr), (1, reps))
    s = jnp.tile(jnp.concatenate([-sin, sin], axis=2).reshape(t, dr), (1, reps))
    if lead_zero:
        z = jnp.zeros((t, lead_zero), F32)
        c = jnp.concatenate([z, c], axis=1)
        s = jnp.concatenate([z, s], axis=1)
    return c, s


def _na_bias_table(rel_bias, cfgs, win):
    v = rel_bias.astype(F32)
    pad = GRID_W - NA_COLS
    ext = jnp.concatenate([jnp.repeat(v[..., :1], pad, axis=-1), v, jnp.repeat(v[..., -1:], pad, axis=-1)], axis=-1)
    col_t = jnp.stack([ext[..., GRID_W - 1 - q:2 * GRID_W - 1 - q] for q in range(GRID_W)], axis=2)
    qc = np.arange(GRID_W)[:, None]
    kc = np.arange(GRID_W)[None, :]
    c0 = np.clip(qc - NA_COLS // 2, 0, GRID_W - NA_COLS)
    col_ok = (kc >= c0) & (kc < c0 + NA_COLS)
    col_t = jnp.where(col_ok, col_t, NEG_INF)
    neg = jnp.full((NA_HEADS, GRID_W, GRID_W), NEG_INF, F32)
    kh = min(NA_ROWS, win)
    tabs = []
    for cfg in cfgs:
        per_row = []
        for delta, off in cfg:
            blocks = [col_t[:, kk - delta + NA_ROWS - 1] if off <= kk < off + kh else neg for kk in range(win)]
            per_row.append(jnp.concatenate(blocks, axis=-1))
        tabs.append(jnp.concatenate(per_row, axis=1))
    return jnp.stack(tabs)


def kernel(x_prompt, x_sample, cache_mla_ckv, cache_mla_krope, cache_diff_k, cache_diff_v, cache_na_k, cache_na_v, state_ret, c, c_ctx, w_ada, b_ada, g_attn, g_ffn, w_in, mla_kv_norm, w_mla_ukv, diff_lambda, diff_norm, na_rel_bias, ret_decay, ret_norm, w_branch, w_o, w_up, conv_w, conv_b, w_down, g_final):
    nbc, seq, d = x_prompt.shape
    nbl, t, _ = x_sample.shape
    past = cache_mla_ckv.shape[2]
    depth = w_in.shape[0]
    mc, ml = nbc * seq, nbl * t

    xc = x_prompt.reshape(mc, d)
    xl = x_sample.reshape(ml, d)

    mod_rows = 16
    cond = jnp.concatenate([c_ctx[None], c, jnp.zeros((mod_rows - 1 - nbl, d), F32)], axis=0)
    mods = _ada_call(cond, w_ada, b_ada).reshape(depth, mod_rows, 6, d)

    cos32q, sin32q = _rope_tables(t, MLA_ROPE, 8)
    cos32k, sin32k = _rope_tables(t, MLA_ROPE, 4, lead_zero=LANE)
    cos64, sin64 = _rope_tables(t, DIFF_QK, 8)

    tm_c = min(1024, mc)
    tm_l = min(1024, t)
    states = []
    for l in range(depth):
        lam_init = 0.8 - 0.6 * math.exp(-0.3 * l)
        w_in_r = _reorder_w_in(w_in[l])
        w_ukv_e = _reorder_w_ukv(w_mla_ukv[l])
        wb = w_branch[l].astype(BF16)
        wo = w_o[l].astype(BF16)
        wup = w_up[l].astype(BF16)
        wdn = w_down[l].astype(BF16)
        kvn = mla_kv_norm[l].reshape(1, MLA_KV_RANK)
        dec = jnp.broadcast_to(ret_decay[l].reshape(2 * RET_HEADS, 1), (2 * RET_HEADS, LANE))
        gn = ret_norm[l].reshape(1, 512)
        dnorm = diff_norm[l].reshape(1, 512)
        g_out = g_final.reshape(1, d) if l == depth - 1 else None
        g_a = g_attn[l].reshape(1, d)
        g_f = g_ffn[l].reshape(1, d)

        def mod(i):
            return mods[l, 0:1, i][:, None, :], mods[l, 1:1 + nbl, i][:, None, :]

        sh1, sc1, ga1, sh2, sc2, ga2 = [mod(i) for i in range(6)]

        pc = _mm_call(xc, w_in_r, out_dtype=F32, tm=tm_c, tn=1024, rows_per_mod=mc, norm=(g_a, sh1[0], sc1[0]))
        ya, yb, yc, yd, ckv_state, sret = _ctx_mix_call(pc, w_ukv_e, kvn, diff_lambda[l], dnorm, dec, gn,
                                                        nb=nbc, seq=seq, lam_init=lam_init)
        states.append((
            ckv_state.reshape(nbc, seq, MLA_KV_RANK),
            pc[:, C_KR + LANE:C_KR + LANE + MLA_ROPE].reshape(nbc, seq, MLA_ROPE),
            pc[:, C_DK:C_DK + 512].reshape(nbc, seq, DIFF_HEADS, 2, DIFF_QK),
            pc[:, C_DV:C_DV + 512].reshape(nbc, seq, DIFF_HEADS, DIFF_V),
            pc[:, C_NK:C_NK + 512].reshape(nbc, seq, NA_HEADS, NA_DIM),
            pc[:, C_NV:C_NV + 512].reshape(nbc, seq, NA_HEADS, NA_DIM),
            sret,
        ))
        xc = _merge_call((ya, yb, yc, yd), pc, wb, wo, xc, ga1[0], rows_per_mod=mc)
        u = _mm_call(xc, wup, out_dtype=BF16, tm=tm_c, tn=1024, rows_per_mod=mc, norm=(g_f, sh2[0], sc2[0]))
        act = _conv_call(u, conv_w[l], conv_b[l], seq=seq)
        xc = _mm_call(act, wdn, out_dtype=F32, tm=256, tn=d, rows_per_mod=mc, resid=(xc, ga2[0]), out_norm=g_out)

        plat = _mm_call(xl, w_in_r, out_dtype=BF16, tm=tm_l, tn=2048, rows_per_mod=t, norm=(g_a, sh1[1], sc1[1]))
        kl_mla, vl_mla, _ = _mla_kv_call(plat, C_CKV, plat, C_KR, w_ukv_e, g=kvn, rope=(cos32k, sin32k), seq=t)
        kr_c = cache_mla_krope[:, l].reshape(nbl * past, MLA_ROPE)
        krblk_c = jnp.concatenate([jnp.zeros((nbl * past, LANE), F32)] + [kr_c] * 4, axis=1)
        kp_mla, vp_mla = _mla_kv_call(cache_mla_ckv[:, l].reshape(nbl * past, MLA_KV_RANK), 0, krblk_c, 0, w_ukv_e)
        dk_l = _rope_call(plat, C_DK, 512, cos64, sin64, DIFF_QK, t)
        tq = min(512, t)
        ya = _mha_call(plat, C_QN, plat, C_QR, [(kl_mla, 0, vl_mla, 0, t), (kp_mla, 0, vp_mla, 0, past)],
                       nb=nbl, tq=tq, t=t, scale=(MLA_NOPE + MLA_ROPE) ** -0.5, rope=(cos32q, sin32q))
        dk_c = cache_diff_k[:, l].reshape(nbl * past, 512)
        dv_c = cache_diff_v[:, l].reshape(nbl * past, 512)
        yb = _diff_call(plat, C_DQ, [(dk_l, 0, plat, C_DV, t), (dk_c, 0, dv_c, 0, past)], diff_lambda[l], dnorm,
                        nb=nbl, tq=tq, t=t, lam_init=lam_init, rope=(cos64, sin64))
        yc = _na_call(plat, cache_na_k[:, l].reshape(nbl * past, 512), cache_na_v[:, l].reshape(nbl * past, 512),
                      na_rel_bias[l], nb=nbl, t=t, past=past)
        s0 = state_ret[:, l].astype(F32).reshape(nbl, 2, 2, LANE, LANE)
        yd = _ret_call(plat, dec, gn, s0, nb=nbl, tq=tq, t=t)
        xl = _merge_call((ya, yb, yc, yd), plat, wb, wo, xl, ga1[1], rows_per_mod=t)
        u = _mm_call(xl, wup, out_dtype=BF16, tm=tm_l, tn=1024, rows_per_mod=t, norm=(g_f, sh2[1], sc2[1]))
        act = _conv_call(u, conv_w[l], conv_b[l], seq=t)
        xl = _mm_call(act, wdn, out_dtype=F32, tm=256, tn=d, rows_per_mod=t, resid=(xl, ga2[1]), out_norm=g_out)

    y_prompt = xc.reshape(nbc, seq, d)
    y_sample = xl.reshape(nbl, t, d)
    outs = [jnp.stack([st[i] for st in states], axis=1) for i in range(7)]
    return (y_prompt, y_sample, *outs)
```

```python
import functools
import math

import jax
import jax.numpy as jnp
import numpy as np
from jax import lax
from jax.experimental import pallas as pl
from jax.experimental.pallas import tpu as pltpu

F32 = jnp.float32
BF16 = jnp.bfloat16

D_MODEL = 2048
GRID_W = 64
ROPE_BASE = 10000.0
EPS = 1e-6
NEG_INF = -1e30
N_BRANCH = 4
BRANCH_W = 512
MLA_HEADS = 8
MLA_NOPE = 64
MLA_ROPE = 32
MLA_V = 64
MLA_KV_RANK = 256
DIFF_QK = 64
DIFF_V = 128
DIFF_HEADS = 4
NA_DIM = 64
NA_HEADS = 8
NA_ROWS = 8
NA_COLS = 16
RET_V = 128
RET_HEADS = 4
RET_QK = 64
D_FF = 5632
CONV_W = 3

LANE = 128
HALF = 64

C_QN = 0
C_QR = 512
C_CKV = 768
C_DQ = 1024
C_DK = 1536
C_DV = 2048
C_NQ = 2560
C_NK = 3072
C_NV = 3584
C_RQ = 4096
C_RK = 4352
C_RV = 4608
C_RG = 5120
C_KR = 5632
C_GATES = 6144
P_WIDTH = C_GATES + N_BRANCH * D_MODEL

VMEM_LIMIT = 56 * 1024 * 1024


def _cparams(sem):
    return pltpu.CompilerParams(dimension_semantics=sem, vmem_limit_bytes=VMEM_LIMIT)


def _dot(a, b):
    return jnp.dot(a, b, preferred_element_type=F32)


def _dot_t(a, b):
    return lax.dot_general(a, b, (((1,), (1,)), ((), ())), preferred_element_type=F32)


def _sigmoid(x):
    return 0.5 * jnp.tanh(0.5 * x) + 0.5


def _silu(x):
    h = 0.5 * x
    return h * (jnp.tanh(h) + 1.0)


def _lane_iota(shape):
    return lax.broadcasted_iota(jnp.int32, shape, len(shape) - 1)


def _half_mask(x, e):
    lane = _lane_iota(x.shape)
    keep = (lane < HALF) if e == 0 else (lane >= HALF)
    return jnp.where(keep, x, 0.0)


def _rope(x, cos, sin, dr):
    w = x.shape[-1]
    nf = dr // 4
    lane = _lane_iota(x.shape)
    first = (lane % (2 * nf)) < nf
    rot = jnp.where(first, pltpu.roll(x, w - nf, 1), pltpu.roll(x, nf, 1))
    return x * cos + rot * sin


def _ada_kernel(c_ref, w_ref, b_ref, o_ref):
    a = _silu(c_ref[...]).astype(BF16)
    o_ref[...] = _dot(a, w_ref[...].astype(BF16)) + b_ref[...]


def _ada_call(cond, w_ada, b_ada):
    depth, d, n = w_ada.shape
    rows = cond.shape[0]
    tn = 2048
    return pl.pallas_call(
        _ada_kernel,
        grid=(depth, n // tn),
        in_specs=[
            pl.BlockSpec((rows, d), lambda l, j: (0, 0)),
            pl.BlockSpec((None, d, tn), lambda l, j: (l, 0, j)),
            pl.BlockSpec((None, 1, tn), lambda l, j: (l, 0, j)),
        ],
        out_specs=pl.BlockSpec((None, rows, tn), lambda l, j: (l, 0, j)),
        out_shape=jax.ShapeDtypeStruct((depth, rows, n), F32),
        compiler_params=_cparams(("parallel", "parallel")),
    )(cond, w_ada, b_ada.reshape(depth, 1, n))


NORM_ROWS = 128


def _mm_kernel(*refs, norm, resid, out_norm):
    refs = list(refs)
    a_ref = refs.pop(0)
    if norm:
        g_ref, sh_ref, sc_ref = refs.pop(0), refs.pop(0), refs.pop(0)
    w_ref = refs.pop(0)
    if resid:
        r_ref, ga_ref = refs.pop(0), refs.pop(0)
    if out_norm:
        go_ref = refs.pop(0)
    o_ref = refs.pop(0)

    if norm:
        h_ref = refs.pop(0)

        @pl.when(pl.program_id(1) == 0)
        def _():
            g = g_ref[...]
            mul = 1.0 + sc_ref[...]
            add = sh_ref[...]

            def body(c, carry):
                r0 = pl.multiple_of(c * NORM_ROWS, NORM_ROWS)
                x = a_ref[pl.ds(r0, NORM_ROWS), :]
                y = x * lax.rsqrt(jnp.mean(x * x, axis=-1, keepdims=True) + EPS) * g
                h_ref[pl.ds(r0, NORM_ROWS), :] = (y * mul + add).astype(BF16)
                return carry

            lax.fori_loop(0, a_ref.shape[0] // NORM_ROWS, body, 0)

        a = h_ref[...]
    else:
        a = a_ref[...]
    acc = _dot(a, w_ref[...])
    if resid:
        acc = r_ref[...] + ga_ref[...] * acc
    if out_norm:
        acc = acc * lax.rsqrt(jnp.mean(acc * acc, axis=-1, keepdims=True) + EPS) * go_ref[...]
    o_ref[...] = acc.astype(o_ref.dtype)


def _mm_call(a, w, *, out_dtype, tm, tn, rows_per_mod, norm=None, resid=None, out_norm=None):
    m, k = a.shape
    n = w.shape[1]
    assert m % tm == 0 and n % tn == 0 and rows_per_mod % tm == 0
    assert out_norm is None or tn == n
    tiles_per_mod = rows_per_mod // tm
    in_specs = [pl.BlockSpec((tm, k), lambda i, j: (i, 0))]
    args = [a]
    if norm is not None:
        g, sh, sc = norm
        in_specs += [
            pl.BlockSpec((1, k), lambda i, j: (0, 0)),
            pl.BlockSpec((None, 1, k), lambda i, j: (i // tiles_per_mod, 0, 0)),
            pl.BlockSpec((None, 1, k), lambda i, j: (i // tiles_per_mod, 0, 0)),
        ]
        args += [g, sh, sc]
    if tn == n:
        in_specs.append(pl.BlockSpec((k, tn), lambda i, j: (0, 0), pipeline_mode=pl.Buffered(1)))
    else:
        in_specs.append(pl.BlockSpec((k, tn), lambda i, j: (0, j)))
    args.append(w)
    if resid is not None:
        x, ga = resid
        in_specs += [
            pl.BlockSpec((tm, tn), lambda i, j: (i, j)),
            pl.BlockSpec((None, 1, tn), lambda i, j: (i // tiles_per_mod, 0, j)),
        ]
        args += [x, ga]
    if out_norm is not None:
        in_specs.append(pl.BlockSpec((1, n), lambda i, j: (0, 0)))
        args.append(out_norm)
    scratch = [pltpu.VMEM((tm, k), BF16)] if norm is not None else []
    return pl.pallas_call(
        functools.partial(_mm_kernel, norm=norm is not None, resid=resid is not None, out_norm=out_norm is not None),
        grid=(m // tm, n // tn),
        in_specs=in_specs,
        out_specs=pl.BlockSpec((tm, tn), lambda i, j: (i, j)),
        out_shape=jax.ShapeDtypeStruct((m, n), out_dtype),
        scratch_shapes=scratch,
        compiler_params=_cparams(("parallel", "arbitrary")),
    )(*args)


def _rope_kernel(x_ref, c_ref, s_ref, o_ref, *, dr):
    o_ref[...] = _rope(x_ref[...].astype(F32), c_ref[...], s_ref[...], dr).astype(o_ref.dtype)


def _rope_call(p, col, width, cos, sin, dr, seq):
    m = p.shape[0]
    tm = 256
    tpb = seq // tm
    cb = col // width
    assert col % width == 0
    return pl.pallas_call(
        functools.partial(_rope_kernel, dr=dr),
        grid=(tpb, m // seq),
        in_specs=[
            pl.BlockSpec((tm, width), lambda i, b: (b * tpb + i, cb)),
            pl.BlockSpec((tm, width), lambda i, b: (i, 0)),
            pl.BlockSpec((tm, width), lambda i, b: (i, 0)),
        ],
        out_specs=pl.BlockSpec((tm, width), lambda i, b: (b * tpb + i, 0)),
        out_shape=jax.ShapeDtypeStruct((m, width), BF16),
        compiler_params=_cparams(("parallel", "parallel")),
    )(p, cos, sin)


MLA_K_W = 4 * 256
MLA_V_W = 4 * 128


def _mla_kv_kernel(*refs, norm, rope):
    refs = list(refs)
    c_ref, kr_ref = refs.pop(0), refs.pop(0)
    if rope:
        cos_ref, sin_ref = refs.pop(0), refs.pop(0)
    if norm:
        g_ref = refs.pop(0)
    w_ref = refs.pop(0)
    k_ref, v_ref = refs.pop(0), refs.pop(0)
    c = c_ref[...].astype(F32)
    if norm:
        ckv_ref = refs.pop(0)
        c = c * lax.rsqrt(jnp.mean(c * c, axis=-1, keepdims=True) + EPS) * g_ref[...]
        ckv_ref[...] = c
    kv = _dot(c.astype(BF16), w_ref[...])
    kr = kr_ref[...].astype(F32)
    if rope:
        kr = _rope(kr, cos_ref[...], sin_ref[...], MLA_ROPE)
    k_ref[...] = (kv[:, :MLA_K_W] + jnp.concatenate([kr] * 4, axis=1)).astype(BF16)
    v_ref[...] = kv[:, MLA_K_W:].astype(BF16)


def _mla_kv_call(ckv_src, ckv_col, kr_src, kr_col, w_e, *, g=None, rope=None, seq=None):
    m = ckv_src.shape[0]
    tm = 256
    assert m % tm == 0 and ckv_col % 256 == 0 and kr_col % 256 == 0
    in_specs = [
        pl.BlockSpec((tm, 256), lambda i: (i, ckv_col // 256)),
        pl.BlockSpec((tm, 256), lambda i: (i, kr_col // 256)),
    ]
    args = [ckv_src, kr_src]
    if rope is not None:
        tpb = seq // tm
        in_specs += [pl.BlockSpec((tm, 256), lambda i: (i % tpb, 0))] * 2
        args += list(rope)
    if g is not None:
        in_specs.append(pl.BlockSpec((1, 256), lambda i: (0, 0)))
        args.append(g)
    in_specs.append(pl.BlockSpec(w_e.shape, lambda i: (0, 0)))
    args.append(w_e)
    out_specs = [pl.BlockSpec((tm, MLA_K_W), lambda i: (i, 0)), pl.BlockSpec((tm, MLA_V_W), lambda i: (i, 0))]
    out_shape = [jax.ShapeDtypeStruct((m, MLA_K_W), BF16), jax.ShapeDtypeStruct((m, MLA_V_W), BF16)]
    if g is not None:
        out_specs.append(pl.BlockSpec((tm, 256), lambda i: (i, 0)))
        out_shape.append(jax.ShapeDtypeStruct((m, 256), F32))
    return pl.pallas_call(
        functools.partial(_mla_kv_kernel, norm=g is not None, rope=rope is not None),
        grid=(m // tm,),
        in_specs=in_specs,
        out_specs=out_specs,
        out_shape=out_shape,
        compiler_params=_cparams(("parallel",)),
    )(*args)


LOG2E = 1.4426950408889634


def _exp_parts(scores, scale=1.0):
    m = scores[0].max(axis=-1, keepdims=True)
    for s in scores[1:]:
        m = jnp.maximum(m, s.max(axis=-1, keepdims=True))
    return [jnp.exp2((s - m) * (scale * LOG2E)) for s in scores]


def _softmax_parts(scores, scale=1.0):
    ps = _exp_parts(scores, scale)
    l = ps[0].sum(axis=-1, keepdims=True)
    for p in ps[1:]:
        l = l + p.sum(axis=-1, keepdims=True)
    return ps, 1.0 / l


def _pv_normalised(ps, vs):
    acc = None
    for p, v in zip(ps, vs):
        t = _dot(p.astype(BF16), jnp.concatenate([v, jnp.ones_like(v)], axis=1))
        acc = t if acc is None else acc + t
    return acc[:, :LANE] / acc[:, LANE:]


def _mha_kernel(*refs, mla, nseg, scale, rope=False):
    refs = list(refs)
    qn_ref = refs.pop(0)
    qr_ref = refs.pop(0) if mla else None
    if rope:
        cos_ref, sin_ref = refs.pop(0), refs.pop(0)
    kv_refs = [(refs.pop(0), refs.pop(0)) for _ in range(nseg)]
    o_ref = refs.pop(0)
    kw = 256 if mla else LANE
    lane = _lane_iota((qn_ref.shape[0], LANE))
    nheads = 8
    if mla:
        qr_all = qr_ref[...].astype(F32)
        if rope:
            qr_all = _rope(qr_all, cos_ref[...], sin_ref[...], MLA_ROPE)

    def head_scores(h):
        j, e = h // 2, h % 2
        q = _half_mask(qn_ref[:, j * LANE:(j + 1) * LANE].astype(F32), e)
        if mla:
            qr = qr_all[:, (h // 4) * LANE:(h // 4 + 1) * LANE]
            qr = jnp.where(lane // MLA_ROPE == h % 4, qr, 0.0)
            q = jnp.concatenate([q, qr], axis=1)
        q = q.astype(BF16)
        return [_dot_t(q, k_ref[:, j * kw:(j + 1) * kw].astype(BF16)) for k_ref, _ in kv_refs]

    outs = []

    def head_values(h, ps):
        j = h // 2
        outs.append(_pv_normalised(ps, [v_ref[:, j * LANE:(j + 1) * LANE].astype(BF16) for _, v_ref in kv_refs]))
        if h % 2 == 1:
            o_ref[:, j * LANE:(j + 1) * LANE] = jnp.where(lane < HALF, outs[0], outs[1]).astype(o_ref.dtype)
            outs.clear()

    scores = head_scores(0)
    pending = None
    for h in range(nheads):
        nxt = head_scores(h + 1) if h + 1 < nheads else None
        ps = _exp_parts(scores, scale)
        if pending is not None:
            head_values(*pending)
        pending = (h, ps)
        scores = nxt
    head_values(*pending)


def _mha_call(qn, qn_col, qr, qr_col, segs, *, nb, tq, t, scale, rope=None):
    mla = qr is not None
    kw = MLA_K_W if mla else 512
    nq = t // tq
    in_specs = [pl.BlockSpec((tq, 512), lambda b, i: (b * nq + i, qn_col // 512))]
    args = [qn]
    if mla:
        in_specs.append(pl.BlockSpec((tq, 256), lambda b, i: (b * nq + i, qr_col // 256)))
        args.append(qr)
    if rope is not None:
        in_specs += [pl.BlockSpec((tq, 256), lambda b, i: (i, 0))] * 2
        args += list(rope)
    for k_arr, k_col, v_arr, v_col, s in segs:
        assert k_col % kw == 0 and v_col % 512 == 0
        in_specs.append(pl.BlockSpec((s, kw), functools.partial(lambda b, i, c: (b, c), c=k_col // kw)))
        in_specs.append(pl.BlockSpec((s, 512), functools.partial(lambda b, i, c: (b, c), c=v_col // 512)))
        args += [k_arr, v_arr]
    return pl.pallas_call(
        functools.partial(_mha_kernel, mla=mla, nseg=len(segs), scale=scale, rope=rope is not None),
        grid=(nb, nq),
        in_specs=in_specs,
        out_specs=pl.BlockSpec((tq, 512), lambda b, i: (b * nq + i, 0)),
        out_shape=jax.ShapeDtypeStruct((nb * t, 512), BF16),
        compiler_params=_cparams(("parallel", "parallel")),
    )(*args)


def _diff_kernel(*refs, nseg, lam_init, rope=False):
    refs = list(refs)
    q_ref = refs.pop(0)
    if rope:
        cos_ref, sin_ref = refs.pop(0), refs.pop(0)
    kv_refs = [(refs.pop(0), refs.pop(0)) for _ in range(nseg)]
    lp_ref, g_ref, o_ref = refs
    lp = lp_ref[...]
    lam = (jnp.exp(jnp.sum(lp[0:1] * lp[1:2], axis=-1, keepdims=True))
           - jnp.exp(jnp.sum(lp[2:3] * lp[3:4], axis=-1, keepdims=True)) + lam_init)
    scale = DIFF_QK ** -0.5
    q_all = q_ref[...].astype(F32)
    if rope:
        q_all = _rope(q_all, cos_ref[...], sin_ref[...], DIFF_QK)

    def head_scores(h):
        q_pair = q_all[:, h * LANE:(h + 1) * LANE]
        return [[_dot_t(_half_mask(q_pair, c).astype(BF16), k_ref[:, h * LANE:(h + 1) * LANE].astype(BF16))
                 for k_ref, _ in kv_refs] for c in range(2)]

    scores = head_scores(0)
    for h in range(DIFF_HEADS):
        nxt = head_scores(h + 1) if h + 1 < DIFF_HEADS else None
        (ps0, inv0), (ps1, inv1) = [_softmax_parts(sc, scale) for sc in scores]
        o = None
        for s in range(nseg):
            a = (ps0[s] * inv0 - lam * (ps1[s] * inv1)).astype(BF16)
            t = _dot(a, kv_refs[s][1][:, h * LANE:(h + 1) * LANE].astype(BF16))
            o = t if o is None else o + t
        o = o * lax.rsqrt(jnp.mean(o * o, axis=-1, keepdims=True) + EPS) * g_ref[:, h * LANE:(h + 1) * LANE]
        o_ref[:, h * LANE:(h + 1) * LANE] = (o * (1.0 - lam_init)).astype(o_ref.dtype)
        scores = nxt


def _diff_call(q, q_col, segs, lp, g, *, nb, tq, t, lam_init, rope=None):
    nq = t // tq
    in_specs = [pl.BlockSpec((tq, 512), lambda b, i: (b * nq + i, q_col // 512))]
    args = [q]
    if rope is not None:
        in_specs += [pl.BlockSpec((tq, 512), lambda b, i: (i, 0))] * 2
        args += list(rope)
    for k_arr, k_col, v_arr, v_col, s in segs:
        in_specs.append(pl.BlockSpec((s, 512), functools.partial(lambda b, i, c: (b, c), c=k_col // 512)))
        in_specs.append(pl.BlockSpec((s, 512), functools.partial(lambda b, i, c: (b, c), c=v_col // 512)))
        args += [k_arr, v_arr]
    in_specs += [pl.BlockSpec((4, DIFF_QK), lambda b, i: (0, 0)), pl.BlockSpec((1, 512), lambda b, i: (0, 0))]
    args += [lp, g]
    return pl.pallas_call(
        functools.partial(_diff_kernel, nseg=len(segs), lam_init=lam_init, rope=rope is not None),
        grid=(nb, nq),
        in_specs=in_specs,
        out_specs=pl.BlockSpec((tq, 512), lambda b, i: (b * nq + i, 0)),
        out_shape=jax.ShapeDtypeStruct((nb * t, 512), BF16),
        compiler_params=_cparams(("parallel", "parallel")),
    )(*args)


NA_QROWS = 4


def _na_plan(rows):
    win = min(rows, NA_QROWS + NA_ROWS)
    kh = min(NA_ROWS, rows)
    cfgs, cfg_idx, starts = [], [], []
    for blk in range(rows // NA_QROWS):
        r0 = blk * NA_QROWS
        s_blk = int(np.clip(r0 - kh // 2, 0, rows - win))
        cfg = tuple((r0 + i - s_blk, int(np.clip(r0 + i - kh // 2, 0, rows - kh)) - s_blk) for i in range(NA_QROWS))
        if cfg not in cfgs:
            cfgs.append(cfg)
        cfg_idx.append(cfgs.index(cfg))
        starts.append(s_blk)
    return win, cfgs, np.array([cfg_idx, starts], np.int32)


def _na_kernel(meta_ref, q_ref, k_ref, v_ref, kc_ref, vc_ref, b_ref, o_ref, *, win):
    start = pl.multiple_of(meta_ref[1, pl.program_id(1)] * GRID_W, GRID_W)
    n_loc = win * GRID_W
    lane = _lane_iota((q_ref.shape[0], LANE))

    def head_scores(h):
        j, e = h // 2, h % 2
        q = _half_mask(q_ref[:, j * LANE:(j + 1) * LANE].astype(F32) * (NA_DIM ** -0.5), e).astype(BF16)
        s_loc = _dot_t(q, k_ref[pl.ds(start, n_loc), j * LANE:(j + 1) * LANE].astype(BF16))
        s_ctx = _dot_t(q, kc_ref[:, j * LANE:(j + 1) * LANE].astype(BF16))
        return [s_loc + b_ref[h], s_ctx]

    scores = head_scores(0)
    outs = []
    for h in range(NA_HEADS):
        nxt = head_scores(h + 1) if h + 1 < NA_HEADS else None
        j = h // 2
        v_loc = v_ref[pl.ds(start, n_loc), j * LANE:(j + 1) * LANE].astype(BF16)
        v_ctx = vc_ref[:, j * LANE:(j + 1) * LANE].astype(BF16)
        outs.append(_pv_normalised(_exp_parts(scores), [v_loc, v_ctx]))
        if h % 2 == 1:
            o_ref[:, j * LANE:(j + 1) * LANE] = jnp.where(lane < HALF, outs[0], outs[1]).astype(o_ref.dtype)
            outs = []
        scores = nxt


def _na_call(p, kc, vc, rel_bias, *, nb, t, past):
    rows = t // GRID_W
    assert rows % NA_QROWS == 0 and rows >= NA_ROWS
    win, cfgs, meta = _na_plan(rows)
    bias_tab = _na_bias_table(rel_bias, cfgs, win)
    nblk = rows // NA_QROWS
    tq = NA_QROWS * GRID_W
    n_loc = win * GRID_W
    grid_spec = pltpu.PrefetchScalarGridSpec(
        num_scalar_prefetch=1,
        grid=(nb, nblk),
        in_specs=[
            pl.BlockSpec((tq, 512), lambda b, r, meta: (b * nblk + r, C_NQ // 512)),
            pl.BlockSpec((t, 512), lambda b, r, meta: (b, C_NK // 512)),
            pl.BlockSpec((t, 512), lambda b, r, meta: (b, C_NV // 512)),
            pl.BlockSpec((past, 512), lambda b, r, meta: (b, 0)),
            pl.BlockSpec((past, 512), lambda b, r, meta: (b, 0)),
            pl.BlockSpec((None, NA_HEADS, tq, n_loc), lambda b, r, meta: (meta[0, r], 0, 0, 0)),
        ],
        out_specs=pl.BlockSpec((tq, 512), lambda b, r, meta: (b * nblk + r, 0)),
    )
    return pl.pallas_call(
        functools.partial(_na_kernel, win=win),
        grid_spec=grid_spec,
        out_shape=jax.ShapeDtypeStruct((nb * t, 512), BF16),
        compiler_params=_cparams(("parallel", "arbitrary")),
    )(jnp.asarray(meta), p, p, p, kc, vc, bias_tab)


def _log_sigmoid(x):
    return jnp.minimum(x, 0.0) - jnp.log(1.0 + jnp.exp(-jnp.abs(x)))


def _ret_kernel(*refs, has_state, t, whole_seq=False):
    refs = list(refs)
    q_ref, k_ref, v_ref, rg_ref, dec_ref, gn_ref = [refs.pop(0) for _ in range(6)]
    s0_ref = refs.pop(0) if has_state else None
    o_ref = refs.pop(0)
    tq = q_ref.shape[0]
    q0 = 0 if whole_seq else pl.program_id(1) * tq
    lg = _log_sigmoid(dec_ref[...])
    ti = (q0 + lax.broadcasted_iota(jnp.int32, (tq, t), 0)).astype(F32)
    ui = lax.broadcasted_iota(jnp.int32, (tq, t), 1).astype(F32)
    diff = ti - ui
    tcol = (q0 + lax.broadcasted_iota(jnp.int32, (tq, 1), 0)).astype(F32)

    def head_scores(h):
        j, e = h // 2, h % 2
        qm = _half_mask(q_ref[:, j * LANE:(j + 1) * LANE].astype(F32), e)
        return qm, _dot_t((qm * (RET_QK ** -0.5)).astype(BF16), k_ref[:, j * LANE:(j + 1) * LANE].astype(BF16))

    cur = head_scores(0)
    for h in range(RET_HEADS):
        nxt = head_scores(h + 1) if h + 1 < RET_HEADS else None
        j = h // 2
        qm, a = cur
        cur = nxt
        lgf = lg[h:h + 1, 0:1]
        lgb = lg[RET_HEADS + h:RET_HEADS + h + 1, 0:1]
        q = qm.astype(BF16)
        dmat = jnp.exp2(diff * jnp.where(diff >= 0, lgf * LOG2E, -lgb * LOG2E))
        y = _dot((a * dmat).astype(BF16), v_ref[:, h * LANE:(h + 1) * LANE].astype(BF16))
        if has_state:
            cf = _dot(q, s0_ref[0, j].astype(BF16))
            cb = _dot(q, s0_ref[1, j].astype(BF16))
            y = y + cf * jnp.exp((tcol + 1.0) * lgf) + cb * jnp.exp((float(t) - tcol) * lgb)
        mu = jnp.mean(y, axis=-1, keepdims=True)
        yc = y - mu
        var = jnp.mean(yc * yc, axis=-1, keepdims=True)
        yn = yc * lax.rsqrt(var + EPS) * gn_ref[:, h * LANE:(h + 1) * LANE]
        o_ref[:, h * LANE:(h + 1) * LANE] = (_silu(rg_ref[:, h * LANE:(h + 1) * LANE].astype(F32)) * yn).astype(o_ref.dtype)


def _ret_call(p, dec, gn, s0, *, nb, tq, t):
    nq = t // tq
    in_specs = [
        pl.BlockSpec((tq, 256), lambda b, i: (b * nq + i, C_RQ // 256)),
        pl.BlockSpec((t, 256), lambda b, i: (b, C_RK // 256)),
        pl.BlockSpec((t, 512), lambda b, i: (b, C_RV // 512)),
        pl.BlockSpec((tq, 512), lambda b, i: (b * nq + i, C_RG // 512)),
        pl.BlockSpec((8, LANE), lambda b, i: (0, 0)),
        pl.BlockSpec((1, 512), lambda b, i: (0, 0)),
    ]
    args = [p, p, p, p, dec, gn]
    if s0 is not None:
        in_specs.append(pl.BlockSpec((None, 2, 2, LANE, LANE), lambda b, i: (b, 0, 0, 0, 0)))
        args.append(s0)
    return pl.pallas_call(
        functools.partial(_ret_kernel, has_state=s0 is not None, t=t),
        grid=(nb, nq),
        in_specs=in_specs,
        out_specs=pl.BlockSpec((tq, 512), lambda b, i: (b * nq + i, 0)),
        out_shape=jax.ShapeDtypeStruct((nb * t, 512), BF16),
        compiler_params=_cparams(("parallel", "parallel")),
    )(*args)


def _ret_state_kernel(k_ref, v_ref, dec_ref, o_ref, *, t):
    lg = _log_sigmoid(dec_ref[...])
    tcol = lax.broadcasted_iota(jnp.int32, (t, 1), 0).astype(F32)
    lane = _lane_iota((t, LANE))
    for j in range(2):
        k_pair = k_ref[:, j * LANE:(j + 1) * LANE].astype(F32) * (RET_QK ** -0.5)
        for d in range(2):
            expo = (float(t) - 1.0 - tcol) if d == 0 else tcol
            w0 = jnp.exp(expo * lg[d * RET_HEADS + 2 * j:d * RET_HEADS + 2 * j + 1, 0:1])
            w1 = jnp.exp(expo * lg[d * RET_HEADS + 2 * j + 1:d * RET_HEADS + 2 * j + 2, 0:1])
            kd = (k_pair * jnp.where(lane < HALF, w0, w1)).T.astype(BF16)
            for e in range(2):
                h = 2 * j + e
                sfull = _dot(kd, v_ref[:, h * LANE:(h + 1) * LANE].astype(BF16))
                o_ref[d, h] = sfull[e * HALF:(e + 1) * HALF, :]


def _ctx_mix_kernel(p_ref, w_ref, kvn_ref, lp_ref, dn_ref, dec_ref, gn_ref,
                    ya_ref, yb_ref, yc_ref, yd_ref, ckv_ref, sret_ref, k_scr, v_scr, *, seq, lam_init):
    def col(c, w):
        return p_ref.at[:, c:c + w]

    _mla_kv_kernel(col(C_CKV, 256), col(C_KR, 256), kvn_ref, w_ref, k_scr, v_scr, ckv_ref, norm=True, rope=False)
    _mha_kernel(col(C_QN, 512), col(C_QR, 256), k_scr, v_scr, ya_ref, mla=True, nseg=1,
                scale=(MLA_NOPE + MLA_ROPE) ** -0.5)
    _diff_kernel(col(C_DQ, 512), col(C_DK, 512), col(C_DV, 512), lp_ref, dn_ref, yb_ref, nseg=1, lam_init=lam_init)
    _mha_kernel(col(C_NQ, 512), col(C_NK, 512), col(C_NV, 512), yc_ref, mla=False, nseg=1, scale=NA_DIM ** -0.5)
    _ret_kernel(col(C_RQ, 256), col(C_RK, 256), col(C_RV, 512), col(C_RG, 512), dec_ref, gn_ref, yd_ref,
                has_state=False, t=seq, whole_seq=True)
    _ret_state_kernel(col(C_RK, 256), col(C_RV, 512), dec_ref, sret_ref, t=seq)


def _ctx_mix_call(p, w_ukv_e, kvn, lp, dnorm, dec, gn, *, nb, seq, lam_init):
    def const(shape):
        return pl.BlockSpec(shape, lambda b: (0,) * len(shape))

    def rows(width, dtype):
        return pl.BlockSpec((seq, width), lambda b: (b, 0)), jax.ShapeDtypeStruct((nb * seq, width), dtype)

    outs = [rows(512, BF16)] * 4 + [rows(MLA_KV_RANK, F32)]
    out_specs = [o[0] for o in outs] + [pl.BlockSpec((None, 2, RET_HEADS, RET_QK, RET_V), lambda b: (b, 0, 0, 0, 0))]
    out_shape = [o[1] for o in outs] + [jax.ShapeDtypeStruct((nb, 2, RET_HEADS, RET_QK, RET_V), F32)]
    return pl.pallas_call(
        functools.partial(_ctx_mix_kernel, seq=seq, lam_init=lam_init),
        grid=(nb,),
        in_specs=[pl.BlockSpec((seq, C_GATES), lambda b: (b, 0)), const(w_ukv_e.shape), const((1, MLA_KV_RANK)),
                  const((4, DIFF_QK)), const((1, 512)), const((8, LANE)), const((1, 512))],
        out_specs=out_specs,
        out_shape=out_shape,
        scratch_shapes=[pltpu.VMEM((seq, MLA_K_W), BF16), pltpu.VMEM((seq, MLA_V_W), BF16)],
        compiler_params=_cparams(("parallel",)),
    )(p, w_ukv_e, kvn, lp, dnorm, dec, gn)


MERGE_CHUNK = 512


def _merge_kernel(ya_ref, yb_ref, yc_ref, yd_ref, g0_ref, g1_ref, g2_ref, g3_ref, wb_ref, wo_ref, x_ref, ga_ref,
                  o_ref, m_ref):
    ys = (ya_ref, yb_ref, yc_ref, yd_ref)
    gs = (g0_ref, g1_ref, g2_ref, g3_ref)
    for c0 in range(0, D_MODEL, MERGE_CHUNK):
        cols = slice(c0, c0 + MERGE_CHUNK)
        m = None
        for i in range(N_BRANCH):
            t = _sigmoid(gs[i][:, cols].astype(F32)) * _dot(ys[i][...], wb_ref[i, :, cols])
            m = t if m is None else m + t
        m_ref[:, cols] = m.astype(BF16)
    o_ref[...] = x_ref[...] + ga_ref[...] * _dot(m_ref[...], wo_ref[...])


def _merge_call(ys, p, w_branch, w_o, x, ga, *, rows_per_mod):
    m = p.shape[0]
    tm = 256
    assert m % tm == 0 and rows_per_mod % tm == 0
    tiles_per_mod = rows_per_mod // tm
    first_gate_block = C_GATES // D_MODEL
    assert C_GATES % D_MODEL == 0
    gate_specs = [
        pl.BlockSpec((tm, D_MODEL), functools.partial(lambda r, i: (r, first_gate_block + i), i=i))
        for i in range(N_BRANCH)
    ]
    return pl.pallas_call(
        _merge_kernel,
        grid=(m // tm,),
        in_specs=[pl.BlockSpec((tm, BRANCH_W), lambda r: (r, 0))] * N_BRANCH + gate_specs + [
            pl.BlockSpec((N_BRANCH, BRANCH_W, D_MODEL), lambda r: (0, 0, 0), pipeline_mode=pl.Buffered(1)),
            pl.BlockSpec((D_MODEL, D_MODEL), lambda r: (0, 0), pipeline_mode=pl.Buffered(1)),
            pl.BlockSpec((tm, D_MODEL), lambda r: (r, 0)),
            pl.BlockSpec((None, 1, D_MODEL), lambda r: (r // tiles_per_mod, 0, 0)),
        ],
        out_specs=pl.BlockSpec((tm, D_MODEL), lambda r: (r, 0)),
        out_shape=jax.ShapeDtypeStruct((m, D_MODEL), F32),
        scratch_shapes=[pltpu.VMEM((tm, D_MODEL), BF16)],
        compiler_params=_cparams(("parallel",)),
    )(*ys, p, p, p, p, w_branch, w_o, x, ga)


HALO = 16


UP_TN = 512
UP_SUB = 256
UP_ROW_PARTS = 1


def _ffn_up_kernel(x_ref, xp_ref, xn_ref, g_ref, sh_ref, sc_ref, wa_ref, wb_ref, cwa_ref, cwb_ref, cba_ref, cbb_ref,
                   o_ref, h_ref, *, seq):
    tm = x_ref.shape[0]
    rows = tm + 2 * HALO
    interior_bounds = seq < tm

    @pl.when(pl.program_id(1) == 0)
    def _():
        g = g_ref[...]
        mul = 1.0 + sc_ref[...]
        add = sh_ref[...]

        def norm(x):
            y = x * lax.rsqrt(jnp.mean(x * x, axis=-1, keepdims=True) + EPS) * g
            return y * mul + add

        first = pl.program_id(0) * tm
        has_prev = ((first % seq) != 0).astype(F32)
        has_next = (((first + tm) % seq) != 0).astype(F32)
        h_ref[0:HALO, :] = (norm(xp_ref[...]) * has_prev).astype(BF16)
        h_ref[HALO + tm:rows, :] = (norm(xn_ref[...]) * has_next).astype(BF16)

        def body(c, carry):
            r0 = pl.multiple_of(c * NORM_ROWS, NORM_ROWS)
            h_ref[pl.ds(HALO + r0, NORM_ROWS), :] = norm(x_ref[pl.ds(r0, NORM_ROWS), :]).astype(BF16)
            return carry

        lax.fori_loop(0, tm // NORM_ROWS, body, 0)

    part = tm // UP_ROW_PARTS
    part_rows = part + 2 * HALO
    ncol = UP_TN // UP_SUB
    subs = [(rp, cp) for rp in range(UP_ROW_PARTS) for cp in range(ncol)]

    def project(rp, cp):
        hs = h_ref[rp * part:rp * part + part_rows, :]
        cols = slice(cp * UP_SUB, (cp + 1) * UP_SUB)
        return _dot(hs, wa_ref[:, cols]), _dot(hs, wb_ref[:, cols])

    def conv(u, w_ref, b_ref, cols, rp):
        um1 = pltpu.roll(u, 1, 0)[HALO:HALO + part]
        up1 = pltpu.roll(u, part_rows - 1, 0)[HALO:HALO + part]
        if interior_bounds:
            pos = (pl.program_id(0) * tm + rp * part + lax.broadcasted_iota(jnp.int32, (part, 1), 0)) % seq
            um1 = um1 * (pos != 0).astype(F32)
            up1 = up1 * (pos != seq - 1).astype(F32)
        return um1 * w_ref[0:1, cols] + u[HALO:HALO + part] * w_ref[1:2, cols] + up1 * w_ref[2:3, cols] + b_ref[:, cols]

    cur = project(*subs[0])
    for s, (rp, cp) in enumerate(subs):
        nxt = project(*subs[s + 1]) if s + 1 < len(subs) else None
        cols = slice(cp * UP_SUB, (cp + 1) * UP_SUB)
        a = conv(cur[0], cwa_ref, cba_ref, cols, rp)
        b = conv(cur[1], cwb_ref, cbb_ref, cols, rp)
        o_ref[rp * part:(rp + 1) * part, cols] = (_silu(a) * b).astype(o_ref.dtype)
        cur = nxt


def _ffn_up_call(x, g, sh, sc, w_up, conv_w, conv_b, *, seq, rows_per_mod, tm):
    m, k = x.shape
    assert m % tm == 0 and rows_per_mod % tm == 0 and D_FF % UP_TN == 0 and (seq % tm == 0 or tm % seq == 0)
    tiles_per_mod = rows_per_mod // tm
    nj = D_FF // UP_TN
    r = tm // HALO
    last = m // HALO - 1
    cb = conv_b.reshape(1, 2 * D_FF)
    return pl.pallas_call(
        functools.partial(_ffn_up_kernel, seq=seq),
        grid=(m // tm, nj),
        in_specs=[
            pl.BlockSpec((tm, k), lambda i, j: (i, 0)),
            pl.BlockSpec((HALO, k), lambda i, j: (jnp.maximum(i * r - 1, 0), 0)),
            pl.BlockSpec((HALO, k), lambda i, j: (jnp.minimum((i + 1) * r, last), 0)),
            pl.BlockSpec((1, k), lambda i, j: (0, 0)),
            pl.BlockSpec((None, 1, k), lambda i, j: (i // tiles_per_mod, 0, 0)),
            pl.BlockSpec((None, 1, k), lambda i, j: (i // tiles_per_mod, 0, 0)),
            pl.BlockSpec((k, UP_TN), lambda i, j: (0, j)),
            pl.BlockSpec((k, UP_TN), lambda i, j: (0, j + nj)),
            pl.BlockSpec((CONV_W, UP_TN), lambda i, j: (0, j)),
            pl.BlockSpec((CONV_W, UP_TN), lambda i, j: (0, j + nj)),
            pl.BlockSpec((1, UP_TN), lambda i, j: (0, j)),
            pl.BlockSpec((1, UP_TN), lambda i, j: (0, j + nj)),
        ],
        out_specs=pl.BlockSpec((tm, UP_TN), lambda i, j: (i, j)),
        out_shape=jax.ShapeDtypeStruct((m, D_FF), BF16),
        scratch_shapes=[pltpu.VMEM((tm + 2 * HALO, k), BF16)],
        compiler_params=_cparams(("parallel", "arbitrary")),
    )(x, x, x, g, sh, sc, w_up, w_up, conv_w, conv_w, cb, cb)


def _reorder_w_in(w):
    d = w.shape[0]
    mq = w[:, :768].reshape(d, MLA_HEADS, MLA_NOPE + MLA_ROPE)
    qn = mq[:, :, :MLA_NOPE].reshape(d, 512)
    qr = mq[:, :, MLA_NOPE:].reshape(d, 256)
    ckv = w[:, 768:1024]
    mkr = w[:, 1024:1056]
    rest = w[:, 1056:5664]
    gates = w[:, 5664:]
    krblk = jnp.concatenate([jnp.zeros((d, LANE), w.dtype)] + [mkr] * 4, axis=1)
    pad = jnp.zeros((d, C_GATES - C_KR - 256), w.dtype)
    return jnp.concatenate([x.astype(BF16) for x in (qn, qr, ckv, rest, krblk, pad, gates)], axis=1)


def _reorder_w_ukv(w):
    r = w.shape[0]
    w3 = w.reshape(r, MLA_HEADS, MLA_NOPE + MLA_V)
    kn = w3[:, :, :MLA_NOPE].reshape(r, 4, LANE)
    vv = w3[:, :, MLA_NOPE:].reshape(r, 512)
    knz = jnp.concatenate([kn, jnp.zeros_like(kn)], axis=2).reshape(r, MLA_K_W)
    return jnp.concatenate([knz, vv], axis=1).astype(BF16)


def _rope_tables(t, dr, reps, lead_zero=0):
    nf = dr // 4
    pos = jnp.arange(t)
    pos = jnp.stack([pos // GRID_W, pos % GRID_W], axis=-1).astype(F32)
    inv = ROPE_BASE ** (-jnp.arange(nf, dtype=F32) / nf)
    ang = pos[:, :, None] * inv
    cos = jnp.cos(ang)
    sin = jnp.sin(ang)
    c = jnp.tile(jnp.concatenate([cos, cos], axis=2).reshape(t, dr), (1, reps))
    s = jnp.tile(jnp.concatenate([-sin, sin], axis=2).reshape(t, dr), (1, reps))
    if lead_zero:
        z = jnp.zeros((t, lead_zero), F32)
        c = jnp.concatenate([z, c], axis=1)
        s = jnp.concatenate([z, s], axis=1)
    return c, s


def _na_bias_table(rel_bias, cfgs, win):
    v = rel_bias.astype(F32)
    pad = GRID_W - NA_COLS
    ext = jnp.concatenate([jnp.repeat(v[..., :1], pad, axis=-1), v, jnp.repeat(v[..., -1:], pad, axis=-1)], axis=-1)
    col_t = jnp.stack([ext[..., GRID_W - 1 - q:2 * GRID_W - 1 - q] for q in range(GRID_W)], axis=2)
    qc = np.arange(GRID_W)[:, None]
    kc = np.arange(GRID_W)[None, :]
    c0 = np.clip(qc - NA_COLS // 2, 0, GRID_W - NA_COLS)
    col_ok = (kc >= c0) & (kc < c0 + NA_COLS)
    col_t = jnp.where(col_ok, col_t, NEG_INF)
    neg = jnp.full((NA_HEADS, GRID_W, GRID_W), NEG_INF, F32)
    kh = min(NA_ROWS, win)
    tabs = []
    for cfg in cfgs:
        per_row = []
        for delta, off in cfg:
            blocks = [col_t[:, kk - delta + NA_ROWS - 1] if off <= kk < off + kh else neg for kk in range(win)]
            per_row.append(jnp.concatenate(blocks, axis=-1))
        tabs.append(jnp.concatenate(per_row, axis=1))
    return jnp.stack(tabs)


def kernel(x_prompt, x_sample, cache_mla_ckv, cache_mla_krope, cache_diff_k, cache_diff_v, cache_na_k, cache_na_v, state_ret, c, c_ctx, w_ada, b_ada, g_attn, g_ffn, w_in, mla_kv_norm, w_mla_ukv, diff_lambda, diff_norm, na_rel_bias, ret_decay, ret_norm, w_branch, w_o, w_up, conv_w, conv_b, w_down, g_final):
    nbc, seq, d = x_prompt.shape
    nbl, t, _ = x_sample.shape
    past = cache_mla_ckv.shape[2]
    depth = w_in.shape[0]
    mc, ml = nbc * seq, nbl * t

    xc = x_prompt.reshape(mc, d)
    xl = x_sample.reshape(ml, d)

    mod_rows = 16
    cond = jnp.concatenate([c_ctx[None], c, jnp.zeros((mod_rows - 1 - nbl, d), F32)], axis=0)
    mods = _ada_call(cond, w_ada, b_ada).reshape(depth, mod_rows, 6, d)

    cos32q, sin32q = _rope_tables(t, MLA_ROPE, 8)
    cos32k, sin32k = _rope_tables(t, MLA_ROPE, 4, lead_zero=LANE)
    cos64, sin64 = _rope_tables(t, DIFF_QK, 8)

    tm_c = min(1024, mc)
    tm_l = min(1024, t)
    states = []
    for l in range(depth):
        lam_init = 0.8 - 0.6 * math.exp(-0.3 * l)
        w_in_r = _reorder_w_in(w_in[l])
        w_ukv_e = _reorder_w_ukv(w_mla_ukv[l])
        wb = w_branch[l].astype(BF16)
        wo = w_o[l].astype(BF16)
        wup = w_up[l].astype(BF16)
        wdn = w_down[l].astype(BF16)
        kvn = mla_kv_norm[l].reshape(1, MLA_KV_RANK)
        dec = jnp.broadcast_to(ret_decay[l].reshape(2 * RET_HEADS, 1), (2 * RET_HEADS, LANE))
        gn = ret_norm[l].reshape(1, 512)
        dnorm = diff_norm[l].reshape(1, 512)
        g_out = g_final.reshape(1, d) if l == depth - 1 else None
        g_a = g_attn[l].reshape(1, d)
        g_f = g_ffn[l].reshape(1, d)

        def mod(i):
            return mods[l, 0:1, i][:, None, :], mods[l, 1:1 + nbl, i][:, None, :]

        sh1, sc1, ga1, sh2, sc2, ga2 = [mod(i) for i in range(6)]

        pc = _mm_call(xc, w_in_r, out_dtype=F32, tm=tm_c, tn=1024, rows_per_mod=mc, norm=(g_a, sh1[0], sc1[0]))
        ya, yb, yc, yd, ckv_state, sret = _ctx_mix_call(pc, w_ukv_e, kvn, diff_lambda[l], dnorm, dec, gn,
                                                        nb=nbc, seq=seq, lam_init=lam_init)
        states.append((
            ckv_state.reshape(nbc, seq, MLA_KV_RANK),
            pc[:, C_KR + LANE:C_KR + LANE + MLA_ROPE].reshape(nbc, seq, MLA_ROPE),
            pc[:, C_DK:C_DK + 512].reshape(nbc, seq, DIFF_HEADS, 2, DIFF_QK),
            pc[:, C_DV:C_DV + 512].reshape(nbc, seq, DIFF_HEADS, DIFF_V),
            pc[:, C_NK:C_NK + 512].reshape(nbc, seq, NA_HEADS, NA_DIM),
            pc[:, C_NV:C_NV + 512].reshape(nbc, seq, NA_HEADS, NA_DIM),
            sret,
        ))
        xc = _merge_call((ya, yb, yc, yd), pc, wb, wo, xc, ga1[0], rows_per_mod=mc)
        act = _ffn_up_call(xc, g_f, sh2[0], sc2[0], wup, conv_w[l], conv_b[l], seq=seq, rows_per_mod=mc, tm=tm_c)
        xc = _mm_call(act, wdn, out_dtype=F32, tm=256, tn=d, rows_per_mod=mc, resid=(xc, ga2[0]), out_norm=g_out)

        plat = _mm_call(xl, w_in_r, out_dtype=BF16, tm=tm_l, tn=2048, rows_per_mod=t, norm=(g_a, sh1[1], sc1[1]))
        kl_mla, vl_mla, _ = _mla_kv_call(plat, C_CKV, plat, C_KR, w_ukv_e, g=kvn, rope=(cos32k, sin32k), seq=t)
        kr_c = cache_mla_krope[:, l].reshape(nbl * past, MLA_ROPE)
        krblk_c = jnp.concatenate([jnp.zeros((nbl * past, LANE), F32)] + [kr_c] * 4, axis=1)
        kp_mla, vp_mla = _mla_kv_call(cache_mla_ckv[:, l].reshape(nbl * past, MLA_KV_RANK), 0, krblk_c, 0, w_ukv_e)
        dk_l = _rope_call(plat, C_DK, 512, cos64, sin64, DIFF_QK, t)
        tq = min(512, t)
        ya = _mha_call(plat, C_QN, plat, C_QR, [(kl_mla, 0, vl_mla, 0, t), (kp_mla, 0, vp_mla, 0, past)],
                       nb=nbl, tq=tq, t=t, scale=(MLA_NOPE + MLA_ROPE) ** -0.5, rope=(cos32q, sin32q))
        dk_c = cache_diff_k[:, l].reshape(nbl * past, 512)
        dv_c = cache_diff_v[:, l].reshape(nbl * past, 512)
        yb = _diff_call(plat, C_DQ, [(dk_l, 0, plat, C_DV, t), (dk_c, 0, dv_c, 0, past)], diff_lambda[l], dnorm,
                        nb=nbl, tq=tq, t=t, lam_init=lam_init, rope=(cos64, sin64))
        yc = _na_call(plat, cache_na_k[:, l].reshape(nbl * past, 512), cache_na_v[:, l].reshape(nbl * past, 512),
                      na_rel_bias[l], nb=nbl, t=t, past=past)
        s0 = state_ret[:, l].astype(F32).reshape(nbl, 2, 2, LANE, LANE)
        yd = _ret_call(plat, dec, gn, s0, nb=nbl, tq=tq, t=t)
        xl = _merge_call((ya, yb, yc, yd), plat, wb, wo, xl, ga1[1], rows_per_mod=t)
        act = _ffn_up_call(xl, g_f, sh2[1], sc2[1], wup, conv_w[l], conv_b[l], seq=t, rows_per_mod=t, tm=tm_l)
        xl = _mm_call(act, wdn, out_dtype=F32, tm=256, tn=d, rows_per_mod=t, resid=(xl, ga2[1]), out_norm=g_out)

    y_prompt = xc.reshape(nbc, seq, d)
    y_sample = xl.reshape(nbl, t, d)
    outs = [jnp.stack([st[i] for st in states], axis=1) for i in range(7)]
    return (y_prompt, y_sample, *outs)
```

```python
import functools
import math

import jax
import jax.numpy as jnp
import numpy as np
from jax import lax
from jax.experimental import pallas as pl
from jax.experimental.pallas import tpu as pltpu

F32 = jnp.float32
BF16 = jnp.bfloat16

D_MODEL = 2048
GRID_W = 64
ROPE_BASE = 10000.0
EPS = 1e-6
NEG_INF = -1e30
N_BRANCH = 4
BRANCH_W = 512
MLA_HEADS = 8
MLA_NOPE = 64
MLA_ROPE = 32
MLA_V = 64
MLA_KV_RANK = 256
DIFF_QK = 64
DIFF_V = 128
DIFF_HEADS = 4
NA_DIM = 64
NA_HEADS = 8
NA_ROWS = 8
NA_COLS = 16
RET_V = 128
RET_HEADS = 4
RET_QK = 64
D_FF = 5632
CONV_W = 3

LANE = 128
HALF = 64

C_CKV = 0
C_KR = 256
C_DK = 512
C_DV = 1024
C_NK = 1536
C_NV = 2048
C_RQ = 2560
C_RK = 2816
STATE_COLS = 3072
C_QN = 3072
C_QR = 3584
C_DQ = 4096
C_NQ = 4608
C_RV = 5120
C_RG = 5632
C_GATES = 6144
P_WIDTH = C_GATES + N_BRANCH * D_MODEL

VMEM_LIMIT = 56 * 1024 * 1024


def _cparams(sem):
    return pltpu.CompilerParams(dimension_semantics=sem, vmem_limit_bytes=VMEM_LIMIT)


def _dot(a, b):
    return jnp.dot(a, b, preferred_element_type=F32)


def _dot_t(a, b):
    return lax.dot_general(a, b, (((1,), (1,)), ((), ())), preferred_element_type=F32)


def _sigmoid(x):
    return 0.5 * jnp.tanh(0.5 * x) + 0.5


def _silu(x):
    h = 0.5 * x
    return h * (jnp.tanh(h) + 1.0)


def _lane_iota(shape):
    return lax.broadcasted_iota(jnp.int32, shape, len(shape) - 1)


def _half_mask(x, e):
    lane = _lane_iota(x.shape)
    keep = (lane < HALF) if e == 0 else (lane >= HALF)
    return jnp.where(keep, x, 0.0)


def _rope(x, cos, sin, dr):
    w = x.shape[-1]
    nf = dr // 4
    lane = _lane_iota(x.shape)
    first = (lane % (2 * nf)) < nf
    rot = jnp.where(first, pltpu.roll(x, w - nf, 1), pltpu.roll(x, nf, 1))
    return x * cos + rot * sin


ADA_TK = 256


def _ada_kernel(c_ref, w_ref, b_ref, o_ref):
    k = pl.program_id(1)

    @pl.when(k == 0)
    def _():
        o_ref[...] = jnp.broadcast_to(b_ref[...], o_ref.shape)

    a = _silu(c_ref[...]).astype(BF16)
    o_ref[...] += _dot(a, w_ref[...].astype(BF16))


def _ada_call(cond, w_ada, b_ada):
    depth, d, n = w_ada.shape
    rows = cond.shape[0]
    assert d % ADA_TK == 0
    return pl.pallas_call(
        _ada_kernel,
        grid=(depth, d // ADA_TK),
        in_specs=[
            pl.BlockSpec((rows, ADA_TK), lambda l, k: (0, k)),
            pl.BlockSpec((None, ADA_TK, n), lambda l, k: (l, k, 0)),
            pl.BlockSpec((None, 1, n), lambda l, k: (l, 0, 0)),
        ],
        out_specs=pl.BlockSpec((None, rows, n), lambda l, k: (l, 0, 0)),
        out_shape=jax.ShapeDtypeStruct((depth, rows, n), F32),
        compiler_params=_cparams(("parallel", "arbitrary")),
    )(cond, w_ada, b_ada.reshape(depth, 1, n))


NORM_ROWS = 128


def _mm_kernel(*refs, norm, resid, out_norm, f32_tiles):
    refs = list(refs)
    a_ref = refs.pop(0)
    if norm:
        g_ref, sh_ref, sc_ref = refs.pop(0), refs.pop(0), refs.pop(0)
    w_ref = refs.pop(0)
    if resid:
        r_ref, ga_ref = refs.pop(0), refs.pop(0)
    if out_norm:
        go_ref = refs.pop(0)
    o_ref = refs.pop(0)
    if f32_tiles:
        side_ref = refs.pop(0)

    if norm:
        h_ref = refs.pop(0)

        @pl.when(pl.program_id(1) == 0)
        def _():
            g = g_ref[...]
            mul = 1.0 + sc_ref[...]
            add = sh_ref[...]

            def body(c, carry):
                r0 = pl.multiple_of(c * NORM_ROWS, NORM_ROWS)
                x = a_ref[pl.ds(r0, NORM_ROWS), :]
                y = x * lax.rsqrt(jnp.mean(x * x, axis=-1, keepdims=True) + EPS) * g
                h_ref[pl.ds(r0, NORM_ROWS), :] = (y * mul + add).astype(BF16)
                return carry

            lax.fori_loop(0, a_ref.shape[0] // NORM_ROWS, body, 0)

        a = h_ref[...]
    else:
        a = a_ref[...]
    acc = _dot(a, w_ref[...])
    if resid:
        acc = r_ref[...] + ga_ref[...] * acc
    if out_norm:
        acc = acc * lax.rsqrt(jnp.mean(acc * acc, axis=-1, keepdims=True) + EPS) * go_ref[...]
    o_ref[...] = acc.astype(o_ref.dtype)
    if f32_tiles:
        @pl.when(pl.program_id(1) < f32_tiles)
        def _():
            side_ref[...] = acc


def _mm_call(a, w, *, out_dtype, tm, tn, rows_per_mod, norm=None, resid=None, out_norm=None, f32_cols=0):
    m, k = a.shape
    n = w.shape[1]
    assert m % tm == 0 and n % tn == 0 and rows_per_mod % tm == 0
    assert out_norm is None or tn == n
    tiles_per_mod = rows_per_mod // tm
    in_specs = [pl.BlockSpec((tm, k), lambda i, j: (i, 0))]
    args = [a]
    if norm is not None:
        g, sh, sc = norm
        in_specs += [
            pl.BlockSpec((1, k), lambda i, j: (0, 0)),
            pl.BlockSpec((None, 1, k), lambda i, j: (i // tiles_per_mod, 0, 0)),
            pl.BlockSpec((None, 1, k), lambda i, j: (i // tiles_per_mod, 0, 0)),
        ]
        args += [g, sh, sc]
    if tn == n:
        in_specs.append(pl.BlockSpec((k, tn), lambda i, j: (0, 0), pipeline_mode=pl.Buffered(1)))
    else:
        in_specs.append(pl.BlockSpec((k, tn), lambda i, j: (0, j)))
    args.append(w)
    if resid is not None:
        x, ga = resid
        in_specs += [
            pl.BlockSpec((tm, tn), lambda i, j: (i, j)),
            pl.BlockSpec((None, 1, tn), lambda i, j: (i // tiles_per_mod, 0, j)),
        ]
        args += [x, ga]
    if out_norm is not None:
        in_specs.append(pl.BlockSpec((1, n), lambda i, j: (0, 0)))
        args.append(out_norm)
    scratch = [pltpu.VMEM((tm, k), BF16)] if norm is not None else []
    assert f32_cols % tn == 0
    f32_tiles = f32_cols // tn
    out_specs = pl.BlockSpec((tm, tn), lambda i, j: (i, j))
    out_shape = jax.ShapeDtypeStruct((m, n), out_dtype)
    if f32_tiles:
        out_specs = [out_specs, pl.BlockSpec((tm, tn), lambda i, j: (i, jnp.minimum(j, f32_tiles - 1)))]
        out_shape = [out_shape, jax.ShapeDtypeStruct((m, f32_cols), F32)]
    return pl.pallas_call(
        functools.partial(_mm_kernel, norm=norm is not None, resid=resid is not None, out_norm=out_norm is not None,
                          f32_tiles=f32_tiles),
        grid=(m // tm, n // tn),
        in_specs=in_specs,
        out_specs=out_specs,
        out_shape=out_shape,
        scratch_shapes=scratch,
        compiler_params=_cparams(("parallel", "arbitrary")),
    )(*args)


def _rope_kernel(x_ref, c_ref, s_ref, o_ref, *, dr):
    o_ref[...] = _rope(x_ref[...].astype(F32), c_ref[...], s_ref[...], dr).astype(o_ref.dtype)


def _rope_call(p, col, width, cos, sin, dr, seq):
    m = p.shape[0]
    tm = 256
    tpb = seq // tm
    cb = col // width
    assert col % width == 0
    return pl.pallas_call(
        functools.partial(_rope_kernel, dr=dr),
        grid=(tpb, m // seq),
        in_specs=[
            pl.BlockSpec((tm, width), lambda i, b: (b * tpb + i, cb)),
            pl.BlockSpec((tm, width), lambda i, b: (i, 0)),
            pl.BlockSpec((tm, width), lambda i, b: (i, 0)),
        ],
        out_specs=pl.BlockSpec((tm, width), lambda i, b: (b * tpb + i, 0)),
        out_shape=jax.ShapeDtypeStruct((m, width), BF16),
        compiler_params=_cparams(("parallel", "parallel")),
    )(p, cos, sin)


MLA_K_W = 4 * 256
MLA_V_W = 4 * 128


def _mla_kv_kernel(*refs, norm, rope):
    refs = list(refs)
    c_ref, kr_ref = refs.pop(0), refs.pop(0)
    if rope:
        cos_ref, sin_ref = refs.pop(0), refs.pop(0)
    if norm:
        g_ref = refs.pop(0)
    w_ref = refs.pop(0)
    k_ref, v_ref = refs.pop(0), refs.pop(0)
    c = c_ref[...].astype(F32)
    if norm:
        ckv_ref = refs.pop(0)
        c = c * lax.rsqrt(jnp.mean(c * c, axis=-1, keepdims=True) + EPS) * g_ref[...]
        ckv_ref[...] = c
    kv = _dot(c.astype(BF16), w_ref[...])
    kr = kr_ref[...].astype(F32)
    if rope:
        kr = _rope(kr, cos_ref[...], sin_ref[...], MLA_ROPE)
    k_ref[...] = (kv[:, :MLA_K_W] + jnp.concatenate([kr] * 4, axis=1)).astype(BF16)
    v_ref[...] = kv[:, MLA_K_W:].astype(BF16)


def _mla_kv_call(ckv_src, ckv_col, kr_src, kr_col, w_e, *, g=None, rope=None, seq=None):
    m = ckv_src.shape[0]
    tm = 256
    assert m % tm == 0 and ckv_col % 256 == 0 and kr_col % 256 == 0
    in_specs = [
        pl.BlockSpec((tm, 256), lambda i: (i, ckv_col // 256)),
        pl.BlockSpec((tm, 256), lambda i: (i, kr_col // 256)),
    ]
    args = [ckv_src, kr_src]
    if rope is not None:
        tpb = seq // tm
        in_specs += [pl.BlockSpec((tm, 256), lambda i: (i % tpb, 0))] * 2
        args += list(rope)
    if g is not None:
        in_specs.append(pl.BlockSpec((1, 256), lambda i: (0, 0)))
        args.append(g)
    in_specs.append(pl.BlockSpec(w_e.shape, lambda i: (0, 0)))
    args.append(w_e)
    out_specs = [pl.BlockSpec((tm, MLA_K_W), lambda i: (i, 0)), pl.BlockSpec((tm, MLA_V_W), lambda i: (i, 0))]
    out_shape = [jax.ShapeDtypeStruct((m, MLA_K_W), BF16), jax.ShapeDtypeStruct((m, MLA_V_W), BF16)]
    if g is not None:
        out_specs.append(pl.BlockSpec((tm, 256), lambda i: (i, 0)))
        out_shape.append(jax.ShapeDtypeStruct((m, 256), F32))
    return pl.pallas_call(
        functools.partial(_mla_kv_kernel, norm=g is not None, rope=rope is not None),
        grid=(m // tm,),
        in_specs=in_specs,
        out_specs=out_specs,
        out_shape=out_shape,
        compiler_params=_cparams(("parallel",)),
    )(*args)


LOG2E = 1.4426950408889634


def _exp_parts(scores, scale=1.0):
    m = scores[0].max(axis=-1, keepdims=True)
    for s in scores[1:]:
        m = jnp.maximum(m, s.max(axis=-1, keepdims=True))
    return [jnp.exp2((s - m) * (scale * LOG2E)) for s in scores]


def _softmax_parts(scores, scale=1.0):
    ps = _exp_parts(scores, scale)
    l = ps[0].sum(axis=-1, keepdims=True)
    for p in ps[1:]:
        l = l + p.sum(axis=-1, keepdims=True)
    return ps, 1.0 / l


def _pv_normalised(ps, vs):
    acc = None
    for p, v in zip(ps, vs):
        t = _dot(p.astype(BF16), jnp.concatenate([v, jnp.ones_like(v)], axis=1))
        acc = t if acc is None else acc + t
    return acc[:, :LANE] / acc[:, LANE:]


def _mha_kernel(*refs, mla, nseg, scale, rope=False):
    refs = list(refs)
    qn_ref = refs.pop(0)
    qr_ref = refs.pop(0) if mla else None
    if rope:
        cos_ref, sin_ref = refs.pop(0), refs.pop(0)
    kv_refs = [(refs.pop(0), refs.pop(0)) for _ in range(nseg)]
    o_ref = refs.pop(0)
    kw = 256 if mla else LANE
    lane = _lane_iota((qn_ref.shape[0], LANE))
    nheads = 8
    if mla:
        qr_all = qr_ref[...].astype(F32)
        if rope:
            qr_all = _rope(qr_all, cos_ref[...], sin_ref[...], MLA_ROPE)

    def head_scores(h):
        j, e = h // 2, h % 2
        q = _half_mask(qn_ref[:, j * LANE:(j + 1) * LANE].astype(F32), e)
        if mla:
            qr = qr_all[:, (h // 4) * LANE:(h // 4 + 1) * LANE]
            qr = jnp.where(lane // MLA_ROPE == h % 4, qr, 0.0)
            q = jnp.concatenate([q, qr], axis=1)
        q = q.astype(BF16)
        return [_dot_t(q, k_ref[:, j * kw:(j + 1) * kw].astype(BF16)) for k_ref, _ in kv_refs]

    outs = []

    def head_values(h, ps):
        j = h // 2
        outs.append(_pv_normalised(ps, [v_ref[:, j * LANE:(j + 1) * LANE].astype(BF16) for _, v_ref in kv_refs]))
        if h % 2 == 1:
            o_ref[:, j * LANE:(j + 1) * LANE] = jnp.where(lane < HALF, outs[0], outs[1]).astype(o_ref.dtype)
            outs.clear()

    scores = head_scores(0)
    pending = None
    for h in range(nheads):
        nxt = head_scores(h + 1) if h + 1 < nheads else None
        ps = _exp_parts(scores, scale)
        if pending is not None:
            head_values(*pending)
        pending = (h, ps)
        scores = nxt
    head_values(*pending)


def _mha_call(qn, qn_col, qr, qr_col, segs, *, nb, tq, t, scale, rope=None):
    mla = qr is not None
    kw = MLA_K_W if mla else 512
    nq = t // tq
    in_specs = [pl.BlockSpec((tq, 512), lambda b, i: (b * nq + i, qn_col // 512))]
    args = [qn]
    if mla:
        in_specs.append(pl.BlockSpec((tq, 256), lambda b, i: (b * nq + i, qr_col // 256)))
        args.append(qr)
    if rope is not None:
        in_specs += [pl.BlockSpec((tq, 256), lambda b, i: (i, 0))] * 2
        args += list(rope)
    for k_arr, k_col, v_arr, v_col, s in segs:
        assert k_col % kw == 0 and v_col % 512 == 0
        in_specs.append(pl.BlockSpec((s, kw), functools.partial(lambda b, i, c: (b, c), c=k_col // kw)))
        in_specs.append(pl.BlockSpec((s, 512), functools.partial(lambda b, i, c: (b, c), c=v_col // 512)))
        args += [k_arr, v_arr]
    return pl.pallas_call(
        functools.partial(_mha_kernel, mla=mla, nseg=len(segs), scale=scale, rope=rope is not None),
        grid=(nb, nq),
        in_specs=in_specs,
        out_specs=pl.BlockSpec((tq, 512), lambda b, i: (b * nq + i, 0)),
        out_shape=jax.ShapeDtypeStruct((nb * t, 512), BF16),
        compiler_params=_cparams(("parallel", "parallel")),
    )(*args)


def _diff_kernel(*refs, nseg, lam_init, rope=False):
    refs = list(refs)
    q_ref = refs.pop(0)
    if rope:
        cos_ref, sin_ref = refs.pop(0), refs.pop(0)
    kv_refs = [(refs.pop(0), refs.pop(0)) for _ in range(nseg)]
    lp_ref, g_ref, o_ref = refs
    lp = lp_ref[...]
    lam = (jnp.exp(jnp.sum(lp[0:1] * lp[1:2], axis=-1, keepdims=True))
           - jnp.exp(jnp.sum(lp[2:3] * lp[3:4], axis=-1, keepdims=True)) + lam_init)
    scale = DIFF_QK ** -0.5
    q_all = q_ref[...].astype(F32)
    if rope:
        q_all = _rope(q_all, cos_ref[...], sin_ref[...], DIFF_QK)

    def head_scores(h):
        q_pair = q_all[:, h * LANE:(h + 1) * LANE]
        return [[_dot_t(_half_mask(q_pair, c).astype(BF16), k_ref[:, h * LANE:(h + 1) * LANE].astype(BF16))
                 for k_ref, _ in kv_refs] for c in range(2)]

    scores = head_scores(0)
    for h in range(DIFF_HEADS):
        nxt = head_scores(h + 1) if h + 1 < DIFF_HEADS else None
        (ps0, inv0), (ps1, inv1) = [_softmax_parts(sc, scale) for sc in scores]
        o = None
        for s in range(nseg):
            a = (ps0[s] * inv0 - lam * (ps1[s] * inv1)).astype(BF16)
            t = _dot(a, kv_refs[s][1][:, h * LANE:(h + 1) * LANE].astype(BF16))
            o = t if o is None else o + t
        o = o * lax.rsqrt(jnp.mean(o * o, axis=-1, keepdims=True) + EPS) * g_ref[:, h * LANE:(h + 1) * LANE]
        o_ref[:, h * LANE:(h + 1) * LANE] = (o * (1.0 - lam_init)).astype(o_ref.dtype)
        scores = nxt


def _diff_call(q, q_col, segs, lp, g, *, nb, tq, t, lam_init, rope=None):
    nq = t // tq
    in_specs = [pl.BlockSpec((tq, 512), lambda b, i: (b * nq + i, q_col // 512))]
    args = [q]
    if rope is not None:
        in_specs += [pl.BlockSpec((tq, 512), lambda b, i: (i, 0))] * 2
        args += list(rope)
    for k_arr, k_col, v_arr, v_col, s in segs:
        in_specs.append(pl.BlockSpec((s, 512), functools.partial(lambda b, i, c: (b, c), c=k_col // 512)))
        in_specs.append(pl.BlockSpec((s, 512), functools.partial(lambda b, i, c: (b, c), c=v_col // 512)))
        args += [k_arr, v_arr]
    in_specs += [pl.BlockSpec((4, DIFF_QK), lambda b, i: (0, 0)), pl.BlockSpec((1, 512), lambda b, i: (0, 0))]
    args += [lp, g]
    return pl.pallas_call(
        functools.partial(_diff_kernel, nseg=len(segs), lam_init=lam_init, rope=rope is not None),
        grid=(nb, nq),
        in_specs=in_specs,
        out_specs=pl.BlockSpec((tq, 512), lambda b, i: (b * nq + i, 0)),
        out_shape=jax.ShapeDtypeStruct((nb * t, 512), BF16),
        compiler_params=_cparams(("parallel", "parallel")),
    )(*args)


NA_QROWS = 4


def _na_plan(rows):
    win = min(rows, NA_QROWS + NA_ROWS)
    kh = min(NA_ROWS, rows)
    cfgs, cfg_idx, starts = [], [], []
    for blk in range(rows // NA_QROWS):
        r0 = blk * NA_QROWS
        s_blk = int(np.clip(r0 - kh // 2, 0, rows - win))
        cfg = tuple((r0 + i - s_blk, int(np.clip(r0 + i - kh // 2, 0, rows - kh)) - s_blk) for i in range(NA_QROWS))
        if cfg not in cfgs:
            cfgs.append(cfg)
        cfg_idx.append(cfgs.index(cfg))
        starts.append(s_blk)
    return win, cfgs, np.array([cfg_idx, starts], np.int32)


def _na_kernel(meta_ref, q_ref, k_ref, v_ref, kc_ref, vc_ref, b_ref, o_ref, *, win):
    cfg = meta_ref[0, pl.program_id(1)]
    start = pl.multiple_of(meta_ref[1, pl.program_id(1)] * GRID_W, GRID_W)
    n_loc = win * GRID_W
    lane = _lane_iota((q_ref.shape[0], LANE))

    def head_scores(h):
        j, e = h // 2, h % 2
        q = _half_mask(q_ref[:, j * LANE:(j + 1) * LANE].astype(F32) * (NA_DIM ** -0.5), e).astype(BF16)
        s_loc = _dot_t(q, k_ref[pl.ds(start, n_loc), j * LANE:(j + 1) * LANE].astype(BF16))
        s_ctx = _dot_t(q, kc_ref[:, j * LANE:(j + 1) * LANE].astype(BF16))
        return [s_loc + b_ref[cfg, h], s_ctx]

    scores = head_scores(0)
    outs = []
    for h in range(NA_HEADS):
        nxt = head_scores(h + 1) if h + 1 < NA_HEADS else None
        j = h // 2
        v_loc = v_ref[pl.ds(start, n_loc), j * LANE:(j + 1) * LANE].astype(BF16)
        v_ctx = vc_ref[:, j * LANE:(j + 1) * LANE].astype(BF16)
        outs.append(_pv_normalised(_exp_parts(scores), [v_loc, v_ctx]))
        if h % 2 == 1:
            o_ref[:, j * LANE:(j + 1) * LANE] = jnp.where(lane < HALF, outs[0], outs[1]).astype(o_ref.dtype)
            outs = []
        scores = nxt


def _na_call(p, kc, vc, rel_bias, *, nb, t, past):
    rows = t // GRID_W
    assert rows % NA_QROWS == 0 and rows >= NA_ROWS
    win, cfgs, meta = _na_plan(rows)
    bias_tab = _na_bias_table(rel_bias, cfgs, win)
    nblk = rows // NA_QROWS
    tq = NA_QROWS * GRID_W
    n_loc = win * GRID_W
    grid_spec = pltpu.PrefetchScalarGridSpec(
        num_scalar_prefetch=1,
        grid=(nb, nblk),
        in_specs=[
            pl.BlockSpec((tq, 512), lambda b, r, meta: (b * nblk + r, C_NQ // 512)),
            pl.BlockSpec((t, 512), lambda b, r, meta: (b, C_NK // 512)),
            pl.BlockSpec((t, 512), lambda b, r, meta: (b, C_NV // 512)),
            pl.BlockSpec((past, 512), lambda b, r, meta: (b, 0)),
            pl.BlockSpec((past, 512), lambda b, r, meta: (b, 0)),
            pl.BlockSpec((len(cfgs), NA_HEADS, tq, n_loc), lambda b, r, meta: (0, 0, 0, 0), pipeline_mode=pl.Buffered(1)),
        ],
        out_specs=pl.BlockSpec((tq, 512), lambda b, r, meta: (b * nblk + r, 0)),
    )
    return pl.pallas_call(
        functools.partial(_na_kernel, win=win),
        grid_spec=grid_spec,
        out_shape=jax.ShapeDtypeStruct((nb * t, 512), BF16),
        compiler_params=_cparams(("parallel", "arbitrary")),
    )(jnp.asarray(meta), p, p, p, kc, vc, bias_tab)


def _log_sigmoid(x):
    return jnp.minimum(x, 0.0) - jnp.log(1.0 + jnp.exp(-jnp.abs(x)))


def _ret_kernel(*refs, has_state, t, whole_seq=False):
    refs = list(refs)
    q_ref, k_ref, v_ref, rg_ref, dec_ref, gn_ref = [refs.pop(0) for _ in range(6)]
    s0_ref = refs.pop(0) if has_state else None
    o_ref = refs.pop(0)
    tq = q_ref.shape[0]
    q0 = 0 if whole_seq else pl.program_id(1) * tq
    lg = _log_sigmoid(dec_ref[...])
    ti = (q0 + lax.broadcasted_iota(jnp.int32, (tq, t), 0)).astype(F32)
    ui = lax.broadcasted_iota(jnp.int32, (tq, t), 1).astype(F32)
    diff = ti - ui
    tcol = (q0 + lax.broadcasted_iota(jnp.int32, (tq, 1), 0)).astype(F32)

    def head_scores(h):
        j, e = h // 2, h % 2
        qm = _half_mask(q_ref[:, j * LANE:(j + 1) * LANE].astype(F32), e)
        return qm, _dot_t((qm * (RET_QK ** -0.5)).astype(BF16), k_ref[:, j * LANE:(j + 1) * LANE].astype(BF16))

    cur = head_scores(0)
    for h in range(RET_HEADS):
        nxt = head_scores(h + 1) if h + 1 < RET_HEADS else None
        j = h // 2
        qm, a = cur
        cur = nxt
        lgf = lg[h:h + 1, 0:1]
        lgb = lg[RET_HEADS + h:RET_HEADS + h + 1, 0:1]
        q = qm.astype(BF16)
        dmat = jnp.exp2(diff * jnp.where(diff >= 0, lgf * LOG2E, -lgb * LOG2E))
        y = _dot((a * dmat).astype(BF16), v_ref[:, h * LANE:(h + 1) * LANE].astype(BF16))
        if has_state:
            cf = _dot(q, s0_ref[0, j].astype(BF16))
            cb = _dot(q, s0_ref[1, j].astype(BF16))
            y = y + cf * jnp.exp((tcol + 1.0) * lgf) + cb * jnp.exp((float(t) - tcol) * lgb)
        mu = jnp.mean(y, axis=-1, keepdims=True)
        yc = y - mu
        var = jnp.mean(yc * yc, axis=-1, keepdims=True)
        yn = yc * lax.rsqrt(var + EPS) * gn_ref[:, h * LANE:(h + 1) * LANE]
        o_ref[:, h * LANE:(h + 1) * LANE] = (_silu(rg_ref[:, h * LANE:(h + 1) * LANE].astype(F32)) * yn).astype(o_ref.dtype)


def _ret_call(p, dec, gn, s0, *, nb, tq, t):
    nq = t // tq
    in_specs = [
        pl.BlockSpec((tq, 256), lambda b, i: (b * nq + i, C_RQ // 256)),
        pl.BlockSpec((t, 256), lambda b, i: (b, C_RK // 256)),
        pl.BlockSpec((t, 512), lambda b, i: (b, C_RV // 512)),
        pl.BlockSpec((tq, 512), lambda b, i: (b * nq + i, C_RG // 512)),
        pl.BlockSpec((8, LANE), lambda b, i: (0, 0)),
        pl.BlockSpec((1, 512), lambda b, i: (0, 0)),
    ]
    args = [p, p, p, p, dec, gn]
    if s0 is not None:
        in_specs.append(pl.BlockSpec((None, 2, 2, LANE, LANE), lambda b, i: (b, 0, 0, 0, 0)))
        args.append(s0)
    return pl.pallas_call(
        functools.partial(_ret_kernel, has_state=s0 is not None, t=t),
        grid=(nb, nq),
        in_specs=in_specs,
        out_specs=pl.BlockSpec((tq, 512), lambda b, i: (b * nq + i, 0)),
        out_shape=jax.ShapeDtypeStruct((nb * t, 512), BF16),
        compiler_params=_cparams(("parallel", "parallel")),
    )(*args)


def _ret_state_kernel(k_ref, v_ref, dec_ref, o_ref, *, t):
    lg = _log_sigmoid(dec_ref[...])
    tcol = lax.broadcasted_iota(jnp.int32, (t, 1), 0).astype(F32)
    lane = _lane_iota((t, LANE))
    for j in range(2):
        k_pair = k_ref[:, j * LANE:(j + 1) * LANE].astype(F32) * (RET_QK ** -0.5)
        for d in range(2):
            expo = (float(t) - 1.0 - tcol) if d == 0 else tcol
            w0 = jnp.exp(expo * lg[d * RET_HEADS + 2 * j:d * RET_HEADS + 2 * j + 1, 0:1])
            w1 = jnp.exp(expo * lg[d * RET_HEADS + 2 * j + 1:d * RET_HEADS + 2 * j + 2, 0:1])
            kd = (k_pair * jnp.where(lane < HALF, w0, w1)).T.astype(BF16)
            for e in range(2):
                h = 2 * j + e
                sfull = _dot(kd, v_ref[:, h * LANE:(h + 1) * LANE].astype(BF16))
                o_ref[d, h] = sfull[e * HALF:(e + 1) * HALF, :]


def _ctx_mix_kernel(p_ref, c32_ref, w_ref, kvn_ref, lp_ref, dn_ref, dec_ref, gn_ref,
                    ya_ref, yb_ref, yc_ref, yd_ref, ckv_ref, sret_ref, k_scr, v_scr, *, seq, lam_init):
    def col(c, w):
        return p_ref.at[:, c:c + w]

    _mla_kv_kernel(c32_ref, col(C_KR, 256), kvn_ref, w_ref, k_scr, v_scr, ckv_ref, norm=True, rope=False)
    _mha_kernel(col(C_QN, 512), col(C_QR, 256), k_scr, v_scr, ya_ref, mla=True, nseg=1,
                scale=(MLA_NOPE + MLA_ROPE) ** -0.5)
    _diff_kernel(col(C_DQ, 512), col(C_DK, 512), col(C_DV, 512), lp_ref, dn_ref, yb_ref, nseg=1, lam_init=lam_init)
    _mha_kernel(col(C_NQ, 512), col(C_NK, 512), col(C_NV, 512), yc_ref, mla=False, nseg=1, scale=NA_DIM ** -0.5)
    _ret_kernel(col(C_RQ, 256), col(C_RK, 256), col(C_RV, 512), col(C_RG, 512), dec_ref, gn_ref, yd_ref,
                has_state=False, t=seq, whole_seq=True)
    _ret_state_kernel(col(C_RK, 256), col(C_RV, 512), dec_ref, sret_ref, t=seq)


def _ctx_mix_call(p, p32, w_ukv_e, kvn, lp, dnorm, dec, gn, *, nb, seq, lam_init):
    def const(shape):
        return pl.BlockSpec(shape, lambda b: (0,) * len(shape))

    def rows(width, dtype):
        return pl.BlockSpec((seq, width), lambda b: (b, 0)), jax.ShapeDtypeStruct((nb * seq, width), dtype)

    outs = [rows(512, BF16)] * 4 + [rows(MLA_KV_RANK, F32)]
    out_specs = [o[0] for o in outs] + [pl.BlockSpec((None, 2, RET_HEADS, RET_QK, RET_V), lambda b: (b, 0, 0, 0, 0))]
    out_shape = [o[1] for o in outs] + [jax.ShapeDtypeStruct((nb, 2, RET_HEADS, RET_QK, RET_V), F32)]
    return pl.pallas_call(
        functools.partial(_ctx_mix_kernel, seq=seq, lam_init=lam_init),
        grid=(nb,),
        in_specs=[pl.BlockSpec((seq, C_GATES), lambda b: (b, 0)),
                  pl.BlockSpec((seq, MLA_KV_RANK), lambda b: (b, C_CKV // MLA_KV_RANK)),
                  const(w_ukv_e.shape), const((1, MLA_KV_RANK)),
                  const((4, DIFF_QK)), const((1, 512)), const((8, LANE)), const((1, 512))],
        out_specs=out_specs,
        out_shape=out_shape,
        scratch_shapes=[pltpu.VMEM((seq, MLA_K_W), BF16), pltpu.VMEM((seq, MLA_V_W), BF16)],
        compiler_params=_cparams(("parallel",)),
    )(p, p32, w_ukv_e, kvn, lp, dnorm, dec, gn)


MERGE_CHUNK = 512


def _merge_kernel(ya_ref, yb_ref, yc_ref, yd_ref, g0_ref, g1_ref, g2_ref, g3_ref, wb_ref, wo_ref, x_ref, ga_ref,
                  o_ref, m_ref):
    ys = (ya_ref, yb_ref, yc_ref, yd_ref)
    gs = (g0_ref, g1_ref, g2_ref, g3_ref)
    for c0 in range(0, D_MODEL, MERGE_CHUNK):
        cols = slice(c0, c0 + MERGE_CHUNK)
        m = None
        for i in range(N_BRANCH):
            t = _sigmoid(gs[i][:, cols].astype(F32)) * _dot(ys[i][...], wb_ref[i, :, cols])
            m = t if m is None else m + t
        m_ref[:, cols] = m.astype(BF16)
    o_ref[...] = x_ref[...] + ga_ref[...] * _dot(m_ref[...], wo_ref[...])


def _merge_call(ys, p, w_branch, w_o, x, ga, *, rows_per_mod):
    m = p.shape[0]
    tm = 256
    assert m % tm == 0 and rows_per_mod % tm == 0
    tiles_per_mod = rows_per_mod // tm
    first_gate_block = C_GATES // D_MODEL
    assert C_GATES % D_MODEL == 0
    gate_specs = [
        pl.BlockSpec((tm, D_MODEL), functools.partial(lambda r, i: (r, first_gate_block + i), i=i))
        for i in range(N_BRANCH)
    ]
    return pl.pallas_call(
        _merge_kernel,
        grid=(m // tm,),
        in_specs=[pl.BlockSpec((tm, BRANCH_W), lambda r: (r, 0))] * N_BRANCH + gate_specs + [
            pl.BlockSpec((N_BRANCH, BRANCH_W, D_MODEL), lambda r: (0, 0, 0), pipeline_mode=pl.Buffered(1)),
            pl.BlockSpec((D_MODEL, D_MODEL), lambda r: (0, 0), pipeline_mode=pl.Buffered(1)),
            pl.BlockSpec((tm, D_MODEL), lambda r: (r, 0)),
            pl.BlockSpec((None, 1, D_MODEL), lambda r: (r // tiles_per_mod, 0, 0)),
        ],
        out_specs=pl.BlockSpec((tm, D_MODEL), lambda r: (r, 0)),
        out_shape=jax.ShapeDtypeStruct((m, D_MODEL), F32),
        scratch_shapes=[pltpu.VMEM((tm, D_MODEL), BF16)],
        compiler_params=_cparams(("parallel",)),
    )(*ys, p, p, p, p, w_branch, w_o, x, ga)


HALO = 16


UP_TN = 512
UP_SUB = 256


def _ffn_up_kernel(x_ref, xp_ref, xn_ref, g_ref, sh_ref, sc_ref, wa_ref, wb_ref, cwa_ref, cwb_ref, cba_ref, cbb_ref,
                   o_ref, h_ref, *, seq):
    tm = x_ref.shape[0]
    rows = tm + 2 * HALO

    @pl.when(pl.program_id(1) == 0)
    def _():
        g = g_ref[...]
        mul = 1.0 + sc_ref[...]
        add = sh_ref[...]

        def norm(x):
            y = x * lax.rsqrt(jnp.mean(x * x, axis=-1, keepdims=True) + EPS) * g
            return (y * mul + add).astype(BF16)

        h_ref[0:HALO, :] = norm(xp_ref[...])
        h_ref[HALO + tm:rows, :] = norm(xn_ref[...])

        def body(c, carry):
            r0 = pl.multiple_of(c * NORM_ROWS, NORM_ROWS)
            h_ref[pl.ds(HALO + r0, NORM_ROWS), :] = norm(x_ref[pl.ds(r0, NORM_ROWS), :])
            return carry

        lax.fori_loop(0, tm // NORM_ROWS, body, 0)

    pos = (pl.program_id(0) * tm + lax.broadcasted_iota(jnp.int32, (tm, 1), 0)) % seq
    keep_prev = (pos != 0).astype(F32)
    keep_next = (pos != seq - 1).astype(F32)
    h = h_ref[...]

    def project(s):
        cols = slice(s * UP_SUB, (s + 1) * UP_SUB)
        return _dot(h, wa_ref[:, cols]), _dot(h, wb_ref[:, cols])

    def conv(u, w_ref, b_ref, cols):
        um1 = pltpu.roll(u, 1, 0)[HALO:HALO + tm] * keep_prev
        up1 = pltpu.roll(u, rows - 1, 0)[HALO:HALO + tm] * keep_next
        return um1 * w_ref[0:1, cols] + u[HALO:HALO + tm] * w_ref[1:2, cols] + up1 * w_ref[2:3, cols] + b_ref[:, cols]

    nsub = UP_TN // UP_SUB
    cur = project(0)
    for s in range(nsub):
        nxt = project(s + 1) if s + 1 < nsub else None
        cols = slice(s * UP_SUB, (s + 1) * UP_SUB)
        a = conv(cur[0], cwa_ref, cba_ref, cols)
        b = conv(cur[1], cwb_ref, cbb_ref, cols)
        o_ref[:, cols] = (_silu(a) * b).astype(o_ref.dtype)
        cur = nxt


def _ffn_up_call(x, g, sh, sc, w_up, conv_w, conv_b, *, seq, rows_per_mod, tm):
    m, k = x.shape
    assert m % tm == 0 and rows_per_mod % tm == 0 and D_FF % UP_TN == 0 and (seq % tm == 0 or tm % seq == 0)
    tiles_per_mod = rows_per_mod // tm
    nj = D_FF // UP_TN
    r = tm // HALO
    last = m // HALO - 1
    cb = conv_b.reshape(1, 2 * D_FF)
    return pl.pallas_call(
        functools.partial(_ffn_up_kernel, seq=seq),
        grid=(m // tm, nj),
        in_specs=[
            pl.BlockSpec((tm, k), lambda i, j: (i, 0)),
            pl.BlockSpec((HALO, k), lambda i, j: (jnp.maximum(i * r - 1, 0), 0)),
            pl.BlockSpec((HALO, k), lambda i, j: (jnp.minimum((i + 1) * r, last), 0)),
            pl.BlockSpec((1, k), lambda i, j: (0, 0)),
            pl.BlockSpec((None, 1, k), lambda i, j: (i // tiles_per_mod, 0, 0)),
            pl.BlockSpec((None, 1, k), lambda i, j: (i // tiles_per_mod, 0, 0)),
            pl.BlockSpec((k, UP_TN), lambda i, j: (0, j)),
            pl.BlockSpec((k, UP_TN), lambda i, j: (0, j + nj)),
            pl.BlockSpec((CONV_W, UP_TN), lambda i, j: (0, j)),
            pl.BlockSpec((CONV_W, UP_TN), lambda i, j: (0, j + nj)),
            pl.BlockSpec((1, UP_TN), lambda i, j: (0, j)),
            pl.BlockSpec((1, UP_TN), lambda i, j: (0, j + nj)),
        ],
        out_specs=pl.BlockSpec((tm, UP_TN), lambda i, j: (i, j)),
        out_shape=jax.ShapeDtypeStruct((m, D_FF), BF16),
        scratch_shapes=[pltpu.VMEM((tm + 2 * HALO, k), BF16)],
        compiler_params=_cparams(("parallel", "arbitrary")),
    )(x, x, x, g, sh, sc, w_up, w_up, conv_w, conv_w, cb, cb)


def _reorder_w_in(w):
    d = w.shape[0]
    mq = w[:, :768].reshape(d, MLA_HEADS, MLA_NOPE + MLA_ROPE)
    qn = mq[:, :, :MLA_NOPE].reshape(d, 512)
    qr = mq[:, :, MLA_NOPE:].reshape(d, 256)
    ckv = w[:, 768:1024]
    mkr = w[:, 1024:1056]
    dq, dk, dv, nq, nk, nv = [w[:, 1056 + 512 * i:1056 + 512 * (i + 1)] for i in range(6)]
    rq, rk = w[:, 4128:4384], w[:, 4384:4640]
    rv, rg = w[:, 4640:5152], w[:, 5152:5664]
    gates = w[:, 5664:]
    krblk = jnp.concatenate([jnp.zeros((d, LANE), w.dtype)] + [mkr] * 4, axis=1)
    pad = jnp.zeros((d, C_DQ - C_QR - 256), w.dtype)
    pieces = (ckv, krblk, dk, dv, nk, nv, rq, rk, qn, qr, pad, dq, nq, rv, rg, gates)
    out = jnp.concatenate([x.astype(BF16) for x in pieces], axis=1)
    assert out.shape[1] == P_WIDTH
    return out


def _reorder_w_ukv(w):
    r = w.shape[0]
    w3 = w.reshape(r, MLA_HEADS, MLA_NOPE + MLA_V)
    kn = w3[:, :, :MLA_NOPE].reshape(r, 4, LANE)
    vv = w3[:, :, MLA_NOPE:].reshape(r, 512)
    knz = jnp.concatenate([kn, jnp.zeros_like(kn)], axis=2).reshape(r, MLA_K_W)
    return jnp.concatenate([knz, vv], axis=1).astype(BF16)


def _rope_tables(t, dr, reps, lead_zero=0):
    nf = dr // 4
    pos = jnp.arange(t)
    pos = jnp.stack([pos // GRID_W, pos % GRID_W], axis=-1).astype(F32)
    inv = ROPE_BASE ** (-jnp.arange(nf, dtype=F32) / nf)
    ang = pos[:, :, None] * inv
    cos = jnp.cos(ang)
    sin = jnp.sin(ang)
    c = jnp.tile(jnp.concatenate([cos, cos], axis=2).reshape(t, dr), (1, reps))
    s = jnp.tile(jnp.concatenate([-sin, sin], axis=2).reshape(t, dr), (1, reps))
    if lead_zero:
        z = jnp.zeros((t, lead_zero), F32)
        c = jnp.concatenate([z, c], axis=1)
        s = jnp.concatenate([z, s], axis=1)
    return c, s


def _na_bias_table(rel_bias, cfgs, win):
    v = rel_bias.astype(F32)
    pad = GRID_W - NA_COLS
    ext = jnp.concatenate([jnp.repeat(v[..., :1], pad, axis=-1), v, jnp.repeat(v[..., -1:], pad, axis=-1)], axis=-1)
    col_t = jnp.stack([ext[..., GRID_W - 1 - q:2 * GRID_W - 1 - q] for q in range(GRID_W)], axis=2)
    qc = np.arange(GRID_W)[:, None]
    kc = np.arange(GRID_W)[None, :]
    c0 = np.clip(qc - NA_COLS // 2, 0, GRID_W - NA_COLS)
    col_ok = (kc >= c0) & (kc < c0 + NA_COLS)
    col_t = jnp.where(col_ok, col_t, NEG_INF)
    neg = jnp.full((NA_HEADS, GRID_W, GRID_W), NEG_INF, F32)
    kh = min(NA_ROWS, win)
    tabs = []
    for cfg in cfgs:
        per_row = []
        for delta, off in cfg:
            blocks = [col_t[:, kk - delta + NA_ROWS - 1] if off <= kk < off + kh else neg for kk in range(win)]
            per_row.append(jnp.concatenate(blocks, axis=-1))
        tabs.append(jnp.concatenate(per_row, axis=1))
    return jnp.stack(tabs)


def kernel(x_prompt, x_sample, cache_mla_ckv, cache_mla_krope, cache_diff_k, cache_diff_v, cache_na_k, cache_na_v, state_ret, c, c_ctx, w_ada, b_ada, g_attn, g_ffn, w_in, mla_kv_norm, w_mla_ukv, diff_lambda, diff_norm, na_rel_bias, ret_decay, ret_norm, w_branch, w_o, w_up, conv_w, conv_b, w_down, g_final):
    nbc, seq, d = x_prompt.shape
    nbl, t, _ = x_sample.shape
    past = cache_mla_ckv.shape[2]
    depth = w_in.shape[0]
    mc, ml = nbc * seq, nbl * t

    xc = x_prompt.reshape(mc, d)
    xl = x_sample.reshape(ml, d)

    mod_rows = 16
    cond = jnp.concatenate([c_ctx[None], c, jnp.zeros((mod_rows - 1 - nbl, d), F32)], axis=0)
    mods = _ada_call(cond, w_ada, b_ada).reshape(depth, mod_rows, 6, d)

    cos32q, sin32q = _rope_tables(t, MLA_ROPE, 8)
    cos32k, sin32k = _rope_tables(t, MLA_ROPE, 4, lead_zero=LANE)
    cos64, sin64 = _rope_tables(t, DIFF_QK, 8)

    tm_c = min(1024, mc)
    tm_l = min(1024, t)
    states = []
    for l in range(depth):
        lam_init = 0.8 - 0.6 * math.exp(-0.3 * l)
        w_in_r = _reorder_w_in(w_in[l])
        w_ukv_e = _reorder_w_ukv(w_mla_ukv[l])
        wb = w_branch[l].astype(BF16)
        wo = w_o[l].astype(BF16)
        wup = w_up[l].astype(BF16)
        wdn = w_down[l].astype(BF16)
        kvn = mla_kv_norm[l].reshape(1, MLA_KV_RANK)
        dec = jnp.broadcast_to(ret_decay[l].reshape(2 * RET_HEADS, 1), (2 * RET_HEADS, LANE))
        gn = ret_norm[l].reshape(1, 512)
        dnorm = diff_norm[l].reshape(1, 512)
        g_out = g_final.reshape(1, d) if l == depth - 1 else None
        g_a = g_attn[l].reshape(1, d)
        g_f = g_ffn[l].reshape(1, d)

        def mod(i):
            return mods[l, 0:1, i][:, None, :], mods[l, 1:1 + nbl, i][:, None, :]

        sh1, sc1, ga1, sh2, sc2, ga2 = [mod(i) for i in range(6)]

        pc, pc32 = _mm_call(xc, w_in_r, out_dtype=BF16, tm=tm_c, tn=1024, rows_per_mod=mc,
                            norm=(g_a, sh1[0], sc1[0]), f32_cols=STATE_COLS)
        ya, yb, yc, yd, ckv_state, sret = _ctx_mix_call(pc, pc32, w_ukv_e, kvn, diff_lambda[l], dnorm, dec, gn,
                                                        nb=nbc, seq=seq, lam_init=lam_init)
        states.append((
            ckv_state.reshape(nbc, seq, MLA_KV_RANK),
            pc32[:, C_KR + LANE:C_KR + LANE + MLA_ROPE].reshape(nbc, seq, MLA_ROPE),
            pc32[:, C_DK:C_DK + 512].reshape(nbc, seq, DIFF_HEADS, 2, DIFF_QK),
            pc32[:, C_DV:C_DV + 512].reshape(nbc, seq, DIFF_HEADS, DIFF_V),
            pc32[:, C_NK:C_NK + 512].reshape(nbc, seq, NA_HEADS, NA_DIM),
            pc32[:, C_NV:C_NV + 512].reshape(nbc, seq, NA_HEADS, NA_DIM),
            sret,
        ))
        xc = _merge_call((ya, yb, yc, yd), pc, wb, wo, xc, ga1[0], rows_per_mod=mc)
        act = _ffn_up_call(xc, g_f, sh2[0], sc2[0], wup, conv_w[l], conv_b[l], seq=seq, rows_per_mod=mc, tm=tm_c)
        xc = _mm_call(act, wdn, out_dtype=F32, tm=256, tn=d, rows_per_mod=mc, resid=(xc, ga2[0]), out_norm=g_out)

        plat = _mm_call(xl, w_in_r, out_dtype=BF16, tm=tm_l, tn=2048, rows_per_mod=t, norm=(g_a, sh1[1], sc1[1]))
        kl_mla, vl_mla, _ = _mla_kv_call(plat, C_CKV, plat, C_KR, w_ukv_e, g=kvn, rope=(cos32k, sin32k), seq=t)
        kr_c = cache_mla_krope[:, l].reshape(nbl * past, MLA_ROPE)
        krblk_c = jnp.concatenate([jnp.zeros((nbl * past, LANE), F32)] + [kr_c] * 4, axis=1)
        kp_mla, vp_mla = _mla_kv_call(cache_mla_ckv[:, l].reshape(nbl * past, MLA_KV_RANK), 0, krblk_c, 0, w_ukv_e)
        dk_l = _rope_call(plat, C_DK, 512, cos64, sin64, DIFF_QK, t)
        tq = min(512, t)
        ya = _mha_call(plat, C_QN, plat, C_QR, [(kl_mla, 0, vl_mla, 0, t), (kp_mla, 0, vp_mla, 0, past)],
                       nb=nbl, tq=tq, t=t, scale=(MLA_NOPE + MLA_ROPE) ** -0.5, rope=(cos32q, sin32q))
        dk_c = cache_diff_k[:, l].reshape(nbl * past, 512)
        dv_c = cache_diff_v[:, l].reshape(nbl * past, 512)
        yb = _diff_call(plat, C_DQ, [(dk_l, 0, plat, C_DV, t), (dk_c, 0, dv_c, 0, past)], diff_lambda[l], dnorm,
                        nb=nbl, tq=tq, t=t, lam_init=lam_init, rope=(cos64, sin64))
        yc = _na_call(plat, cache_na_k[:, l].reshape(nbl * past, 512), cache_na_v[:, l].reshape(nbl * past, 512),
                      na_rel_bias[l], nb=nbl, t=t, past=past)
        s0 = state_ret[:, l].astype(F32).reshape(nbl, 2, 2, LANE, LANE)
        yd = _ret_call(plat, dec, gn, s0, nb=nbl, tq=tq, t=t)
        xl = _merge_call((ya, yb, yc, yd), plat, wb, wo, xl, ga1[1], rows_per_mod=t)
        act = _ffn_up_call(xl, g_f, sh2[1], sc2[1], wup, conv_w[l], conv_b[l], seq=t, rows_per_mod=t, tm=tm_l)
        xl = _mm_call(act, wdn, out_dtype=F32, tm=256, tn=d, rows_per_mod=t, resid=(xl, ga2[1]), out_norm=g_out)

    y_prompt = xc.reshape(nbc, seq, d)
    y_sample = xl.reshape(nbl, t, d)
    outs = [jnp.stack([st[i] for st in states], axis=1) for i in range(7)]
    return (y_prompt, y_sample, *outs)
```

```python
import functools
import math

import jax
import jax.numpy as jnp
import numpy as np
from jax import lax
from jax.experimental import pallas as pl
from jax.experimental.pallas import tpu as pltpu

F32 = jnp.float32
BF16 = jnp.bfloat16

D_MODEL = 2048
GRID_W = 64
ROPE_BASE = 10000.0
EPS = 1e-6
NEG_INF = -1e30
N_BRANCH = 4
BRANCH_W = 512
MLA_HEADS = 8
MLA_NOPE = 64
MLA_ROPE = 32
MLA_V = 64
MLA_KV_RANK = 256
DIFF_QK = 64
DIFF_V = 128
DIFF_HEADS = 4
NA_DIM = 64
NA_HEADS = 8
NA_ROWS = 8
NA_COLS = 16
RET_V = 128
RET_HEADS = 4
RET_QK = 64
D_FF = 5632
CONV_W = 3

LANE = 128
HALF = 64

C_CKV = 0
C_KR = 256
C_DK = 512
C_DV = 1024
C_NK = 1536
C_NV = 2048
C_RQ = 2560
C_RK = 2816
STATE_COLS = 3072
C_QN = 3072
C_QR = 3584
C_DQ = 4096
C_NQ = 4608
C_RV = 5120
C_RG = 5632
C_GATES = 6144
P_WIDTH = C_GATES + N_BRANCH * D_MODEL

VMEM_LIMIT = 56 * 1024 * 1024


def _cparams(sem):
    return pltpu.CompilerParams(dimension_semantics=sem, vmem_limit_bytes=VMEM_LIMIT)


def _dot(a, b):
    return jnp.dot(a, b, preferred_element_type=F32)


def _dot_t(a, b):
    return lax.dot_general(a, b, (((1,), (1,)), ((), ())), preferred_element_type=F32)


def _sigmoid(x):
    return 0.5 * jnp.tanh(0.5 * x) + 0.5


def _silu(x):
    h = 0.5 * x
    return h * (jnp.tanh(h) + 1.0)


def _lane_iota(shape):
    return lax.broadcasted_iota(jnp.int32, shape, len(shape) - 1)


def _half_mask(x, e):
    lane = _lane_iota(x.shape)
    keep = (lane < HALF) if e == 0 else (lane >= HALF)
    return jnp.where(keep, x, 0.0)


def _rope(x, cos, sin, dr):
    w = x.shape[-1]
    nf = dr // 4
    lane = _lane_iota(x.shape)
    first = (lane % (2 * nf)) < nf
    rot = jnp.where(first, pltpu.roll(x, w - nf, 1), pltpu.roll(x, nf, 1))
    return x * cos + rot * sin


ADA_TK = 256


def _ada_kernel(c_ref, w_ref, b_ref, o_ref):
    k = pl.program_id(1)

    @pl.when(k == 0)
    def _():
        o_ref[...] = jnp.broadcast_to(b_ref[...], o_ref.shape)

    a = _silu(c_ref[...]).astype(BF16)
    o_ref[...] += _dot(a, w_ref[...].astype(BF16))


def _ada_call(cond, w_ada, b_ada):
    depth, d, n = w_ada.shape
    rows = cond.shape[0]
    assert d % ADA_TK == 0
    return pl.pallas_call(
        _ada_kernel,
        grid=(depth, d // ADA_TK),
        in_specs=[
            pl.BlockSpec((rows, ADA_TK), lambda l, k: (0, k)),
            pl.BlockSpec((None, ADA_TK, n), lambda l, k: (l, k, 0)),
            pl.BlockSpec((None, 1, n), lambda l, k: (l, 0, 0)),
        ],
        out_specs=pl.BlockSpec((None, rows, n), lambda l, k: (l, 0, 0)),
        out_shape=jax.ShapeDtypeStruct((depth, rows, n), F32),
        compiler_params=_cparams(("parallel", "arbitrary")),
    )(cond, w_ada, b_ada.reshape(depth, 1, n))


NORM_ROWS = 128


def _mm_kernel(*refs, norm, resid, out_norm, f32_tiles):
    refs = list(refs)
    a_ref = refs.pop(0)
    if norm:
        g_ref, sh_ref, sc_ref = refs.pop(0), refs.pop(0), refs.pop(0)
    w_ref = refs.pop(0)
    if resid:
        r_ref, ga_ref = refs.pop(0), refs.pop(0)
    if out_norm:
        go_ref = refs.pop(0)
    o_ref = refs.pop(0)
    if f32_tiles:
        side_ref = refs.pop(0)

    if norm:
        h_ref = refs.pop(0)

        @pl.when(pl.program_id(1) == 0)
        def _():
            g = g_ref[...]
            mul = 1.0 + sc_ref[...]
            add = sh_ref[...]

            def body(c, carry):
                r0 = pl.multiple_of(c * NORM_ROWS, NORM_ROWS)
                x = a_ref[pl.ds(r0, NORM_ROWS), :]
                y = x * lax.rsqrt(jnp.mean(x * x, axis=-1, keepdims=True) + EPS) * g
                h_ref[pl.ds(r0, NORM_ROWS), :] = (y * mul + add).astype(BF16)
                return carry

            lax.fori_loop(0, a_ref.shape[0] // NORM_ROWS, body, 0)

        a = h_ref[...]
    else:
        a = a_ref[...]
    acc = _dot(a, w_ref[...])
    if resid:
        acc = r_ref[...] + ga_ref[...] * acc
    if out_norm:
        acc = acc * lax.rsqrt(jnp.mean(acc * acc, axis=-1, keepdims=True) + EPS) * go_ref[...]
    o_ref[...] = acc.astype(o_ref.dtype)
    if f32_tiles:
        @pl.when(pl.program_id(1) < f32_tiles)
        def _():
            side_ref[...] = acc


def _mm_call(a, w, *, out_dtype, tm, tn, rows_per_mod, norm=None, resid=None, out_norm=None, f32_cols=0):
    m, k = a.shape
    n = w.shape[1]
    assert m % tm == 0 and n % tn == 0 and rows_per_mod % tm == 0
    assert out_norm is None or tn == n
    tiles_per_mod = rows_per_mod // tm
    in_specs = [pl.BlockSpec((tm, k), lambda i, j: (i, 0))]
    args = [a]
    if norm is not None:
        g, sh, sc = norm
        in_specs += [
            pl.BlockSpec((1, k), lambda i, j: (0, 0)),
            pl.BlockSpec((None, 1, k), lambda i, j: (i // tiles_per_mod, 0, 0)),
            pl.BlockSpec((None, 1, k), lambda i, j: (i // tiles_per_mod, 0, 0)),
        ]
        args += [g, sh, sc]
    if tn == n:
        in_specs.append(pl.BlockSpec((k, tn), lambda i, j: (0, 0), pipeline_mode=pl.Buffered(1)))
    else:
        in_specs.append(pl.BlockSpec((k, tn), lambda i, j: (0, j)))
    args.append(w)
    if resid is not None:
        x, ga = resid
        in_specs += [
            pl.BlockSpec((tm, tn), lambda i, j: (i, j)),
            pl.BlockSpec((None, 1, tn), lambda i, j: (i // tiles_per_mod, 0, j)),
        ]
        args += [x, ga]
    if out_norm is not None:
        in_specs.append(pl.BlockSpec((1, n), lambda i, j: (0, 0)))
        args.append(out_norm)
    scratch = [pltpu.VMEM((tm, k), BF16)] if norm is not None else []
    assert f32_cols % tn == 0
    f32_tiles = f32_cols // tn
    out_specs = pl.BlockSpec((tm, tn), lambda i, j: (i, j))
    out_shape = jax.ShapeDtypeStruct((m, n), out_dtype)
    if f32_tiles:
        out_specs = [out_specs, pl.BlockSpec((tm, tn), lambda i, j: (i, jnp.minimum(j, f32_tiles - 1)))]
        out_shape = [out_shape, jax.ShapeDtypeStruct((m, f32_cols), F32)]
    return pl.pallas_call(
        functools.partial(_mm_kernel, norm=norm is not None, resid=resid is not None, out_norm=out_norm is not None,
                          f32_tiles=f32_tiles),
        grid=(m // tm, n // tn),
        in_specs=in_specs,
        out_specs=out_specs,
        out_shape=out_shape,
        scratch_shapes=scratch,
        compiler_params=_cparams(("parallel", "arbitrary")),
    )(*args)


def _rope_kernel(x_ref, c_ref, s_ref, o_ref, *, dr):
    o_ref[...] = _rope(x_ref[...].astype(F32), c_ref[...], s_ref[...], dr).astype(o_ref.dtype)


def _rope_call(p, col, width, cos, sin, dr, seq):
    m = p.shape[0]
    tm = 256
    tpb = seq // tm
    cb = col // width
    assert col % width == 0
    return pl.pallas_call(
        functools.partial(_rope_kernel, dr=dr),
        grid=(tpb, m // seq),
        in_specs=[
            pl.BlockSpec((tm, width), lambda i, b: (b * tpb + i, cb)),
            pl.BlockSpec((tm, width), lambda i, b: (i, 0)),
            pl.BlockSpec((tm, width), lambda i, b: (i, 0)),
        ],
        out_specs=pl.BlockSpec((tm, width), lambda i, b: (b * tpb + i, 0)),
        out_shape=jax.ShapeDtypeStruct((m, width), BF16),
        compiler_params=_cparams(("parallel", "parallel")),
    )(p, cos, sin)


MLA_K_W = 4 * 256
MLA_V_W = 4 * 128


def _mla_kv_kernel(*refs, norm, rope):
    refs = list(refs)
    c_ref, kr_ref = refs.pop(0), refs.pop(0)
    if rope:
        cos_ref, sin_ref = refs.pop(0), refs.pop(0)
    if norm:
        g_ref = refs.pop(0)
    w_ref = refs.pop(0)
    k_ref, v_ref = refs.pop(0), refs.pop(0)
    c = c_ref[...].astype(F32)
    if norm:
        ckv_ref = refs.pop(0)
        c = c * lax.rsqrt(jnp.mean(c * c, axis=-1, keepdims=True) + EPS) * g_ref[...]
        ckv_ref[...] = c
    kv = _dot(c.astype(BF16), w_ref[...])
    kr = kr_ref[...].astype(F32)
    if rope:
        kr = _rope(kr, cos_ref[...], sin_ref[...], MLA_ROPE)
    k_ref[...] = (kv[:, :MLA_K_W] + jnp.concatenate([kr] * 4, axis=1)).astype(BF16)
    v_ref[...] = kv[:, MLA_K_W:].astype(BF16)


def _mla_kv_call(ckv_src, ckv_col, kr_src, kr_col, w_e, *, g=None, rope=None, seq=None):
    m = ckv_src.shape[0]
    tm = 256
    assert m % tm == 0 and ckv_col % 256 == 0 and kr_col % 256 == 0
    in_specs = [
        pl.BlockSpec((tm, 256), lambda i: (i, ckv_col // 256)),
        pl.BlockSpec((tm, 256), lambda i: (i, kr_col // 256)),
    ]
    args = [ckv_src, kr_src]
    if rope is not None:
        tpb = seq // tm
        in_specs += [pl.BlockSpec((tm, 256), lambda i: (i % tpb, 0))] * 2
        args += list(rope)
    if g is not None:
        in_specs.append(pl.BlockSpec((1, 256), lambda i: (0, 0)))
        args.append(g)
    in_specs.append(pl.BlockSpec(w_e.shape, lambda i: (0, 0)))
    args.append(w_e)
    out_specs = [pl.BlockSpec((tm, MLA_K_W), lambda i: (i, 0)), pl.BlockSpec((tm, MLA_V_W), lambda i: (i, 0))]
    out_shape = [jax.ShapeDtypeStruct((m, MLA_K_W), BF16), jax.ShapeDtypeStruct((m, MLA_V_W), BF16)]
    if g is not None:
        out_specs.append(pl.BlockSpec((tm, 256), lambda i: (i, 0)))
        out_shape.append(jax.ShapeDtypeStruct((m, 256), F32))
    return pl.pallas_call(
        functools.partial(_mla_kv_kernel, norm=g is not None, rope=rope is not None),
        grid=(m // tm,),
        in_specs=in_specs,
        out_specs=out_specs,
        out_shape=out_shape,
        compiler_params=_cparams(("parallel",)),
    )(*args)


LOG2E = 1.4426950408889634


def _exp_parts(scores, scale=1.0):
    m = scores[0].max(axis=-1, keepdims=True)
    for s in scores[1:]:
        m = jnp.maximum(m, s.max(axis=-1, keepdims=True))
    return [jnp.exp2((s - m) * (scale * LOG2E)) for s in scores]


def _softmax_parts(scores, scale=1.0):
    ps = _exp_parts(scores, scale)
    l = ps[0].sum(axis=-1, keepdims=True)
    for p in ps[1:]:
        l = l + p.sum(axis=-1, keepdims=True)
    return ps, 1.0 / l


def _pv_normalised(ps, vs):
    acc = None
    for p, v in zip(ps, vs):
        t = _dot(p.astype(BF16), jnp.concatenate([v, jnp.ones_like(v)], axis=1))
        acc = t if acc is None else acc + t
    return acc[:, :LANE] / acc[:, LANE:]


def _mha_kernel(*refs, mla, nseg, scale, rope=False):
    refs = list(refs)
    qn_ref = refs.pop(0)
    qr_ref = refs.pop(0) if mla else None
    if rope:
        cos_ref, sin_ref = refs.pop(0), refs.pop(0)
    kv_refs = [(refs.pop(0), refs.pop(0)) for _ in range(nseg)]
    o_ref = refs.pop(0)
    kw = 256 if mla else LANE
    lane = _lane_iota((qn_ref.shape[0], LANE))
    nheads = 8
    if mla:
        qr_all = qr_ref[...].astype(F32)
        if rope:
            qr_all = _rope(qr_all, cos_ref[...], sin_ref[...], MLA_ROPE)

    def head_scores(h):
        j, e = h // 2, h % 2
        q = _half_mask(qn_ref[:, j * LANE:(j + 1) * LANE].astype(F32), e)
        if mla:
            qr = qr_all[:, (h // 4) * LANE:(h // 4 + 1) * LANE]
            qr = jnp.where(lane // MLA_ROPE == h % 4, qr, 0.0)
            q = jnp.concatenate([q, qr], axis=1)
        q = q.astype(BF16)
        return [_dot_t(q, k_ref[:, j * kw:(j + 1) * kw].astype(BF16)) for k_ref, _ in kv_refs]

    outs = []

    def head_values(h, ps):
        j = h // 2
        outs.append(_pv_normalised(ps, [v_ref[:, j * LANE:(j + 1) * LANE].astype(BF16) for _, v_ref in kv_refs]))
        if h % 2 == 1:
            o_ref[:, j * LANE:(j + 1) * LANE] = jnp.where(lane < HALF, outs[0], outs[1]).astype(o_ref.dtype)
            outs.clear()

    scores = head_scores(0)
    pending = None
    for h in range(nheads):
        nxt = head_scores(h + 1) if h + 1 < nheads else None
        ps = _exp_parts(scores, scale)
        if pending is not None:
            head_values(*pending)
        pending = (h, ps)
        scores = nxt
    head_values(*pending)


def _mha_call(qn, qn_col, qr, qr_col, segs, *, nb, tq, t, scale, rope=None):
    mla = qr is not None
    kw = MLA_K_W if mla else 512
    nq = t // tq
    in_specs = [pl.BlockSpec((tq, 512), lambda b, i: (b * nq + i, qn_col // 512))]
    args = [qn]
    if mla:
        in_specs.append(pl.BlockSpec((tq, 256), lambda b, i: (b * nq + i, qr_col // 256)))
        args.append(qr)
    if rope is not None:
        in_specs += [pl.BlockSpec((tq, 256), lambda b, i: (i, 0))] * 2
        args += list(rope)
    for k_arr, k_col, v_arr, v_col, s in segs:
        assert k_col % kw == 0 and v_col % 512 == 0
        in_specs.append(pl.BlockSpec((s, kw), functools.partial(lambda b, i, c: (b, c), c=k_col // kw)))
        in_specs.append(pl.BlockSpec((s, 512), functools.partial(lambda b, i, c: (b, c), c=v_col // 512)))
        args += [k_arr, v_arr]
    return pl.pallas_call(
        functools.partial(_mha_kernel, mla=mla, nseg=len(segs), scale=scale, rope=rope is not None),
        grid=(nb, nq),
        in_specs=in_specs,
        out_specs=pl.BlockSpec((tq, 512), lambda b, i: (b * nq + i, 0)),
        out_shape=jax.ShapeDtypeStruct((nb * t, 512), BF16),
        compiler_params=_cparams(("parallel", "parallel")),
    )(*args)


def _diff_kernel(*refs, nseg, lam_init, rope=False):
    refs = list(refs)
    q_ref = refs.pop(0)
    if rope:
        cos_ref, sin_ref = refs.pop(0), refs.pop(0)
    kv_refs = [(refs.pop(0), refs.pop(0)) for _ in range(nseg)]
    lp_ref, g_ref, o_ref = refs
    lp = lp_ref[...]
    lam = (jnp.exp(jnp.sum(lp[0:1] * lp[1:2], axis=-1, keepdims=True))
           - jnp.exp(jnp.sum(lp[2:3] * lp[3:4], axis=-1, keepdims=True)) + lam_init)
    scale = DIFF_QK ** -0.5
    q_all = q_ref[...].astype(F32)
    if rope:
        q_all = _rope(q_all, cos_ref[...], sin_ref[...], DIFF_QK)

    def head_scores(h):
        q_pair = q_all[:, h * LANE:(h + 1) * LANE]
        return [[_dot_t(_half_mask(q_pair, c).astype(BF16), k_ref[:, h * LANE:(h + 1) * LANE].astype(BF16))
                 for k_ref, _ in kv_refs] for c in range(2)]

    scores = head_scores(0)
    for h in range(DIFF_HEADS):
        nxt = head_scores(h + 1) if h + 1 < DIFF_HEADS else None
        (ps0, inv0), (ps1, inv1) = [_softmax_parts(sc, scale) for sc in scores]
        o = None
        for s in range(nseg):
            a = (ps0[s] * inv0 - lam * (ps1[s] * inv1)).astype(BF16)
            t = _dot(a, kv_refs[s][1][:, h * LANE:(h + 1) * LANE].astype(BF16))
            o = t if o is None else o + t
        o = o * lax.rsqrt(jnp.mean(o * o, axis=-1, keepdims=True) + EPS) * g_ref[:, h * LANE:(h + 1) * LANE]
        o_ref[:, h * LANE:(h + 1) * LANE] = (o * (1.0 - lam_init)).astype(o_ref.dtype)
        scores = nxt


def _diff_call(q, q_col, segs, lp, g, *, nb, tq, t, lam_init, rope=None):
    nq = t // tq
    in_specs = [pl.BlockSpec((tq, 512), lambda b, i: (b * nq + i, q_col // 512))]
    args = [q]
    if rope is not None:
        in_specs += [pl.BlockSpec((tq, 512), lambda b, i: (i, 0))] * 2
        args += list(rope)
    for k_arr, k_col, v_arr, v_col, s in segs:
        in_specs.append(pl.BlockSpec((s, 512), functools.partial(lambda b, i, c: (b, c), c=k_col // 512)))
        in_specs.append(pl.BlockSpec((s, 512), functools.partial(lambda b, i, c: (b, c), c=v_col // 512)))
        args += [k_arr, v_arr]
    in_specs += [pl.BlockSpec((4, DIFF_QK), lambda b, i: (0, 0)), pl.BlockSpec((1, 512), lambda b, i: (0, 0))]
    args += [lp, g]
    return pl.pallas_call(
        functools.partial(_diff_kernel, nseg=len(segs), lam_init=lam_init, rope=rope is not None),
        grid=(nb, nq),
        in_specs=in_specs,
        out_specs=pl.BlockSpec((tq, 512), lambda b, i: (b * nq + i, 0)),
        out_shape=jax.ShapeDtypeStruct((nb * t, 512), BF16),
        compiler_params=_cparams(("parallel", "parallel")),
    )(*args)


NA_QROWS = 4


def _na_plan(rows):
    win = min(rows, NA_QROWS + NA_ROWS)
    kh = min(NA_ROWS, rows)
    cfgs, cfg_idx, starts = [], [], []
    for blk in range(rows // NA_QROWS):
        r0 = blk * NA_QROWS
        s_blk = int(np.clip(r0 - kh // 2, 0, rows - win))
        cfg = tuple((r0 + i - s_blk, int(np.clip(r0 + i - kh // 2, 0, rows - kh)) - s_blk) for i in range(NA_QROWS))
        if cfg not in cfgs:
            cfgs.append(cfg)
        cfg_idx.append(cfgs.index(cfg))
        starts.append(s_blk)
    return win, cfgs, np.array([cfg_idx, starts], np.int32)


def _na_kernel(meta_ref, q_ref, k_ref, v_ref, kc_ref, vc_ref, b_ref, o_ref, *, win):
    cfg = meta_ref[0, pl.program_id(1)]
    start = pl.multiple_of(meta_ref[1, pl.program_id(1)] * GRID_W, GRID_W)
    n_loc = win * GRID_W
    lane = _lane_iota((q_ref.shape[0], LANE))

    def head_scores(h):
        j, e = h // 2, h % 2
        q = _half_mask(q_ref[:, j * LANE:(j + 1) * LANE].astype(F32) * (NA_DIM ** -0.5), e).astype(BF16)
        s_loc = _dot_t(q, k_ref[pl.ds(start, n_loc), j * LANE:(j + 1) * LANE].astype(BF16))
        s_ctx = _dot_t(q, kc_ref[:, j * LANE:(j + 1) * LANE].astype(BF16))
        return [s_loc + b_ref[cfg, h], s_ctx]

    scores = head_scores(0)
    outs = []
    for h in range(NA_HEADS):
        nxt = head_scores(h + 1) if h + 1 < NA_HEADS else None
        j = h // 2
        v_loc = v_ref[pl.ds(start, n_loc), j * LANE:(j + 1) * LANE].astype(BF16)
        v_ctx = vc_ref[:, j * LANE:(j + 1) * LANE].astype(BF16)
        outs.append(_pv_normalised(_exp_parts(scores), [v_loc, v_ctx]))
        if h % 2 == 1:
            o_ref[:, j * LANE:(j + 1) * LANE] = jnp.where(lane < HALF, outs[0], outs[1]).astype(o_ref.dtype)
            outs = []
        scores = nxt


def _na_call(p, kc, vc, rel_bias, *, nb, t, past):
    rows = t // GRID_W
    assert rows % NA_QROWS == 0 and rows >= NA_ROWS
    win, cfgs, meta = _na_plan(rows)
    bias_tab = _na_bias_table(rel_bias, cfgs, win)
    nblk = rows // NA_QROWS
    tq = NA_QROWS * GRID_W
    n_loc = win * GRID_W
    grid_spec = pltpu.PrefetchScalarGridSpec(
        num_scalar_prefetch=1,
        grid=(nb, nblk),
        in_specs=[
            pl.BlockSpec((tq, 512), lambda b, r, meta: (b * nblk + r, C_NQ // 512)),
            pl.BlockSpec((t, 512), lambda b, r, meta: (b, C_NK // 512)),
            pl.BlockSpec((t, 512), lambda b, r, meta: (b, C_NV // 512)),
            pl.BlockSpec((past, 512), lambda b, r, meta: (b, 0)),
            pl.BlockSpec((past, 512), lambda b, r, meta: (b, 0)),
            pl.BlockSpec((len(cfgs), NA_HEADS, tq, n_loc), lambda b, r, meta: (0, 0, 0, 0), pipeline_mode=pl.Buffered(1)),
        ],
        out_specs=pl.BlockSpec((tq, 512), lambda b, r, meta: (b * nblk + r, 0)),
    )
    return pl.pallas_call(
        functools.partial(_na_kernel, win=win),
        grid_spec=grid_spec,
        out_shape=jax.ShapeDtypeStruct((nb * t, 512), BF16),
        compiler_params=_cparams(("parallel", "arbitrary")),
    )(jnp.asarray(meta), p, p, p, kc, vc, bias_tab)


def _log_sigmoid(x):
    return jnp.minimum(x, 0.0) - jnp.log(1.0 + jnp.exp(-jnp.abs(x)))


def _ret_kernel(*refs, has_state, t, whole_seq=False):
    refs = list(refs)
    q_ref, k_ref, v_ref, rg_ref, dec_ref, gn_ref = [refs.pop(0) for _ in range(6)]
    s0_ref = refs.pop(0) if has_state else None
    o_ref = refs.pop(0)
    tq = q_ref.shape[0]
    q0 = 0 if whole_seq else pl.program_id(1) * tq
    lg = _log_sigmoid(dec_ref[...])
    ti = (q0 + lax.broadcasted_iota(jnp.int32, (tq, t), 0)).astype(F32)
    ui = lax.broadcasted_iota(jnp.int32, (tq, t), 1).astype(F32)
    diff = ti - ui
    tcol = (q0 + lax.broadcasted_iota(jnp.int32, (tq, 1), 0)).astype(F32)

    def head_scores(h):
        j, e = h // 2, h % 2
        qm = _half_mask(q_ref[:, j * LANE:(j + 1) * LANE].astype(F32), e)
        return qm, _dot_t((qm * (RET_QK ** -0.5)).astype(BF16), k_ref[:, j * LANE:(j + 1) * LANE].astype(BF16))

    cur = head_scores(0)
    for h in range(RET_HEADS):
        nxt = head_scores(h + 1) if h + 1 < RET_HEADS else None
        j = h // 2
        qm, a = cur
        cur = nxt
        lgf = lg[h:h + 1, 0:1]
        lgb = lg[RET_HEADS + h:RET_HEADS + h + 1, 0:1]
        q = qm.astype(BF16)
        dmat = jnp.exp2(diff * jnp.where(diff >= 0, lgf * LOG2E, -lgb * LOG2E))
        y = _dot((a * dmat).astype(BF16), v_ref[:, h * LANE:(h + 1) * LANE].astype(BF16))
        if has_state:
            cf = _dot(q, s0_ref[0, j].astype(BF16))
            cb = _dot(q, s0_ref[1, j].astype(BF16))
            y = y + cf * jnp.exp((tcol + 1.0) * lgf) + cb * jnp.exp((float(t) - tcol) * lgb)
        mu = jnp.mean(y, axis=-1, keepdims=True)
        yc = y - mu
        var = jnp.mean(yc * yc, axis=-1, keepdims=True)
        yn = yc * lax.rsqrt(var + EPS) * gn_ref[:, h * LANE:(h + 1) * LANE]
        o_ref[:, h * LANE:(h + 1) * LANE] = (_silu(rg_ref[:, h * LANE:(h + 1) * LANE].astype(F32)) * yn).astype(o_ref.dtype)


def _ret_call(p, dec, gn, s0, *, nb, tq, t):
    nq = t // tq
    in_specs = [
        pl.BlockSpec((tq, 256), lambda b, i: (b * nq + i, C_RQ // 256)),
        pl.BlockSpec((t, 256), lambda b, i: (b, C_RK // 256)),
        pl.BlockSpec((t, 512), lambda b, i: (b, C_RV // 512)),
        pl.BlockSpec((tq, 512), lambda b, i: (b * nq + i, C_RG // 512)),
        pl.BlockSpec((8, LANE), lambda b, i: (0, 0)),
        pl.BlockSpec((1, 512), lambda b, i: (0, 0)),
    ]
    args = [p, p, p, p, dec, gn]
    if s0 is not None:
        in_specs.append(pl.BlockSpec((None, 2, 2, LANE, LANE), lambda b, i: (b, 0, 0, 0, 0)))
        args.append(s0)
    return pl.pallas_call(
        functools.partial(_ret_kernel, has_state=s0 is not None, t=t),
        grid=(nb, nq),
        in_specs=in_specs,
        out_specs=pl.BlockSpec((tq, 512), lambda b, i: (b * nq + i, 0)),
        out_shape=jax.ShapeDtypeStruct((nb * t, 512), BF16),
        compiler_params=_cparams(("parallel", "parallel")),
    )(*args)


def _ret_state_kernel(k_ref, v_ref, dec_ref, o_ref, *, t):
    lg = _log_sigmoid(dec_ref[...])
    tcol = lax.broadcasted_iota(jnp.int32, (t, 1), 0).astype(F32)
    lane = _lane_iota((t, LANE))
    for j in range(2):
        k_pair = k_ref[:, j * LANE:(j + 1) * LANE].astype(F32) * (RET_QK ** -0.5)
        for d in range(2):
            expo = (float(t) - 1.0 - tcol) if d == 0 else tcol
            w0 = jnp.exp(expo * lg[d * RET_HEADS + 2 * j:d * RET_HEADS + 2 * j + 1, 0:1])
            w1 = jnp.exp(expo * lg[d * RET_HEADS + 2 * j + 1:d * RET_HEADS + 2 * j + 2, 0:1])
            kd = (k_pair * jnp.where(lane < HALF, w0, w1)).T.astype(BF16)
            for e in range(2):
                h = 2 * j + e
                sfull = _dot(kd, v_ref[:, h * LANE:(h + 1) * LANE].astype(BF16))
                o_ref[d, h] = sfull[e * HALF:(e + 1) * HALF, :]


def _ctx_mix_kernel(p_ref, c32_ref, w_ref, kvn_ref, lp_ref, dn_ref, dec_ref, gn_ref,
                    ya_ref, yb_ref, yc_ref, yd_ref, ckv_ref, sret_ref, k_scr, v_scr, *, seq, lam_init):
    def col(c, w):
        return p_ref.at[:, c:c + w]

    _mla_kv_kernel(c32_ref, col(C_KR, 256), kvn_ref, w_ref, k_scr, v_scr, ckv_ref, norm=True, rope=False)
    _mha_kernel(col(C_QN, 512), col(C_QR, 256), k_scr, v_scr, ya_ref, mla=True, nseg=1,
                scale=(MLA_NOPE + MLA_ROPE) ** -0.5)
    _diff_kernel(col(C_DQ, 512), col(C_DK, 512), col(C_DV, 512), lp_ref, dn_ref, yb_ref, nseg=1, lam_init=lam_init)
    _mha_kernel(col(C_NQ, 512), col(C_NK, 512), col(C_NV, 512), yc_ref, mla=False, nseg=1, scale=NA_DIM ** -0.5)
    _ret_kernel(col(C_RQ, 256), col(C_RK, 256), col(C_RV, 512), col(C_RG, 512), dec_ref, gn_ref, yd_ref,
                has_state=False, t=seq, whole_seq=True)
    _ret_state_kernel(col(C_RK, 256), col(C_RV, 512), dec_ref, sret_ref, t=seq)


def _ctx_mix_call(p, p32, w_ukv_e, kvn, lp, dnorm, dec, gn, *, nb, seq, lam_init):
    def const(shape):
        return pl.BlockSpec(shape, lambda b: (0,) * len(shape))

    def rows(width, dtype):
        return pl.BlockSpec((seq, width), lambda b: (b, 0)), jax.ShapeDtypeStruct((nb * seq, width), dtype)

    outs = [rows(512, BF16)] * 4 + [rows(MLA_KV_RANK, F32)]
    out_specs = [o[0] for o in outs] + [pl.BlockSpec((None, 2, RET_HEADS, RET_QK, RET_V), lambda b: (b, 0, 0, 0, 0))]
    out_shape = [o[1] for o in outs] + [jax.ShapeDtypeStruct((nb, 2, RET_HEADS, RET_QK, RET_V), F32)]
    return pl.pallas_call(
        functools.partial(_ctx_mix_kernel, seq=seq, lam_init=lam_init),
        grid=(nb,),
        in_specs=[pl.BlockSpec((seq, C_GATES), lambda b: (b, 0)),
                  pl.BlockSpec((seq, MLA_KV_RANK), lambda b: (b, C_CKV // MLA_KV_RANK)),
                  const(w_ukv_e.shape), const((1, MLA_KV_RANK)),
                  const((4, DIFF_QK)), const((1, 512)), const((8, LANE)), const((1, 512))],
        out_specs=out_specs,
        out_shape=out_shape,
        scratch_shapes=[pltpu.VMEM((seq, MLA_K_W), BF16), pltpu.VMEM((seq, MLA_V_W), BF16)],
        compiler_params=_cparams(("parallel",)),
    )(p, p32, w_ukv_e, kvn, lp, dnorm, dec, gn)


MERGE_CHUNK = 512


def _merge_kernel(ya_ref, yb_ref, yc_ref, yd_ref, g0_ref, g1_ref, g2_ref, g3_ref, wb_ref, wo_ref, x_ref, ga_ref,
                  o_ref, m_ref):
    ys = (ya_ref, yb_ref, yc_ref, yd_ref)
    gs = (g0_ref, g1_ref, g2_ref, g3_ref)
    for c0 in range(0, D_MODEL, MERGE_CHUNK):
        cols = slice(c0, c0 + MERGE_CHUNK)
        m = None
        for i in range(N_BRANCH):
            t = _sigmoid(gs[i][:, cols].astype(F32)) * _dot(ys[i][...], wb_ref[i, :, cols])
            m = t if m is None else m + t
        m_ref[:, cols] = m.astype(BF16)
    o_ref[...] = x_ref[...] + ga_ref[...] * _dot(m_ref[...], wo_ref[...])


def _merge_call(ys, p, w_branch, w_o, x, ga, *, rows_per_mod):
    m = p.shape[0]
    tm = 256
    assert m % tm == 0 and rows_per_mod % tm == 0
    tiles_per_mod = rows_per_mod // tm
    first_gate_block = C_GATES // D_MODEL
    assert C_GATES % D_MODEL == 0
    gate_specs = [
        pl.BlockSpec((tm, D_MODEL), functools.partial(lambda r, i: (r, first_gate_block + i), i=i))
        for i in range(N_BRANCH)
    ]
    return pl.pallas_call(
        _merge_kernel,
        grid=(m // tm,),
        in_specs=[pl.BlockSpec((tm, BRANCH_W), lambda r: (r, 0))] * N_BRANCH + gate_specs + [
            pl.BlockSpec((N_BRANCH, BRANCH_W, D_MODEL), lambda r: (0, 0, 0), pipeline_mode=pl.Buffered(1)),
            pl.BlockSpec((D_MODEL, D_MODEL), lambda r: (0, 0), pipeline_mode=pl.Buffered(1)),
            pl.BlockSpec((tm, D_MODEL), lambda r: (r, 0)),
            pl.BlockSpec((None, 1, D_MODEL), lambda r: (r // tiles_per_mod, 0, 0)),
        ],
        out_specs=pl.BlockSpec((tm, D_MODEL), lambda r: (r, 0)),
        out_shape=jax.ShapeDtypeStruct((m, D_MODEL), F32),
        scratch_shapes=[pltpu.VMEM((tm, D_MODEL), BF16)],
        compiler_params=_cparams(("parallel",)),
    )(*ys, p, p, p, p, w_branch, w_o, x, ga)


HALO = 16


UP_TN = 512
UP_SUB = 256
UP_ROW_CHUNKS = 6


def _ffn_up_kernel(x_ref, xp_ref, xn_ref, g_ref, sh_ref, sc_ref, wa_ref, wb_ref, cwa_ref, cwb_ref, cba_ref, cbb_ref,
                   o_ref, h_ref, *, seq):
    tm = x_ref.shape[0]
    rows = tm + 2 * HALO

    @pl.when(pl.program_id(1) == 0)
    def _():
        g = g_ref[...]
        mul = 1.0 + sc_ref[...]
        add = sh_ref[...]

        def norm(x):
            y = x * lax.rsqrt(jnp.mean(x * x, axis=-1, keepdims=True) + EPS) * g
            return (y * mul + add).astype(BF16)

        h_ref[0:HALO, :] = norm(xp_ref[...])
        h_ref[HALO + tm:rows, :] = norm(xn_ref[...])

        def body(c, carry):
            r0 = pl.multiple_of(c * NORM_ROWS, NORM_ROWS)
            h_ref[pl.ds(HALO + r0, NORM_ROWS), :] = norm(x_ref[pl.ds(r0, NORM_ROWS), :])
            return carry

        lax.fori_loop(0, tm // NORM_ROWS, body, 0)

    pos = (pl.program_id(0) * tm + lax.broadcasted_iota(jnp.int32, (tm, 1), 0)) % seq
    keep_prev = (pos != 0).astype(F32)
    keep_next = (pos != seq - 1).astype(F32)
    h = h_ref[...]

    def project(s):
        cols = slice(s * UP_SUB, (s + 1) * UP_SUB)
        nchunk = max(n for n in range(1, UP_ROW_CHUNKS + 1) if rows % (HALO * n) == 0)
        chunk = rows // nchunk
        parts = [(_dot(h[r0:r0 + chunk], wa_ref[:, cols]), _dot(h[r0:r0 + chunk], wb_ref[:, cols]))
                 for r0 in range(0, rows, chunk)]
        return jnp.concatenate([p[0] for p in parts], axis=0), jnp.concatenate([p[1] for p in parts], axis=0)

    def conv(u, w_ref, b_ref, cols):
        um1 = pltpu.roll(u, 1, 0)[HALO:HALO + tm] * keep_prev
        up1 = pltpu.roll(u, rows - 1, 0)[HALO:HALO + tm] * keep_next
        return um1 * w_ref[0:1, cols] + u[HALO:HALO + tm] * w_ref[1:2, cols] + up1 * w_ref[2:3, cols] + b_ref[:, cols]

    nsub = UP_TN // UP_SUB
    cur = project(0)
    for s in range(nsub):
        nxt = project(s + 1) if s + 1 < nsub else None
        cols = slice(s * UP_SUB, (s + 1) * UP_SUB)
        a = conv(cur[0], cwa_ref, cba_ref, cols)
        b = conv(cur[1], cwb_ref, cbb_ref, cols)
        o_ref[:, cols] = (_silu(a) * b).astype(o_ref.dtype)
        cur = nxt


def _ffn_up_call(x, g, sh, sc, w_up, conv_w, conv_b, *, seq, rows_per_mod, tm):
    m, k = x.shape
    assert m % tm == 0 and rows_per_mod % tm == 0 and D_FF % UP_TN == 0 and (seq % tm == 0 or tm % seq == 0)
    tiles_per_mod = rows_per_mod // tm
    nj = D_FF // UP_TN
    r = tm // HALO
    last = m // HALO - 1
    cb = conv_b.reshape(1, 2 * D_FF)
    return pl.pallas_call(
        functools.partial(_ffn_up_kernel, seq=seq),
        grid=(m // tm, nj),
        in_specs=[
            pl.BlockSpec((tm, k), lambda i, j: (i, 0)),
            pl.BlockSpec((HALO, k), lambda i, j: (jnp.maximum(i * r - 1, 0), 0)),
            pl.BlockSpec((HALO, k), lambda i, j: (jnp.minimum((i + 1) * r, last), 0)),
            pl.BlockSpec((1, k), lambda i, j: (0, 0)),
            pl.BlockSpec((None, 1, k), lambda i, j: (i // tiles_per_mod, 0, 0)),
            pl.BlockSpec((None, 1, k), lambda i, j: (i // tiles_per_mod, 0, 0)),
            pl.BlockSpec((k, UP_TN), lambda i, j: (0, j)),
            pl.BlockSpec((k, UP_TN), lambda i, j: (0, j + nj)),
            pl.BlockSpec((CONV_W, UP_TN), lambda i, j: (0, j)),
            pl.BlockSpec((CONV_W, UP_TN), lambda i, j: (0, j + nj)),
            pl.BlockSpec((1, UP_TN), lambda i, j: (0, j)),
            pl.BlockSpec((1, UP_TN), lambda i, j: (0, j + nj)),
        ],
        out_specs=pl.BlockSpec((tm, UP_TN), lambda i, j: (i, j)),
        out_shape=jax.ShapeDtypeStruct((m, D_FF), BF16),
        scratch_shapes=[pltpu.VMEM((tm + 2 * HALO, k), BF16)],
        compiler_params=_cparams(("parallel", "arbitrary")),
    )(x, x, x, g, sh, sc, w_up, w_up, conv_w, conv_w, cb, cb)


def _reorder_w_in(w):
    d = w.shape[0]
    mq = w[:, :768].reshape(d, MLA_HEADS, MLA_NOPE + MLA_ROPE)
    qn = mq[:, :, :MLA_NOPE].reshape(d, 512)
    qr = mq[:, :, MLA_NOPE:].reshape(d, 256)
    ckv = w[:, 768:1024]
    mkr = w[:, 1024:1056]
    dq, dk, dv, nq, nk, nv = [w[:, 1056 + 512 * i:1056 + 512 * (i + 1)] for i in range(6)]
    rq, rk = w[:, 4128:4384], w[:, 4384:4640]
    rv, rg = w[:, 4640:5152], w[:, 5152:5664]
    gates = w[:, 5664:]
    krblk = jnp.concatenate([jnp.zeros((d, LANE), w.dtype)] + [mkr] * 4, axis=1)
    pad = jnp.zeros((d, C_DQ - C_QR - 256), w.dtype)
    pieces = (ckv, krblk, dk, dv, nk, nv, rq, rk, qn, qr, pad, dq, nq, rv, rg, gates)
    out = jnp.concatenate([x.astype(BF16) for x in pieces], axis=1)
    assert out.shape[1] == P_WIDTH
    return out


def _reorder_w_ukv(w):
    r = w.shape[0]
    w3 = w.reshape(r, MLA_HEADS, MLA_NOPE + MLA_V)
    kn = w3[:, :, :MLA_NOPE].reshape(r, 4, LANE)
    vv = w3[:, :, MLA_NOPE:].reshape(r, 512)
    knz = jnp.concatenate([kn, jnp.zeros_like(kn)], axis=2).reshape(r, MLA_K_W)
    return jnp.concatenate([knz, vv], axis=1).astype(BF16)


def _rope_tables(t, dr, reps, lead_zero=0):
    nf = dr // 4
    pos = jnp.arange(t)
    pos = jnp.stack([pos // GRID_W, pos % GRID_W], axis=-1).astype(F32)
    inv = ROPE_BASE ** (-jnp.arange(nf, dtype=F32) / nf)
    ang = pos[:, :, None] * inv
    cos = jnp.cos(ang)
    sin = jnp.sin(ang)
    c = jnp.tile(jnp.concatenate([cos, cos], axis=2).reshape(t, dr), (1, reps))
    s = jnp.tile(jnp.concatenate([-sin, sin], axis=2).reshape(t, dr), (1, reps))
    if lead_zero:
        z = jnp.zeros((t, lead_zero), F32)
        c = jnp.concatenate([z, c], axis=1)
        s = jnp.concatenate([z, s], axis=1)
    return c, s


def _na_bias_table(rel_bias, cfgs, win):
    v = rel_bias.astype(F32)
    pad = GRID_W - NA_COLS
    ext = jnp.concatenate([jnp.repeat(v[..., :1], pad, axis=-1), v, jnp.repeat(v[..., -1:], pad, axis=-1)], axis=-1)
    col_t = jnp.stack([ext[..., GRID_W - 1 - q:2 * GRID_W - 1 - q] for q in range(GRID_W)], axis=2)
    qc = np.arange(GRID_W)[:, None]
    kc = np.arange(GRID_W)[None, :]
    c0 = np.clip(qc - NA_COLS // 2, 0, GRID_W - NA_COLS)
    col_ok = (kc >= c0) & (kc < c0 + NA_COLS)
    col_t = jnp.where(col_ok, col_t, NEG_INF)
    neg = jnp.full((NA_HEADS, GRID_W, GRID_W), NEG_INF, F32)
    kh = min(NA_ROWS, win)
    tabs = []
    for cfg in cfgs:
        per_row = []
        for delta, off in cfg:
            blocks = [col_t[:, kk - delta + NA_ROWS - 1] if off <= kk < off + kh else neg for kk in range(win)]
            per_row.append(jnp.concatenate(blocks, axis=-1))
        tabs.append(jnp.concatenate(per_row, axis=1))
    return jnp.stack(tabs)


def kernel(x_prompt, x_sample, cache_mla_ckv, cache_mla_krope, cache_diff_k, cache_diff_v, cache_na_k, cache_na_v, state_ret, c, c_ctx, w_ada, b_ada, g_attn, g_ffn, w_in, mla_kv_norm, w_mla_ukv, diff_lambda, diff_norm, na_rel_bias, ret_decay, ret_norm, w_branch, w_o, w_up, conv_w, conv_b, w_down, g_final):
    nbc, seq, d = x_prompt.shape
    nbl, t, _ = x_sample.shape
    past = cache_mla_ckv.shape[2]
    depth = w_in.shape[0]
    mc, ml = nbc * seq, nbl * t

    xc = x_prompt.reshape(mc, d)
    xl = x_sample.reshape(ml, d)

    mod_rows = 16
    cond = jnp.concatenate([c_ctx[None], c, jnp.zeros((mod_rows - 1 - nbl, d), F32)], axis=0)
    mods = _ada_call(cond, w_ada, b_ada).reshape(depth, mod_rows, 6, d)

    cos32q, sin32q = _rope_tables(t, MLA_ROPE, 8)
    cos32k, sin32k = _rope_tables(t, MLA_ROPE, 4, lead_zero=LANE)
    cos64, sin64 = _rope_tables(t, DIFF_QK, 8)

    tm_c = min(1024, mc)
    tm_l = min(1024, t)
    states = []
    for l in range(depth):
        lam_init = 0.8 - 0.6 * math.exp(-0.3 * l)
        w_in_r = _reorder_w_in(w_in[l])
        w_ukv_e = _reorder_w_ukv(w_mla_ukv[l])
        wb = w_branch[l].astype(BF16)
        wo = w_o[l].astype(BF16)
        wup = w_up[l].astype(BF16)
        wdn = w_down[l].astype(BF16)
        kvn = mla_kv_norm[l].reshape(1, MLA_KV_RANK)
        dec = jnp.broadcast_to(ret_decay[l].reshape(2 * RET_HEADS, 1), (2 * RET_HEADS, LANE))
        gn = ret_norm[l].reshape(1, 512)
        dnorm = diff_norm[l].reshape(1, 512)
        g_out = g_final.reshape(1, d) if l == depth - 1 else None
        g_a = g_attn[l].reshape(1, d)
        g_f = g_ffn[l].reshape(1, d)

        def mod(i):
            return mods[l, 0:1, i][:, None, :], mods[l, 1:1 + nbl, i][:, None, :]

        sh1, sc1, ga1, sh2, sc2, ga2 = [mod(i) for i in range(6)]

        pc, pc32 = _mm_call(xc, w_in_r, out_dtype=BF16, tm=tm_c, tn=1024, rows_per_mod=mc,
                            norm=(g_a, sh1[0], sc1[0]), f32_cols=STATE_COLS)
        ya, yb, yc, yd, ckv_state, sret = _ctx_mix_call(pc, pc32, w_ukv_e, kvn, diff_lambda[l], dnorm, dec, gn,
                                                        nb=nbc, seq=seq, lam_init=lam_init)
        states.append((
            ckv_state.reshape(nbc, seq, MLA_KV_RANK),
            pc32[:, C_KR + LANE:C_KR + LANE + MLA_ROPE].reshape(nbc, seq, MLA_ROPE),
            pc32[:, C_DK:C_DK + 512].reshape(nbc, seq, DIFF_HEADS, 2, DIFF_QK),
            pc32[:, C_DV:C_DV + 512].reshape(nbc, seq, DIFF_HEADS, DIFF_V),
            pc32[:, C_NK:C_NK + 512].reshape(nbc, seq, NA_HEADS, NA_DIM),
            pc32[:, C_NV:C_NV + 512].reshape(nbc, seq, NA_HEADS, NA_DIM),
            sret,
        ))
        xc = _merge_call((ya, yb, yc, yd), pc, wb, wo, xc, ga1[0], rows_per_mod=mc)
        act = _ffn_up_call(xc, g_f, sh2[0], sc2[0], wup, conv_w[l], conv_b[l], seq=seq, rows_per_mod=mc, tm=tm_c)
        xc = _mm_call(act, wdn, out_dtype=F32, tm=256, tn=d, rows_per_mod=mc, resid=(xc, ga2[0]), out_norm=g_out)

        plat = _mm_call(xl, w_in_r, out_dtype=BF16, tm=tm_l, tn=2048, rows_per_mod=t, norm=(g_a, sh1[1], sc1[1]))
        kl_mla, vl_mla, _ = _mla_kv_call(plat, C_CKV, plat, C_KR, w_ukv_e, g=kvn, rope=(cos32k, sin32k), seq=t)
        kr_c = cache_mla_krope[:, l].reshape(nbl * past, MLA_ROPE)
        krblk_c = jnp.concatenate([jnp.zeros((nbl * past, LANE), F32)] + [kr_c] * 4, axis=1)
        kp_mla, vp_mla = _mla_kv_call(cache_mla_ckv[:, l].reshape(nbl * past, MLA_KV_RANK), 0, krblk_c, 0, w_ukv_e)
        dk_l = _rope_call(plat, C_DK, 512, cos64, sin64, DIFF_QK, t)
        tq = min(512, t)
        ya = _mha_call(plat, C_QN, plat, C_QR, [(kl_mla, 0, vl_mla, 0, t), (kp_mla, 0, vp_mla, 0, past)],
                       nb=nbl, tq=tq, t=t, scale=(MLA_NOPE + MLA_ROPE) ** -0.5, rope=(cos32q, sin32q))
        dk_c = cache_diff_k[:, l].reshape(nbl * past, 512)
        dv_c = cache_diff_v[:, l].reshape(nbl * past, 512)
        yb = _diff_call(plat, C_DQ, [(dk_l, 0, plat, C_DV, t), (dk_c, 0, dv_c, 0, past)], diff_lambda[l], dnorm,
                        nb=nbl, tq=tq, t=t, lam_init=lam_init, rope=(cos64, sin64))
        yc = _na_call(plat, cache_na_k[:, l].reshape(nbl * past, 512), cache_na_v[:, l].reshape(nbl * past, 512),
                      na_rel_bias[l], nb=nbl, t=t, past=past)
        s0 = state_ret[:, l].astype(F32).reshape(nbl, 2, 2, LANE, LANE)
        yd = _ret_call(plat, dec, gn, s0, nb=nbl, tq=tq, t=t)
        xl = _merge_call((ya, yb, yc, yd), plat, wb, wo, xl, ga1[1], rows_per_mod=t)
        act = _ffn_up_call(xl, g_f, sh2[1], sc2[1], wup, conv_w[l], conv_b[l], seq=t, rows_per_mod=t, tm=tm_l)
        xl = _mm_call(act, wdn, out_dtype=F32, tm=256, tn=d, rows_per_mod=t, resid=(xl, ga2[1]), out_norm=g_out)

    y_prompt = xc.reshape(nbc, seq, d)
    y_sample = xl.reshape(nbl, t, d)
    outs = [jnp.stack([st[i] for st in states], axis=1) for i in range(7)]
    return (y_prompt, y_sample, *outs)
```

```python
import functools
import math

import jax
import jax.numpy as jnp
import numpy as np
from jax import lax
from jax.experimental import pallas as pl
from jax.experimental.pallas import tpu as pltpu

F32 = jnp.float32
BF16 = jnp.bfloat16

D_MODEL = 2048
GRID_W = 64
ROPE_BASE = 10000.0
EPS = 1e-6
NEG_INF = -1e30
N_BRANCH = 4
BRANCH_W = 512
MLA_HEADS = 8
MLA_NOPE = 64
MLA_ROPE = 32
MLA_V = 64
MLA_KV_RANK = 256
DIFF_QK = 64
DIFF_V = 128
DIFF_HEADS = 4
NA_DIM = 64
NA_HEADS = 8
NA_ROWS = 8
NA_COLS = 16
RET_V = 128
RET_HEADS = 4
RET_QK = 64
D_FF = 5632
CONV_W = 3

LANE = 128
HALF = 64

C_CKV = 0
C_KR = 256
C_DK = 512
C_DV = 1024
C_NK = 1536
C_NV = 2048
C_RQ = 2560
C_RK = 2816
STATE_COLS = 3072
C_QN = 3072
C_QR = 3584
C_DQ = 4096
C_NQ = 4608
C_RV = 5120
C_RG = 5632
C_GATES = 6144
P_WIDTH = C_GATES + N_BRANCH * D_MODEL

VMEM_LIMIT = 56 * 1024 * 1024


def _cparams(sem):
    return pltpu.CompilerParams(dimension_semantics=sem, vmem_limit_bytes=VMEM_LIMIT)


def _dot(a, b):
    return jnp.dot(a, b, preferred_element_type=F32)


def _dot_t(a, b):
    return lax.dot_general(a, b, (((1,), (1,)), ((), ())), preferred_element_type=F32)


def _sigmoid(x):
    return 0.5 * jnp.tanh(0.5 * x) + 0.5


def _silu(x):
    h = 0.5 * x
    return h * (jnp.tanh(h) + 1.0)


def _lane_iota(shape):
    return lax.broadcasted_iota(jnp.int32, shape, len(shape) - 1)


def _half_mask(x, e):
    lane = _lane_iota(x.shape)
    keep = (lane < HALF) if e == 0 else (lane >= HALF)
    return jnp.where(keep, x, 0.0)


def _rope(x, cos, sin, dr):
    w = x.shape[-1]
    nf = dr // 4
    lane = _lane_iota(x.shape)
    first = (lane % (2 * nf)) < nf
    rot = jnp.where(first, pltpu.roll(x, w - nf, 1), pltpu.roll(x, nf, 1))
    return x * cos + rot * sin


ADA_TK = 256


def _ada_kernel(c_ref, w_ref, b_ref, o_ref):
    k = pl.program_id(1)

    @pl.when(k == 0)
    def _():
        o_ref[...] = jnp.broadcast_to(b_ref[...], o_ref.shape)

    a = _silu(c_ref[...]).astype(BF16)
    o_ref[...] += _dot(a, w_ref[...].astype(BF16))


def _ada_call(cond, w_ada, b_ada):
    depth, d, n = w_ada.shape
    rows = cond.shape[0]
    assert d % ADA_TK == 0
    return pl.pallas_call(
        _ada_kernel,
        grid=(depth, d // ADA_TK),
        in_specs=[
            pl.BlockSpec((rows, ADA_TK), lambda l, k: (0, k)),
            pl.BlockSpec((None, ADA_TK, n), lambda l, k: (l, k, 0)),
            pl.BlockSpec((None, 1, n), lambda l, k: (l, 0, 0)),
        ],
        out_specs=pl.BlockSpec((None, rows, n), lambda l, k: (l, 0, 0)),
        out_shape=jax.ShapeDtypeStruct((depth, rows, n), F32),
        compiler_params=_cparams(("parallel", "arbitrary")),
    )(cond, w_ada, b_ada.reshape(depth, 1, n))


NORM_ROWS = 128


def _mm_kernel(*refs, norm, resid, out_norm, f32_tiles):
    refs = list(refs)
    a_ref = refs.pop(0)
    if norm:
        g_ref, sh_ref, sc_ref = refs.pop(0), refs.pop(0), refs.pop(0)
    w_ref = refs.pop(0)
    if resid:
        r_ref, ga_ref = refs.pop(0), refs.pop(0)
    if out_norm:
        go_ref = refs.pop(0)
    o_ref = refs.pop(0)
    if f32_tiles:
        side_ref = refs.pop(0)

    if norm:
        h_ref = refs.pop(0)

        @pl.when(pl.program_id(1) == 0)
        def _():
            g = g_ref[...]
            mul = 1.0 + sc_ref[...]
            add = sh_ref[...]

            def body(c, carry):
                r0 = pl.multiple_of(c * NORM_ROWS, NORM_ROWS)
                x = a_ref[pl.ds(r0, NORM_ROWS), :]
                y = x * lax.rsqrt(jnp.mean(x * x, axis=-1, keepdims=True) + EPS) * g
                h_ref[pl.ds(r0, NORM_ROWS), :] = (y * mul + add).astype(BF16)
                return carry

            lax.fori_loop(0, a_ref.shape[0] // NORM_ROWS, body, 0)

        a = h_ref[...]
    else:
        a = a_ref[...]
    acc = _dot(a, w_ref[...])
    if resid:
        acc = r_ref[...] + ga_ref[...] * acc
    if out_norm:
        acc = acc * lax.rsqrt(jnp.mean(acc * acc, axis=-1, keepdims=True) + EPS) * go_ref[...]
    o_ref[...] = acc.astype(o_ref.dtype)
    if f32_tiles:
        @pl.when(pl.program_id(1) < f32_tiles)
        def _():
            side_ref[...] = acc


def _mm_call(a, w, *, out_dtype, tm, tn, rows_per_mod, norm=None, resid=None, out_norm=None, f32_cols=0):
    m, k = a.shape
    n = w.shape[1]
    assert m % tm == 0 and n % tn == 0 and rows_per_mod % tm == 0
    assert out_norm is None or tn == n
    tiles_per_mod = rows_per_mod // tm
    in_specs = [pl.BlockSpec((tm, k), lambda i, j: (i, 0))]
    args = [a]
    if norm is not None:
        g, sh, sc = norm
        in_specs += [
            pl.BlockSpec((1, k), lambda i, j: (0, 0)),
            pl.BlockSpec((None, 1, k), lambda i, j: (i // tiles_per_mod, 0, 0)),
            pl.BlockSpec((None, 1, k), lambda i, j: (i // tiles_per_mod, 0, 0)),
        ]
        args += [g, sh, sc]
    if tn == n:
        in_specs.append(pl.BlockSpec((k, tn), lambda i, j: (0, 0), pipeline_mode=pl.Buffered(1)))
    else:
        in_specs.append(pl.BlockSpec((k, tn), lambda i, j: (0, j)))
    args.append(w)
    if resid is not None:
        x, ga = resid
        in_specs += [
            pl.BlockSpec((tm, tn), lambda i, j: (i, j)),
            pl.BlockSpec((None, 1, tn), lambda i, j: (i // tiles_per_mod, 0, j)),
        ]
        args += [x, ga]
    if out_norm is not None:
        in_specs.append(pl.BlockSpec((1, n), lambda i, j: (0, 0)))
        args.append(out_norm)
    scratch = [pltpu.VMEM((tm, k), BF16)] if norm is not None else []
    assert f32_cols % tn == 0
    f32_tiles = f32_cols // tn
    out_specs = pl.BlockSpec((tm, tn), lambda i, j: (i, j))
    out_shape = jax.ShapeDtypeStruct((m, n), out_dtype)
    if f32_tiles:
        out_specs = [out_specs, pl.BlockSpec((tm, tn), lambda i, j: (i, jnp.minimum(j, f32_tiles - 1)))]
        out_shape = [out_shape, jax.ShapeDtypeStruct((m, f32_cols), F32)]
    return pl.pallas_call(
        functools.partial(_mm_kernel, norm=norm is not None, resid=resid is not None, out_norm=out_norm is not None,
                          f32_tiles=f32_tiles),
        grid=(m // tm, n // tn),
        in_specs=in_specs,
        out_specs=out_specs,
        out_shape=out_shape,
        scratch_shapes=scratch,
        compiler_params=_cparams(("parallel", "arbitrary")),
    )(*args)


def _rope_kernel(x_ref, c_ref, s_ref, o_ref, *, dr):
    o_ref[...] = _rope(x_ref[...].astype(F32), c_ref[...], s_ref[...], dr).astype(o_ref.dtype)


def _rope_call(p, col, width, cos, sin, dr, seq):
    m = p.shape[0]
    tm = 256
    tpb = seq // tm
    cb = col // width
    assert col % width == 0
    return pl.pallas_call(
        functools.partial(_rope_kernel, dr=dr),
        grid=(tpb, m // seq),
        in_specs=[
            pl.BlockSpec((tm, width), lambda i, b: (b * tpb + i, cb)),
            pl.BlockSpec((tm, width), lambda i, b: (i, 0)),
            pl.BlockSpec((tm, width), lambda i, b: (i, 0)),
        ],
        out_specs=pl.BlockSpec((tm, width), lambda i, b: (b * tpb + i, 0)),
        out_shape=jax.ShapeDtypeStruct((m, width), BF16),
        compiler_params=_cparams(("parallel", "parallel")),
    )(p, cos, sin)


MLA_K_W = 4 * 256
MLA_V_W = 4 * 128


def _mla_kv_kernel(*refs, norm, rope):
    refs = list(refs)
    c_ref, kr_ref = refs.pop(0), refs.pop(0)
    if rope:
        cos_ref, sin_ref = refs.pop(0), refs.pop(0)
    if norm:
        g_ref = refs.pop(0)
    w_ref = refs.pop(0)
    k_ref, v_ref = refs.pop(0), refs.pop(0)
    c = c_ref[...].astype(F32)
    if norm:
        ckv_ref = refs.pop(0)
        c = c * lax.rsqrt(jnp.mean(c * c, axis=-1, keepdims=True) + EPS) * g_ref[...]
        ckv_ref[...] = c
    kv = _dot(c.astype(BF16), w_ref[...])
    kr = kr_ref[...].astype(F32)
    if rope:
        kr = _rope(kr, cos_ref[...], sin_ref[...], MLA_ROPE)
    k_ref[...] = (kv[:, :MLA_K_W] + jnp.concatenate([kr] * 4, axis=1)).astype(BF16)
    v_ref[...] = kv[:, MLA_K_W:].astype(BF16)


def _mla_kv_call(ckv_src, ckv_col, kr_src, kr_col, w_e, *, g=None, rope=None, seq=None):
    m = ckv_src.shape[0]
    tm = 256
    assert m % tm == 0 and ckv_col % 256 == 0 and kr_col % 256 == 0
    in_specs = [
        pl.BlockSpec((tm, 256), lambda i: (i, ckv_col // 256)),
        pl.BlockSpec((tm, 256), lambda i: (i, kr_col // 256)),
    ]
    args = [ckv_src, kr_src]
    if rope is not None:
        tpb = seq // tm
        in_specs += [pl.BlockSpec((tm, 256), lambda i: (i % tpb, 0))] * 2
        args += list(rope)
    if g is not None:
        in_specs.append(pl.BlockSpec((1, 256), lambda i: (0, 0)))
        args.append(g)
    in_specs.append(pl.BlockSpec(w_e.shape, lambda i: (0, 0)))
    args.append(w_e)
    out_specs = [pl.BlockSpec((tm, MLA_K_W), lambda i: (i, 0)), pl.BlockSpec((tm, MLA_V_W), lambda i: (i, 0))]
    out_shape = [jax.ShapeDtypeStruct((m, MLA_K_W), BF16), jax.ShapeDtypeStruct((m, MLA_V_W), BF16)]
    if g is not None:
        out_specs.append(pl.BlockSpec((tm, 256), lambda i: (i, 0)))
        out_shape.append(jax.ShapeDtypeStruct((m, 256), F32))
    return pl.pallas_call(
        functools.partial(_mla_kv_kernel, norm=g is not None, rope=rope is not None),
        grid=(m // tm,),
        in_specs=in_specs,
        out_specs=out_specs,
        out_shape=out_shape,
        compiler_params=_cparams(("parallel",)),
    )(*args)


LOG2E = 1.4426950408889634


def _exp_parts(scores, scale=1.0):
    m = scores[0].max(axis=-1, keepdims=True)
    for s in scores[1:]:
        m = jnp.maximum(m, s.max(axis=-1, keepdims=True))
    return [jnp.exp2((s - m) * (scale * LOG2E)) for s in scores]


def _softmax_parts(scores, scale=1.0):
    ps = _exp_parts(scores, scale)
    l = ps[0].sum(axis=-1, keepdims=True)
    for p in ps[1:]:
        l = l + p.sum(axis=-1, keepdims=True)
    return ps, 1.0 / l


def _pv_normalised(ps, vs):
    acc = None
    for p, v in zip(ps, vs):
        t = _dot(p.astype(BF16), jnp.concatenate([v, jnp.ones_like(v)], axis=1))
        acc = t if acc is None else acc + t
    return acc[:, :LANE] / acc[:, LANE:]


def _mha_kernel(*refs, mla, nseg, scale, rope=False):
    refs = list(refs)
    qn_ref = refs.pop(0)
    qr_ref = refs.pop(0) if mla else None
    if rope:
        cos_ref, sin_ref = refs.pop(0), refs.pop(0)
    kv_refs = [(refs.pop(0), refs.pop(0)) for _ in range(nseg)]
    o_ref = refs.pop(0)
    kw = 256 if mla else LANE
    lane = _lane_iota((qn_ref.shape[0], LANE))
    nheads = 8
    if mla:
        qr_all = qr_ref[...].astype(F32)
        if rope:
            qr_all = _rope(qr_all, cos_ref[...], sin_ref[...], MLA_ROPE)

    def head_scores(h):
        j, e = h // 2, h % 2
        q = _half_mask(qn_ref[:, j * LANE:(j + 1) * LANE].astype(F32), e)
        if mla:
            qr = qr_all[:, (h // 4) * LANE:(h // 4 + 1) * LANE]
            qr = jnp.where(lane // MLA_ROPE == h % 4, qr, 0.0)
            q = jnp.concatenate([q, qr], axis=1)
        q = q.astype(BF16)
        return [_dot_t(q, k_ref[:, j * kw:(j + 1) * kw].astype(BF16)) for k_ref, _ in kv_refs]

    outs = []

    def head_values(h, ps):
        j = h // 2
        outs.append(_pv_normalised(ps, [v_ref[:, j * LANE:(j + 1) * LANE].astype(BF16) for _, v_ref in kv_refs]))
        if h % 2 == 1:
            o_ref[:, j * LANE:(j + 1) * LANE] = jnp.where(lane < HALF, outs[0], outs[1]).astype(o_ref.dtype)
            outs.clear()

    scores = head_scores(0)
    pending = None
    for h in range(nheads):
        nxt = head_scores(h + 1) if h + 1 < nheads else None
        ps = _exp_parts(scores, scale)
        if pending is not None:
            head_values(*pending)
        pending = (h, ps)
        scores = nxt
    head_values(*pending)


def _mha_call(qn, qn_col, qr, qr_col, segs, *, nb, tq, t, scale, rope=None):
    mla = qr is not None
    kw = MLA_K_W if mla else 512
    nq = t // tq
    in_specs = [pl.BlockSpec((tq, 512), lambda b, i: (b * nq + i, qn_col // 512))]
    args = [qn]
    if mla:
        in_specs.append(pl.BlockSpec((tq, 256), lambda b, i: (b * nq + i, qr_col // 256)))
        args.append(qr)
    if rope is not None:
        in_specs += [pl.BlockSpec((tq, 256), lambda b, i: (i, 0))] * 2
        args += list(rope)
    for k_arr, k_col, v_arr, v_col, s in segs:
        assert k_col % kw == 0 and v_col % 512 == 0
        in_specs.append(pl.BlockSpec((s, kw), functools.partial(lambda b, i, c: (b, c), c=k_col // kw)))
        in_specs.append(pl.BlockSpec((s, 512), functools.partial(lambda b, i, c: (b, c), c=v_col // 512)))
        args += [k_arr, v_arr]
    return pl.pallas_call(
        functools.partial(_mha_kernel, mla=mla, nseg=len(segs), scale=scale, rope=rope is not None),
        grid=(nb, nq),
        in_specs=in_specs,
        out_specs=pl.BlockSpec((tq, 512), lambda b, i: (b * nq + i, 0)),
        out_shape=jax.ShapeDtypeStruct((nb * t, 512), BF16),
        compiler_params=_cparams(("parallel", "parallel")),
    )(*args)


def _diff_kernel(*refs, nseg, lam_init, rope=False):
    refs = list(refs)
    q_ref = refs.pop(0)
    if rope:
        cos_ref, sin_ref = refs.pop(0), refs.pop(0)
    kv_refs = [(refs.pop(0), refs.pop(0)) for _ in range(nseg)]
    lp_ref, g_ref, o_ref = refs
    lp = lp_ref[...]
    lam = (jnp.exp(jnp.sum(lp[0:1] * lp[1:2], axis=-1, keepdims=True))
           - jnp.exp(jnp.sum(lp[2:3] * lp[3:4], axis=-1, keepdims=True)) + lam_init)
    scale = DIFF_QK ** -0.5
    q_all = q_ref[...].astype(F32)
    if rope:
        q_all = _rope(q_all, cos_ref[...], sin_ref[...], DIFF_QK)

    def head_scores(h):
        q_pair = q_all[:, h * LANE:(h + 1) * LANE]
        return [[_dot_t(_half_mask(q_pair, c).astype(BF16), k_ref[:, h * LANE:(h + 1) * LANE].astype(BF16))
                 for k_ref, _ in kv_refs] for c in range(2)]

    scores = head_scores(0)
    for h in range(DIFF_HEADS):
        nxt = head_scores(h + 1) if h + 1 < DIFF_HEADS else None
        (ps0, inv0), (ps1, inv1) = [_softmax_parts(sc, scale) for sc in scores]
        o = None
        for s in range(nseg):
            a = (ps0[s] * inv0 - lam * (ps1[s] * inv1)).astype(BF16)
            t = _dot(a, kv_refs[s][1][:, h * LANE:(h + 1) * LANE].astype(BF16))
            o = t if o is None else o + t
        o = o * lax.rsqrt(jnp.mean(o * o, axis=-1, keepdims=True) + EPS) * g_ref[:, h * LANE:(h + 1) * LANE]
        o_ref[:, h * LANE:(h + 1) * LANE] = (o * (1.0 - lam_init)).astype(o_ref.dtype)
        scores = nxt


def _diff_call(q, q_col, segs, lp, g, *, nb, tq, t, lam_init, rope=None):
    nq = t // tq
    in_specs = [pl.BlockSpec((tq, 512), lambda b, i: (b * nq + i, q_col // 512))]
    args = [q]
    if rope is not None:
        in_specs += [pl.BlockSpec((tq, 512), lambda b, i: (i, 0))] * 2
        args += list(rope)
    for k_arr, k_col, v_arr, v_col, s in segs:
        in_specs.append(pl.BlockSpec((s, 512), functools.partial(lambda b, i, c: (b, c), c=k_col // 512)))
        in_specs.append(pl.BlockSpec((s, 512), functools.partial(lambda b, i, c: (b, c), c=v_col // 512)))
        args += [k_arr, v_arr]
    in_specs += [pl.BlockSpec((4, DIFF_QK), lambda b, i: (0, 0)), pl.BlockSpec((1, 512), lambda b, i: (0, 0))]
    args += [lp, g]
    return pl.pallas_call(
        functools.partial(_diff_kernel, nseg=len(segs), lam_init=lam_init, rope=rope is not None),
        grid=(nb, nq),
        in_specs=in_specs,
        out_specs=pl.BlockSpec((tq, 512), lambda b, i: (b * nq + i, 0)),
        out_shape=jax.ShapeDtypeStruct((nb * t, 512), BF16),
        compiler_params=_cparams(("parallel", "parallel")),
    )(*args)


NA_QROWS = 4


def _na_plan(rows):
    win = min(rows, NA_QROWS + NA_ROWS)
    kh = min(NA_ROWS, rows)
    cfgs, cfg_idx, starts = [], [], []
    for blk in range(rows // NA_QROWS):
        r0 = blk * NA_QROWS
        s_blk = int(np.clip(r0 - kh // 2, 0, rows - win))
        cfg = tuple((r0 + i - s_blk, int(np.clip(r0 + i - kh // 2, 0, rows - kh)) - s_blk) for i in range(NA_QROWS))
        if cfg not in cfgs:
            cfgs.append(cfg)
        cfg_idx.append(cfgs.index(cfg))
        starts.append(s_blk)
    return win, cfgs, np.array([cfg_idx, starts], np.int32)


def _na_kernel(meta_ref, q_ref, k_ref, v_ref, kc_ref, vc_ref, b_ref, o_ref, *, win):
    cfg = meta_ref[0, pl.program_id(1)]
    start = pl.multiple_of(meta_ref[1, pl.program_id(1)] * GRID_W, GRID_W)
    n_loc = win * GRID_W
    lane = _lane_iota((q_ref.shape[0], LANE))

    def head_scores(h):
        j, e = h // 2, h % 2
        q = _half_mask(q_ref[:, j * LANE:(j + 1) * LANE].astype(F32) * (NA_DIM ** -0.5), e).astype(BF16)
        s_loc = _dot_t(q, k_ref[pl.ds(start, n_loc), j * LANE:(j + 1) * LANE].astype(BF16))
        s_ctx = _dot_t(q, kc_ref[:, j * LANE:(j + 1) * LANE].astype(BF16))
        return [s_loc + b_ref[cfg, h], s_ctx]

    scores = head_scores(0)
    outs = []
    for h in range(NA_HEADS):
        nxt = head_scores(h + 1) if h + 1 < NA_HEADS else None
        j = h // 2
        v_loc = v_ref[pl.ds(start, n_loc), j * LANE:(j + 1) * LANE].astype(BF16)
        v_ctx = vc_ref[:, j * LANE:(j + 1) * LANE].astype(BF16)
        outs.append(_pv_normalised(_exp_parts(scores), [v_loc, v_ctx]))
        if h % 2 == 1:
            o_ref[:, j * LANE:(j + 1) * LANE] = jnp.where(lane < HALF, outs[0], outs[1]).astype(o_ref.dtype)
            outs = []
        scores = nxt


def _na_call(p, kc, vc, rel_bias, *, nb, t, past):
    rows = t // GRID_W
    assert rows % NA_QROWS == 0 and rows >= NA_ROWS
    win, cfgs, meta = _na_plan(rows)
    bias_tab = _na_bias_table(rel_bias, cfgs, win)
    nblk = rows // NA_QROWS
    tq = NA_QROWS * GRID_W
    n_loc = win * GRID_W
    grid_spec = pltpu.PrefetchScalarGridSpec(
        num_scalar_prefetch=1,
        grid=(nb, nblk),
        in_specs=[
            pl.BlockSpec((tq, 512), lambda b, r, meta: (b * nblk + r, C_NQ // 512)),
            pl.BlockSpec((t, 512), lambda b, r, meta: (b, C_NK // 512)),
            pl.BlockSpec((t, 512), lambda b, r, meta: (b, C_NV // 512)),
            pl.BlockSpec((past, 512), lambda b, r, meta: (b, 0)),
            pl.BlockSpec((past, 512), lambda b, r, meta: (b, 0)),
            pl.BlockSpec((len(cfgs), NA_HEADS, tq, n_loc), lambda b, r, meta: (0, 0, 0, 0), pipeline_mode=pl.Buffered(1)),
        ],
        out_specs=pl.BlockSpec((tq, 512), lambda b, r, meta: (b * nblk + r, 0)),
    )
    return pl.pallas_call(
        functools.partial(_na_kernel, win=win),
        grid_spec=grid_spec,
        out_shape=jax.ShapeDtypeStruct((nb * t, 512), BF16),
        compiler_params=_cparams(("parallel", "arbitrary")),
    )(jnp.asarray(meta), p, p, p, kc, vc, bias_tab)


def _log_sigmoid(x):
    return jnp.minimum(x, 0.0) - jnp.log(1.0 + jnp.exp(-jnp.abs(x)))


def _ret_kernel(*refs, has_state, t, whole_seq=False):
    refs = list(refs)
    q_ref, k_ref, v_ref, rg_ref, dec_ref, gn_ref = [refs.pop(0) for _ in range(6)]
    s0_ref = refs.pop(0) if has_state else None
    o_ref = refs.pop(0)
    tq = q_ref.shape[0]
    q0 = 0 if whole_seq else pl.program_id(1) * tq
    lg = _log_sigmoid(dec_ref[...])
    nkb = t // tq
    qb = q0 // tq
    tloc = lax.broadcasted_iota(jnp.int32, (tq, 1), 0).astype(F32)
    dloc = (lax.broadcasted_iota(jnp.int32, (tq, tq), 0) - lax.broadcasted_iota(jnp.int32, (tq, tq), 1)).astype(F32)
    tcol = q0 + tloc
    qstart = pl.multiple_of(q0, tq) if not whole_seq else 0

    for h in range(RET_HEADS):
        j, e = h // 2, h % 2
        lgf = lg[h:h + 1, 0:1]
        lgb = lg[RET_HEADS + h:RET_HEADS + h + 1, 0:1]
        qm = _half_mask(q_ref[:, j * LANE:(j + 1) * LANE].astype(F32), e)
        q = qm.astype(BF16)
        qs = qm * (RET_QK ** -0.5)
        k_d = k_ref[pl.ds(qstart, tq), j * LANE:(j + 1) * LANE].astype(BF16)
        v_d = v_ref[pl.ds(qstart, tq), h * LANE:(h + 1) * LANE].astype(BF16)
        dmat = jnp.exp2(dloc * jnp.where(dloc >= 0, lgf * LOG2E, -lgb * LOG2E))
        y = _dot((_dot_t(qs.astype(BF16), k_d) * dmat).astype(BF16), v_d)
        for kb in range(0 if whole_seq else nkb):
            before = kb < qb
            off = ((qb - kb) * tq).astype(F32)
            rf = jnp.where(before, jnp.exp2(tloc * (lgf * LOG2E)), jnp.exp2((float(tq) - tloc) * (lgb * LOG2E)))
            cf = jnp.where(before, jnp.exp2((off - tloc) * (lgf * LOG2E)),
                           jnp.exp2((tloc - off - float(tq)) * (lgb * LOG2E)))
            cf = jnp.where(kb == qb, 0.0, cf)
            k_b = k_ref[kb * tq:(kb + 1) * tq, j * LANE:(j + 1) * LANE].astype(F32)
            a = _dot_t((qs * rf).astype(BF16), (k_b * cf).astype(BF16))
            y = y + _dot(a.astype(BF16), v_ref[kb * tq:(kb + 1) * tq, h * LANE:(h + 1) * LANE].astype(BF16))
        if has_state:
            cf = _dot(q, s0_ref[0, j].astype(BF16))
            cb = _dot(q, s0_ref[1, j].astype(BF16))
            y = y + cf * jnp.exp((tcol + 1.0) * lgf) + cb * jnp.exp((float(t) - tcol) * lgb)
        mu = jnp.mean(y, axis=-1, keepdims=True)
        yc = y - mu
        var = jnp.mean(yc * yc, axis=-1, keepdims=True)
        yn = yc * lax.rsqrt(var + EPS) * gn_ref[:, h * LANE:(h + 1) * LANE]
        o_ref[:, h * LANE:(h + 1) * LANE] = (_silu(rg_ref[:, h * LANE:(h + 1) * LANE].astype(F32)) * yn).astype(o_ref.dtype)


def _ret_call(p, dec, gn, s0, *, nb, tq, t):
    nq = t // tq
    in_specs = [
        pl.BlockSpec((tq, 256), lambda b, i: (b * nq + i, C_RQ // 256)),
        pl.BlockSpec((t, 256), lambda b, i: (b, C_RK // 256)),
        pl.BlockSpec((t, 512), lambda b, i: (b, C_RV // 512)),
        pl.BlockSpec((tq, 512), lambda b, i: (b * nq + i, C_RG // 512)),
        pl.BlockSpec((8, LANE), lambda b, i: (0, 0)),
        pl.BlockSpec((1, 512), lambda b, i: (0, 0)),
    ]
    args = [p, p, p, p, dec, gn]
    if s0 is not None:
        in_specs.append(pl.BlockSpec((None, 2, 2, LANE, LANE), lambda b, i: (b, 0, 0, 0, 0)))
        args.append(s0)
    return pl.pallas_call(
        functools.partial(_ret_kernel, has_state=s0 is not None, t=t),
        grid=(nb, nq),
        in_specs=in_specs,
        out_specs=pl.BlockSpec((tq, 512), lambda b, i: (b * nq + i, 0)),
        out_shape=jax.ShapeDtypeStruct((nb * t, 512), BF16),
        compiler_params=_cparams(("parallel", "parallel")),
    )(*args)


def _ret_state_kernel(k_ref, v_ref, dec_ref, o_ref, *, t):
    lg = _log_sigmoid(dec_ref[...])
    tcol = lax.broadcasted_iota(jnp.int32, (t, 1), 0).astype(F32)
    lane = _lane_iota((t, LANE))
    for j in range(2):
        k_pair = k_ref[:, j * LANE:(j + 1) * LANE].astype(F32) * (RET_QK ** -0.5)
        for d in range(2):
            expo = (float(t) - 1.0 - tcol) if d == 0 else tcol
            w0 = jnp.exp(expo * lg[d * RET_HEADS + 2 * j:d * RET_HEADS + 2 * j + 1, 0:1])
            w1 = jnp.exp(expo * lg[d * RET_HEADS + 2 * j + 1:d * RET_HEADS + 2 * j + 2, 0:1])
            kd = (k_pair * jnp.where(lane < HALF, w0, w1)).T.astype(BF16)
            for e in range(2):
                h = 2 * j + e
                sfull = _dot(kd, v_ref[:, h * LANE:(h + 1) * LANE].astype(BF16))
                o_ref[d, h] = sfull[e * HALF:(e + 1) * HALF, :]


def _ctx_mix_kernel(p_ref, c32_ref, w_ref, kvn_ref, lp_ref, dn_ref, dec_ref, gn_ref,
                    ya_ref, yb_ref, yc_ref, yd_ref, ckv_ref, sret_ref, k_scr, v_scr, *, seq, lam_init):
    def col(c, w):
        return p_ref.at[:, c:c + w]

    _mla_kv_kernel(c32_ref, col(C_KR, 256), kvn_ref, w_ref, k_scr, v_scr, ckv_ref, norm=True, rope=False)
    _mha_kernel(col(C_QN, 512), col(C_QR, 256), k_scr, v_scr, ya_ref, mla=True, nseg=1,
                scale=(MLA_NOPE + MLA_ROPE) ** -0.5)
    _diff_kernel(col(C_DQ, 512), col(C_DK, 512), col(C_DV, 512), lp_ref, dn_ref, yb_ref, nseg=1, lam_init=lam_init)
    _mha_kernel(col(C_NQ, 512), col(C_NK, 512), col(C_NV, 512), yc_ref, mla=False, nseg=1, scale=NA_DIM ** -0.5)
    _ret_kernel(col(C_RQ, 256), col(C_RK, 256), col(C_RV, 512), col(C_RG, 512), dec_ref, gn_ref, yd_ref,
                has_state=False, t=seq, whole_seq=True)
    _ret_state_kernel(col(C_RK, 256), col(C_RV, 512), dec_ref, sret_ref, t=seq)


def _ctx_mix_call(p, p32, w_ukv_e, kvn, lp, dnorm, dec, gn, *, nb, seq, lam_init):
    def const(shape):
        return pl.BlockSpec(shape, lambda b: (0,) * len(shape))

    def rows(width, dtype):
        return pl.BlockSpec((seq, width), lambda b: (b, 0)), jax.ShapeDtypeStruct((nb * seq, width), dtype)

    outs = [rows(512, BF16)] * 4 + [rows(MLA_KV_RANK, F32)]
    out_specs = [o[0] for o in outs] + [pl.BlockSpec((None, 2, RET_HEADS, RET_QK, RET_V), lambda b: (b, 0, 0, 0, 0))]
    out_shape = [o[1] for o in outs] + [jax.ShapeDtypeStruct((nb, 2, RET_HEADS, RET_QK, RET_V), F32)]
    return pl.pallas_call(
        functools.partial(_ctx_mix_kernel, seq=seq, lam_init=lam_init),
        grid=(nb,),
        in_specs=[pl.BlockSpec((seq, C_GATES), lambda b: (b, 0)),
                  pl.BlockSpec((seq, MLA_KV_RANK), lambda b: (b, C_CKV // MLA_KV_RANK)),
                  const(w_ukv_e.shape), const((1, MLA_KV_RANK)),
                  const((4, DIFF_QK)), const((1, 512)), const((8, LANE)), const((1, 512))],
        out_specs=out_specs,
        out_shape=out_shape,
        scratch_shapes=[pltpu.VMEM((seq, MLA_K_W), BF16), pltpu.VMEM((seq, MLA_V_W), BF16)],
        compiler_params=_cparams(("parallel",)),
    )(p, p32, w_ukv_e, kvn, lp, dnorm, dec, gn)


MERGE_CHUNK = 512


def _merge_kernel(ya_ref, yb_ref, yc_ref, yd_ref, g0_ref, g1_ref, g2_ref, g3_ref, wb_ref, wo_ref, x_ref, ga_ref,
                  o_ref, m_ref):
    ys = (ya_ref, yb_ref, yc_ref, yd_ref)
    gs = (g0_ref, g1_ref, g2_ref, g3_ref)
    for c0 in range(0, D_MODEL, MERGE_CHUNK):
        cols = slice(c0, c0 + MERGE_CHUNK)
        m = None
        for i in range(N_BRANCH):
            t = _sigmoid(gs[i][:, cols].astype(F32)) * _dot(ys[i][...], wb_ref[i, :, cols])
            m = t if m is None else m + t
        m_ref[:, cols] = m.astype(BF16)
    o_ref[...] = x_ref[...] + ga_ref[...] * _dot(m_ref[...], wo_ref[...])


def _merge_call(ys, p, w_branch, w_o, x, ga, *, rows_per_mod):
    m = p.shape[0]
    tm = 256
    assert m % tm == 0 and rows_per_mod % tm == 0
    tiles_per_mod = rows_per_mod // tm
    first_gate_block = C_GATES // D_MODEL
    assert C_GATES % D_MODEL == 0
    gate_specs = [
        pl.BlockSpec((tm, D_MODEL), functools.partial(lambda r, i: (r, first_gate_block + i), i=i))
        for i in range(N_BRANCH)
    ]
    return pl.pallas_call(
        _merge_kernel,
        grid=(m // tm,),
        in_specs=[pl.BlockSpec((tm, BRANCH_W), lambda r: (r, 0))] * N_BRANCH + gate_specs + [
            pl.BlockSpec((N_BRANCH, BRANCH_W, D_MODEL), lambda r: (0, 0, 0), pipeline_mode=pl.Buffered(1)),
            pl.BlockSpec((D_MODEL, D_MODEL), lambda r: (0, 0), pipeline_mode=pl.Buffered(1)),
            pl.BlockSpec((tm, D_MODEL), lambda r: (r, 0)),
            pl.BlockSpec((None, 1, D_MODEL), lambda r: (r // tiles_per_mod, 0, 0)),
        ],
        out_specs=pl.BlockSpec((tm, D_MODEL), lambda r: (r, 0)),
        out_shape=jax.ShapeDtypeStruct((m, D_MODEL), F32),
        scratch_shapes=[pltpu.VMEM((tm, D_MODEL), BF16)],
        compiler_params=_cparams(("parallel",)),
    )(*ys, p, p, p, p, w_branch, w_o, x, ga)


HALO = 16


UP_TN = 512
UP_SUB = 256
UP_ROW_CHUNKS = 6


def _ffn_up_kernel(x_ref, xp_ref, xn_ref, g_ref, sh_ref, sc_ref, wa_ref, wb_ref, cwa_ref, cwb_ref, cba_ref, cbb_ref,
                   o_ref, h_ref, *, seq):
    tm = x_ref.shape[0]
    rows = tm + 2 * HALO

    @pl.when(pl.program_id(1) == 0)
    def _():
        g = g_ref[...]
        mul = 1.0 + sc_ref[...]
        add = sh_ref[...]

        def norm(x):
            y = x * lax.rsqrt(jnp.mean(x * x, axis=-1, keepdims=True) + EPS) * g
            return (y * mul + add).astype(BF16)

        h_ref[0:HALO, :] = norm(xp_ref[...])
        h_ref[HALO + tm:rows, :] = norm(xn_ref[...])

        def body(c, carry):
            r0 = pl.multiple_of(c * NORM_ROWS, NORM_ROWS)
            h_ref[pl.ds(HALO + r0, NORM_ROWS), :] = norm(x_ref[pl.ds(r0, NORM_ROWS), :])
            return carry

        lax.fori_loop(0, tm // NORM_ROWS, body, 0)

    pos = (pl.program_id(0) * tm + lax.broadcasted_iota(jnp.int32, (tm, 1), 0)) % seq
    keep_prev = (pos != 0).astype(F32)
    keep_next = (pos != seq - 1).astype(F32)
    h = h_ref[...]

    def project(s):
        cols = slice(s * UP_SUB, (s + 1) * UP_SUB)
        nchunk = max(n for n in range(1, UP_ROW_CHUNKS + 1) if rows % (HALO * n) == 0)
        chunk = rows // nchunk
        parts = [(_dot(h[r0:r0 + chunk], wa_ref[:, cols]), _dot(h[r0:r0 + chunk], wb_ref[:, cols]))
                 for r0 in range(0, rows, chunk)]
        return jnp.concatenate([p[0] for p in parts], axis=0), jnp.concatenate([p[1] for p in parts], axis=0)

    def conv(u, w_ref, b_ref, cols):
        um1 = pltpu.roll(u, 1, 0)[HALO:HALO + tm] * keep_prev
        up1 = pltpu.roll(u, rows - 1, 0)[HALO:HALO + tm] * keep_next
        return um1 * w_ref[0:1, cols] + u[HALO:HALO + tm] * w_ref[1:2, cols] + up1 * w_ref[2:3, cols] + b_ref[:, cols]

    nsub = UP_TN // UP_SUB
    cur = project(0)
    for s in range(nsub):
        nxt = project(s + 1) if s + 1 < nsub else None
        cols = slice(s * UP_SUB, (s + 1) * UP_SUB)
        a = conv(cur[0], cwa_ref, cba_ref, cols)
        b = conv(cur[1], cwb_ref, cbb_ref, cols)
        o_ref[:, cols] = (_silu(a) * b).astype(o_ref.dtype)
        cur = nxt


def _ffn_up_call(x, g, sh, sc, w_up, conv_w, conv_b, *, seq, rows_per_mod, tm):
    m, k = x.shape
    assert m % tm == 0 and rows_per_mod % tm == 0 and D_FF % UP_TN == 0 and (seq % tm == 0 or tm % seq == 0)
    tiles_per_mod = rows_per_mod // tm
    nj = D_FF // UP_TN
    r = tm // HALO
    last = m // HALO - 1
    cb = conv_b.reshape(1, 2 * D_FF)
    return pl.pallas_call(
        functools.partial(_ffn_up_kernel, seq=seq),
        grid=(m // tm, nj),
        in_specs=[
            pl.BlockSpec((tm, k), lambda i, j: (i, 0)),
            pl.BlockSpec((HALO, k), lambda i, j: (jnp.maximum(i * r - 1, 0), 0)),
            pl.BlockSpec((HALO, k), lambda i, j: (jnp.minimum((i + 1) * r, last), 0)),
            pl.BlockSpec((1, k), lambda i, j: (0, 0)),
            pl.BlockSpec((None, 1, k), lambda i, j: (i // tiles_per_mod, 0, 0)),
            pl.BlockSpec((None, 1, k), lambda i, j: (i // tiles_per_mod, 0, 0)),
            pl.BlockSpec((k, UP_TN), lambda i, j: (0, j)),
            pl.BlockSpec((k, UP_TN), lambda i, j: (0, j + nj)),
            pl.BlockSpec((CONV_W, UP_TN), lambda i, j: (0, j)),
            pl.BlockSpec((CONV_W, UP_TN), lambda i, j: (0, j + nj)),
            pl.BlockSpec((1, UP_TN), lambda i, j: (0, j)),
            pl.BlockSpec((1, UP_TN), lambda i, j: (0, j + nj)),
        ],
        out_specs=pl.BlockSpec((tm, UP_TN), lambda i, j: (i, j)),
        out_shape=jax.ShapeDtypeStruct((m, D_FF), BF16),
        scratch_shapes=[pltpu.VMEM((tm + 2 * HALO, k), BF16)],
        compiler_params=_cparams(("parallel", "arbitrary")),
    )(x, x, x, g, sh, sc, w_up, w_up, conv_w, conv_w, cb, cb)


def _reorder_w_in(w):
    d = w.shape[0]
    mq = w[:, :768].reshape(d, MLA_HEADS, MLA_NOPE + MLA_ROPE)
    qn = mq[:, :, :MLA_NOPE].reshape(d, 512)
    qr = mq[:, :, MLA_NOPE:].reshape(d, 256)
    ckv = w[:, 768:1024]
    mkr = w[:, 1024:1056]
    dq, dk, dv, nq, nk, nv = [w[:, 1056 + 512 * i:1056 + 512 * (i + 1)] for i in range(6)]
    rq, rk = w[:, 4128:4384], w[:, 4384:4640]
    rv, rg = w[:, 4640:5152], w[:, 5152:5664]
    gates = w[:, 5664:]
    krblk = jnp.concatenate([jnp.zeros((d, LANE), w.dtype)] + [mkr] * 4, axis=1)
    pad = jnp.zeros((d, C_DQ - C_QR - 256), w.dtype)
    pieces = (ckv, krblk, dk, dv, nk, nv, rq, rk, qn, qr, pad, dq, nq, rv, rg, gates)
    out = jnp.concatenate([x.astype(BF16) for x in pieces], axis=1)
    assert out.shape[1] == P_WIDTH
    return out


def _reorder_w_ukv(w):
    r = w.shape[0]
    w3 = w.reshape(r, MLA_HEADS, MLA_NOPE + MLA_V)
    kn = w3[:, :, :MLA_NOPE].reshape(r, 4, LANE)
    vv = w3[:, :, MLA_NOPE:].reshape(r, 512)
    knz = jnp.concatenate([kn, jnp.zeros_like(kn)], axis=2).reshape(r, MLA_K_W)
    return jnp.concatenate([knz, vv], axis=1).astype(BF16)


def _rope_tables(t, dr, reps, lead_zero=0):
    nf = dr // 4
    pos = jnp.arange(t)
    pos = jnp.stack([pos // GRID_W, pos % GRID_W], axis=-1).astype(F32)
    inv = ROPE_BASE ** (-jnp.arange(nf, dtype=F32) / nf)
    ang = pos[:, :, None] * inv
    cos = jnp.cos(ang)
    sin = jnp.sin(ang)
    c = jnp.tile(jnp.concatenate([cos, cos], axis=2).reshape(t, dr), (1, reps))
    s = jnp.tile(jnp.concatenate([-sin, sin], axis=2).reshape(t, dr), (1, reps))
    if lead_zero:
        z = jnp.zeros((t, lead_zero), F32)
        c = jnp.concatenate([z, c], axis=1)
        s = jnp.concatenate([z, s], axis=1)
    return c, s


def _na_bias_table(rel_bias, cfgs, win):
    v = rel_bias.astype(F32)
    pad = GRID_W - NA_COLS
    ext = jnp.concatenate([jnp.repeat(v[..., :1], pad, axis=-1), v, jnp.repeat(v[..., -1:], pad, axis=-1)], axis=-1)
    col_t = jnp.stack([ext[..., GRID_W - 1 - q:2 * GRID_W - 1 - q] for q in range(GRID_W)], axis=2)
    qc = np.arange(GRID_W)[:, None]
    kc = np.arange(GRID_W)[None, :]
    c0 = np.clip(qc - NA_COLS // 2, 0, GRID_W - NA_COLS)
    col_ok = (kc >= c0) & (kc < c0 + NA_COLS)
    col_t = jnp.where(col_ok, col_t, NEG_INF)
    neg = jnp.full((NA_HEADS, GRID_W, GRID_W), NEG_INF, F32)
    kh = min(NA_ROWS, win)
    tabs = []
    for cfg in cfgs:
        per_row = []
        for delta, off in cfg:
            blocks = [col_t[:, kk - delta + NA_ROWS - 1] if off <= kk < off + kh else neg for kk in range(win)]
            per_row.append(jnp.concatenate(blocks, axis=-1))
        tabs.append(jnp.concatenate(per_row, axis=1))
    return jnp.stack(tabs)


def kernel(x_prompt, x_sample, cache_mla_ckv, cache_mla_krope, cache_diff_k, cache_diff_v, cache_na_k, cache_na_v, state_ret, c, c_ctx, w_ada, b_ada, g_attn, g_ffn, w_in, mla_kv_norm, w_mla_ukv, diff_lambda, diff_norm, na_rel_bias, ret_decay, ret_norm, w_branch, w_o, w_up, conv_w, conv_b, w_down, g_final):
    nbc, seq, d = x_prompt.shape
    nbl, t, _ = x_sample.shape
    past = cache_mla_ckv.shape[2]
    depth = w_in.shape[0]
    mc, ml = nbc * seq, nbl * t

    xc = x_prompt.reshape(mc, d)
    xl = x_sample.reshape(ml, d)

    mod_rows = 16
    cond = jnp.concatenate([c_ctx[None], c, jnp.zeros((mod_rows - 1 - nbl, d), F32)], axis=0)
    mods = _ada_call(cond, w_ada, b_ada).reshape(depth, mod_rows, 6, d)

    cos32q, sin32q = _rope_tables(t, MLA_ROPE, 8)
    cos32k, sin32k = _rope_tables(t, MLA_ROPE, 4, lead_zero=LANE)
    cos64, sin64 = _rope_tables(t, DIFF_QK, 8)

    tm_c = min(1024, mc)
    tm_l = min(1024, t)
    states = []
    for l in range(depth):
        lam_init = 0.8 - 0.6 * math.exp(-0.3 * l)
        w_in_r = _reorder_w_in(w_in[l])
        w_ukv_e = _reorder_w_ukv(w_mla_ukv[l])
        wb = w_branch[l].astype(BF16)
        wo = w_o[l].astype(BF16)
        wup = w_up[l].astype(BF16)
        wdn = w_down[l].astype(BF16)
        kvn = mla_kv_norm[l].reshape(1, MLA_KV_RANK)
        dec = jnp.broadcast_to(ret_decay[l].reshape(2 * RET_HEADS, 1), (2 * RET_HEADS, LANE))
        gn = ret_norm[l].reshape(1, 512)
        dnorm = diff_norm[l].reshape(1, 512)
        g_out = g_final.reshape(1, d) if l == depth - 1 else None
        g_a = g_attn[l].reshape(1, d)
        g_f = g_ffn[l].reshape(1, d)

        def mod(i):
            return mods[l, 0:1, i][:, None, :], mods[l, 1:1 + nbl, i][:, None, :]

        sh1, sc1, ga1, sh2, sc2, ga2 = [mod(i) for i in range(6)]

        pc, pc32 = _mm_call(xc, w_in_r, out_dtype=BF16, tm=tm_c, tn=1024, rows_per_mod=mc,
                            norm=(g_a, sh1[0], sc1[0]), f32_cols=STATE_COLS)
        ya, yb, yc, yd, ckv_state, sret = _ctx_mix_call(pc, pc32, w_ukv_e, kvn, diff_lambda[l], dnorm, dec, gn,
                                                        nb=nbc, seq=seq, lam_init=lam_init)
        states.append((
            ckv_state.reshape(nbc, seq, MLA_KV_RANK),
            pc32[:, C_KR + LANE:C_KR + LANE + MLA_ROPE].reshape(nbc, seq, MLA_ROPE),
            pc32[:, C_DK:C_DK + 512].reshape(nbc, seq, DIFF_HEADS, 2, DIFF_QK),
            pc32[:, C_DV:C_DV + 512].reshape(nbc, seq, DIFF_HEADS, DIFF_V),
            pc32[:, C_NK:C_NK + 512].reshape(nbc, seq, NA_HEADS, NA_DIM),
            pc32[:, C_NV:C_NV + 512].reshape(nbc, seq, NA_HEADS, NA_DIM),
            sret,
        ))
        xc = _merge_call((ya, yb, yc, yd), pc, wb, wo, xc, ga1[0], rows_per_mod=mc)
        act = _ffn_up_call(xc, g_f, sh2[0], sc2[0], wup, conv_w[l], conv_b[l], seq=seq, rows_per_mod=mc, tm=tm_c)
        xc = _mm_call(act, wdn, out_dtype=F32, tm=256, tn=d, rows_per_mod=mc, resid=(xc, ga2[0]), out_norm=g_out)

        plat = _mm_call(xl, w_in_r, out_dtype=BF16, tm=tm_l, tn=2048, rows_per_mod=t, norm=(g_a, sh1[1], sc1[1]))
        kl_mla, vl_mla, _ = _mla_kv_call(plat, C_CKV, plat, C_KR, w_ukv_e, g=kvn, rope=(cos32k, sin32k), seq=t)
        kr_c = cache_mla_krope[:, l].reshape(nbl * past, MLA_ROPE)
        krblk_c = jnp.concatenate([jnp.zeros((nbl * past, LANE), F32)] + [kr_c] * 4, axis=1)
        kp_mla, vp_mla = _mla_kv_call(cache_mla_ckv[:, l].reshape(nbl * past, MLA_KV_RANK), 0, krblk_c, 0, w_ukv_e)
        dk_l = _rope_call(plat, C_DK, 512, cos64, sin64, DIFF_QK, t)
        tq = min(512, t)
        ya = _mha_call(plat, C_QN, plat, C_QR, [(kl_mla, 0, vl_mla, 0, t), (kp_mla, 0, vp_mla, 0, past)],
                       nb=nbl, tq=tq, t=t, scale=(MLA_NOPE + MLA_ROPE) ** -0.5, rope=(cos32q, sin32q))
        dk_c = cache_diff_k[:, l].reshape(nbl * past, 512)
        dv_c = cache_diff_v[:, l].reshape(nbl * past, 512)
        yb = _diff_call(plat, C_DQ, [(dk_l, 0, plat, C_DV, t), (dk_c, 0, dv_c, 0, past)], diff_lambda[l], dnorm,
                        nb=nbl, tq=tq, t=t, lam_init=lam_init, rope=(cos64, sin64))
        yc = _na_call(plat, cache_na_k[:, l].reshape(nbl * past, 512), cache_na_v[:, l].reshape(nbl * past, 512),
                      na_rel_bias[l], nb=nbl, t=t, past=past)
        s0 = state_ret[:, l].astype(F32).reshape(nbl, 2, 2, LANE, LANE)
        yd = _ret_call(plat, dec, gn, s0, nb=nbl, tq=tq, t=t)
        xl = _merge_call((ya, yb, yc, yd), plat, wb, wo, xl, ga1[1], rows_per_mod=t)
        act = _ffn_up_call(xl, g_f, sh2[1], sc2[1], wup, conv_w[l], conv_b[l], seq=t, rows_per_mod=t, tm=tm_l)
        xl = _mm_call(act, wdn, out_dtype=F32, tm=256, tn=d, rows_per_mod=t, resid=(xl, ga2[1]), out_norm=g_out)

    y_prompt = xc.reshape(nbc, seq, d)
    y_sample = xl.reshape(nbl, t, d)
    outs = [jnp.stack([st[i] for st in states], axis=1) for i in range(7)]
    return (y_prompt, y_sample, *outs)
```
